```python
import math
import jax, jax.numpy as jnp
from jax import lax
import numpy as np

D_MODEL = 2048
BATCH = 2
SEQ = 4096
DEPTH = 1

CHUNK = 64
D_RWKV = D_MODEL // 2
RWKV_HEAD = 64
RWKV_HEADS = D_RWKV // RWKV_HEAD
DECAY_LORA = max(32, int(round(1.8 * D_MODEL ** 0.5 / 32)) * 32)
A_LORA = max(32, int(round(1.8 * D_MODEL ** 0.5 / 32)) * 32)
GATE_LORA = max(32, int(round(0.6 * D_MODEL ** 0.8 / 32)) * 32)
RWKV_GN_EPS = 64e-5
D_MLSTM = D_MODEL // 2
MLSTM_HEADS = 4
MLSTM_DV = D_MLSTM // MLSTM_HEADS
MLSTM_DQK = MLSTM_DV // 2
D_QK = MLSTM_HEADS * MLSTM_DQK
CONV_K = 4
D_FF = 4 * D_MODEL
LN_EPS = 1e-5
ALPHA = (2.0 * DEPTH) ** 0.25
BETA = (8.0 * DEPTH) ** -0.25

RWKV_SIZES = (D_RWKV, D_RWKV, D_RWKV, DECAY_LORA, A_LORA, GATE_LORA)
MLSTM_SIZES = (D_QK, D_QK, D_MLSTM, MLSTM_HEADS, MLSTM_HEADS, D_MLSTM)
GATE_SIZES = (D_MODEL, D_MODEL)
N_RWKV_COLS = sum(RWKV_SIZES)
N_MLSTM_COLS = sum(MLSTM_SIZES)
D_IN = N_RWKV_COLS + N_MLSTM_COLS + sum(GATE_SIZES)

kernel_name = "hybrid_rwkv7_mlstm_gated_deepnorm_adaln"


def _split(z, sizes):
    idx = [int(i) for i in np.cumsum(sizes)[:-1]]
    return jnp.split(z, idx, axis=-1)


def _ln(x, eps=LN_EPS):
    x = x.astype(jnp.float32)
    mu = jnp.mean(x, axis=-1, keepdims=True)
    var = jnp.mean(jnp.square(x - mu), axis=-1, keepdims=True)
    return (x - mu) * lax.rsqrt(var + eps)


def _token_shift(z, mu):
    prev = jnp.pad(z, ((0, 0), (1, 0), (0, 0)))[:, :-1]
    return z + (prev - z) * mu


def _causal_dwconv(z, w, b):
    zp = jnp.pad(z, ((0, 0), (CONV_K - 1, 0), (0, 0)))
    out = lax.conv_general_dilated(zp, w[:, None, :].astype(z.dtype), window_strides=(1,), padding='VALID',
                                   dimension_numbers=('NWC', 'WIO', 'NWC'), feature_group_count=z.shape[-1])
    return out + b


def _rwkv7(r, k, v, wd, ad, gd, w0, w_decay_up, a0, w_a_up, w_g_up, k_k, k_a, r_k, gn_g, gn_b):
    B, T, _ = r.shape
    H, N = RWKV_HEADS, RWKV_HEAD
    logw = -jax.nn.softplus(-(w0 + jnp.tanh(wd) @ w_decay_up)) - 0.5
    decay = jnp.exp(-jnp.exp(logw))
    a = jax.nn.sigmoid(a0 + ad @ w_a_up)
    g = jax.nn.sigmoid(gd) @ w_g_up
    heads = lambda z: z.reshape(B, T, H, N)
    kk = heads(k * k_k)
    kk = kk * lax.rsqrt(jnp.sum(kk * kk, axis=-1, keepdims=True) + 1e-12)
    kt = k * (1.0 + (a - 1.0) * k_a)
    r_h, w_h, k_h, v_h, a_h = heads(r), heads(decay), heads(kt), heads(v), heads(a)

    def step(S, inp):
        rt, wt, ktt, vt, kkt, at = inp
        Skk = jnp.einsum('bhvk,bhk->bhv', S, kkt)
        S = (S * wt[:, :, None, :] - Skk[..., None] * (kkt * at)[:, :, None, :]
             + vt[..., None] * ktt[:, :, None, :])
        y = jnp.einsum('bhvk,bhk->bhv', S, rt)
        return S, y

    xs = tuple(jnp.moveaxis(z, 1, 0) for z in (r_h, w_h, k_h, v_h, kk, a_h))
    S0 = jnp.zeros((B, H, N, N), jnp.float32)
    _, y = lax.scan(step, S0, xs)
    y = jnp.moveaxis(y, 0, 1)
    mu = jnp.mean(y, axis=-1, keepdims=True)
    var = jnp.mean(jnp.square(y - mu), axis=-1, keepdims=True)
    y = ((y - mu) * lax.rsqrt(var + RWKV_GN_EPS)).reshape(B, T, H * N) * gn_g + gn_b
    bonus = jnp.sum(r_h * k_h * r_k, axis=-1, keepdims=True) * v_h
    return (y + bonus.reshape(B, T, H * N)) * g


def _mlstm(q, k, v, ig, fg, og):
    B, T, _ = q.shape
    H, L = MLSTM_HEADS, CHUNK
    NC = T // L

    def chunks(z, d):
        return z.reshape(B, NC, L, H, d).transpose(1, 0, 3, 2, 4)

    qc = chunks(q, MLSTM_DQK)
    kc = chunks(k, MLSTM_DQK) * (MLSTM_DQK ** -0.5)
    vc = chunks(v, MLSTM_DV)
    ic = ig.reshape(B, NC, L, H).transpose(1, 0, 3, 2)
    lfc = jax.nn.log_sigmoid(fg).reshape(B, NC, L, H).transpose(1, 0, 3, 2)
    causal = jnp.tril(jnp.ones((L, L), dtype=bool))

    def step(carry, inp):
        C, n, m = carry
        qj, kj, vj, ij, lfj = inp
        b = jnp.cumsum(lfj, axis=-1)
        g = b[..., -1]
        Dm = jnp.where(causal, b[..., :, None] - b[..., None, :] + ij[..., None, :], -jnp.inf)
        inter = b + m[..., None]
        mt = jnp.maximum(inter, jnp.max(Dm, axis=-1))
        s = jnp.einsum('bhtd,bhsd->bhts', qj, kj) * jnp.exp(Dm - mt[..., None])
        winter = jnp.exp(inter - mt)
        num = (winter[..., None] * jnp.einsum('bhvd,bhtd->bhtv', C, qj)
               + jnp.einsum('bhts,bhsv->bhtv', s, vj))
        den = winter * jnp.einsum('bhd,bhtd->bht', n, qj) + jnp.sum(s, axis=-1)
        h = num / jnp.maximum(jnp.abs(den), jnp.exp(-mt))[..., None]
        ls = g[..., None] - b + ij
        m_new = jnp.maximum(g + m, jnp.max(ls, axis=-1))
        ws = jnp.exp(ls - m_new[..., None])
        wc = jnp.exp(g + m - m_new)
        C = wc[..., None, None] * C + jnp.einsum('bhs,bhsv,bhsd->bhvd', ws, vj, kj)
        n = wc[..., None] * n + jnp.einsum('bhs,bhsd->bhd', ws, kj)
        return (C, n, m_new), h

    carry0 = (jnp.zeros((B, H, MLSTM_DV, MLSTM_DQK), jnp.float32),
              jnp.zeros((B, H, MLSTM_DQK), jnp.float32),
              jnp.zeros((B, H), jnp.float32))
    _, h = lax.scan(step, carry0, (qc, kc, vc, ic, lfc))
    h = h.transpose(1, 0, 3, 2, 4).reshape(B, T, H * MLSTM_DV)
    return jax.nn.sigmoid(og) * h


def setup_inputs(seed: int = 0) -> dict:
    key = jax.random.key(seed)
    keys = list(jax.random.split(key, 40))

    def nrm(shape, scale):
        return jax.random.normal(keys.pop(), shape, jnp.float32) * scale

    Ld = DEPTH
    ratio = jnp.arange(D_RWKV, dtype=jnp.float32) / (D_RWKV - 1)
    return {
        "x": nrm((BATCH, SEQ, D_MODEL), 1.0),
        "c": nrm((BATCH, D_MODEL), 1.0),
        "w_ada": nrm((Ld, D_MODEL, 6 * D_MODEL), D_MODEL ** -0.5),
        "b_ada": nrm((Ld, 6 * D_MODEL), 0.02),
        "w_in": nrm((Ld, D_MODEL, D_IN), D_MODEL ** -0.5),
        "mu_shift": jax.random.uniform(keys.pop(), (Ld, N_RWKV_COLS), jnp.float32),
        "w0": (-6.5 + 5.0 * ratio ** 0.85)[None] + nrm((Ld, D_RWKV), 0.05),
        "w_decay_up": nrm((Ld, DECAY_LORA, D_RWKV), 0.5 * DECAY_LORA ** -0.5),
        "a0": nrm((Ld, D_RWKV), 0.1),
        "w_a_up": nrm((Ld, A_LORA, D_RWKV), A_LORA ** -0.5),
        "w_g_up": nrm((Ld, GATE_LORA, D_RWKV), GATE_LORA ** -0.5),
        "k_k": 0.85 + nrm((Ld, D_RWKV), 0.02),
        "k_a": 1.0 + nrm((Ld, D_RWKV), 0.02),
        "r_k": nrm((Ld, RWKV_HEADS, RWKV_HEAD), 0.1),
        "gn_g": 1.0 + nrm((Ld, D_RWKV), 0.02),
        "gn_b": nrm((Ld, D_RWKV), 0.02),
        "w_conv": nrm((Ld, CONV_K, 2 * D_QK), CONV_K ** -0.5),
        "b_conv": nrm((Ld, 2 * D_QK), 0.02),
        "b_igate": nrm((Ld, MLSTM_HEADS), 0.1),
        "b_fgate": jnp.linspace(3.0, 6.0, MLSTM_HEADS, dtype=jnp.float32)[None] + nrm((Ld, MLSTM_HEADS), 0.05),
        "w_branch_a": nrm((Ld, D_RWKV, D_MODEL), D_RWKV ** -0.5),
        "w_branch_b": nrm((Ld, D_MLSTM, D_MODEL), D_MLSTM ** -0.5),
        "w_out": nrm((Ld, D_MODEL, D_MODEL), BETA * D_MODEL ** -0.5),
        "ln1_g": 1.0 + nrm((Ld, D_MODEL), 0.02),
        "ln1_b": nrm((Ld, D_MODEL), 0.02),
        "w_ff1": nrm((Ld, D_MODEL, D_FF), D_MODEL ** -0.5),
        "b_ff1": nrm((Ld, D_FF), 0.02),
        "w_ff2": nrm((Ld, D_FF, D_MODEL), BETA * D_FF ** -0.5),
        "b_ff2": nrm((Ld, D_MODEL), 0.02),
        "ln2_g": 1.0 + nrm((Ld, D_MODEL), 0.02),
        "ln2_b": nrm((Ld, D_MODEL), 0.02),
    }


def reference(x, c, w_ada, b_ada, w_in, mu_shift, w0, w_decay_up, a0, w_a_up, w_g_up, k_k, k_a, r_k,
              gn_g, gn_b, w_conv, b_conv, b_igate, b_fgate, w_branch_a, w_branch_b, w_out,
              ln1_g, ln1_b, w_ff1, b_ff1, w_ff2, b_ff2, ln2_g, ln2_b):
    out_dtype = x.dtype
    x = x.astype(jnp.float32)
    c = c.astype(jnp.float32)
    for l in range(DEPTH):
        ada = jax.nn.silu(c) @ w_ada[l] + b_ada[l]
        sh1, sc1, g1, sh2, sc2, g2 = jnp.split(ada[:, None, :], 6, axis=-1)

        u = _ln(x) * (1.0 + sc1) + sh1
        z = u @ w_in[l]
        z_rwkv, z_mlstm, z_gate = _split(z, (N_RWKV_COLS, N_MLSTM_COLS, sum(GATE_SIZES)))
        z_rwkv = _token_shift(z_rwkv, mu_shift[l])
        r, k, v, wd, ad, gd = _split(z_rwkv, RWKV_SIZES)
        ya = _rwkv7(r, k, v, wd, ad, gd, w0[l], w_decay_up[l], a0[l], w_a_up[l], w_g_up[l],
                    k_k[l], k_a[l], r_k[l], gn_g[l], gn_b[l])
        mq, mk, mv, mi, mf, mo = _split(z_mlstm, MLSTM_SIZES)
        qk = jax.nn.silu(_causal_dwconv(jnp.concatenate([mq, mk], axis=-1), w_conv[l], b_conv[l]))
        mq, mk = jnp.split(qk, 2, axis=-1)
        yb = _mlstm(mq, mk, mv, mi + b_igate[l], mf + b_fgate[l], mo)
        gate_a, gate_b = _split(z_gate, GATE_SIZES)
        merged = (jax.nn.sigmoid(gate_a) * (ya @ w_branch_a[l])
                  + jax.nn.sigmoid(gate_b) * (yb @ w_branch_b[l]))
        y = merged @ w_out[l]
        x = _ln(ALPHA * x + g1 * y) * ln1_g[l] + ln1_b[l]

        u2 = _ln(x) * (1.0 + sc2) + sh2
        y2 = jnp.square(jax.nn.relu(u2 @ w_ff1[l] + b_ff1[l])) @ w_ff2[l] + b_ff2[l]
        x = _ln(ALPHA * x + g2 * y2) * ln2_g[l] + ln2_b[l]
    return x.astype(out_dtype)
```

```python
import functools

import jax
import jax.numpy as jnp
from jax import lax
from jax.experimental import pallas as pl
from jax.experimental.pallas import tpu as pltpu

F32 = jnp.float32
BF16 = jnp.bfloat16

D_MODEL = 2048
DEPTH = 1
D_RWKV = D_MODEL // 2
RWKV_HEAD = 64
DECAY_LORA = 96
A_LORA = 96
GATE_LORA = 256
RWKV_GN_EPS = 64e-5
D_MLSTM = D_MODEL // 2
MLSTM_HEADS = 4
MLSTM_DV = D_MLSTM // MLSTM_HEADS
MLSTM_DQK = MLSTM_DV // 2
D_QK = MLSTM_HEADS * MLSTM_DQK
CONV_K = 4
D_FF = 4 * D_MODEL
LN_EPS = 1e-5
ALPHA = (2.0 * DEPTH) ** 0.25

LANES = 128
LORA_PAD = 128
RW_COLS = 3 * D_RWKV + GATE_LORA + 2 * LORA_PAD
ML_GATE_PAD = 128
ML_COLS = 2 * D_QK + 2 * D_MLSTM + ML_GATE_PAD
RWKV_CHUNK = 64
RWKV_TB = 256
MLSTM_CHUNK = 256
VMEM_LIMIT = 56 * 1024 * 1024

NN = (((1,), (0,)), ((), ()))
NT = (((1,), (1,)), ((), ()))
TN = (((0,), (0,)), ((), ()))


def _cparams(sem):
    return pltpu.CompilerParams(dimension_semantics=sem, vmem_limit_bytes=VMEM_LIMIT)


def _bf16_parts(x, n):
    if x.dtype == BF16:
        return [x]
    parts, rem = [], x
    for i in range(n):
        p = rem.astype(BF16)
        parts.append(p)
        if i + 1 < n:
            rem = rem - p.astype(F32)
    return parts


def _mdot(a, b, na=1, nb=1, dims=NN):
    ap, bp = _bf16_parts(a, na), _bf16_parts(b, nb)
    order = max(len(ap), len(bp))
    acc = None
    for i, x in enumerate(ap):
        for j, y in enumerate(bp):
            if i + j < order:
                t = lax.dot_general(x, y, dims, preferred_element_type=F32)
                acc = t if acc is None else acc + t
    return acc


def _sigmoid(x):
    return 1.0 / (1.0 + jnp.exp(-x))


def _softplus(x):
    return jnp.maximum(x, 0.0) + jnp.log(1.0 + jnp.exp(-jnp.abs(x)))


def _layernorm(x):
    mu = jnp.mean(x, axis=-1, keepdims=True)
    xc = x - mu
    var = jnp.mean(xc * xc, axis=-1, keepdims=True)
    return xc * lax.rsqrt(var + LN_EPS)


def _iota2(shape, dim):
    return lax.broadcasted_iota(jnp.int32, shape, dim)


def _head_block_ones(n, head):
    r, c = _iota2((n, n), 0), _iota2((n, n), 1)
    return jnp.where((r // head) == (c // head), 1.0, 0.0).astype(BF16)


def _head_sum(x, ones_blk):
    return _mdot(x, ones_blk, na=2, nb=1)


def _ada_kernel(ct_ref, w_ref, b_ref, o_ref):
    tn = w_ref.shape[1]

    def body(i, acc):
        a0, a1 = acc
        r0 = pl.multiple_of(i * 8, 8)
        wv = w_ref[pl.ds(r0, 8), :]
        cv = ct_ref[pl.ds(r0, 8), :]
        sv = cv * _sigmoid(cv)
        return a0 + sv[:, 0:1] * wv, a1 + sv[:, 1:2] * wv

    zero = jnp.zeros((8, tn), F32)
    a0, a1 = lax.fori_loop(0, w_ref.shape[0] // 8, body, (zero, zero), unroll=8)
    o_ref[0:1, :] = jnp.sum(a0, axis=0, keepdims=True) + b_ref[...]
    o_ref[1:2, :] = jnp.sum(a1, axis=0, keepdims=True) + b_ref[...]


def _ada(c, w_ada, b_ada):
    batch = c.shape[0]
    assert batch == 2
    n = w_ada.shape[1]
    tn = 1536
    return pl.pallas_call(
        _ada_kernel,
        out_shape=jax.ShapeDtypeStruct((batch, n), F32),
        grid=(n // tn,),
        in_specs=[pl.BlockSpec((D_MODEL, batch), lambda j: (0, 0)),
                  pl.BlockSpec((D_MODEL, tn), lambda j: (0, j)),
                  pl.BlockSpec((1, tn), lambda j: (0, j))],
        out_specs=pl.BlockSpec((batch, tn), lambda j: (0, j)),
        compiler_params=_cparams(("arbitrary",)),
        name="ada",
    )(c.T, w_ada, b_ada.reshape(1, n))


def _lnmod_kernel(x_ref, ada_ref, o_ref):
    xn = _layernorm(x_ref[...])
    sh, sc = ada_ref[0, 0:1, :], ada_ref[0, 1:2, :]
    o_ref[...] = (xn * (1.0 + sc) + sh).astype(o_ref.dtype)


def _lnmod(x2, ada3, seq):
    m = x2.shape[0]
    tb = 512
    per_b = seq // tb
    return pl.pallas_call(
        _lnmod_kernel,
        out_shape=jax.ShapeDtypeStruct((m, D_MODEL), BF16),
        grid=(m // tb,),
        in_specs=[pl.BlockSpec((tb, D_MODEL), lambda i: (i, 0)),
                  pl.BlockSpec((1, 6, D_MODEL), lambda i: (i // per_b, 0, 0))],
        out_specs=pl.BlockSpec((tb, D_MODEL), lambda i: (i, 0)),
        compiler_params=_cparams(("arbitrary",)),
        name="lnmod",
    )(x2, ada3)


def _mm_kernel(x_ref, w_ref, o_ref):
    o_ref[...] = jnp.dot(x_ref[...], w_ref[...], preferred_element_type=F32).astype(o_ref.dtype)


def _matmul(x, w, tm, tn, out_dtype, name):
    m, k = x.shape
    n = w.shape[1]
    return pl.pallas_call(
        _mm_kernel,
        out_shape=jax.ShapeDtypeStruct((m, n), out_dtype),
        grid=(n // tn, m // tm),
        in_specs=[pl.BlockSpec((tm, k), lambda j, i: (i, 0)),
                  pl.BlockSpec((k, tn), lambda j, i: (0, j))],
        out_specs=pl.BlockSpec((tm, tn), lambda j, i: (i, j)),
        compiler_params=_cparams(("arbitrary", "arbitrary")),
        name=name,
    )(x, w)


def _rwkv_prep_kernel(z_ref, mu_ref, w0_ref, a0_ref, kk_ref, ka_ref, rk_ref, wdu_ref, wau_ref, wgu_ref,
                      r_o, ld_o, kt_o, v_o, kkn_o, bb_o, g_o, bonus_o, carry_ref):
    tb = z_ref.shape[0]

    @pl.when(pl.program_id(1) == 0)
    def _():
        carry_ref[...] = jnp.zeros_like(carry_ref)

    z = z_ref[...]
    prev = pltpu.roll(z, 1, 0)
    prev = jnp.where(_iota2(z.shape, 0) == 0, carry_ref[0:1, :], prev)
    carry_ref[0:1, :] = z[tb - 1:tb, :]
    zs = z + (prev - z) * mu_ref[...]

    c = D_RWKV
    r, k, v = zs[:, 0:c], zs[:, c:2 * c], zs[:, 2 * c:3 * c]
    gd = zs[:, 3 * c:3 * c + GATE_LORA]
    wd = zs[:, 3 * c + GATE_LORA:3 * c + GATE_LORA + LORA_PAD]
    ad = zs[:, 3 * c + GATE_LORA + LORA_PAD:]

    logw = -_softplus(-(w0_ref[...] + _mdot(jnp.tanh(wd), wdu_ref[...], 2, 2))) - 0.5
    ld = -jnp.exp(logw)
    a = _sigmoid(a0_ref[...] + _mdot(ad, wau_ref[...], 2, 2))
    g = _mdot(_sigmoid(gd), wgu_ref[...], 1, 1)

    ones_blk = _head_block_ones(LANES, RWKV_HEAD)
    kq = k * kk_ref[...]
    kt = k * (1.0 + (a - 1.0) * ka_ref[...])
    sq = kq * kq
    bn = r * kt * rk_ref[...]
    ss = jnp.concatenate([_head_sum(sq[:, s:s + LANES], ones_blk) for s in range(0, c, LANES)], axis=1)
    bs = jnp.concatenate([_head_sum(bn[:, s:s + LANES], ones_blk) for s in range(0, c, LANES)], axis=1)
    kkn = kq * lax.rsqrt(ss + 1e-12)

    r_o[...] = r
    ld_o[...] = ld
    kt_o[...] = kt
    v_o[...] = v
    kkn_o[...] = kkn
    bb_o[...] = a * kkn
    g_o[...] = g
    bonus_o[...] = bs * v


def _rwkv_prep(z_rw, mu_rw, w0, a0, k_k, k_a, r_k, wdu, wau, wgu, batch, seq):
    m = z_rw.shape[0]
    tb = RWKV_TB
    per_b = seq // tb
    row = lambda b, j: (b * per_b + j, 0)
    const = lambda b, j: (0, 0)
    vec = pl.BlockSpec((1, D_RWKV), const)
    out = jax.ShapeDtypeStruct((m, D_RWKV), F32)
    return pl.pallas_call(
        _rwkv_prep_kernel,
        out_shape=[out] * 8,
        grid=(batch, per_b),
        in_specs=[pl.BlockSpec((tb, RW_COLS), row), pl.BlockSpec((1, RW_COLS), const),
                  vec, vec, vec, vec, vec,
                  pl.BlockSpec((LORA_PAD, D_RWKV), const), pl.BlockSpec((LORA_PAD, D_RWKV), const),
                  pl.BlockSpec((GATE_LORA, D_RWKV), const)],
        out_specs=[pl.BlockSpec((tb, D_RWKV), row)] * 8,
        scratch_shapes=[pltpu.VMEM((8, RW_COLS), F32)],
        compiler_params=_cparams(("arbitrary", "arbitrary")),
        name="rwkv_prep",
    )(z_rw, mu_rw, w0, a0, k_k, k_a, r_k, wdu, wau, wgu)


def _rwkv_core_kernel(r_ref, ld_ref, kt_ref, v_ref, kk_ref, bb_ref, g_ref, bonus_ref, gng_ref, gnb_ref,
                      ya_ref, s_ref):
    tb = r_ref.shape[0]
    lc = RWKV_CHUNK
    n = 2 * lc
    nchunk = tb // lc

    @pl.when(pl.program_id(2) == 0)
    def _():
        s_ref[...] = jnp.zeros_like(s_ref)

    lane = _iota2((1, LANES), 1)
    m0 = jnp.where(lane < RWKV_HEAD, 1.0, 0.0)
    m1 = 1.0 - m0
    ri, ci = _iota2((n, n), 0), _iota2((n, n), 1)
    same_head = (ri // lc) == (ci // lc)
    strict = jnp.where(same_head & (ri > ci), 1.0, 0.0)
    incl = jnp.where(same_head & (ri >= ci), 1.0, 0.0)
    eye = jnp.where(ri == ci, 1.0, 0.0)

    rt, ct = _iota2((tb, tb), 0), _iota2((tb, tb), 1)
    tri = jnp.where(((rt // lc) == (ct // lc)) & (ct <= rt), 1.0, 0.0).astype(BF16)

    ld = ld_ref[...]
    cl = _mdot(tri, ld, 1, 3)
    e_pos = jnp.exp(cl)
    e_neg = jnp.exp(-cl)
    a_hat = kk_ref[...] * jnp.exp(cl - ld)
    b_hat = bb_ref[...] * e_neg
    k_hat = kt_ref[...] * e_neg
    r_hat = r_ref[...] * e_pos
    v_all = v_ref[...]

    def stack(x):
        return jnp.concatenate([x * m0, x * m1], axis=0)

    ys = []
    for cidx in range(nchunk):
        sl = slice(cidx * lc, (cidx + 1) * lc)
        a_st, b_st, k_st, r_st, v_st = (stack(x[sl]) for x in (a_hat, b_hat, k_hat, r_hat, v_all))
        ar = jnp.concatenate([a_st, r_st], axis=0)
        bk = jnp.concatenate([b_st, k_st], axis=0)
        p = _mdot(ar, bk, 2, 2, NT)
        mab = p[0:n, 0:n] * strict
        mak = p[0:n, n:] * strict
        mrb = p[n:, 0:n] * incl
        mrk = p[n:, n:] * incl

        x = eye - jnp.where((ri // 2 == ci // 2), mab, 0.0)
        s = 2
        while s < lc:
            nk = jnp.where((ri // (2 * s) == ci // (2 * s)) & (ri // s != ci // s), mab, 0.0)
            x = x - _mdot(x, _mdot(nk, x, 2, 2), 2, 2)
            s *= 2

        w1 = _mdot(mak, v_st, 2, 2)
        ta = _mdot(x, jnp.concatenate([w1, a_st], axis=1), 2, 2)
        u0, gm = -ta[:, 0:LANES], ta[:, LANES:]
        y0 = _mdot(mrb, u0, 2, 2) + _mdot(mrk, v_st, 2, 2)
        q = r_st - _mdot(mrb, gm, 2, 2)
        pl_row = e_pos[(cidx + 1) * lc - 1:(cidx + 1) * lc, :]
        ac = (jnp.where(_iota2((LANES, LANES), 0) == _iota2((LANES, LANES), 1), 1.0, 0.0)
              - _mdot(gm, b_st, 2, 2, TN)) * pl_row
        cc = (_mdot(u0, b_st, 2, 2, TN) + _mdot(v_st, k_st, 2, 2, TN)) * pl_row

        s0 = s_ref[...]
        y_st = y0 + _mdot(q, s0, 2, 2, NT)
        s_ref[...] = _mdot(s0, ac, 2, 2) + cc
        ys.append(y_st[0:lc] + y_st[lc:])

    y = jnp.concatenate(ys, axis=0)
    ones_blk = _head_block_ones(LANES, RWKV_HEAD)
    mu = _head_sum(y, ones_blk) * (1.0 / RWKV_HEAD)
    yc = y - mu
    var = _head_sum(yc * yc, ones_blk) * (1.0 / RWKV_HEAD)
    yn = yc * lax.rsqrt(var + RWKV_GN_EPS) * gng_ref[...] + gnb_ref[...]
    ya_ref[...] = ((yn + bonus_ref[...]) * g_ref[...]).astype(ya_ref.dtype)


def _rwkv_core(r, ld, kt, v, kkn, bb, g, bonus, gn_g, gn_b, batch, seq):
    m = r.shape[0]
    tb = RWKV_TB
    per_b = seq // tb
    npair = D_RWKV // LANES
    slab = pl.BlockSpec((tb, LANES), lambda b, p, j: (b * per_b + j, p))
    vec = pl.BlockSpec((1, LANES), lambda b, p, j: (0, p))
    return pl.pallas_call(
        _rwkv_core_kernel,
        out_shape=jax.ShapeDtypeStruct((m, D_RWKV), BF16),
        grid=(batch, npair, per_b),
        in_specs=[slab] * 8 + [vec, vec],
        out_specs=slab,
        scratch_shapes=[pltpu.VMEM((LANES, LANES), F32)],
        compiler_params=_cparams(("arbitrary", "arbitrary", "arbitrary")),
        name="rwkv_core",
    )(r, ld, kt, v, kkn, bb, g, bonus, gn_g, gn_b)


def _mlstm_kernel(z_ref, wconv_ref, bconv_ref, gbias_ref, yb_ref, tail_ref, ct_ref, n_ref, m_ref):
    lc = z_ref.shape[0]
    dqk, dv, nh = MLSTM_DQK, MLSTM_DV, MLSTM_HEADS

    @pl.when(pl.program_id(1) == 0)
    def _():
        tail_ref[...] = jnp.zeros_like(tail_ref)
        ct_ref[...] = jnp.zeros_like(ct_ref)
        n_ref[...] = jnp.zeros_like(n_ref)
        m_ref[...] = jnp.zeros_like(m_ref)

    zqk = z_ref[:, 0:2 * D_QK]
    ext = jnp.concatenate([tail_ref[...], zqk], axis=0)
    tail_ref[...] = zqk[lc - 8:lc, :]
    conv = bconv_ref[...] + wconv_ref[CONV_K - 1:CONV_K, :] * zqk
    for d in range(1, CONV_K):
        conv = conv + wconv_ref[CONV_K - 1 - d:CONV_K - d, :] * ext[8 - d:8 - d + lc, :]
    qk = conv * _sigmoid(conv)
    q_all = qk[:, 0:D_QK]
    k_all = qk[:, D_QK:] * (dqk ** -0.5)
    v_all = z_ref[:, 2 * D_QK:2 * D_QK + D_MLSTM]
    o_all = z_ref[:, 2 * D_QK + D_MLSTM:2 * D_QK + 2 * D_MLSTM]

    gz = z_ref[:, 2 * D_QK + 2 * D_MLSTM:] + gbias_ref[...]
    lf_col = jnp.minimum(gz, 0.0) - jnp.log(1.0 + jnp.exp(-jnp.abs(gz)))
    rr, cc = _iota2((lc, lc), 0), _iota2((lc, lc), 1)
    causal = cc <= rr
    tri = jnp.where(causal, 1.0, 0.0).astype(BF16)
    b_col = _mdot(tri, lf_col, 1, 3)
    sel = jnp.where(_iota2((8, LANES), 0) == _iota2((8, LANES), 1), 1.0, 0.0).astype(BF16)
    g_row = _mdot(sel, gz, 1, 3, NT)
    lf_row = jnp.minimum(g_row, 0.0) - jnp.log(1.0 + jnp.exp(-jnp.abs(g_row)))
    b_row = _mdot(lf_row, tri, 3, 1, NT)

    for h in range(nh):
        q = q_all[:, h * dqk:(h + 1) * dqk]
        k = k_all[:, h * dqk:(h + 1) * dqk]
        v = v_all[:, h * dv:(h + 1) * dv]
        bc = b_col[:, nh + h:nh + h + 1]
        ic = gz[:, h:h + 1]
        br = b_row[nh + h:nh + h + 1, :]
        ir = g_row[h:h + 1, :]
        g_tot = b_col[lc - 1:lc, nh + h:nh + h + 1]
        m_prev = m_ref[0:1, h:h + 1]

        dm = jnp.where(causal, bc - br + ir, -jnp.inf)
        inter = bc + m_prev
        mt = jnp.maximum(inter, jnp.max(dm, axis=-1, keepdims=True))
        sc = _mdot(q, k, 1, 1, NT) * jnp.exp(dm - mt)
        winter = jnp.exp(inter - mt)
        ct = ct_ref[h]
        nvec = n_ref[h:h + 1, :]
        num = winter * _mdot(q, ct, 1, 1) + _mdot(sc, v, 1, 1)
        den = winter * jnp.sum(q * nvec, axis=-1, keepdims=True) + jnp.sum(sc, axis=-1, keepdims=True)
        hout = num / jnp.maximum(jnp.abs(den), jnp.exp(-mt))

        m_new = jnp.maximum(g_tot + m_prev, jnp.max(g_tot - br + ir, axis=-1, keepdims=True))
        ws = jnp.exp(g_tot - bc + ic - m_new)
        wc = jnp.exp(g_tot + m_prev - m_new)
        kw = k * ws
        ct_ref[h] = wc * ct + _mdot(kw, v, 1, 1, TN)
        n_ref[h:h + 1, :] = wc * nvec + jnp.sum(kw, axis=0, keepdims=True)
        m_ref[0:1, h:h + 1] = m_new

        o = o_all[:, h * dv:(h + 1) * dv]
        yb_ref[:, h * dv:(h + 1) * dv] = (_sigmoid(o) * hout).astype(yb_ref.dtype)


def _mlstm(z_ml, w_conv, b_conv, gbias, batch, seq):
    m = z_ml.shape[0]
    lc = MLSTM_CHUNK
    per_b = seq // lc
    const = lambda b, j: (0, 0)
    return pl.pallas_call(
        _mlstm_kernel,
        out_shape=jax.ShapeDtypeStruct((m, D_MLSTM), BF16),
        grid=(batch, per_b),
        in_specs=[pl.BlockSpec((lc, ML_COLS), lambda b, j: (b * per_b + j, 0)),
                  pl.BlockSpec((CONV_K, 2 * D_QK), const),
                  pl.BlockSpec((1, 2 * D_QK), const),
                  pl.BlockSpec((1, ML_GATE_PAD), const)],
        out_specs=pl.BlockSpec((lc, D_MLSTM), lambda b, j: (b * per_b + j, 0)),
        scratch_shapes=[pltpu.VMEM((8, 2 * D_QK), F32),
                        pltpu.VMEM((MLSTM_HEADS, MLSTM_DQK, MLSTM_DV), F32),
                        pltpu.VMEM((8, MLSTM_DQK), F32),
                        pltpu.VMEM((8, LANES), F32)],
        compiler_params=_cparams(("arbitrary", "arbitrary")),
        name="mlstm",
    )(z_ml, w_conv, b_conv, gbias)


def _merge_kernel(ya_ref, yb_ref, zg_ref, wa_ref, wb_ref, o_ref):
    pa = jnp.dot(ya_ref[...], wa_ref[...], preferred_element_type=F32)
    pb = jnp.dot(yb_ref[...], wb_ref[...], preferred_element_type=F32)
    ga = _sigmoid(zg_ref[:, 0:D_MODEL])
    gb = _sigmoid(zg_ref[:, D_MODEL:])
    o_ref[...] = (ga * pa + gb * pb).astype(o_ref.dtype)


def _merge(ya, yb, zg, wa, wb):
    m = ya.shape[0]
    tm = 256
    const = lambda i: (0, 0)
    return pl.pallas_call(
        _merge_kernel,
        out_shape=jax.ShapeDtypeStruct((m, D_MODEL), BF16),
        grid=(m // tm,),
        in_specs=[pl.BlockSpec((tm, D_RWKV), lambda i: (i, 0)),
                  pl.BlockSpec((tm, D_MLSTM), lambda i: (i, 0)),
                  pl.BlockSpec((tm, 2 * D_MODEL), lambda i: (i, 0)),
                  pl.BlockSpec((D_RWKV, D_MODEL), const),
                  pl.BlockSpec((D_MLSTM, D_MODEL), const)],
        out_specs=pl.BlockSpec((tm, D_MODEL), lambda i: (i, 0)),
        compiler_params=_cparams(("arbitrary",)),
        name="merge",
    )(ya, yb, zg, wa, wb)


def _outproj_kernel(mg_ref, w_ref, x_ref, ada_ref, g_ref, b_ref, x1_ref, u2_ref):
    y = jnp.dot(mg_ref[...], w_ref[...], preferred_element_type=F32)
    g1 = ada_ref[0, 2:3, :]
    x1 = _layernorm(ALPHA * x_ref[...] + g1 * y) * g_ref[...] + b_ref[...]
    x1_ref[...] = x1
    sh2, sc2 = ada_ref[0, 3:4, :], ada_ref[0, 4:5, :]
    u2_ref[...] = (_layernorm(x1) * (1.0 + sc2) + sh2).astype(u2_ref.dtype)


def _outproj(merged, w_out, x2, ada3, ln_g, ln_b, seq):
    m = merged.shape[0]
    tm = 256
    per_b = seq // tm
    const = lambda i: (0, 0)
    row = pl.BlockSpec((tm, D_MODEL), lambda i: (i, 0))
    return pl.pallas_call(
        _outproj_kernel,
        out_shape=[jax.ShapeDtypeStruct((m, D_MODEL), F32), jax.ShapeDtypeStruct((m, D_MODEL), BF16)],
        grid=(m // tm,),
        in_specs=[row, pl.BlockSpec((D_MODEL, D_MODEL), const), row,
                  pl.BlockSpec((1, 6, D_MODEL), lambda i: (i // per_b, 0, 0)),
                  pl.BlockSpec((1, D_MODEL), const), pl.BlockSpec((1, D_MODEL), const)],
        out_specs=[row, row],
        compiler_params=_cparams(("arbitrary",)),
        name="outproj",
    )(merged, w_out, x2, ada3, ln_g, ln_b)


def _ff1_kernel(x_ref, w_ref, b_ref, o_ref):
    h = jnp.dot(x_ref[...], w_ref[...], preferred_element_type=F32) + b_ref[...]
    h = jnp.maximum(h, 0.0)
    o_ref[...] = (h * h).astype(o_ref.dtype)


def _ff1(u2, w1, b1):
    m, k = u2.shape
    n = w1.shape[1]
    tm, tn = 1024, 1024
    return pl.pallas_call(
        _ff1_kernel,
        out_shape=jax.ShapeDtypeStruct((m, n), BF16),
        grid=(n // tn, m // tm),
        in_specs=[pl.BlockSpec((tm, k), lambda j, i: (i, 0)),
                  pl.BlockSpec((k, tn), lambda j, i: (0, j)),
                  pl.BlockSpec((1, tn), lambda j, i: (0, j))],
        out_specs=pl.BlockSpec((tm, tn), lambda j, i: (i, j)),
        compiler_params=_cparams(("arbitrary", "arbitrary")),
        name="ff1",
    )(u2, w1, b1)


def _ff2_kernel(h_ref, w_ref, b_ref, x1_ref, ada_ref, g_ref, bb_ref, o_ref, acc_ref):
    kk = pl.program_id(1)

    @pl.when(kk == 0)
    def _():
        acc_ref[...] = jnp.zeros_like(acc_ref)

    acc_ref[...] += jnp.dot(h_ref[...], w_ref[...], preferred_element_type=F32)

    @pl.when(kk == pl.num_programs(1) - 1)
    def _():
        y2 = acc_ref[...] + b_ref[...]
        g2 = ada_ref[0, 5:6, :]
        o_ref[...] = (_layernorm(ALPHA * x1_ref[...] + g2 * y2) * g_ref[...] + bb_ref[...]).astype(o_ref.dtype)


def _ff2(h, w2, b2, x1, ada3, ln_g, ln_b, seq, out_dtype):
    m, k = h.shape
    tm, tk = 512, 1024
    per_b = seq // tm
    const = lambda i, kk: (0, 0)
    row = pl.BlockSpec((tm, D_MODEL), lambda i, kk: (i, 0))
    return pl.pallas_call(
        _ff2_kernel,
        out_shape=jax.ShapeDtypeStruct((m, D_MODEL), out_dtype),
        grid=(m // tm, k // tk),
        in_specs=[pl.BlockSpec((tm, tk), lambda i, kk: (i, kk)),
                  pl.BlockSpec((tk, D_MODEL), lambda i, kk: (kk, 0)),
                  pl.BlockSpec((1, D_MODEL), const), row,
                  pl.BlockSpec((1, 6, D_MODEL), lambda i, kk: (i // per_b, 0, 0)),
                  pl.BlockSpec((1, D_MODEL), const), pl.BlockSpec((1, D_MODEL), const)],
        out_specs=row,
        scratch_shapes=[pltpu.VMEM((tm, D_MODEL), F32)],
        compiler_params=_cparams(("arbitrary", "arbitrary")),
        name="ff2",
    )(h, w2, b2, x1, ada3, ln_g, ln_b)


def _pad_cols(w, width):
    return jnp.pad(w, ((0, 0), (0, width - w.shape[1])))


def _pad_rows(w, height):
    return jnp.pad(w, ((0, height - w.shape[0]), (0, 0)))


def _split_w_in(w):
    c = D_RWKV
    o = 0
    r_k_v = w[:, o:o + 3 * c]; o += 3 * c
    wd = w[:, o:o + DECAY_LORA]; o += DECAY_LORA
    ad = w[:, o:o + A_LORA]; o += A_LORA
    gd = w[:, o:o + GATE_LORA]; o += GATE_LORA
    q_k = w[:, o:o + 2 * D_QK]; o += 2 * D_QK
    mv = w[:, o:o + D_MLSTM]; o += D_MLSTM
    i_f = w[:, o:o + 2 * MLSTM_HEADS]; o += 2 * MLSTM_HEADS
    mo = w[:, o:o + D_MLSTM]; o += D_MLSTM
    gates = w[:, o:]
    w_rw = jnp.concatenate([r_k_v, gd, _pad_cols(wd, LORA_PAD), _pad_cols(ad, LORA_PAD)], axis=1)
    w_ml = jnp.concatenate([q_k, mv, mo, _pad_cols(i_f, ML_GATE_PAD)], axis=1)
    return w_rw, w_ml, gates


def _layer(x2, c, batch, seq, w_ada, b_ada, w_in, mu_shift, w0, w_decay_up, a0, w_a_up, w_g_up, k_k, k_a, r_k,
           gn_g, gn_b, w_conv, b_conv, b_igate, b_fgate, w_branch_a, w_branch_b, w_out,
           ln1_g, ln1_b, w_ff1, b_ff1, w_ff2, b_ff2, ln2_g, ln2_b):
    row = lambda v: v.reshape(1, -1)

    ada3 = _ada(c, w_ada, b_ada).reshape(batch, 6, D_MODEL)
    u = _lnmod(x2, ada3, seq)

    w_rw, w_ml, w_gt = _split_w_in(w_in)
    z_rw = _matmul(u, w_rw.astype(BF16), 1024, 512, F32, "win_rwkv")
    z_ml = _matmul(u, w_ml.astype(BF16), 1024, 640, F32, "win_mlstm")
    z_gt = _matmul(u, w_gt.astype(BF16), 1024, 1024, F32, "win_gate")

    cc = D_RWKV
    mu = mu_shift
    mu_rw = jnp.concatenate([mu[:3 * cc], mu[3 * cc + DECAY_LORA + A_LORA:],
                             jnp.pad(mu[3 * cc:3 * cc + DECAY_LORA], (0, LORA_PAD - DECAY_LORA)),
                             jnp.pad(mu[3 * cc + DECAY_LORA:3 * cc + DECAY_LORA + A_LORA], (0, LORA_PAD - A_LORA))])
    prep = _rwkv_prep(z_rw, row(mu_rw), row(w0), row(a0), row(k_k), row(k_a), row(r_k),
                      _pad_rows(w_decay_up, LORA_PAD), _pad_rows(w_a_up, LORA_PAD), w_g_up, batch, seq)
    ya = _rwkv_core(*prep, row(gn_g), row(gn_b), batch, seq)

    gbias = jnp.pad(jnp.concatenate([b_igate, b_fgate]), (0, ML_GATE_PAD - 2 * MLSTM_HEADS))
    yb = _mlstm(z_ml, w_conv, row(b_conv), row(gbias), batch, seq)

    merged = _merge(ya, yb, z_gt, w_branch_a.astype(BF16), w_branch_b.astype(BF16))
    x1, u2 = _outproj(merged, w_out.astype(BF16), x2, ada3, row(ln1_g), row(ln1_b), seq)

    h = _ff1(u2, w_ff1.astype(BF16), row(b_ff1))
    return _ff2(h, w_ff2.astype(BF16), row(b_ff2), x1, ada3, row(ln2_g), row(ln2_b), seq, F32)


def kernel(x, c, w_ada, b_ada, w_in, mu_shift, w0, w_decay_up, a0, w_a_up, w_g_up, k_k, k_a, r_k, gn_g, gn_b,
           w_conv, b_conv, b_igate, b_fgate, w_branch_a, w_branch_b, w_out, ln1_g, ln1_b, w_ff1, b_ff1,
           w_ff2, b_ff2, ln2_g, ln2_b):
    out_dtype = x.dtype
    batch, seq, d = x.shape
    assert d == D_MODEL and seq % 1024 == 0 and w_ada.shape[0] == DEPTH
    x2 = x.astype(F32).reshape(batch * seq, d)
    cf = c.astype(F32)
    for l in range(DEPTH):
        x2 = _layer(x2, cf, batch, seq, w_ada[l], b_ada[l], w_in[l], mu_shift[l], w0[l], w_decay_up[l], a0[l],
                    w_a_up[l], w_g_up[l], k_k[l], k_a[l], r_k[l].reshape(-1), gn_g[l], gn_b[l], w_conv[l],
                    b_conv[l], b_igate[l], b_fgate[l], w_branch_a[l], w_branch_b[l], w_out[l], ln1_g[l],
                    ln1_b[l], w_ff1[l], b_ff1[l], w_ff2[l], b_ff2[l], ln2_g[l], ln2_b[l])
    return x2.reshape(batch, seq, d).astype(out_dtype)
```

```python
import functools

import jax
import jax.numpy as jnp
from jax import lax
from jax.experimental import pallas as pl
from jax.experimental.pallas import tpu as pltpu

F32 = jnp.float32
BF16 = jnp.bfloat16

D_MODEL = 2048
DEPTH = 1
D_RWKV = D_MODEL // 2
RWKV_HEAD = 64
DECAY_LORA = 96
A_LORA = 96
GATE_LORA = 256
RWKV_GN_EPS = 64e-5
D_MLSTM = D_MODEL // 2
MLSTM_HEADS = 4
MLSTM_DV = D_MLSTM // MLSTM_HEADS
MLSTM_DQK = MLSTM_DV // 2
D_QK = MLSTM_HEADS * MLSTM_DQK
CONV_K = 4
D_FF = 4 * D_MODEL
LN_EPS = 1e-5
ALPHA = (2.0 * DEPTH) ** 0.25

LANES = 128
LORA_PAD = 128
RW_COLS = 3 * D_RWKV + GATE_LORA + 2 * LORA_PAD
ML_GATE_PAD = 128
ML_COLS = 2 * D_QK + 2 * D_MLSTM + ML_GATE_PAD
RWKV_CHUNK = 64
RWKV_TB = 256
RWKV_CORE_TB = 512
MLSTM_CHUNK = 256
VMEM_LIMIT = 56 * 1024 * 1024

NN = (((1,), (0,)), ((), ()))
NT = (((1,), (1,)), ((), ()))
TN = (((0,), (0,)), ((), ()))


def _cparams(sem):
    return pltpu.CompilerParams(dimension_semantics=sem, vmem_limit_bytes=VMEM_LIMIT)


def _bf16_parts(x, n):
    if x.dtype == BF16:
        return [x]
    parts, rem = [], x
    for i in range(n):
        p = rem.astype(BF16)
        parts.append(p)
        if i + 1 < n:
            rem = rem - p.astype(F32)
    return parts


def _mdot(a, b, na=1, nb=1, dims=NN):
    ap, bp = _bf16_parts(a, na), _bf16_parts(b, nb)
    order = max(len(ap), len(bp))
    acc = None
    for i, x in enumerate(ap):
        for j, y in enumerate(bp):
            if i + j < order:
                t = lax.dot_general(x, y, dims, preferred_element_type=F32)
                acc = t if acc is None else acc + t
    return acc


def _sigmoid(x):
    return 1.0 / (1.0 + jnp.exp(-x))


def _softplus(x):
    return jnp.maximum(x, 0.0) + jnp.log(1.0 + jnp.exp(-jnp.abs(x)))


def _layernorm(x):
    mu = jnp.mean(x, axis=-1, keepdims=True)
    xc = x - mu
    var = jnp.mean(xc * xc, axis=-1, keepdims=True)
    return xc * lax.rsqrt(var + LN_EPS)


def _iota2(shape, dim):
    return lax.broadcasted_iota(jnp.int32, shape, dim)


def _head_block_ones(n, head):
    r, c = _iota2((n, n), 0), _iota2((n, n), 1)
    return jnp.where((r // head) == (c // head), 1.0, 0.0).astype(BF16)


def _head_sum(x, ones_blk):
    return _mdot(x, ones_blk, na=2, nb=1)


def _ada_kernel(ct_ref, w_ref, b_ref, o_ref):
    tn = w_ref.shape[1]

    def body(i, acc):
        a0, a1 = acc
        r0 = pl.multiple_of(i * 8, 8)
        wv = w_ref[pl.ds(r0, 8), :]
        cv = ct_ref[pl.ds(r0, 8), :]
        sv = cv * _sigmoid(cv)
        return a0 + sv[:, 0:1] * wv, a1 + sv[:, 1:2] * wv

    zero = jnp.zeros((8, tn), F32)
    a0, a1 = lax.fori_loop(0, w_ref.shape[0] // 8, body, (zero, zero), unroll=8)
    o_ref[0:1, :] = jnp.sum(a0, axis=0, keepdims=True) + b_ref[...]
    o_ref[1:2, :] = jnp.sum(a1, axis=0, keepdims=True) + b_ref[...]


def _ada(c, w_ada, b_ada):
    batch = c.shape[0]
    assert batch == 2
    n = w_ada.shape[1]
    tn = 1536
    return pl.pallas_call(
        _ada_kernel,
        out_shape=jax.ShapeDtypeStruct((batch, n), F32),
        grid=(n // tn,),
        in_specs=[pl.BlockSpec((D_MODEL, batch), lambda j: (0, 0)),
                  pl.BlockSpec((D_MODEL, tn), lambda j: (0, j)),
                  pl.BlockSpec((1, tn), lambda j: (0, j))],
        out_specs=pl.BlockSpec((batch, tn), lambda j: (0, j)),
        compiler_params=_cparams(("arbitrary",)),
        name="ada",
    )(c.T, w_ada, b_ada.reshape(1, n))


def _lnmod_kernel(x_ref, ada_ref, o_ref):
    xn = _layernorm(x_ref[...])
    sh, sc = ada_ref[0, 0:1, :], ada_ref[0, 1:2, :]
    o_ref[...] = (xn * (1.0 + sc) + sh).astype(o_ref.dtype)


def _lnmod(x2, ada3, seq):
    m = x2.shape[0]
    tb = 512
    per_b = seq // tb
    return pl.pallas_call(
        _lnmod_kernel,
        out_shape=jax.ShapeDtypeStruct((m, D_MODEL), BF16),
        grid=(m // tb,),
        in_specs=[pl.BlockSpec((tb, D_MODEL), lambda i: (i, 0)),
                  pl.BlockSpec((1, 6, D_MODEL), lambda i: (i // per_b, 0, 0))],
        out_specs=pl.BlockSpec((tb, D_MODEL), lambda i: (i, 0)),
        compiler_params=_cparams(("arbitrary",)),
        name="lnmod",
    )(x2, ada3)


def _mm_kernel(x_ref, w_ref, o_ref):
    o_ref[...] = jnp.dot(x_ref[...], w_ref[...], preferred_element_type=F32).astype(o_ref.dtype)


def _matmul(x, w, tm, tn, out_dtype, name):
    m, k = x.shape
    n = w.shape[1]
    return pl.pallas_call(
        _mm_kernel,
        out_shape=jax.ShapeDtypeStruct((m, n), out_dtype),
        grid=(n // tn, m // tm),
        in_specs=[pl.BlockSpec((tm, k), lambda j, i: (i, 0)),
                  pl.BlockSpec((k, tn), lambda j, i: (0, j))],
        out_specs=pl.BlockSpec((tm, tn), lambda j, i: (i, j)),
        compiler_params=_cparams(("arbitrary", "arbitrary")),
        name=name,
    )(x, w)


def _rwkv_prep_kernel(z_ref, mu_ref, w0_ref, a0_ref, kk_ref, ka_ref, rk_ref, wdu_ref, wau_ref, wgu_ref,
                      r_o, ld_o, kt_o, v_o, kkn_o, bb_o, g_o, bonus_o, carry_ref):
    tb = z_ref.shape[0]

    @pl.when(pl.program_id(1) == 0)
    def _():
        carry_ref[...] = jnp.zeros_like(carry_ref)

    z = z_ref[...]
    prev = pltpu.roll(z, 1, 0)
    prev = jnp.where(_iota2(z.shape, 0) == 0, carry_ref[0:1, :], prev)
    carry_ref[0:1, :] = z[tb - 1:tb, :]
    zs = z + (prev - z) * mu_ref[...]

    c = D_RWKV
    r, k, v = zs[:, 0:c], zs[:, c:2 * c], zs[:, 2 * c:3 * c]
    gd = zs[:, 3 * c:3 * c + GATE_LORA]
    wd = zs[:, 3 * c + GATE_LORA:3 * c + GATE_LORA + LORA_PAD]
    ad = zs[:, 3 * c + GATE_LORA + LORA_PAD:]

    logw = -_softplus(-(w0_ref[...] + _mdot(jnp.tanh(wd), wdu_ref[...], 2, 2))) - 0.5
    ld = -jnp.exp(logw)
    a = _sigmoid(a0_ref[...] + _mdot(ad, wau_ref[...], 2, 2))
    g = _mdot(_sigmoid(gd), wgu_ref[...], 1, 1)

    ones_blk = _head_block_ones(LANES, RWKV_HEAD)
    kq = k * kk_ref[...]
    kt = k * (1.0 + (a - 1.0) * ka_ref[...])
    sq = kq * kq
    bn = r * kt * rk_ref[...]
    ss = jnp.concatenate([_head_sum(sq[:, s:s + LANES], ones_blk) for s in range(0, c, LANES)], axis=1)
    bs = jnp.concatenate([_head_sum(bn[:, s:s + LANES], ones_blk) for s in range(0, c, LANES)], axis=1)
    kkn = kq * lax.rsqrt(ss + 1e-12)

    r_o[...] = r
    ld_o[...] = ld
    kt_o[...] = kt
    v_o[...] = v
    kkn_o[...] = kkn
    bb_o[...] = a * kkn
    g_o[...] = g
    bonus_o[...] = bs * v


def _rwkv_prep(z_rw, mu_rw, w0, a0, k_k, k_a, r_k, wdu, wau, wgu, batch, seq):
    m = z_rw.shape[0]
    tb = RWKV_TB
    per_b = seq // tb
    row = lambda b, j: (b * per_b + j, 0)
    const = lambda b, j: (0, 0)
    vec = pl.BlockSpec((1, D_RWKV), const)
    out = jax.ShapeDtypeStruct((m, D_RWKV), F32)
    return pl.pallas_call(
        _rwkv_prep_kernel,
        out_shape=[out] * 8,
        grid=(batch, per_b),
        in_specs=[pl.BlockSpec((tb, RW_COLS), row), pl.BlockSpec((1, RW_COLS), const),
                  vec, vec, vec, vec, vec,
                  pl.BlockSpec((LORA_PAD, D_RWKV), const), pl.BlockSpec((LORA_PAD, D_RWKV), const),
                  pl.BlockSpec((GATE_LORA, D_RWKV), const)],
        out_specs=[pl.BlockSpec((tb, D_RWKV), row)] * 8,
        scratch_shapes=[pltpu.VMEM((8, RW_COLS), F32)],
        compiler_params=_cparams(("arbitrary", "arbitrary")),
        name="rwkv_prep",
    )(z_rw, mu_rw, w0, a0, k_k, k_a, r_k, wdu, wau, wgu)


def _rwkv_chunk_kernel(r_ref, ld_ref, kt_ref, v_ref, kk_ref, bb_ref, y0_ref, q_ref, a_ref, c_ref):
    tb = r_ref.shape[0]
    lc = RWKV_CHUNK
    n = 2 * lc
    nchunk = tb // lc

    lane = _iota2((1, LANES), 1)
    m0 = jnp.where(lane < RWKV_HEAD, 1.0, 0.0)
    m1 = 1.0 - m0
    ri, ci = _iota2((n, n), 0), _iota2((n, n), 1)
    same_head = (ri // lc) == (ci // lc)
    strict = jnp.where(same_head & (ri > ci), 1.0, 0.0)
    incl = jnp.where(same_head & (ri >= ci), 1.0, 0.0)
    eye = jnp.where(ri == ci, 1.0, 0.0)

    rt, ct = _iota2((tb, tb), 0), _iota2((tb, tb), 1)
    tri = jnp.where(((rt // lc) == (ct // lc)) & (ct <= rt), 1.0, 0.0).astype(BF16)

    ld = ld_ref[...]
    cl = _mdot(tri, ld, 1, 3)
    e_pos = jnp.exp(cl)
    e_neg = jnp.exp(-cl)
    a_hat = kk_ref[...] * jnp.exp(cl - ld)
    b_hat = bb_ref[...] * e_neg
    k_hat = kt_ref[...] * e_neg
    r_hat = r_ref[...] * e_pos
    v_all = v_ref[...]

    def stack(x):
        return jnp.concatenate([x * m0, x * m1], axis=0)

    def bf(t):
        return t.astype(BF16)

    chunks = range(nchunk)
    st = [[stack(x[c * lc:(c + 1) * lc]) for x in (a_hat, b_hat, k_hat, r_hat, v_all)] for c in chunks]
    a_st, b_st, k_st, r_st, v_st = ([st[c][i] for c in chunks] for i in range(5))
    b_b, k_b, v_b = ([bf(t) for t in ts] for ts in (b_st, k_st, v_st))
    p = [_mdot(jnp.concatenate([a_st[c], r_st[c]], axis=0),
               jnp.concatenate([b_b[c], k_b[c]], axis=0), dims=NT) for c in chunks]
    mab = [p[c][0:n, 0:n] * strict for c in chunks]
    mab_b = [bf(t) for t in mab]
    mak_b = [bf(p[c][0:n, n:] * strict) for c in chunks]
    mrb_b = [bf(p[c][n:, 0:n] * incl) for c in chunks]
    mrk_b = [bf(p[c][n:, n:] * incl) for c in chunks]

    x = [eye - jnp.where((ri // 2 == ci // 2), mab[c], 0.0) for c in chunks]
    w1 = [_mdot(mak_b[c], v_b[c]) for c in chunks]
    zero_b = jnp.zeros((), BF16)
    s = 2
    while s < lc:
        lvl = (ri // (2 * s) == ci // (2 * s)) & (ri // s != ci // s)
        x_b = [bf(t) for t in x]
        nx = [_mdot(jnp.where(lvl, mab_b[c], zero_b), x_b[c]) for c in chunks]
        x = [x[c] - _mdot(x_b[c], nx[c]) for c in chunks]
        s *= 2
    res = [(eye - x[c]) - _mdot(mab[c], x[c], 2, 2) for c in chunks]
    x = [x[c] + _mdot(x[c], res[c], 2, 2) for c in chunks]

    ta = [_mdot(x[c], jnp.concatenate([w1[c], a_st[c]], axis=1)) for c in chunks]
    u0_b = [bf(-ta[c][:, 0:LANES]) for c in chunks]
    gm_b = [bf(ta[c][:, LANES:]) for c in chunks]
    y0 = [_mdot(mrb_b[c], u0_b[c]) + _mdot(mrk_b[c], v_b[c]) for c in chunks]
    q = [r_st[c] - _mdot(mrb_b[c], gm_b[c]) for c in chunks]
    pl_row = [e_pos[(c + 1) * lc - 1:(c + 1) * lc, :] for c in chunks]
    ac = [(eye - _mdot(gm_b[c], b_b[c], dims=TN)) * pl_row[c] for c in chunks]
    cc = [(_mdot(u0_b[c], b_b[c], dims=TN) + _mdot(v_b[c], k_b[c], dims=TN)) * pl_row[c] for c in chunks]

    for c in chunks:
        rows = slice(c * lc, (c + 1) * lc)
        y0_ref[rows, :] = y0[c][0:lc] + y0[c][lc:]
        q_ref[rows, :] = (q[c][0:lc] + q[c][lc:]).astype(q_ref.dtype)
        a_ref[0, c, 0] = ac[c].astype(a_ref.dtype)
        c_ref[0, c, 0] = cc[c]


def _rwkv_chunk(r, ld, kt, v, kkn, bb, batch, seq):
    m = r.shape[0]
    tb = RWKV_CORE_TB
    per_b = seq // tb
    cpb = tb // RWKV_CHUNK
    npair = D_RWKV // LANES
    slab = pl.BlockSpec((tb, LANES), lambda b, p, j: (b * per_b + j, p))
    mat = pl.BlockSpec((1, cpb, 1, LANES, LANES), lambda b, p, j: (b, j, p, 0, 0))
    mshape = (batch, seq // RWKV_CHUNK, npair, LANES, LANES)
    return pl.pallas_call(
        _rwkv_chunk_kernel,
        out_shape=[jax.ShapeDtypeStruct((m, D_RWKV), F32), jax.ShapeDtypeStruct((m, D_RWKV), BF16),
                   jax.ShapeDtypeStruct(mshape, BF16), jax.ShapeDtypeStruct(mshape, F32)],
        grid=(batch, npair, per_b),
        in_specs=[slab] * 6,
        out_specs=[slab, slab, mat, mat],
        compiler_params=_cparams(("arbitrary", "arbitrary", "arbitrary")),
        name="rwkv_chunk",
    )(r, ld, kt, v, kkn, bb)


def _rwkv_scan_kernel(y0_ref, q_ref, a_ref, c_ref, g_ref, bonus_ref, gng_ref, gnb_ref, ya_ref, s_ref):
    nb, rows, _ = y0_ref.shape
    lc = RWKV_CHUNK
    npair = D_RWKV // LANES

    @pl.when(pl.program_id(0) == 0)
    def _():
        s_ref[...] = jnp.zeros_like(s_ref)

    ones_blk = _head_block_ones(LANES, RWKV_HEAD)
    streams = [(b, p) for b in range(nb) for p in range(npair)]
    for ci in range(rows // lc):
        rs = slice(ci * lc, (ci + 1) * lc)
        state = [s_ref[b * npair + p] for b, p in streams]
        state_b = [t.astype(BF16) for t in state]
        y = [y0_ref[b, rs, p * LANES:(p + 1) * LANES]
             + _mdot(q_ref[b, rs, p * LANES:(p + 1) * LANES], state_b[i], dims=NT)
             for i, (b, p) in enumerate(streams)]
        for i, (b, p) in enumerate(streams):
            s_ref[b * npair + p] = _mdot(state_b[i], a_ref[b, ci, p]) + c_ref[b, ci, p]
        mu = [_head_sum(t, ones_blk) * (1.0 / RWKV_HEAD) for t in y]
        yc = [y[i] - mu[i] for i in range(len(streams))]
        var = [_head_sum(t * t, ones_blk) * (1.0 / RWKV_HEAD) for t in yc]
        for i, (b, p) in enumerate(streams):
            ls = slice(p * LANES, (p + 1) * LANES)
            yn = yc[i] * lax.rsqrt(var[i] + RWKV_GN_EPS) * gng_ref[:, ls] + gnb_ref[:, ls]
            ya_ref[b, rs, ls] = ((yn + bonus_ref[b, rs, ls]) * g_ref[b, rs, ls]).astype(ya_ref.dtype)


def _rwkv_scan(y0, q, a_mat, c_mat, g, bonus, gn_g, gn_b, batch, seq):
    rows = RWKV_CHUNK
    npair = D_RWKV // LANES
    tok = pl.BlockSpec((batch, rows, D_RWKV), lambda j: (0, j, 0))
    mat = pl.BlockSpec((batch, rows // RWKV_CHUNK, npair, LANES, LANES), lambda j: (0, j, 0, 0, 0))
    vec = pl.BlockSpec((1, D_RWKV), lambda j: (0, 0))
    shape3 = (batch, seq, D_RWKV)
    return pl.pallas_call(
        _rwkv_scan_kernel,
        out_shape=jax.ShapeDtypeStruct(shape3, BF16),
        grid=(seq // rows,),
        in_specs=[tok, tok, mat, mat, tok, tok, vec, vec],
        out_specs=tok,
        scratch_shapes=[pltpu.VMEM((batch * npair, LANES, LANES), F32)],
        compiler_params=_cparams(("arbitrary",)),
        name="rwkv_scan",
    )(y0.reshape(shape3), q.reshape(shape3), a_mat, c_mat, g.reshape(shape3), bonus.reshape(shape3),
      gn_g, gn_b).reshape(batch * seq, D_RWKV)


def _mlstm_kernel(z_ref, wconv_ref, bconv_ref, gbias_ref, yb_ref, tail_ref, ct_ref, n_ref, m_ref):
    lc = z_ref.shape[0]
    dqk, dv, nh = MLSTM_DQK, MLSTM_DV, MLSTM_HEADS

    @pl.when(pl.program_id(1) == 0)
    def _():
        tail_ref[...] = jnp.zeros_like(tail_ref)
        ct_ref[...] = jnp.zeros_like(ct_ref)
        n_ref[...] = jnp.zeros_like(n_ref)
        m_ref[...] = jnp.zeros_like(m_ref)

    zqk = z_ref[:, 0:2 * D_QK]
    ext = jnp.concatenate([tail_ref[...], zqk], axis=0)
    tail_ref[...] = zqk[lc - 8:lc, :]
    conv = bconv_ref[...] + wconv_ref[CONV_K - 1:CONV_K, :] * zqk
    for d in range(1, CONV_K):
        conv = conv + wconv_ref[CONV_K - 1 - d:CONV_K - d, :] * ext[8 - d:8 - d + lc, :]
    qk = conv * _sigmoid(conv)
    q_all = qk[:, 0:D_QK]
    k_all = qk[:, D_QK:] * (dqk ** -0.5)
    v_all = z_ref[:, 2 * D_QK:2 * D_QK + D_MLSTM]
    o_all = z_ref[:, 2 * D_QK + D_MLSTM:2 * D_QK + 2 * D_MLSTM]

    gz = z_ref[:, 2 * D_QK + 2 * D_MLSTM:] + gbias_ref[...]
    lf_col = jnp.minimum(gz, 0.0) - jnp.log(1.0 + jnp.exp(-jnp.abs(gz)))
    rr, cc = _iota2((lc, lc), 0), _iota2((lc, lc), 1)
    causal = cc <= rr
    tri = jnp.where(causal, 1.0, 0.0).astype(BF16)
    b_col = _mdot(tri, lf_col, 1, 3)
    sel = jnp.where(_iota2((8, LANES), 0) == _iota2((8, LANES), 1), 1.0, 0.0).astype(BF16)
    g_row = _mdot(sel, gz, 1, 3, NT)
    lf_row = jnp.minimum(g_row, 0.0) - jnp.log(1.0 + jnp.exp(-jnp.abs(g_row)))
    b_row = _mdot(lf_row, tri, 3, 1, NT)

    for h in range(nh):
        q = q_all[:, h * dqk:(h + 1) * dqk]
        k = k_all[:, h * dqk:(h + 1) * dqk]
        v = v_all[:, h * dv:(h + 1) * dv]
        bc = b_col[:, nh + h:nh + h + 1]
        ic = gz[:, h:h + 1]
        br = b_row[nh + h:nh + h + 1, :]
        ir = g_row[h:h + 1, :]
        g_tot = b_col[lc - 1:lc, nh + h:nh + h + 1]
        m_prev = m_ref[0:1, h:h + 1]

        dm = jnp.where(causal, bc - br + ir, -jnp.inf)
        inter = bc + m_prev
        mt = jnp.maximum(inter, jnp.max(dm, axis=-1, keepdims=True))
        sc = _mdot(q, k, 1, 1, NT) * jnp.exp(dm - mt)
        winter = jnp.exp(inter - mt)
        ct = ct_ref[h]
        nvec = n_ref[h:h + 1, :]
        num = winter * _mdot(q, ct, 1, 1) + _mdot(sc, v, 1, 1)
        den = winter * jnp.sum(q * nvec, axis=-1, keepdims=True) + jnp.sum(sc, axis=-1, keepdims=True)
        hout = num / jnp.maximum(jnp.abs(den), jnp.exp(-mt))

        m_new = jnp.maximum(g_tot + m_prev, jnp.max(g_tot - br + ir, axis=-1, keepdims=True))
        ws = jnp.exp(g_tot - bc + ic - m_new)
        wc = jnp.exp(g_tot + m_prev - m_new)
        kw = k * ws
        ct_ref[h] = wc * ct + _mdot(kw, v, 1, 1, TN)
        n_ref[h:h + 1, :] = wc * nvec + jnp.sum(kw, axis=0, keepdims=True)
        m_ref[0:1, h:h + 1] = m_new

        o = o_all[:, h * dv:(h + 1) * dv]
        yb_ref[:, h * dv:(h + 1) * dv] = (_sigmoid(o) * hout).astype(yb_ref.dtype)


def _mlstm(z_ml, w_conv, b_conv, gbias, batch, seq):
    m = z_ml.shape[0]
    lc = MLSTM_CHUNK
    per_b = seq // lc
    const = lambda b, j: (0, 0)
    return pl.pallas_call(
        _mlstm_kernel,
        out_shape=jax.ShapeDtypeStruct((m, D_MLSTM), BF16),
        grid=(batch, per_b),
        in_specs=[pl.BlockSpec((lc, ML_COLS), lambda b, j: (b * per_b + j, 0)),
                  pl.BlockSpec((CONV_K, 2 * D_QK), const),
                  pl.BlockSpec((1, 2 * D_QK), const),
                  pl.BlockSpec((1, ML_GATE_PAD), const)],
        out_specs=pl.BlockSpec((lc, D_MLSTM), lambda b, j: (b * per_b + j, 0)),
        scratch_shapes=[pltpu.VMEM((8, 2 * D_QK), F32),
                        pltpu.VMEM((MLSTM_HEADS, MLSTM_DQK, MLSTM_DV), F32),
                        pltpu.VMEM((8, MLSTM_DQK), F32),
                        pltpu.VMEM((8, LANES), F32)],
        compiler_params=_cparams(("arbitrary", "arbitrary")),
        name="mlstm",
    )(z_ml, w_conv, b_conv, gbias)


def _merge_kernel(ya_ref, yb_ref, zg_ref, wa_ref, wb_ref, o_ref):
    pa = jnp.dot(ya_ref[...], wa_ref[...], preferred_element_type=F32)
    pb = jnp.dot(yb_ref[...], wb_ref[...], preferred_element_type=F32)
    ga = _sigmoid(zg_ref[:, 0:D_MODEL])
    gb = _sigmoid(zg_ref[:, D_MODEL:])
    o_ref[...] = (ga * pa + gb * pb).astype(o_ref.dtype)


def _merge(ya, yb, zg, wa, wb):
    m = ya.shape[0]
    tm = 256
    const = lambda i: (0, 0)
    return pl.pallas_call(
        _merge_kernel,
        out_shape=jax.ShapeDtypeStruct((m, D_MODEL), BF16),
        grid=(m // tm,),
        in_specs=[pl.BlockSpec((tm, D_RWKV), lambda i: (i, 0)),
                  pl.BlockSpec((tm, D_MLSTM), lambda i: (i, 0)),
                  pl.BlockSpec((tm, 2 * D_MODEL), lambda i: (i, 0)),
                  pl.BlockSpec((D_RWKV, D_MODEL), const),
                  pl.BlockSpec((D_MLSTM, D_MODEL), const)],
        out_specs=pl.BlockSpec((tm, D_MODEL), lambda i: (i, 0)),
        compiler_params=_cparams(("arbitrary",)),
        name="merge",
    )(ya, yb, zg, wa, wb)


def _outproj_kernel(mg_ref, w_ref, x_ref, ada_ref, g_ref, b_ref, x1_ref, u2_ref):
    y = jnp.dot(mg_ref[...], w_ref[...], preferred_element_type=F32)
    g1 = ada_ref[0, 2:3, :]
    x1 = _layernorm(ALPHA * x_ref[...] + g1 * y) * g_ref[...] + b_ref[...]
    x1_ref[...] = x1
    sh2, sc2 = ada_ref[0, 3:4, :], ada_ref[0, 4:5, :]
    u2_ref[...] = (_layernorm(x1) * (1.0 + sc2) + sh2).astype(u2_ref.dtype)


def _outproj(merged, w_out, x2, ada3, ln_g, ln_b, seq):
    m = merged.shape[0]
    tm = 256
    per_b = seq // tm
    const = lambda i: (0, 0)
    row = pl.BlockSpec((tm, D_MODEL), lambda i: (i, 0))
    return pl.pallas_call(
        _outproj_kernel,
        out_shape=[jax.ShapeDtypeStruct((m, D_MODEL), F32), jax.ShapeDtypeStruct((m, D_MODEL), BF16)],
        grid=(m // tm,),
        in_specs=[row, pl.BlockSpec((D_MODEL, D_MODEL), const), row,
                  pl.BlockSpec((1, 6, D_MODEL), lambda i: (i // per_b, 0, 0)),
                  pl.BlockSpec((1, D_MODEL), const), pl.BlockSpec((1, D_MODEL), const)],
        out_specs=[row, row],
        compiler_params=_cparams(("arbitrary",)),
        name="outproj",
    )(merged, w_out, x2, ada3, ln_g, ln_b)


def _ff1_kernel(x_ref, w_ref, b_ref, o_ref):
    h = jnp.dot(x_ref[...], w_ref[...], preferred_element_type=F32) + b_ref[...]
    h = jnp.maximum(h, 0.0)
    o_ref[...] = (h * h).astype(o_ref.dtype)


def _ff1(u2, w1, b1):
    m, k = u2.shape
    n = w1.shape[1]
    tm, tn = 1024, 1024
    return pl.pallas_call(
        _ff1_kernel,
        out_shape=jax.ShapeDtypeStruct((m, n), BF16),
        grid=(n // tn, m // tm),
        in_specs=[pl.BlockSpec((tm, k), lambda j, i: (i, 0)),
                  pl.BlockSpec((k, tn), lambda j, i: (0, j)),
                  pl.BlockSpec((1, tn), lambda j, i: (0, j))],
        out_specs=pl.BlockSpec((tm, tn), lambda j, i: (i, j)),
        compiler_params=_cparams(("arbitrary", "arbitrary")),
        name="ff1",
    )(u2, w1, b1)


def _ff2_kernel(h_ref, w_ref, b_ref, x1_ref, ada_ref, g_ref, bb_ref, o_ref, acc_ref):
    kk = pl.program_id(1)

    @pl.when(kk == 0)
    def _():
        acc_ref[...] = jnp.zeros_like(acc_ref)

    acc_ref[...] += jnp.dot(h_ref[...], w_ref[...], preferred_element_type=F32)

    @pl.when(kk == pl.num_programs(1) - 1)
    def _():
        y2 = acc_ref[...] + b_ref[...]
        g2 = ada_ref[0, 5:6, :]
        o_ref[...] = (_layernorm(ALPHA * x1_ref[...] + g2 * y2) * g_ref[...] + bb_ref[...]).astype(o_ref.dtype)


def _ff2(h, w2, b2, x1, ada3, ln_g, ln_b, seq, out_dtype):
    m, k = h.shape
    tm, tk = 512, 1024
    per_b = seq // tm
    const = lambda i, kk: (0, 0)
    row = pl.BlockSpec((tm, D_MODEL), lambda i, kk: (i, 0))
    return pl.pallas_call(
        _ff2_kernel,
        out_shape=jax.ShapeDtypeStruct((m, D_MODEL), out_dtype),
        grid=(m // tm, k // tk),
        in_specs=[pl.BlockSpec((tm, tk), lambda i, kk: (i, kk)),
                  pl.BlockSpec((tk, D_MODEL), lambda i, kk: (kk, 0)),
                  pl.BlockSpec((1, D_MODEL), const), row,
                  pl.BlockSpec((1, 6, D_MODEL), lambda i, kk: (i // per_b, 0, 0)),
                  pl.BlockSpec((1, D_MODEL), const), pl.BlockSpec((1, D_MODEL), const)],
        out_specs=row,
        scratch_shapes=[pltpu.VMEM((tm, D_MODEL), F32)],
        compiler_params=_cparams(("arbitrary", "arbitrary")),
        name="ff2",
    )(h, w2, b2, x1, ada3, ln_g, ln_b)


def _pad_cols(w, width):
    return jnp.pad(w, ((0, 0), (0, width - w.shape[1])))


def _pad_rows(w, height):
    return jnp.pad(w, ((0, height - w.shape[0]), (0, 0)))


def _split_w_in(w):
    c = D_RWKV
    o = 0
    r_k_v = w[:, o:o + 3 * c]; o += 3 * c
    wd = w[:, o:o + DECAY_LORA]; o += DECAY_LORA
    ad = w[:, o:o + A_LORA]; o += A_LORA
    gd = w[:, o:o + GATE_LORA]; o += GATE_LORA
    q_k = w[:, o:o + 2 * D_QK]; o += 2 * D_QK
    mv = w[:, o:o + D_MLSTM]; o += D_MLSTM
    i_f = w[:, o:o + 2 * MLSTM_HEADS]; o += 2 * MLSTM_HEADS
    mo = w[:, o:o + D_MLSTM]; o += D_MLSTM
    gates = w[:, o:]
    w_rw = jnp.concatenate([r_k_v, gd, _pad_cols(wd, LORA_PAD), _pad_cols(ad, LORA_PAD)], axis=1)
    w_ml = jnp.concatenate([q_k, mv, mo, _pad_cols(i_f, ML_GATE_PAD)], axis=1)
    return w_rw, w_ml, gates


def _layer(x2, c, batch, seq, w_ada, b_ada, w_in, mu_shift, w0, w_decay_up, a0, w_a_up, w_g_up, k_k, k_a, r_k,
           gn_g, gn_b, w_conv, b_conv, b_igate, b_fgate, w_branch_a, w_branch_b, w_out,
           ln1_g, ln1_b, w_ff1, b_ff1, w_ff2, b_ff2, ln2_g, ln2_b):
    row = lambda v: v.reshape(1, -1)

    ada3 = _ada(c, w_ada, b_ada).reshape(batch, 6, D_MODEL)
    u = _lnmod(x2, ada3, seq)

    w_rw, w_ml, w_gt = _split_w_in(w_in)
    z_rw = _matmul(u, w_rw.astype(BF16), 1024, 512, F32, "win_rwkv")
    z_ml = _matmul(u, w_ml.astype(BF16), 1024, 640, F32, "win_mlstm")
    z_gt = _matmul(u, w_gt.astype(BF16), 1024, 1024, F32, "win_gate")

    cc = D_RWKV
    mu = mu_shift
    mu_rw = jnp.concatenate([mu[:3 * cc], mu[3 * cc + DECAY_LORA + A_LORA:],
                             jnp.pad(mu[3 * cc:3 * cc + DECAY_LORA], (0, LORA_PAD - DECAY_LORA)),
                             jnp.pad(mu[3 * cc + DECAY_LORA:3 * cc + DECAY_LORA + A_LORA], (0, LORA_PAD - A_LORA))])
    prep = _rwkv_prep(z_rw, row(mu_rw), row(w0), row(a0), row(k_k), row(k_a), row(r_k),
                      _pad_rows(w_decay_up, LORA_PAD), _pad_rows(w_a_up, LORA_PAD), w_g_up, batch, seq)
    r, ld, kt, v, kkn, bb, g, bonus = prep
    y0, q, a_mat, c_mat = _rwkv_chunk(r, ld, kt, v, kkn, bb, batch, seq)
    ya = _rwkv_scan(y0, q, a_mat, c_mat, g, bonus, row(gn_g), row(gn_b), batch, seq)

    gbias = jnp.pad(jnp.concatenate([b_igate, b_fgate]), (0, ML_GATE_PAD - 2 * MLSTM_HEADS))
    yb = _mlstm(z_ml, w_conv, row(b_conv), row(gbias), batch, seq)

    merged = _merge(ya, yb, z_gt, w_branch_a.astype(BF16), w_branch_b.astype(BF16))
    x1, u2 = _outproj(merged, w_out.astype(BF16), x2, ada3, row(ln1_g), row(ln1_b), seq)

    h = _ff1(u2, w_ff1.astype(BF16), row(b_ff1))
    return _ff2(h, w_ff2.astype(BF16), row(b_ff2), x1, ada3, row(ln2_g), row(ln2_b), seq, F32)


def kernel(x, c, w_ada, b_ada, w_in, mu_shift, w0, w_decay_up, a0, w_a_up, w_g_up, k_k, k_a, r_k, gn_g, gn_b,
           w_conv, b_conv, b_igate, b_fgate, w_branch_a, w_branch_b, w_out, ln1_g, ln1_b, w_ff1, b_ff1,
           w_ff2, b_ff2, ln2_g, ln2_b):
    out_dtype = x.dtype
    batch, seq, d = x.shape
    assert d == D_MODEL and seq % 1024 == 0 and w_ada.shape[0] == DEPTH
    x2 = x.astype(F32).reshape(batch * seq, d)
    cf = c.astype(F32)
    for l in range(DEPTH):
        x2 = _layer(x2, cf, batch, seq, w_ada[l], b_ada[l], w_in[l], mu_shift[l], w0[l], w_decay_up[l], a0[l],
                    w_a_up[l], w_g_up[l], k_k[l], k_a[l], r_k[l].reshape(-1), gn_g[l], gn_b[l], w_conv[l],
                    b_conv[l], b_igate[l], b_fgate[l], w_branch_a[l], w_branch_b[l], w_out[l], ln1_g[l],
                    ln1_b[l], w_ff1[l], b_ff1[l], w_ff2[l], b_ff2[l], ln2_g[l], ln2_b[l])
    return x2.reshape(batch, seq, d).astype(out_dtype)
```

```python
import functools

import jax
import jax.numpy as jnp
from jax import lax
from jax.experimental import pallas as pl
from jax.experimental.pallas import tpu as pltpu

F32 = jnp.float32
BF16 = jnp.bfloat16

D_MODEL = 2048
DEPTH = 1
D_RWKV = D_MODEL // 2
RWKV_HEAD = 64
DECAY_LORA = 96
A_LORA = 96
GATE_LORA = 256
RWKV_GN_EPS = 64e-5
D_MLSTM = D_MODEL // 2
MLSTM_HEADS = 4
MLSTM_DV = D_MLSTM // MLSTM_HEADS
MLSTM_DQK = MLSTM_DV // 2
D_QK = MLSTM_HEADS * MLSTM_DQK
CONV_K = 4
D_FF = 4 * D_MODEL
LN_EPS = 1e-5
ALPHA = (2.0 * DEPTH) ** 0.25

LANES = 128
LORA_PAD = 128
RW_COLS = 3 * D_RWKV + GATE_LORA + 2 * LORA_PAD
ML_GATE_PAD = 128
ML_COLS = 2 * D_QK + 2 * D_MLSTM + ML_GATE_PAD
RWKV_CHUNK = 64
RWKV_TB = 256
RWKV_CORE_TB = 1024
MLSTM_CHUNK = 256
VMEM_LIMIT = 56 * 1024 * 1024

NN = (((1,), (0,)), ((), ()))
NT = (((1,), (1,)), ((), ()))
TN = (((0,), (0,)), ((), ()))


def _cparams(sem):
    return pltpu.CompilerParams(dimension_semantics=sem, vmem_limit_bytes=VMEM_LIMIT)


def _bf16_parts(x, n):
    if x.dtype == BF16:
        return [x]
    parts, rem = [], x
    for i in range(n):
        p = rem.astype(BF16)
        parts.append(p)
        if i + 1 < n:
            rem = rem - p.astype(F32)
    return parts


def _mdot(a, b, na=1, nb=1, dims=NN):
    ap, bp = _bf16_parts(a, na), _bf16_parts(b, nb)
    order = max(len(ap), len(bp))
    acc = None
    for i, x in enumerate(ap):
        for j, y in enumerate(bp):
            if i + j < order:
                t = lax.dot_general(x, y, dims, preferred_element_type=F32)
                acc = t if acc is None else acc + t
    return acc


def _sigmoid(x):
    return 1.0 / (1.0 + jnp.exp(-x))


def _softplus(x):
    return jnp.maximum(x, 0.0) + jnp.log(1.0 + jnp.exp(-jnp.abs(x)))


def _layernorm(x):
    mu = jnp.mean(x, axis=-1, keepdims=True)
    xc = x - mu
    var = jnp.mean(xc * xc, axis=-1, keepdims=True)
    return xc * lax.rsqrt(var + LN_EPS)


def _iota2(shape, dim):
    return lax.broadcasted_iota(jnp.int32, shape, dim)


def _head_block_ones(n, head):
    r, c = _iota2((n, n), 0), _iota2((n, n), 1)
    return jnp.where((r // head) == (c // head), 1.0, 0.0).astype(BF16)


def _head_sum(x, ones_blk):
    return _mdot(x, ones_blk, na=2, nb=1)


def _ada_kernel(ct_ref, w_ref, b_ref, o_ref):
    tn = w_ref.shape[1]

    def body(i, acc):
        a0, a1 = acc
        r0 = pl.multiple_of(i * 8, 8)
        wv = w_ref[pl.ds(r0, 8), :]
        cv = ct_ref[pl.ds(r0, 8), :]
        sv = cv * _sigmoid(cv)
        return a0 + sv[:, 0:1] * wv, a1 + sv[:, 1:2] * wv

    zero = jnp.zeros((8, tn), F32)
    a0, a1 = lax.fori_loop(0, w_ref.shape[0] // 8, body, (zero, zero), unroll=8)
    o_ref[0:1, :] = jnp.sum(a0, axis=0, keepdims=True) + b_ref[...]
    o_ref[1:2, :] = jnp.sum(a1, axis=0, keepdims=True) + b_ref[...]


def _ada(c, w_ada, b_ada):
    batch = c.shape[0]
    assert batch == 2
    n = w_ada.shape[1]
    tn = 1536
    return pl.pallas_call(
        _ada_kernel,
        out_shape=jax.ShapeDtypeStruct((batch, n), F32),
        grid=(n // tn,),
        in_specs=[pl.BlockSpec((D_MODEL, batch), lambda j: (0, 0)),
                  pl.BlockSpec((D_MODEL, tn), lambda j: (0, j)),
                  pl.BlockSpec((1, tn), lambda j: (0, j))],
        out_specs=pl.BlockSpec((batch, tn), lambda j: (0, j)),
        compiler_params=_cparams(("arbitrary",)),
        name="ada",
    )(c.T, w_ada, b_ada.reshape(1, n))


def _lnmod_kernel(x_ref, ada_ref, o_ref):
    xn = _layernorm(x_ref[...])
    sh, sc = ada_ref[0, 0:1, :], ada_ref[0, 1:2, :]
    o_ref[...] = (xn * (1.0 + sc) + sh).astype(o_ref.dtype)


def _lnmod(x2, ada3, seq):
    m = x2.shape[0]
    tb = 512
    per_b = seq // tb
    return pl.pallas_call(
        _lnmod_kernel,
        out_shape=jax.ShapeDtypeStruct((m, D_MODEL), BF16),
        grid=(m // tb,),
        in_specs=[pl.BlockSpec((tb, D_MODEL), lambda i: (i, 0)),
                  pl.BlockSpec((1, 6, D_MODEL), lambda i: (i // per_b, 0, 0))],
        out_specs=pl.BlockSpec((tb, D_MODEL), lambda i: (i, 0)),
        compiler_params=_cparams(("arbitrary",)),
        name="lnmod",
    )(x2, ada3)


def _mm_kernel(x_ref, w_ref, o_ref):
    o_ref[...] = jnp.dot(x_ref[...], w_ref[...], preferred_element_type=F32).astype(o_ref.dtype)


def _matmul(x, w, tm, tn, out_dtype, name):
    m, k = x.shape
    n = w.shape[1]
    return pl.pallas_call(
        _mm_kernel,
        out_shape=jax.ShapeDtypeStruct((m, n), out_dtype),
        grid=(n // tn, m // tm),
        in_specs=[pl.BlockSpec((tm, k), lambda j, i: (i, 0)),
                  pl.BlockSpec((k, tn), lambda j, i: (0, j))],
        out_specs=pl.BlockSpec((tm, tn), lambda j, i: (i, j)),
        compiler_params=_cparams(("arbitrary", "arbitrary")),
        name=name,
    )(x, w)


def _rwkv_prep_kernel(z_ref, mu_ref, w0_ref, a0_ref, kk_ref, ka_ref, rk_ref, wdu_ref, wau_ref, wgu_ref,
                      r_o, ld_o, kt_o, v_o, kkn_o, bb_o, g_o, bonus_o, carry_ref):
    tb = z_ref.shape[0]

    @pl.when(pl.program_id(1) == 0)
    def _():
        carry_ref[...] = jnp.zeros_like(carry_ref)

    z = z_ref[...]
    prev = pltpu.roll(z, 1, 0)
    prev = jnp.where(_iota2(z.shape, 0) == 0, carry_ref[0:1, :], prev)
    carry_ref[0:1, :] = z[tb - 1:tb, :]
    zs = z + (prev - z) * mu_ref[...]

    c = D_RWKV
    r, k, v = zs[:, 0:c], zs[:, c:2 * c], zs[:, 2 * c:3 * c]
    gd = zs[:, 3 * c:3 * c + GATE_LORA]
    wd = zs[:, 3 * c + GATE_LORA:3 * c + GATE_LORA + LORA_PAD]
    ad = zs[:, 3 * c + GATE_LORA + LORA_PAD:]

    logw = -_softplus(-(w0_ref[...] + _mdot(jnp.tanh(wd), wdu_ref[...], 2, 2))) - 0.5
    ld = -jnp.exp(logw)
    a = _sigmoid(a0_ref[...] + _mdot(ad, wau_ref[...], 2, 2))
    g = _mdot(_sigmoid(gd), wgu_ref[...], 1, 1)

    ones_blk = _head_block_ones(LANES, RWKV_HEAD)
    kq = k * kk_ref[...]
    kt = k * (1.0 + (a - 1.0) * ka_ref[...])
    sq = kq * kq
    bn = r * kt * rk_ref[...]
    ss = jnp.concatenate([_head_sum(sq[:, s:s + LANES], ones_blk) for s in range(0, c, LANES)], axis=1)
    bs = jnp.concatenate([_head_sum(bn[:, s:s + LANES], ones_blk) for s in range(0, c, LANES)], axis=1)
    kkn = kq * lax.rsqrt(ss + 1e-12)

    r_o[...] = r
    ld_o[...] = ld
    kt_o[...] = kt
    v_o[...] = v
    kkn_o[...] = kkn
    bb_o[...] = a * kkn
    g_o[...] = g
    bonus_o[...] = bs * v


def _rwkv_prep(z_rw, mu_rw, w0, a0, k_k, k_a, r_k, wdu, wau, wgu, batch, seq):
    m = z_rw.shape[0]
    tb = RWKV_TB
    per_b = seq // tb
    row = lambda b, j: (b * per_b + j, 0)
    const = lambda b, j: (0, 0)
    vec = pl.BlockSpec((1, D_RWKV), const)
    out = jax.ShapeDtypeStruct((m, D_RWKV), F32)
    return pl.pallas_call(
        _rwkv_prep_kernel,
        out_shape=[out] * 8,
        grid=(batch, per_b),
        in_specs=[pl.BlockSpec((tb, RW_COLS), row), pl.BlockSpec((1, RW_COLS), const),
                  vec, vec, vec, vec, vec,
                  pl.BlockSpec((LORA_PAD, D_RWKV), const), pl.BlockSpec((LORA_PAD, D_RWKV), const),
                  pl.BlockSpec((GATE_LORA, D_RWKV), const)],
        out_specs=[pl.BlockSpec((tb, D_RWKV), row)] * 8,
        scratch_shapes=[pltpu.VMEM((8, RW_COLS), F32)],
        compiler_params=_cparams(("arbitrary", "arbitrary")),
        name="rwkv_prep",
    )(z_rw, mu_rw, w0, a0, k_k, k_a, r_k, wdu, wau, wgu)


def _rwkv_chunk_kernel(r_ref, ld_ref, kt_ref, v_ref, kk_ref, bb_ref, y0_ref, q_ref, a_ref, c_ref):
    tb = r_ref.shape[0]
    lc = RWKV_CHUNK
    n = 2 * lc
    nchunk = tb // lc

    lane = _iota2((1, LANES), 1)
    m0 = jnp.where(lane < RWKV_HEAD, 1.0, 0.0)
    m1 = 1.0 - m0
    ri, ci = _iota2((n, n), 0), _iota2((n, n), 1)
    same_head = (ri // lc) == (ci // lc)
    strict = jnp.where(same_head & (ri > ci), 1.0, 0.0)
    incl = jnp.where(same_head & (ri >= ci), 1.0, 0.0)
    eye = jnp.where(ri == ci, 1.0, 0.0)

    tw = 4 * lc
    rt, ct = _iota2((tw, tw), 0), _iota2((tw, tw), 1)
    tri = jnp.where(((rt // lc) == (ct // lc)) & (ct <= rt), 1.0, 0.0).astype(BF16)

    ld = ld_ref[...]
    cl = jnp.concatenate([_mdot(tri, ld[i:i + tw], 1, 3) for i in range(0, tb, tw)], axis=0)
    e_pos = jnp.exp(cl)
    e_neg = jnp.exp(-cl)
    a_hat = kk_ref[...] * jnp.exp(cl - ld)
    b_hat = bb_ref[...] * e_neg
    k_hat = kt_ref[...] * e_neg
    r_hat = r_ref[...] * e_pos
    v_all = v_ref[...]

    def stack(x):
        return jnp.concatenate([x * m0, x * m1], axis=0)

    def bf(t):
        return t.astype(BF16)

    chunks = range(nchunk)
    st = [[stack(x[c * lc:(c + 1) * lc]) for x in (a_hat, b_hat, k_hat, r_hat, v_all)] for c in chunks]
    a_st, b_st, k_st, r_st, v_st = ([st[c][i] for c in chunks] for i in range(5))
    b_b, k_b, v_b = ([bf(t) for t in ts] for ts in (b_st, k_st, v_st))
    p = [_mdot(jnp.concatenate([a_st[c], r_st[c]], axis=0),
               jnp.concatenate([b_b[c], k_b[c]], axis=0), dims=NT) for c in chunks]
    mab = [p[c][0:n, 0:n] * strict for c in chunks]
    mab_b = [bf(t) for t in mab]
    mak_b = [bf(p[c][0:n, n:] * strict) for c in chunks]
    mrb_b = [bf(p[c][n:, 0:n] * incl) for c in chunks]
    mrk_b = [bf(p[c][n:, n:] * incl) for c in chunks]

    x = [eye - jnp.where((ri // 2 == ci // 2), mab[c], 0.0) for c in chunks]
    w1 = [_mdot(mak_b[c], v_b[c]) for c in chunks]
    zero_b = jnp.zeros((), BF16)
    s = 2
    while s < lc:
        lvl = (ri // (2 * s) == ci // (2 * s)) & (ri // s != ci // s)
        x_b = [bf(t) for t in x]
        nx = [_mdot(jnp.where(lvl, mab_b[c], zero_b), x_b[c]) for c in chunks]
        x = [x[c] - _mdot(x_b[c], nx[c]) for c in chunks]
        s *= 2
    res = [(eye - x[c]) - _mdot(mab[c], x[c], 2, 2) for c in chunks]
    x = [x[c] + _mdot(x[c], res[c], 2, 2) for c in chunks]

    ta = [_mdot(x[c], jnp.concatenate([w1[c], a_st[c]], axis=1)) for c in chunks]
    ug_b = [bf(jnp.concatenate([-ta[c][:, 0:LANES], ta[c][:, LANES:]], axis=1)) for c in chunks]
    ru = [_mdot(mrb_b[c], ug_b[c]) for c in chunks]
    y0 = [ru[c][:, 0:LANES] + _mdot(mrk_b[c], v_b[c]) for c in chunks]
    q = [r_st[c] - ru[c][:, LANES:] for c in chunks]
    pl_row = [e_pos[(c + 1) * lc - 1:(c + 1) * lc, :] for c in chunks]
    bu = [_mdot(bf(b_st[c] * pl_row[c]), ug_b[c], dims=TN) for c in chunks]
    at = [eye * pl_row[c] - bu[c][:, LANES:] for c in chunks]
    ct = [bu[c][:, 0:LANES] + _mdot(bf(k_st[c] * pl_row[c]), v_b[c], dims=TN) for c in chunks]

    for c in chunks:
        rows = slice(c * lc, (c + 1) * lc)
        y0_ref[rows, :] = y0[c][0:lc] + y0[c][lc:]
        q_ref[rows, :] = (q[c][0:lc] + q[c][lc:]).astype(q_ref.dtype)
        a_ref[0, c, 0] = at[c].astype(a_ref.dtype)
        c_ref[0, c, 0] = ct[c]


def _rwkv_chunk(r, ld, kt, v, kkn, bb, batch, seq):
    m = r.shape[0]
    tb = RWKV_CORE_TB
    per_b = seq // tb
    cpb = tb // RWKV_CHUNK
    npair = D_RWKV // LANES
    slab = pl.BlockSpec((tb, LANES), lambda b, p, j: (b * per_b + j, p))
    mat = pl.BlockSpec((1, cpb, 1, LANES, LANES), lambda b, p, j: (b, j, p, 0, 0))
    mshape = (batch, seq // RWKV_CHUNK, npair, LANES, LANES)
    return pl.pallas_call(
        _rwkv_chunk_kernel,
        out_shape=[jax.ShapeDtypeStruct((m, D_RWKV), F32), jax.ShapeDtypeStruct((m, D_RWKV), BF16),
                   jax.ShapeDtypeStruct(mshape, BF16), jax.ShapeDtypeStruct(mshape, F32)],
        grid=(batch, npair, per_b),
        in_specs=[slab] * 6,
        out_specs=[slab, slab, mat, mat],
        compiler_params=_cparams(("arbitrary", "arbitrary", "arbitrary")),
        name="rwkv_chunk",
    )(r, ld, kt, v, kkn, bb)


def _rwkv_scan_kernel(y0_ref, q_ref, a_ref, c_ref, g_ref, bonus_ref, gng_ref, gnb_ref, ya_ref, s_ref):
    nb, rows, _ = y0_ref.shape
    lc = RWKV_CHUNK
    npair = D_RWKV // LANES

    @pl.when(pl.program_id(0) == 0)
    def _():
        s_ref[...] = jnp.zeros_like(s_ref)

    ones_blk = _head_block_ones(LANES, RWKV_HEAD)
    streams = [(b, p) for b in range(nb) for p in range(npair)]
    for ci in range(rows // lc):
        rs = slice(ci * lc, (ci + 1) * lc)
        state_b = [s_ref[b * npair + p].astype(BF16) for b, p in streams]
        y = [y0_ref[b, rs, p * LANES:(p + 1) * LANES]
             + _mdot(q_ref[b, rs, p * LANES:(p + 1) * LANES], state_b[i])
             for i, (b, p) in enumerate(streams)]
        for i, (b, p) in enumerate(streams):
            s_ref[b * npair + p] = _mdot(a_ref[b, ci, p], state_b[i]) + c_ref[b, ci, p]
        mu = [_head_sum(t, ones_blk) * (1.0 / RWKV_HEAD) for t in y]
        yc = [y[i] - mu[i] for i in range(len(streams))]
        var = [_head_sum(t * t, ones_blk) * (1.0 / RWKV_HEAD) for t in yc]
        for i, (b, p) in enumerate(streams):
            ls = slice(p * LANES, (p + 1) * LANES)
            yn = yc[i] * lax.rsqrt(var[i] + RWKV_GN_EPS) * gng_ref[:, ls] + gnb_ref[:, ls]
            ya_ref[b, rs, ls] = ((yn + bonus_ref[b, rs, ls]) * g_ref[b, rs, ls]).astype(ya_ref.dtype)


def _rwkv_scan(y0, q, a_mat, c_mat, g, bonus, gn_g, gn_b, batch, seq):
    rows = RWKV_CHUNK
    npair = D_RWKV // LANES
    tok = pl.BlockSpec((batch, rows, D_RWKV), lambda j: (0, j, 0))
    mat = pl.BlockSpec((batch, rows // RWKV_CHUNK, npair, LANES, LANES), lambda j: (0, j, 0, 0, 0))
    vec = pl.BlockSpec((1, D_RWKV), lambda j: (0, 0))
    shape3 = (batch, seq, D_RWKV)
    return pl.pallas_call(
        _rwkv_scan_kernel,
        out_shape=jax.ShapeDtypeStruct(shape3, BF16),
        grid=(seq // rows,),
        in_specs=[tok, tok, mat, mat, tok, tok, vec, vec],
        out_specs=tok,
        scratch_shapes=[pltpu.VMEM((batch * npair, LANES, LANES), F32)],
        compiler_params=_cparams(("arbitrary",)),
        name="rwkv_scan",
    )(y0.reshape(shape3), q.reshape(shape3), a_mat, c_mat, g.reshape(shape3), bonus.reshape(shape3),
      gn_g, gn_b).reshape(batch * seq, D_RWKV)


def _mlstm_kernel(z_ref, wconv_ref, bconv_ref, gbias_ref, yb_ref, tail_ref, ct_ref, n_ref, m_ref):
    lc = z_ref.shape[0]
    dqk, dv, nh = MLSTM_DQK, MLSTM_DV, MLSTM_HEADS

    @pl.when(pl.program_id(1) == 0)
    def _():
        tail_ref[...] = jnp.zeros_like(tail_ref)
        ct_ref[...] = jnp.zeros_like(ct_ref)
        n_ref[...] = jnp.zeros_like(n_ref)
        m_ref[...] = jnp.zeros_like(m_ref)

    zqk = z_ref[:, 0:2 * D_QK]
    ext = jnp.concatenate([tail_ref[...], zqk], axis=0)
    tail_ref[...] = zqk[lc - 8:lc, :]
    conv = bconv_ref[...] + wconv_ref[CONV_K - 1:CONV_K, :] * zqk
    for d in range(1, CONV_K):
        conv = conv + wconv_ref[CONV_K - 1 - d:CONV_K - d, :] * ext[8 - d:8 - d + lc, :]
    qk = conv * _sigmoid(conv)
    q_all = qk[:, 0:D_QK]
    k_all = qk[:, D_QK:] * (dqk ** -0.5)
    v_all = z_ref[:, 2 * D_QK:2 * D_QK + D_MLSTM]
    o_all = z_ref[:, 2 * D_QK + D_MLSTM:2 * D_QK + 2 * D_MLSTM]

    gz = z_ref[:, 2 * D_QK + 2 * D_MLSTM:] + gbias_ref[...]
    lf_col = jnp.minimum(gz, 0.0) - jnp.log(1.0 + jnp.exp(-jnp.abs(gz)))
    rr, cc = _iota2((lc, lc), 0), _iota2((lc, lc), 1)
    causal = cc <= rr
    tri = jnp.where(causal, 1.0, 0.0).astype(BF16)
    b_col = _mdot(tri, lf_col, 1, 3)
    sel = jnp.where(_iota2((8, LANES), 0) == _iota2((8, LANES), 1), 1.0, 0.0).astype(BF16)
    g_row = _mdot(sel, gz, 1, 3, NT)
    lf_row = jnp.minimum(g_row, 0.0) - jnp.log(1.0 + jnp.exp(-jnp.abs(g_row)))
    b_row = _mdot(lf_row, tri, 3, 1, NT)

    heads = range(nh)
    qf = [q_all[:, h * dqk:(h + 1) * dqk] for h in heads]
    q = [t.astype(BF16) for t in qf]
    k = [k_all[:, h * dqk:(h + 1) * dqk] for h in heads]
    v = [v_all[:, h * dv:(h + 1) * dv].astype(BF16) for h in heads]
    bc = [b_col[:, nh + h:nh + h + 1] for h in heads]
    ic = [gz[:, h:h + 1] for h in heads]
    br = [b_row[nh + h:nh + h + 1, :] for h in heads]
    ir = [g_row[h:h + 1, :] for h in heads]
    g_tot = [b_col[lc - 1:lc, nh + h:nh + h + 1] for h in heads]
    m_prev = [m_ref[0:1, h:h + 1] for h in heads]
    ct = [ct_ref[h] for h in heads]
    nvec = [n_ref[h:h + 1, :] for h in heads]

    qk = [_mdot(q[h], k[h], dims=NT) for h in heads]
    qc = [_mdot(q[h], ct[h]) for h in heads]
    dm = [jnp.where(causal, bc[h] - br[h] + ir[h], -jnp.inf) for h in heads]
    inter = [bc[h] + m_prev[h] for h in heads]
    mt = [jnp.maximum(inter[h], jnp.max(dm[h], axis=-1, keepdims=True)) for h in heads]
    sc = [qk[h] * jnp.exp(dm[h] - mt[h]) for h in heads]
    winter = [jnp.exp(inter[h] - mt[h]) for h in heads]
    sv = [_mdot(sc[h], v[h]) for h in heads]
    m_new = [jnp.maximum(g_tot[h] + m_prev[h], jnp.max(g_tot[h] - br[h] + ir[h], axis=-1, keepdims=True))
             for h in heads]
    kw = [k[h] * jnp.exp(g_tot[h] - bc[h] + ic[h] - m_new[h]) for h in heads]
    kv = [_mdot(kw[h], v[h], dims=TN) for h in heads]
    for h in heads:
        num = winter[h] * qc[h] + sv[h]
        den = (winter[h] * jnp.sum(qf[h] * nvec[h], axis=-1, keepdims=True)
               + jnp.sum(sc[h], axis=-1, keepdims=True))
        hout = num / jnp.maximum(jnp.abs(den), jnp.exp(-mt[h]))
        wc = jnp.exp(g_tot[h] + m_prev[h] - m_new[h])
        ct_ref[h] = wc * ct[h] + kv[h]
        n_ref[h:h + 1, :] = wc * nvec[h] + jnp.sum(kw[h], axis=0, keepdims=True)
        m_ref[0:1, h:h + 1] = m_new[h]
        o = o_all[:, h * dv:(h + 1) * dv]
        yb_ref[:, h * dv:(h + 1) * dv] = (_sigmoid(o) * hout).astype(yb_ref.dtype)


def _mlstm(z_ml, w_conv, b_conv, gbias, batch, seq):
    m = z_ml.shape[0]
    lc = MLSTM_CHUNK
    per_b = seq // lc
    const = lambda b, j: (0, 0)
    return pl.pallas_call(
        _mlstm_kernel,
        out_shape=jax.ShapeDtypeStruct((m, D_MLSTM), BF16),
        grid=(batch, per_b),
        in_specs=[pl.BlockSpec((lc, ML_COLS), lambda b, j: (b * per_b + j, 0)),
                  pl.BlockSpec((CONV_K, 2 * D_QK), const),
                  pl.BlockSpec((1, 2 * D_QK), const),
                  pl.BlockSpec((1, ML_GATE_PAD), const)],
        out_specs=pl.BlockSpec((lc, D_MLSTM), lambda b, j: (b * per_b + j, 0)),
        scratch_shapes=[pltpu.VMEM((8, 2 * D_QK), F32),
                        pltpu.VMEM((MLSTM_HEADS, MLSTM_DQK, MLSTM_DV), F32),
                        pltpu.VMEM((8, MLSTM_DQK), F32),
                        pltpu.VMEM((8, LANES), F32)],
        compiler_params=_cparams(("arbitrary", "arbitrary")),
        name="mlstm",
    )(z_ml, w_conv, b_conv, gbias)


def _merge_kernel(ya_ref, yb_ref, zg_ref, wa_ref, wb_ref, o_ref):
    pa = jnp.dot(ya_ref[...], wa_ref[...], preferred_element_type=F32)
    pb = jnp.dot(yb_ref[...], wb_ref[...], preferred_element_type=F32)
    ga = _sigmoid(zg_ref[:, 0:D_MODEL].astype(F32))
    gb = _sigmoid(zg_ref[:, D_MODEL:].astype(F32))
    o_ref[...] = (ga * pa + gb * pb).astype(o_ref.dtype)


def _merge(ya, yb, zg, wa, wb):
    m = ya.shape[0]
    tm = 512
    const = lambda i: (0, 0)
    return pl.pallas_call(
        _merge_kernel,
        out_shape=jax.ShapeDtypeStruct((m, D_MODEL), BF16),
        grid=(m // tm,),
        in_specs=[pl.BlockSpec((tm, D_RWKV), lambda i: (i, 0)),
                  pl.BlockSpec((tm, D_MLSTM), lambda i: (i, 0)),
                  pl.BlockSpec((tm, 2 * D_MODEL), lambda i: (i, 0)),
                  pl.BlockSpec((D_RWKV, D_MODEL), const),
                  pl.BlockSpec((D_MLSTM, D_MODEL), const)],
        out_specs=pl.BlockSpec((tm, D_MODEL), lambda i: (i, 0)),
        compiler_params=_cparams(("arbitrary",)),
        name="merge",
    )(ya, yb, zg, wa, wb)


def _outproj_kernel(mg_ref, w_ref, x_ref, ada_ref, g_ref, b_ref, x1_ref, u2_ref):
    y = jnp.dot(mg_ref[...], w_ref[...], preferred_element_type=F32)
    g1 = ada_ref[0, 2:3, :]
    x1 = _layernorm(ALPHA * x_ref[...] + g1 * y) * g_ref[...] + b_ref[...]
    x1_ref[...] = x1
    sh2, sc2 = ada_ref[0, 3:4, :], ada_ref[0, 4:5, :]
    u2_ref[...] = (_layernorm(x1) * (1.0 + sc2) + sh2).astype(u2_ref.dtype)


def _outproj(merged, w_out, x2, ada3, ln_g, ln_b, seq):
    m = merged.shape[0]
    tm = 256
    per_b = seq // tm
    const = lambda i: (0, 0)
    row = pl.BlockSpec((tm, D_MODEL), lambda i: (i, 0))
    return pl.pallas_call(
        _outproj_kernel,
        out_shape=[jax.ShapeDtypeStruct((m, D_MODEL), F32), jax.ShapeDtypeStruct((m, D_MODEL), BF16)],
        grid=(m // tm,),
        in_specs=[row, pl.BlockSpec((D_MODEL, D_MODEL), const), row,
                  pl.BlockSpec((1, 6, D_MODEL), lambda i: (i // per_b, 0, 0)),
                  pl.BlockSpec((1, D_MODEL), const), pl.BlockSpec((1, D_MODEL), const)],
        out_specs=[row, row],
        compiler_params=_cparams(("arbitrary",)),
        name="outproj",
    )(merged, w_out, x2, ada3, ln_g, ln_b)


def _ff1_kernel(x_ref, w_ref, b_ref, o_ref, wb_ref):
    @pl.when(pl.program_id(1) == 0)
    def _():
        wb_ref[...] = w_ref[...].astype(wb_ref.dtype)

    h = jnp.dot(x_ref[...], wb_ref[...], preferred_element_type=F32) + b_ref[...]
    h = jnp.maximum(h, 0.0)
    o_ref[...] = (h * h).astype(o_ref.dtype)


def _ff1(u2, w1, b1):
    m, k = u2.shape
    n = w1.shape[1]
    tm, tn = 1024, 1024
    return pl.pallas_call(
        _ff1_kernel,
        out_shape=jax.ShapeDtypeStruct((m, n), BF16),
        grid=(n // tn, m // tm),
        in_specs=[pl.BlockSpec((tm, k), lambda j, i: (i, 0)),
                  pl.BlockSpec((k, tn), lambda j, i: (0, j)),
                  pl.BlockSpec((1, tn), lambda j, i: (0, j))],
        out_specs=pl.BlockSpec((tm, tn), lambda j, i: (i, j)),
        scratch_shapes=[pltpu.VMEM((k, tn), BF16)],
        compiler_params=_cparams(("arbitrary", "arbitrary")),
        name="ff1",
    )(u2, w1, b1)


def _ff2_kernel(h_ref, w_ref, b_ref, x1_ref, ada_ref, g_ref, bb_ref, o_ref, acc_ref):
    kk = pl.program_id(1)

    @pl.when(kk == 0)
    def _():
        acc_ref[...] = jnp.zeros_like(acc_ref)

    acc_ref[...] += jnp.dot(h_ref[...], w_ref[...], preferred_element_type=F32)

    @pl.when(kk == pl.num_programs(1) - 1)
    def _():
        y2 = acc_ref[...] + b_ref[...]
        g2 = ada_ref[0, 5:6, :]
        o_ref[...] = (_layernorm(ALPHA * x1_ref[...] + g2 * y2) * g_ref[...] + bb_ref[...]).astype(o_ref.dtype)


def _ff2(h, w2, b2, x1, ada3, ln_g, ln_b, seq, out_dtype):
    m, k = h.shape
    tm, tk = 512, 2048
    per_b = seq // tm
    const = lambda i, kk: (0, 0)
    row = pl.BlockSpec((tm, D_MODEL), lambda i, kk: (i, 0))
    return pl.pallas_call(
        _ff2_kernel,
        out_shape=jax.ShapeDtypeStruct((m, D_MODEL), out_dtype),
        grid=(m // tm, k // tk),
        in_specs=[pl.BlockSpec((tm, tk), lambda i, kk: (i, kk)),
                  pl.BlockSpec((tk, D_MODEL), lambda i, kk: (kk, 0)),
                  pl.BlockSpec((1, D_MODEL), const), row,
                  pl.BlockSpec((1, 6, D_MODEL), lambda i, kk: (i // per_b, 0, 0)),
                  pl.BlockSpec((1, D_MODEL), const), pl.BlockSpec((1, D_MODEL), const)],
        out_specs=row,
        scratch_shapes=[pltpu.VMEM((tm, D_MODEL), F32)],
        compiler_params=_cparams(("arbitrary", "arbitrary")),
        name="ff2",
    )(h, w2, b2, x1, ada3, ln_g, ln_b)


def _pad_cols(w, width):
    return jnp.pad(w, ((0, 0), (0, width - w.shape[1])))


def _pad_rows(w, height):
    return jnp.pad(w, ((0, height - w.shape[0]), (0, 0)))


def _split_w_in(w):
    c = D_RWKV
    o = 0
    r_k_v = w[:, o:o + 3 * c]; o += 3 * c
    wd = w[:, o:o + DECAY_LORA]; o += DECAY_LORA
    ad = w[:, o:o + A_LORA]; o += A_LORA
    gd = w[:, o:o + GATE_LORA]; o += GATE_LORA
    q_k = w[:, o:o + 2 * D_QK]; o += 2 * D_QK
    mv = w[:, o:o + D_MLSTM]; o += D_MLSTM
    i_f = w[:, o:o + 2 * MLSTM_HEADS]; o += 2 * MLSTM_HEADS
    mo = w[:, o:o + D_MLSTM]; o += D_MLSTM
    gates = w[:, o:]
    w_rw = jnp.concatenate([r_k_v, gd, _pad_cols(wd, LORA_PAD), _pad_cols(ad, LORA_PAD)], axis=1)
    w_ml = jnp.concatenate([q_k, mv, mo, _pad_cols(i_f, ML_GATE_PAD)], axis=1)
    return w_rw, w_ml, gates


def _layer(x2, c, batch, seq, w_ada, b_ada, w_in, mu_shift, w0, w_decay_up, a0, w_a_up, w_g_up, k_k, k_a, r_k,
           gn_g, gn_b, w_conv, b_conv, b_igate, b_fgate, w_branch_a, w_branch_b, w_out,
           ln1_g, ln1_b, w_ff1, b_ff1, w_ff2, b_ff2, ln2_g, ln2_b):
    row = lambda v: v.reshape(1, -1)

    ada3 = _ada(c, w_ada, b_ada).reshape(batch, 6, D_MODEL)
    u = _lnmod(x2, ada3, seq)

    w_rw, w_ml, w_gt = _split_w_in(w_in.astype(BF16))
    z_rw = _matmul(u, w_rw, 2048, 512, F32, "win_rwkv")
    z_ml = _matmul(u, w_ml, 2048, 640, F32, "win_mlstm")
    z_gt = _matmul(u, w_gt, 1024, 1024, BF16, "win_gate")

    cc = D_RWKV
    mu = mu_shift
    mu_rw = jnp.concatenate([mu[:3 * cc], mu[3 * cc + DECAY_LORA + A_LORA:],
                             jnp.pad(mu[3 * cc:3 * cc + DECAY_LORA], (0, LORA_PAD - DECAY_LORA)),
                             jnp.pad(mu[3 * cc + DECAY_LORA:3 * cc + DECAY_LORA + A_LORA], (0, LORA_PAD - A_LORA))])
    prep = _rwkv_prep(z_rw, row(mu_rw), row(w0), row(a0), row(k_k), row(k_a), row(r_k),
                      _pad_rows(w_decay_up, LORA_PAD), _pad_rows(w_a_up, LORA_PAD), w_g_up, batch, seq)
    r, ld, kt, v, kkn, bb, g, bonus = prep
    y0, q, a_mat, c_mat = _rwkv_chunk(r, ld, kt, v, kkn, bb, batch, seq)
    ya = _rwkv_scan(y0, q, a_mat, c_mat, g, bonus, row(gn_g), row(gn_b), batch, seq)

    gbias = jnp.pad(jnp.concatenate([b_igate, b_fgate]), (0, ML_GATE_PAD - 2 * MLSTM_HEADS))
    yb = _mlstm(z_ml, w_conv, row(b_conv), row(gbias), batch, seq)

    merged = _merge(ya, yb, z_gt, w_branch_a.astype(BF16), w_branch_b.astype(BF16))
    x1, u2 = _outproj(merged, w_out.astype(BF16), x2, ada3, row(ln1_g), row(ln1_b), seq)

    h = _ff1(u2, w_ff1, row(b_ff1))
    return _ff2(h, w_ff2.astype(BF16), row(b_ff2), x1, ada3, row(ln2_g), row(ln2_b), seq, F32)


def kernel(x, c, w_ada, b_ada, w_in, mu_shift, w0, w_decay_up, a0, w_a_up, w_g_up, k_k, k_a, r_k, gn_g, gn_b,
           w_conv, b_conv, b_igate, b_fgate, w_branch_a, w_branch_b, w_out, ln1_g, ln1_b, w_ff1, b_ff1,
           w_ff2, b_ff2, ln2_g, ln2_b):
    out_dtype = x.dtype
    batch, seq, d = x.shape
    assert d == D_MODEL and seq % 1024 == 0 and w_ada.shape[0] == DEPTH
    x2 = x.astype(F32).reshape(batch * seq, d)
    cf = c.astype(F32)
    for l in range(DEPTH):
        x2 = _layer(x2, cf, batch, seq, w_ada[l], b_ada[l], w_in[l], mu_shift[l], w0[l], w_decay_up[l], a0[l],
                    w_a_up[l], w_g_up[l], k_k[l], k_a[l], r_k[l].reshape(-1), gn_g[l], gn_b[l], w_conv[l],
                    b_conv[l], b_igate[l], b_fgate[l], w_branch_a[l], w_branch_b[l], w_out[l], ln1_g[l],
                    ln1_b[l], w_ff1[l], b_ff1[l], w_ff2[l], b_ff2[l], ln2_g[l], ln2_b[l])
    return x2.reshape(batch, seq, d).astype(out_dtype)
```

```python
import functools

import jax
import jax.numpy as jnp
from jax import lax
from jax.experimental import pallas as pl
from jax.experimental.pallas import tpu as pltpu

F32 = jnp.float32
BF16 = jnp.bfloat16

D_MODEL = 2048
DEPTH = 1
D_RWKV = D_MODEL // 2
RWKV_HEAD = 64
DECAY_LORA = 96
A_LORA = 96
GATE_LORA = 256
RWKV_GN_EPS = 64e-5
D_MLSTM = D_MODEL // 2
MLSTM_HEADS = 4
MLSTM_DV = D_MLSTM // MLSTM_HEADS
MLSTM_DQK = MLSTM_DV // 2
D_QK = MLSTM_HEADS * MLSTM_DQK
CONV_K = 4
D_FF = 4 * D_MODEL
LN_EPS = 1e-5
ALPHA = (2.0 * DEPTH) ** 0.25

LANES = 128
LORA_PAD = 128
RW_COLS = 3 * D_RWKV + GATE_LORA + 2 * LORA_PAD
ML_GATE_PAD = 128
ML_COLS = 2 * D_QK + 2 * D_MLSTM + ML_GATE_PAD
RWKV_CHUNK = 64
RWKV_TB = 256
RWKV_CORE_TB = 1024
MLSTM_CHUNK = 256
EPILOGUE_ROWS = 256
VMEM_LIMIT = 56 * 1024 * 1024

NN = (((1,), (0,)), ((), ()))
NT = (((1,), (1,)), ((), ()))
TN = (((0,), (0,)), ((), ()))


def _cparams(sem):
    return pltpu.CompilerParams(dimension_semantics=sem, vmem_limit_bytes=VMEM_LIMIT)


def _bf16_parts(x, n):
    if x.dtype == BF16:
        return [x]
    parts, rem = [], x
    for i in range(n):
        p = rem.astype(BF16)
        parts.append(p)
        if i + 1 < n:
            rem = rem - p.astype(F32)
    return parts


def _mdot(a, b, na=1, nb=1, dims=NN):
    ap, bp = _bf16_parts(a, na), _bf16_parts(b, nb)
    order = max(len(ap), len(bp))
    acc = None
    for i, x in enumerate(ap):
        for j, y in enumerate(bp):
            if i + j < order:
                t = lax.dot_general(x, y, dims, preferred_element_type=F32)
                acc = t if acc is None else acc + t
    return acc


def _sigmoid(x):
    return 1.0 / (1.0 + jnp.exp(-x))


def _softplus(x):
    return jnp.maximum(x, 0.0) + jnp.log(1.0 + jnp.exp(-jnp.abs(x)))


def _layernorm(x):
    mu = jnp.mean(x, axis=-1, keepdims=True)
    xc = x - mu
    var = jnp.mean(xc * xc, axis=-1, keepdims=True)
    return xc * lax.rsqrt(var + LN_EPS)


def _iota2(shape, dim):
    return lax.broadcasted_iota(jnp.int32, shape, dim)


def _head_block_ones(n, head):
    r, c = _iota2((n, n), 0), _iota2((n, n), 1)
    return jnp.where((r // head) == (c // head), 1.0, 0.0).astype(BF16)


def _head_sum(x, ones_blk):
    return _mdot(x, ones_blk, na=2, nb=1)


def _ada_kernel(ct_ref, w_ref, b_ref, o_ref):
    tn = w_ref.shape[1]

    def body(i, acc):
        a0, a1 = acc
        r0 = pl.multiple_of(i * 8, 8)
        wv = w_ref[pl.ds(r0, 8), :]
        cv = ct_ref[pl.ds(r0, 8), :]
        sv = cv * _sigmoid(cv)
        return a0 + sv[:, 0:1] * wv, a1 + sv[:, 1:2] * wv

    zero = jnp.zeros((8, tn), F32)
    a0, a1 = lax.fori_loop(0, w_ref.shape[0] // 8, body, (zero, zero), unroll=8)
    o_ref[0:1, :] = jnp.sum(a0, axis=0, keepdims=True) + b_ref[...]
    o_ref[1:2, :] = jnp.sum(a1, axis=0, keepdims=True) + b_ref[...]


def _ada(c, w_ada, b_ada):
    batch = c.shape[0]
    assert batch == 2
    n = w_ada.shape[1]
    tn = 1536
    return pl.pallas_call(
        _ada_kernel,
        out_shape=jax.ShapeDtypeStruct((batch, n), F32),
        grid=(n // tn,),
        in_specs=[pl.BlockSpec((D_MODEL, batch), lambda j: (0, 0)),
                  pl.BlockSpec((D_MODEL, tn), lambda j: (0, j)),
                  pl.BlockSpec((1, tn), lambda j: (0, j))],
        out_specs=pl.BlockSpec((batch, tn), lambda j: (0, j)),
        compiler_params=_cparams(("arbitrary",)),
        name="ada",
    )(c.T, w_ada, b_ada.reshape(1, n))


def _lnmod_kernel(x_ref, ada_ref, o_ref):
    xn = _layernorm(x_ref[...])
    sh, sc = ada_ref[0, 0:1, :], ada_ref[0, 1:2, :]
    o_ref[...] = (xn * (1.0 + sc) + sh).astype(o_ref.dtype)


def _lnmod(x2, ada3, seq):
    m = x2.shape[0]
    tb = 512
    per_b = seq // tb
    return pl.pallas_call(
        _lnmod_kernel,
        out_shape=jax.ShapeDtypeStruct((m, D_MODEL), BF16),
        grid=(m // tb,),
        in_specs=[pl.BlockSpec((tb, D_MODEL), lambda i: (i, 0)),
                  pl.BlockSpec((1, 6, D_MODEL), lambda i: (i // per_b, 0, 0))],
        out_specs=pl.BlockSpec((tb, D_MODEL), lambda i: (i, 0)),
        compiler_params=_cparams(("arbitrary",)),
        name="lnmod",
    )(x2, ada3)


def _mm_kernel(x_ref, w_ref, o_ref):
    o_ref[...] = jnp.dot(x_ref[...], w_ref[...], preferred_element_type=F32).astype(o_ref.dtype)


def _matmul(x, w, tm, tn, out_dtype, name):
    m, k = x.shape
    n = w.shape[1]
    return pl.pallas_call(
        _mm_kernel,
        out_shape=jax.ShapeDtypeStruct((m, n), out_dtype),
        grid=(n // tn, m // tm),
        in_specs=[pl.BlockSpec((tm, k), lambda j, i: (i, 0)),
                  pl.BlockSpec((k, tn), lambda j, i: (0, j))],
        out_specs=pl.BlockSpec((tm, tn), lambda j, i: (i, j)),
        compiler_params=_cparams(("arbitrary", "arbitrary")),
        name=name,
    )(x, w)


def _rwkv_prep_kernel(z_ref, mu_ref, w0_ref, a0_ref, kk_ref, ka_ref, rk_ref, wdu_ref, wau_ref, wgu_ref,
                      r_o, ld_o, kt_o, v_o, kkn_o, bb_o, g_o, bonus_o, carry_ref):
    tb = z_ref.shape[0]

    @pl.when(pl.program_id(1) == 0)
    def _():
        carry_ref[...] = jnp.zeros_like(carry_ref)

    z = z_ref[...]
    prev = pltpu.roll(z, 1, 0)
    prev = jnp.where(_iota2(z.shape, 0) == 0, carry_ref[0:1, :], prev)
    carry_ref[0:1, :] = z[tb - 1:tb, :]
    zs = z + (prev - z) * mu_ref[...]

    c = D_RWKV
    r, k, v = zs[:, 0:c], zs[:, c:2 * c], zs[:, 2 * c:3 * c]
    gd = zs[:, 3 * c:3 * c + GATE_LORA]
    wd = zs[:, 3 * c + GATE_LORA:3 * c + GATE_LORA + LORA_PAD]
    ad = zs[:, 3 * c + GATE_LORA + LORA_PAD:]

    logw = -_softplus(-(w0_ref[...] + _mdot(jnp.tanh(wd), wdu_ref[...], 2, 2))) - 0.5
    ld = -jnp.exp(logw)
    a = _sigmoid(a0_ref[...] + _mdot(ad, wau_ref[...], 2, 2))
    g = _mdot(_sigmoid(gd), wgu_ref[...], 1, 1)

    ones_blk = _head_block_ones(LANES, RWKV_HEAD)
    kq = k * kk_ref[...]
    kt = k * (1.0 + (a - 1.0) * ka_ref[...])
    sq = kq * kq
    bn = r * kt * rk_ref[...]
    ss = jnp.concatenate([_head_sum(sq[:, s:s + LANES], ones_blk) for s in range(0, c, LANES)], axis=1)
    bs = jnp.concatenate([_head_sum(bn[:, s:s + LANES], ones_blk) for s in range(0, c, LANES)], axis=1)
    kkn = kq * lax.rsqrt(ss + 1e-12)

    r_o[...] = r.astype(r_o.dtype)
    ld_o[...] = ld
    kt_o[...] = kt.astype(kt_o.dtype)
    v_o[...] = v.astype(v_o.dtype)
    kkn_o[...] = kkn.astype(kkn_o.dtype)
    bb_o[...] = (a * kkn).astype(bb_o.dtype)
    g_o[...] = g.astype(g_o.dtype)
    bonus_o[...] = (bs * v).astype(bonus_o.dtype)


def _rwkv_prep(z_rw, mu_rw, w0, a0, k_k, k_a, r_k, wdu, wau, wgu, batch, seq):
    m = z_rw.shape[0]
    tb = RWKV_TB
    per_b = seq // tb
    row = lambda b, j: (b * per_b + j, 0)
    const = lambda b, j: (0, 0)
    vec = pl.BlockSpec((1, D_RWKV), const)
    outs = [jax.ShapeDtypeStruct((m, D_RWKV), F32 if i == 1 else BF16) for i in range(8)]
    return pl.pallas_call(
        _rwkv_prep_kernel,
        out_shape=outs,
        grid=(batch, per_b),
        in_specs=[pl.BlockSpec((tb, RW_COLS), row), pl.BlockSpec((1, RW_COLS), const),
                  vec, vec, vec, vec, vec,
                  pl.BlockSpec((LORA_PAD, D_RWKV), const), pl.BlockSpec((LORA_PAD, D_RWKV), const),
                  pl.BlockSpec((GATE_LORA, D_RWKV), const)],
        out_specs=[pl.BlockSpec((tb, D_RWKV), row)] * 8,
        scratch_shapes=[pltpu.VMEM((8, RW_COLS), F32)],
        compiler_params=_cparams(("arbitrary", "arbitrary")),
        name="rwkv_prep",
    )(z_rw, mu_rw, w0, a0, k_k, k_a, r_k, wdu, wau, wgu)


def _rwkv_chunk_kernel(r_ref, ld_ref, kt_ref, v_ref, kk_ref, bb_ref, y0_ref, q_ref, a_ref, c_ref):
    tb = r_ref.shape[0]
    lc = RWKV_CHUNK
    n = 2 * lc
    nchunk = tb // lc

    lane = _iota2((1, LANES), 1)
    m0 = jnp.where(lane < RWKV_HEAD, 1.0, 0.0)
    m1 = 1.0 - m0
    ri, ci = _iota2((n, n), 0), _iota2((n, n), 1)
    same_head = (ri // lc) == (ci // lc)
    strict = jnp.where(same_head & (ri > ci), 1.0, 0.0)
    incl = jnp.where(same_head & (ri >= ci), 1.0, 0.0)
    eye = jnp.where(ri == ci, 1.0, 0.0)

    tw = 4 * lc
    rt, ct = _iota2((tw, tw), 0), _iota2((tw, tw), 1)
    tri = jnp.where(((rt // lc) == (ct // lc)) & (ct <= rt), 1.0, 0.0).astype(BF16)

    ld = ld_ref[...]
    cl = jnp.concatenate([_mdot(tri, ld[i:i + tw], 1, 3) for i in range(0, tb, tw)], axis=0)
    e_pos = jnp.exp(cl)
    e_neg = jnp.exp(-cl)
    a_hat = kk_ref[...] * jnp.exp(cl - ld)
    b_hat = bb_ref[...] * e_neg
    k_hat = kt_ref[...] * e_neg
    r_hat = r_ref[...] * e_pos
    v_all = v_ref[...]

    def stack(x):
        return jnp.concatenate([x * m0, x * m1], axis=0)

    def bf(t):
        return t.astype(BF16)

    chunks = range(nchunk)
    st = [[stack(x[c * lc:(c + 1) * lc]) for x in (a_hat, b_hat, k_hat, r_hat, v_all)] for c in chunks]
    a_st, b_st, k_st, r_st, v_st = ([st[c][i] for c in chunks] for i in range(5))
    b_b, k_b, v_b = ([bf(t) for t in ts] for ts in (b_st, k_st, v_st))
    p = [_mdot(jnp.concatenate([a_st[c], r_st[c]], axis=0),
               jnp.concatenate([b_b[c], k_b[c]], axis=0), dims=NT) for c in chunks]
    mab = [p[c][0:n, 0:n] * strict for c in chunks]
    mab_b = [bf(t) for t in mab]
    mak_b = [bf(p[c][0:n, n:] * strict) for c in chunks]
    mrb_b = [bf(p[c][n:, 0:n] * incl) for c in chunks]
    mrk_b = [bf(p[c][n:, n:] * incl) for c in chunks]

    x = [eye - jnp.where((ri // 2 == ci // 2), mab[c], 0.0) for c in chunks]
    w1 = [_mdot(mak_b[c], v_b[c]) for c in chunks]
    zero_b = jnp.zeros((), BF16)
    s = 2
    while s < lc:
        lvl = (ri // (2 * s) == ci // (2 * s)) & (ri // s != ci // s)
        x_b = [bf(t) for t in x]
        nx = [_mdot(jnp.where(lvl, mab_b[c], zero_b), x_b[c]) for c in chunks]
        x = [x[c] - _mdot(x_b[c], nx[c]) for c in chunks]
        s *= 2

    ta = [_mdot(x[c], jnp.concatenate([w1[c], a_st[c]], axis=1)) for c in chunks]
    zeros_b = jnp.zeros((n, LANES), BF16)
    ugv_b = [jnp.concatenate([bf(jnp.concatenate([-ta[c][:, 0:LANES], ta[c][:, LANES:]], axis=1)),
                              jnp.concatenate([v_b[c], zeros_b], axis=1)], axis=0) for c in chunks]
    ru = [_mdot(jnp.concatenate([mrb_b[c], mrk_b[c]], axis=1), ugv_b[c]) for c in chunks]
    y0 = [ru[c][:, 0:LANES] for c in chunks]
    q = [r_st[c] - ru[c][:, LANES:] for c in chunks]
    pl_row = [e_pos[(c + 1) * lc - 1:(c + 1) * lc, :] for c in chunks]
    bk_p = [bf(jnp.concatenate([b_st[c] * pl_row[c], k_st[c] * pl_row[c]], axis=0)) for c in chunks]
    bu = [_mdot(bk_p[c], ugv_b[c], dims=TN) for c in chunks]
    ct = [bu[c][:, 0:LANES] for c in chunks]
    at = [eye * pl_row[c] - bu[c][:, LANES:] for c in chunks]

    for c in chunks:
        rows = slice(c * lc, (c + 1) * lc)
        y0_ref[rows, :] = y0[c][0:lc] + y0[c][lc:]
        q_ref[rows, :] = (q[c][0:lc] + q[c][lc:]).astype(q_ref.dtype)
        a_ref[0, c, 0] = at[c].astype(a_ref.dtype)
        c_ref[0, c, 0] = ct[c]


def _rwkv_chunk(r, ld, kt, v, kkn, bb, batch, seq):
    m = r.shape[0]
    tb = RWKV_CORE_TB
    per_b = seq // tb
    cpb = tb // RWKV_CHUNK
    npair = D_RWKV // LANES
    slab = pl.BlockSpec((tb, LANES), lambda b, p, j: (b * per_b + j, p))
    mat = pl.BlockSpec((1, cpb, 1, LANES, LANES), lambda b, p, j: (b, j, p, 0, 0))
    mshape = (batch, seq // RWKV_CHUNK, npair, LANES, LANES)
    return pl.pallas_call(
        _rwkv_chunk_kernel,
        out_shape=[jax.ShapeDtypeStruct((m, D_RWKV), F32), jax.ShapeDtypeStruct((m, D_RWKV), BF16),
                   jax.ShapeDtypeStruct(mshape, BF16), jax.ShapeDtypeStruct(mshape, F32)],
        grid=(batch, npair, per_b),
        in_specs=[slab] * 6,
        out_specs=[slab, slab, mat, mat],
        compiler_params=_cparams(("arbitrary", "arbitrary", "arbitrary")),
        name="rwkv_chunk",
    )(r, ld, kt, v, kkn, bb)


def _rwkv_scan_kernel(y0_ref, q_ref, a_ref, c_ref, g_ref, bonus_ref, gng_ref, gnb_ref, ya_ref, s_ref):
    nb, rows, _ = y0_ref.shape
    lc = RWKV_CHUNK
    npair = D_RWKV // LANES

    @pl.when(pl.program_id(0) == 0)
    def _():
        s_ref[...] = jnp.zeros_like(s_ref)

    ones_blk = _head_block_ones(LANES, RWKV_HEAD)
    streams = [(b, p) for b in range(nb) for p in range(npair)]
    for ci in range(rows // lc):
        rs = slice(ci * lc, (ci + 1) * lc)
        state_b = [s_ref[b * npair + p].astype(BF16) for b, p in streams]
        y = [y0_ref[b, rs, p * LANES:(p + 1) * LANES]
             + _mdot(q_ref[b, rs, p * LANES:(p + 1) * LANES], state_b[i])
             for i, (b, p) in enumerate(streams)]
        for i, (b, p) in enumerate(streams):
            s_ref[b * npair + p] = _mdot(a_ref[b, ci, p], state_b[i]) + c_ref[b, ci, p]
        mu = [_head_sum(t, ones_blk) * (1.0 / RWKV_HEAD) for t in y]
        yc = [y[i] - mu[i] for i in range(len(streams))]
        var = [_head_sum(t * t, ones_blk) * (1.0 / RWKV_HEAD) for t in yc]
        for i, (b, p) in enumerate(streams):
            ls = slice(p * LANES, (p + 1) * LANES)
            yn = yc[i] * lax.rsqrt(var[i] + RWKV_GN_EPS) * gng_ref[:, ls] + gnb_ref[:, ls]
            ya_ref[b, rs, ls] = ((yn + bonus_ref[b, rs, ls]) * g_ref[b, rs, ls]).astype(ya_ref.dtype)


def _rwkv_scan(y0, q, a_mat, c_mat, g, bonus, gn_g, gn_b, batch, seq):
    rows = RWKV_CHUNK
    npair = D_RWKV // LANES
    tok = pl.BlockSpec((batch, rows, D_RWKV), lambda j: (0, j, 0))
    mat = pl.BlockSpec((batch, rows // RWKV_CHUNK, npair, LANES, LANES), lambda j: (0, j, 0, 0, 0))
    vec = pl.BlockSpec((1, D_RWKV), lambda j: (0, 0))
    shape3 = (batch, seq, D_RWKV)
    return pl.pallas_call(
        _rwkv_scan_kernel,
        out_shape=jax.ShapeDtypeStruct(shape3, BF16),
        grid=(seq // rows,),
        in_specs=[tok, tok, mat, mat, tok, tok, vec, vec],
        out_specs=tok,
        scratch_shapes=[pltpu.VMEM((batch * npair, LANES, LANES), F32)],
        compiler_params=_cparams(("arbitrary",)),
        name="rwkv_scan",
    )(y0.reshape(shape3), q.reshape(shape3), a_mat, c_mat, g.reshape(shape3), bonus.reshape(shape3),
      gn_g, gn_b).reshape(batch * seq, D_RWKV)


def _mlstm_kernel(z_ref, wconv_ref, bconv_ref, gbias_ref, yb_ref, tail_ref, ct_ref, n_ref, m_ref):
    lc = z_ref.shape[0]
    dqk, dv, nh = MLSTM_DQK, MLSTM_DV, MLSTM_HEADS

    @pl.when(pl.program_id(1) == 0)
    def _():
        tail_ref[...] = jnp.zeros_like(tail_ref)
        ct_ref[...] = jnp.zeros_like(ct_ref)
        n_ref[...] = jnp.zeros_like(n_ref)
        m_ref[...] = jnp.zeros_like(m_ref)

    zqk = z_ref[:, 0:2 * D_QK]
    ext = jnp.concatenate([tail_ref[...], zqk], axis=0)
    tail_ref[...] = zqk[lc - 8:lc, :]
    conv = bconv_ref[...] + wconv_ref[CONV_K - 1:CONV_K, :] * zqk
    for d in range(1, CONV_K):
        conv = conv + wconv_ref[CONV_K - 1 - d:CONV_K - d, :] * ext[8 - d:8 - d + lc, :]
    qk = conv * _sigmoid(conv)
    q_all = qk[:, 0:D_QK]
    k_all = qk[:, D_QK:] * (dqk ** -0.5)
    v_all = z_ref[:, 2 * D_QK:2 * D_QK + D_MLSTM]
    o_all = z_ref[:, 2 * D_QK + D_MLSTM:2 * D_QK + 2 * D_MLSTM]

    gz = z_ref[:, 2 * D_QK + 2 * D_MLSTM:] + gbias_ref[...]
    lf_col = jnp.minimum(gz, 0.0) - jnp.log(1.0 + jnp.exp(-jnp.abs(gz)))
    rr, cc = _iota2((lc, lc), 0), _iota2((lc, lc), 1)
    causal = cc <= rr
    tri = jnp.where(causal, 1.0, 0.0).astype(BF16)
    b_col = _mdot(tri, lf_col, 1, 3)
    sel = jnp.where(_iota2((8, LANES), 0) == _iota2((8, LANES), 1), 1.0, 0.0).astype(BF16)
    g_row = _mdot(sel, gz, 1, 3, NT)
    lf_row = jnp.minimum(g_row, 0.0) - jnp.log(1.0 + jnp.exp(-jnp.abs(g_row)))
    b_row = _mdot(lf_row, tri, 3, 1, NT)

    heads = range(nh)
    qf = [q_all[:, h * dqk:(h + 1) * dqk] for h in heads]
    q = [t.astype(BF16) for t in qf]
    k = [k_all[:, h * dqk:(h + 1) * dqk] for h in heads]
    v = [v_all[:, h * dv:(h + 1) * dv].astype(BF16) for h in heads]
    bc = [b_col[:, nh + h:nh + h + 1] for h in heads]
    ic = [gz[:, h:h + 1] for h in heads]
    br = [b_row[nh + h:nh + h + 1, :] for h in heads]
    ir = [g_row[h:h + 1, :] for h in heads]
    g_tot = [b_col[lc - 1:lc, nh + h:nh + h + 1] for h in heads]
    m_prev = [m_ref[0:1, h:h + 1] for h in heads]
    ct = [ct_ref[h] for h in heads]
    nvec = [n_ref[h:h + 1, :] for h in heads]

    qk = [_mdot(q[h], k[h], dims=NT) for h in heads]
    qc = [_mdot(q[h], ct[h]) for h in heads]
    dm = [jnp.where(causal, bc[h] - br[h] + ir[h], -jnp.inf) for h in heads]
    inter = [bc[h] + m_prev[h] for h in heads]
    mt = [jnp.maximum(inter[h], jnp.max(dm[h], axis=-1, keepdims=True)) for h in heads]
    sc = [qk[h] * jnp.exp(dm[h] - mt[h]) for h in heads]
    winter = [jnp.exp(inter[h] - mt[h]) for h in heads]
    sv = [_mdot(sc[h], v[h]) for h in heads]
    m_new = [jnp.maximum(g_tot[h] + m_prev[h], jnp.max(g_tot[h] - br[h] + ir[h], axis=-1, keepdims=True))
             for h in heads]
    kw = [k[h] * jnp.exp(g_tot[h] - bc[h] + ic[h] - m_new[h]) for h in heads]
    kv = [_mdot(kw[h], v[h], dims=TN) for h in heads]
    for h in heads:
        num = winter[h] * qc[h] + sv[h]
        den = (winter[h] * jnp.sum(qf[h] * nvec[h], axis=-1, keepdims=True)
               + jnp.sum(sc[h], axis=-1, keepdims=True))
        hout = num / jnp.maximum(jnp.abs(den), jnp.exp(-mt[h]))
        wc = jnp.exp(g_tot[h] + m_prev[h] - m_new[h])
        ct_ref[h] = wc * ct[h] + kv[h]
        n_ref[h:h + 1, :] = wc * nvec[h] + jnp.sum(kw[h], axis=0, keepdims=True)
        m_ref[0:1, h:h + 1] = m_new[h]
        o = o_all[:, h * dv:(h + 1) * dv]
        yb_ref[:, h * dv:(h + 1) * dv] = (_sigmoid(o) * hout).astype(yb_ref.dtype)


def _mlstm(z_ml, w_conv, b_conv, gbias, batch, seq):
    m = z_ml.shape[0]
    lc = MLSTM_CHUNK
    per_b = seq // lc
    const = lambda b, j: (0, 0)
    return pl.pallas_call(
        _mlstm_kernel,
        out_shape=jax.ShapeDtypeStruct((m, D_MLSTM), BF16),
        grid=(batch, per_b),
        in_specs=[pl.BlockSpec((lc, ML_COLS), lambda b, j: (b * per_b + j, 0)),
                  pl.BlockSpec((CONV_K, 2 * D_QK), const),
                  pl.BlockSpec((1, 2 * D_QK), const),
                  pl.BlockSpec((1, ML_GATE_PAD), const)],
        out_specs=pl.BlockSpec((lc, D_MLSTM), lambda b, j: (b * per_b + j, 0)),
        scratch_shapes=[pltpu.VMEM((8, 2 * D_QK), F32),
                        pltpu.VMEM((MLSTM_HEADS, MLSTM_DQK, MLSTM_DV), F32),
                        pltpu.VMEM((8, MLSTM_DQK), F32),
                        pltpu.VMEM((8, LANES), F32)],
        compiler_params=_cparams(("arbitrary", "arbitrary")),
        name="mlstm",
    )(z_ml, w_conv, b_conv, gbias)


def _merge_kernel(ya_ref, yb_ref, zg_ref, wa_ref, wb_ref, o_ref):
    groups = [slice(i, i + EPILOGUE_ROWS) for i in range(0, ya_ref.shape[0], EPILOGUE_ROWS)]
    pab = [(jnp.dot(ya_ref[rows, :], wa_ref[...], preferred_element_type=F32),
            jnp.dot(yb_ref[rows, :], wb_ref[...], preferred_element_type=F32)) for rows in groups]
    for (pa, pb), rows in zip(pab, groups):
        ga = _sigmoid(zg_ref[rows, 0:D_MODEL].astype(F32))
        gb = _sigmoid(zg_ref[rows, D_MODEL:].astype(F32))
        o_ref[rows, :] = (ga * pa + gb * pb).astype(o_ref.dtype)


def _merge(ya, yb, zg, wa, wb):
    m = ya.shape[0]
    tm = 512
    const = lambda i: (0, 0)
    return pl.pallas_call(
        _merge_kernel,
        out_shape=jax.ShapeDtypeStruct((m, D_MODEL), BF16),
        grid=(m // tm,),
        in_specs=[pl.BlockSpec((tm, D_RWKV), lambda i: (i, 0)),
                  pl.BlockSpec((tm, D_MLSTM), lambda i: (i, 0)),
                  pl.BlockSpec((tm, 2 * D_MODEL), lambda i: (i, 0)),
                  pl.BlockSpec((D_RWKV, D_MODEL), const),
                  pl.BlockSpec((D_MLSTM, D_MODEL), const)],
        out_specs=pl.BlockSpec((tm, D_MODEL), lambda i: (i, 0)),
        compiler_params=_cparams(("arbitrary",)),
        name="merge",
    )(ya, yb, zg, wa, wb)


def _outproj_kernel(mg_ref, w_ref, x_ref, ada_ref, g_ref, b_ref, x1_ref, u2_ref):
    groups = [slice(i, i + EPILOGUE_ROWS) for i in range(0, mg_ref.shape[0], EPILOGUE_ROWS)]
    ys = [jnp.dot(mg_ref[rows, :], w_ref[...], preferred_element_type=F32) for rows in groups]
    g1 = ada_ref[0, 2:3, :]
    sh2, sc2 = ada_ref[0, 3:4, :], ada_ref[0, 4:5, :]
    for y, rows in zip(ys, groups):
        x1 = _layernorm(ALPHA * x_ref[rows, :] + g1 * y) * g_ref[...] + b_ref[...]
        x1_ref[rows, :] = x1
        u2_ref[rows, :] = (_layernorm(x1) * (1.0 + sc2) + sh2).astype(u2_ref.dtype)


def _outproj(merged, w_out, x2, ada3, ln_g, ln_b, seq):
    m = merged.shape[0]
    tm = 512
    per_b = seq // tm
    const = lambda i: (0, 0)
    row = pl.BlockSpec((tm, D_MODEL), lambda i: (i, 0))
    return pl.pallas_call(
        _outproj_kernel,
        out_shape=[jax.ShapeDtypeStruct((m, D_MODEL), F32), jax.ShapeDtypeStruct((m, D_MODEL), BF16)],
        grid=(m // tm,),
        in_specs=[row, pl.BlockSpec((D_MODEL, D_MODEL), const, pipeline_mode=pl.Buffered(1)), row,
                  pl.BlockSpec((1, 6, D_MODEL), lambda i: (i // per_b, 0, 0)),
                  pl.BlockSpec((1, D_MODEL), const), pl.BlockSpec((1, D_MODEL), const)],
        out_specs=[row, row],
        compiler_params=_cparams(("arbitrary",)),
        name="outproj",
    )(merged, w_out, x2, ada3, ln_g, ln_b)


def _ff1_kernel(x_ref, w_ref, b_ref, o_ref, wb_ref):
    @pl.when(pl.program_id(1) == 0)
    def _():
        wb_ref[...] = w_ref[...].astype(wb_ref.dtype)

    h = jnp.dot(x_ref[...], wb_ref[...], preferred_element_type=F32) + b_ref[...]
    h = jnp.maximum(h, 0.0)
    o_ref[...] = (h * h).astype(o_ref.dtype)


def _ff1(u2, w1, b1):
    m, k = u2.shape
    n = w1.shape[1]
    tm, tn = 1024, 1024
    return pl.pallas_call(
        _ff1_kernel,
        out_shape=jax.ShapeDtypeStruct((m, n), BF16),
        grid=(n // tn, m // tm),
        in_specs=[pl.BlockSpec((tm, k), lambda j, i: (i, 0)),
                  pl.BlockSpec((k, tn), lambda j, i: (0, j)),
                  pl.BlockSpec((1, tn), lambda j, i: (0, j))],
        out_specs=pl.BlockSpec((tm, tn), lambda j, i: (i, j)),
        scratch_shapes=[pltpu.VMEM((k, tn), BF16)],
        compiler_params=_cparams(("arbitrary", "arbitrary")),
        name="ff1",
    )(u2, w1, b1)


def _ff2_kernel(h_ref, w_ref, b_ref, x1_ref, ada_ref, g_ref, bb_ref, o_ref, acc_ref):
    kk = pl.program_id(1)

    @pl.when(kk == 0)
    def _():
        acc_ref[...] = jnp.zeros_like(acc_ref)

    acc_ref[...] += jnp.dot(h_ref[...], w_ref[...], preferred_element_type=F32)

    @pl.when(kk == pl.num_programs(1) - 1)
    def _():
        y2 = acc_ref[...] + b_ref[...]
        g2 = ada_ref[0, 5:6, :]
        o_ref[...] = (_layernorm(ALPHA * x1_ref[...] + g2 * y2) * g_ref[...] + bb_ref[...]).astype(o_ref.dtype)


def _ff2(h, w2, b2, x1, ada3, ln_g, ln_b, seq, out_dtype):
    m, k = h.shape
    tm, tk = 512, 2048
    per_b = seq // tm
    const = lambda i, kk: (0, 0)
    row = pl.BlockSpec((tm, D_MODEL), lambda i, kk: (i, 0))
    return pl.pallas_call(
        _ff2_kernel,
        out_shape=jax.ShapeDtypeStruct((m, D_MODEL), out_dtype),
        grid=(m // tm, k // tk),
        in_specs=[pl.BlockSpec((tm, tk), lambda i, kk: (i, kk)),
                  pl.BlockSpec((tk, D_MODEL), lambda i, kk: (kk, 0)),
                  pl.BlockSpec((1, D_MODEL), const), row,
                  pl.BlockSpec((1, 6, D_MODEL), lambda i, kk: (i // per_b, 0, 0)),
                  pl.BlockSpec((1, D_MODEL), const), pl.BlockSpec((1, D_MODEL), const)],
        out_specs=row,
        scratch_shapes=[pltpu.VMEM((tm, D_MODEL), F32)],
        compiler_params=_cparams(("arbitrary", "arbitrary")),
        name="ff2",
    )(h, w2, b2, x1, ada3, ln_g, ln_b)


def _pad_rows(w, height):
    return jnp.pad(w, ((0, height - w.shape[0]), (0, 0)))


def _win_regroup_kernel(w_ref, rw_ref, ml_ref, gt_ref):
    tr = w_ref.shape[0]

    def put(dst, d0, s0, n):
        dst[:, d0:d0 + n] = w_ref[:, s0:s0 + n].astype(dst.dtype)

    def zero(dst, d0, n):
        dst[:, d0:d0 + n] = jnp.zeros((tr, n), dst.dtype)

    c = D_RWKV
    s_wd = 3 * c
    s_ad = s_wd + DECAY_LORA
    s_gd = s_ad + A_LORA
    s_q = s_gd + GATE_LORA
    s_v = s_q + 2 * D_QK
    s_if = s_v + D_MLSTM
    s_o = s_if + 2 * MLSTM_HEADS
    s_gate = s_o + D_MLSTM
    put(rw_ref, 0, 0, 3 * c)
    put(rw_ref, 3 * c, s_gd, GATE_LORA)
    d_wd = 3 * c + GATE_LORA
    put(rw_ref, d_wd, s_wd, DECAY_LORA)
    zero(rw_ref, d_wd + DECAY_LORA, LORA_PAD - DECAY_LORA)
    put(rw_ref, d_wd + LORA_PAD, s_ad, A_LORA)
    zero(rw_ref, d_wd + LORA_PAD + A_LORA, LORA_PAD - A_LORA)
    put(ml_ref, 0, s_q, 2 * D_QK + D_MLSTM)
    put(ml_ref, 2 * D_QK + D_MLSTM, s_o, D_MLSTM)
    d_if = 2 * D_QK + 2 * D_MLSTM
    put(ml_ref, d_if, s_if, 2 * MLSTM_HEADS)
    zero(ml_ref, d_if + 2 * MLSTM_HEADS, ML_GATE_PAD - 2 * MLSTM_HEADS)
    put(gt_ref, 0, s_gate, 2 * D_MODEL)


def _win_regroup(w_in):
    k, n = w_in.shape
    assert n == 3 * D_RWKV + DECAY_LORA + A_LORA + GATE_LORA + 2 * D_QK + 2 * D_MLSTM + 2 * MLSTM_HEADS + 2 * D_MODEL
    tr = 128
    widths = (RW_COLS, ML_COLS, 2 * D_MODEL)
    return pl.pallas_call(
        _win_regroup_kernel,
        out_shape=[jax.ShapeDtypeStruct((k, w), BF16) for w in widths],
        grid=(k // tr,),
        in_specs=[pl.BlockSpec((tr, n), lambda i: (i, 0))],
        out_specs=[pl.BlockSpec((tr, w), lambda i: (i, 0)) for w in widths],
        compiler_params=_cparams(("arbitrary",)),
        name="win_regroup",
    )(w_in)


def _layer(x2, c, batch, seq, w_ada, b_ada, w_in, mu_shift, w0, w_decay_up, a0, w_a_up, w_g_up, k_k, k_a, r_k,
           gn_g, gn_b, w_conv, b_conv, b_igate, b_fgate, w_branch_a, w_branch_b, w_out,
           ln1_g, ln1_b, w_ff1, b_ff1, w_ff2, b_ff2, ln2_g, ln2_b):
    row = lambda v: v.reshape(1, -1)

    ada3 = _ada(c, w_ada, b_ada).reshape(batch, 6, D_MODEL)
    u = _lnmod(x2, ada3, seq)

    w_rw, w_ml, w_gt = _win_regroup(w_in)
    z_rw = _matmul(u, w_rw, 2048, 512, F32, "win_rwkv")
    z_ml = _matmul(u, w_ml, 2048, 640, F32, "win_mlstm")
    z_gt = _matmul(u, w_gt, 1024, 1024, BF16, "win_gate")

    cc = D_RWKV
    mu = mu_shift
    mu_rw = jnp.concatenate([mu[:3 * cc], mu[3 * cc + DECAY_LORA + A_LORA:],
                             jnp.pad(mu[3 * cc:3 * cc + DECAY_LORA], (0, LORA_PAD - DECAY_LORA)),
                             jnp.pad(mu[3 * cc + DECAY_LORA:3 * cc + DECAY_LORA + A_LORA], (0, LORA_PAD - A_LORA))])
    prep = _rwkv_prep(z_rw, row(mu_rw), row(w0), row(a0), row(k_k), row(k_a), row(r_k),
                      _pad_rows(w_decay_up, LORA_PAD), _pad_rows(w_a_up, LORA_PAD), w_g_up, batch, seq)
    r, ld, kt, v, kkn, bb, g, bonus = prep
    y0, q, a_mat, c_mat = _rwkv_chunk(r, ld, kt, v, kkn, bb, batch, seq)
    ya = _rwkv_scan(y0, q, a_mat, c_mat, g, bonus, row(gn_g), row(gn_b), batch, seq)

    gbias = jnp.pad(jnp.concatenate([b_igate, b_fgate]), (0, ML_GATE_PAD - 2 * MLSTM_HEADS))
    yb = _mlstm(z_ml, w_conv, row(b_conv), row(gbias), batch, seq)

    merged = _merge(ya, yb, z_gt, w_branch_a.astype(BF16), w_branch_b.astype(BF16))
    x1, u2 = _outproj(merged, w_out.astype(BF16), x2, ada3, row(ln1_g), row(ln1_b), seq)

    h = _ff1(u2, w_ff1, row(b_ff1))
    return _ff2(h, w_ff2.astype(BF16), row(b_ff2), x1, ada3, row(ln2_g), row(ln2_b), seq, F32)


def kernel(x, c, w_ada, b_ada, w_in, mu_shift, w0, w_decay_up, a0, w_a_up, w_g_up, k_k, k_a, r_k, gn_g, gn_b,
           w_conv, b_conv, b_igate, b_fgate, w_branch_a, w_branch_b, w_out, ln1_g, ln1_b, w_ff1, b_ff1,
           w_ff2, b_ff2, ln2_g, ln2_b):
    out_dtype = x.dtype
    batch, seq, d = x.shape
    assert d == D_MODEL and seq % 1024 == 0 and w_ada.shape[0] == DEPTH
    x2 = x.astype(F32).reshape(batch * seq, d)
    cf = c.astype(F32)
    for l in range(DEPTH):
        x2 = _layer(x2, cf, batch, seq, w_ada[l], b_ada[l], w_in[l], mu_shift[l], w0[l], w_decay_up[l], a0[l],
                    w_a_up[l], w_g_up[l], k_k[l], k_a[l], r_k[l].reshape(-1), gn_g[l], gn_b[l], w_conv[l],
                    b_conv[l], b_igate[l], b_fgate[l], w_branch_a[l], w_branch_b[l], w_out[l], ln1_g[l],
                    ln1_b[l], w_ff1[l], b_ff1[l], w_ff2[l], b_ff2[l], ln2_g[l], ln2_b[l])
    return x2.reshape(batch, seq, d).astype(out_dtype)
```

```python
import functools

import jax
import jax.numpy as jnp
from jax import lax
from jax.experimental import pallas as pl
from jax.experimental.pallas import tpu as pltpu

F32 = jnp.float32
BF16 = jnp.bfloat16

D_MODEL = 2048
DEPTH = 1
D_RWKV = D_MODEL // 2
RWKV_HEAD = 64
DECAY_LORA = 96
A_LORA = 96
GATE_LORA = 256
RWKV_GN_EPS = 64e-5
D_MLSTM = D_MODEL // 2
MLSTM_HEADS = 4
MLSTM_DV = D_MLSTM // MLSTM_HEADS
MLSTM_DQK = MLSTM_DV // 2
D_QK = MLSTM_HEADS * MLSTM_DQK
CONV_K = 4
D_FF = 4 * D_MODEL
LN_EPS = 1e-5
ALPHA = (2.0 * DEPTH) ** 0.25

LANES = 128
LORA_PAD = 128
N_RWKV_COLS = 3 * D_RWKV + DECAY_LORA + A_LORA + GATE_LORA
N_MLSTM_COLS = 2 * D_QK + 2 * D_MLSTM + 2 * MLSTM_HEADS
RW_COLS = 3584
ML_GATE_PAD = 128
ML_COLS = 2 * D_QK + 2 * D_MLSTM
RWKV_CHUNK = 64
RWKV_TB = 256
RWKV_CORE_TB = 1024
MLSTM_CHUNK = 256
EPILOGUE_ROWS = 256
VMEM_LIMIT = 56 * 1024 * 1024

NN = (((1,), (0,)), ((), ()))
NT = (((1,), (1,)), ((), ()))
TN = (((0,), (0,)), ((), ()))


def _cparams(sem):
    return pltpu.CompilerParams(dimension_semantics=sem, vmem_limit_bytes=VMEM_LIMIT)


def _bf16_parts(x, n):
    if x.dtype == BF16:
        return [x]
    parts, rem = [], x
    for i in range(n):
        p = rem.astype(BF16)
        parts.append(p)
        if i + 1 < n:
            rem = rem - p.astype(F32)
    return parts


def _mdot(a, b, na=1, nb=1, dims=NN):
    ap, bp = _bf16_parts(a, na), _bf16_parts(b, nb)
    order = max(len(ap), len(bp))
    acc = None
    for i, x in enumerate(ap):
        for j, y in enumerate(bp):
            if i + j < order:
                t = lax.dot_general(x, y, dims, preferred_element_type=F32)
                acc = t if acc is None else acc + t
    return acc


def _sigmoid(x):
    return 1.0 / (1.0 + jnp.exp(-x))


def _softplus(x):
    return jnp.maximum(x, 0.0) + jnp.log(1.0 + jnp.exp(-jnp.abs(x)))


def _layernorm(x):
    mu = jnp.mean(x, axis=-1, keepdims=True)
    xc = x - mu
    var = jnp.mean(xc * xc, axis=-1, keepdims=True)
    return xc * lax.rsqrt(var + LN_EPS)


def _iota2(shape, dim):
    return lax.broadcasted_iota(jnp.int32, shape, dim)


def _head_block_ones(n, head):
    r, c = _iota2((n, n), 0), _iota2((n, n), 1)
    return jnp.where((r // head) == (c // head), 1.0, 0.0).astype(BF16)


def _head_sum(x, ones_blk):
    return _mdot(x, ones_blk, na=2, nb=1)


def _ada_kernel(ct_ref, w_ref, b_ref, o_ref):
    tn = w_ref.shape[1]

    def body(i, acc):
        a0, a1 = acc
        r0 = pl.multiple_of(i * 8, 8)
        wv = w_ref[pl.ds(r0, 8), :]
        cv = ct_ref[pl.ds(r0, 8), :]
        sv = cv * _sigmoid(cv)
        return a0 + sv[:, 0:1] * wv, a1 + sv[:, 1:2] * wv

    zero = jnp.zeros((8, tn), F32)
    a0, a1 = lax.fori_loop(0, w_ref.shape[0] // 8, body, (zero, zero), unroll=8)
    o_ref[0:1, :] = jnp.sum(a0, axis=0, keepdims=True) + b_ref[...]
    o_ref[1:2, :] = jnp.sum(a1, axis=0, keepdims=True) + b_ref[...]


def _ada(c, w_ada, b_ada):
    batch = c.shape[0]
    assert batch == 2
    n = w_ada.shape[1]
    tn = 1536
    return pl.pallas_call(
        _ada_kernel,
        out_shape=jax.ShapeDtypeStruct((batch, n), F32),
        grid=(n // tn,),
        in_specs=[pl.BlockSpec((D_MODEL, batch), lambda j: (0, 0)),
                  pl.BlockSpec((D_MODEL, tn), lambda j: (0, j)),
                  pl.BlockSpec((1, tn), lambda j: (0, j))],
        out_specs=pl.BlockSpec((batch, tn), lambda j: (0, j)),
        compiler_params=_cparams(("arbitrary",)),
        name="ada",
    )(c.T, w_ada, b_ada.reshape(1, n))


def _lnmod_kernel(x_ref, ada_ref, o_ref):
    xn = _layernorm(x_ref[...])
    sh, sc = ada_ref[0, 0:1, :], ada_ref[0, 1:2, :]
    o_ref[...] = (xn * (1.0 + sc) + sh).astype(o_ref.dtype)


def _lnmod(x2, ada3, seq):
    m = x2.shape[0]
    tb = 512
    per_b = seq // tb
    return pl.pallas_call(
        _lnmod_kernel,
        out_shape=jax.ShapeDtypeStruct((m, D_MODEL), BF16),
        grid=(m // tb,),
        in_specs=[pl.BlockSpec((tb, D_MODEL), lambda i: (i, 0)),
                  pl.BlockSpec((1, 6, D_MODEL), lambda i: (i // per_b, 0, 0))],
        out_specs=pl.BlockSpec((tb, D_MODEL), lambda i: (i, 0)),
        compiler_params=_cparams(("arbitrary",)),
        name="lnmod",
    )(x2, ada3)


def _win_kernel(x_ref, wt_ref, o_ref, wb_ref):
    @pl.when(pl.program_id(1) == 0)
    def _():
        wb_ref[...] = wt_ref[...].astype(wb_ref.dtype)

    o_ref[...] = lax.dot_general(x_ref[...], wb_ref[...], NT, preferred_element_type=F32).astype(o_ref.dtype)


def _win_matmul(x, wt, col0, n_out, tm, tn, out_dtype, name, skip=None):
    m, k = x.shape
    skip_tile, gap = skip if skip else (n_out // tn, 0)
    assert col0 % 8 == 0 and gap % 8 == 0 and n_out % tn == 0 and col0 + n_out + gap <= wt.shape[0]

    def w_index(j, i):
        return pl.multiple_of(col0 + j * tn + jnp.where(j >= skip_tile, gap, 0), 8), 0

    return pl.pallas_call(
        _win_kernel,
        out_shape=jax.ShapeDtypeStruct((m, n_out), out_dtype),
        grid=(n_out // tn, m // tm),
        in_specs=[pl.BlockSpec((tm, k), lambda j, i: (i, 0)),
                  pl.BlockSpec((pl.Element(tn), pl.Element(k)), w_index)],
        out_specs=pl.BlockSpec((tm, tn), lambda j, i: (i, j)),
        scratch_shapes=[pltpu.VMEM((tn, k), BF16)],
        compiler_params=_cparams(("arbitrary", "arbitrary")),
        name=name,
    )(x, wt)


def _rwkv_prep_kernel(z_ref, mu_ref, w0_ref, a0_ref, kk_ref, ka_ref, rk_ref, wdu_ref, wau_ref, wgu_ref,
                      r_o, ld_o, kt_o, v_o, kkn_o, bb_o, g_o, bonus_o, carry_ref):
    tb = z_ref.shape[0]

    @pl.when(pl.program_id(1) == 0)
    def _():
        carry_ref[...] = jnp.zeros_like(carry_ref)

    z = z_ref[...]
    prev = pltpu.roll(z, 1, 0)
    prev = jnp.where(_iota2(z.shape, 0) == 0, carry_ref[0:1, :], prev)
    carry_ref[0:1, :] = z[tb - 1:tb, :]
    zs = z + (prev - z) * mu_ref[...]

    c = D_RWKV
    r, k, v = zs[:, 0:c], zs[:, c:2 * c], zs[:, 2 * c:3 * c]
    wd = zs[:, 3 * c:3 * c + LORA_PAD]
    ad = zs[:, 3 * c + DECAY_LORA:3 * c + DECAY_LORA + LORA_PAD]
    gd = zs[:, 3 * c + DECAY_LORA + A_LORA:3 * c + DECAY_LORA + A_LORA + GATE_LORA]

    logw = -_softplus(-(w0_ref[...] + _mdot(jnp.tanh(wd), wdu_ref[...], 2, 2))) - 0.5
    ld = -jnp.exp(logw)
    a = _sigmoid(a0_ref[...] + _mdot(ad, wau_ref[...], 2, 2))
    g = _mdot(_sigmoid(gd), wgu_ref[...], 1, 1)

    ones_blk = _head_block_ones(LANES, RWKV_HEAD)
    kq = k * kk_ref[...]
    kt = k * (1.0 + (a - 1.0) * ka_ref[...])
    sq = kq * kq
    bn = r * kt * rk_ref[...]
    ss = jnp.concatenate([_head_sum(sq[:, s:s + LANES], ones_blk) for s in range(0, c, LANES)], axis=1)
    bs = jnp.concatenate([_head_sum(bn[:, s:s + LANES], ones_blk) for s in range(0, c, LANES)], axis=1)
    kkn = kq * lax.rsqrt(ss + 1e-12)

    r_o[...] = r.astype(r_o.dtype)
    ld_o[...] = ld
    kt_o[...] = kt.astype(kt_o.dtype)
    v_o[...] = v.astype(v_o.dtype)
    kkn_o[...] = kkn.astype(kkn_o.dtype)
    bb_o[...] = (a * kkn).astype(bb_o.dtype)
    g_o[...] = g.astype(g_o.dtype)
    bonus_o[...] = (bs * v).astype(bonus_o.dtype)


def _rwkv_prep(z_rw, mu_rw, w0, a0, k_k, k_a, r_k, wdu, wau, wgu, batch, seq):
    m = z_rw.shape[0]
    tb = RWKV_TB
    per_b = seq // tb
    row = lambda b, j: (b * per_b + j, 0)
    const = lambda b, j: (0, 0)
    vec = pl.BlockSpec((1, D_RWKV), const)
    outs = [jax.ShapeDtypeStruct((m, D_RWKV), F32 if i == 1 else BF16) for i in range(8)]
    return pl.pallas_call(
        _rwkv_prep_kernel,
        out_shape=outs,
        grid=(batch, per_b),
        in_specs=[pl.BlockSpec((tb, RW_COLS), row), pl.BlockSpec((1, RW_COLS), const),
                  vec, vec, vec, vec, vec,
                  pl.BlockSpec((LORA_PAD, D_RWKV), const), pl.BlockSpec((LORA_PAD, D_RWKV), const),
                  pl.BlockSpec((GATE_LORA, D_RWKV), const)],
        out_specs=[pl.BlockSpec((tb, D_RWKV), row)] * 8,
        scratch_shapes=[pltpu.VMEM((8, RW_COLS), F32)],
        compiler_params=_cparams(("arbitrary", "arbitrary")),
        name="rwkv_prep",
    )(z_rw, mu_rw, w0, a0, k_k, k_a, r_k, wdu, wau, wgu)


def _rwkv_chunk_kernel(r_ref, ld_ref, kt_ref, v_ref, kk_ref, bb_ref, y0_ref, q_ref, a_ref, c_ref):
    tb = r_ref.shape[0]
    lc = RWKV_CHUNK
    n = 2 * lc
    nchunk = tb // lc

    lane = _iota2((1, LANES), 1)
    m0 = jnp.where(lane < RWKV_HEAD, 1.0, 0.0)
    m1 = 1.0 - m0
    ri, ci = _iota2((n, n), 0), _iota2((n, n), 1)
    same_head = (ri // lc) == (ci // lc)
    strict = jnp.where(same_head & (ri > ci), 1.0, 0.0)
    incl = jnp.where(same_head & (ri >= ci), 1.0, 0.0)
    eye = jnp.where(ri == ci, 1.0, 0.0)

    tw = 4 * lc
    rt, ct = _iota2((tw, tw), 0), _iota2((tw, tw), 1)
    tri = jnp.where(((rt // lc) == (ct // lc)) & (ct <= rt), 1.0, 0.0).astype(BF16)

    ld = ld_ref[...]
    cl = jnp.concatenate([_mdot(tri, ld[i:i + tw], 1, 3) for i in range(0, tb, tw)], axis=0)
    e_pos = jnp.exp(cl)
    e_neg = jnp.exp(-cl)
    a_hat = kk_ref[...] * jnp.exp(cl - ld)
    b_hat = bb_ref[...] * e_neg
    k_hat = kt_ref[...] * e_neg
    r_hat = r_ref[...] * e_pos
    v_all = v_ref[...]

    def stack(x):
        return jnp.concatenate([x * m0, x * m1], axis=0)

    def bf(t):
        return t.astype(BF16)

    chunks = range(nchunk)
    st = [[stack(x[c * lc:(c + 1) * lc]) for x in (a_hat, b_hat, k_hat, r_hat, v_all)] for c in chunks]
    a_st, b_st, k_st, r_st, v_st = ([st[c][i] for c in chunks] for i in range(5))
    b_b, k_b, v_b = ([bf(t) for t in ts] for ts in (b_st, k_st, v_st))
    p = [_mdot(jnp.concatenate([a_st[c], r_st[c]], axis=0),
               jnp.concatenate([b_b[c], k_b[c]], axis=0), dims=NT) for c in chunks]
    mab = [p[c][0:n, 0:n] * strict for c in chunks]
    mab_b = [bf(t) for t in mab]
    mak_b = [bf(p[c][0:n, n:] * strict) for c in chunks]
    mrb_b = [bf(p[c][n:, 0:n] * incl) for c in chunks]
    mrk_b = [bf(p[c][n:, n:] * incl) for c in chunks]

    x = [eye - jnp.where((ri // 2 == ci // 2), mab[c], 0.0) for c in chunks]
    w1 = [_mdot(mak_b[c], v_b[c]) for c in chunks]
    zero_b = jnp.zeros((), BF16)
    s = 2
    while s < lc:
        lvl = (ri // (2 * s) == ci // (2 * s)) & (ri // s != ci // s)
        x_b = [bf(t) for t in x]
        nx = [_mdot(jnp.where(lvl, mab_b[c], zero_b), x_b[c]) for c in chunks]
        x = [x[c] - _mdot(x_b[c], nx[c]) for c in chunks]
        s *= 2

    ta = [_mdot(x[c], jnp.concatenate([w1[c], a_st[c]], axis=1)) for c in chunks]
    zeros_b = jnp.zeros((n, LANES), BF16)
    ugv_b = [jnp.concatenate([bf(jnp.concatenate([-ta[c][:, 0:LANES], ta[c][:, LANES:]], axis=1)),
                              jnp.concatenate([v_b[c], zeros_b], axis=1)], axis=0) for c in chunks]
    ru = [_mdot(jnp.concatenate([mrb_b[c], mrk_b[c]], axis=1), ugv_b[c]) for c in chunks]
    y0 = [ru[c][:, 0:LANES] for c in chunks]
    q = [r_st[c] - ru[c][:, LANES:] for c in chunks]
    pl_row = [e_pos[(c + 1) * lc - 1:(c + 1) * lc, :] for c in chunks]
    bk_p = [bf(jnp.concatenate([b_st[c] * pl_row[c], k_st[c] * pl_row[c]], axis=0)) for c in chunks]
    bu = [_mdot(bk_p[c], ugv_b[c], dims=TN) for c in chunks]
    ct = [bu[c][:, 0:LANES] for c in chunks]
    at = [eye * pl_row[c] - bu[c][:, LANES:] for c in chunks]

    for c in chunks:
        rows = slice(c * lc, (c + 1) * lc)
        y0_ref[rows, :] = y0[c][0:lc] + y0[c][lc:]
        q_ref[rows, :] = (q[c][0:lc] + q[c][lc:]).astype(q_ref.dtype)
        a_ref[0, c, 0] = at[c].astype(a_ref.dtype)
        c_ref[0, c, 0] = ct[c]


def _rwkv_chunk(r, ld, kt, v, kkn, bb, batch, seq):
    m = r.shape[0]
    tb = RWKV_CORE_TB
    per_b = seq // tb
    cpb = tb // RWKV_CHUNK
    npair = D_RWKV // LANES
    slab = pl.BlockSpec((tb, LANES), lambda b, p, j: (b * per_b + j, p))
    mat = pl.BlockSpec((1, cpb, 1, LANES, LANES), lambda b, p, j: (b, j, p, 0, 0))
    mshape = (batch, seq // RWKV_CHUNK, npair, LANES, LANES)
    return pl.pallas_call(
        _rwkv_chunk_kernel,
        out_shape=[jax.ShapeDtypeStruct((m, D_RWKV), F32), jax.ShapeDtypeStruct((m, D_RWKV), BF16),
                   jax.ShapeDtypeStruct(mshape, BF16), jax.ShapeDtypeStruct(mshape, F32)],
        grid=(batch, npair, per_b),
        in_specs=[slab] * 6,
        out_specs=[slab, slab, mat, mat],
        compiler_params=_cparams(("arbitrary", "arbitrary", "arbitrary")),
        name="rwkv_chunk",
    )(r, ld, kt, v, kkn, bb)


def _rwkv_scan_kernel(y0_ref, q_ref, a_ref, c_ref, g_ref, bonus_ref, gng_ref, gnb_ref, ya_ref, s_ref):
    nb, rows, _ = y0_ref.shape
    lc = RWKV_CHUNK
    npair = D_RWKV // LANES

    @pl.when(pl.program_id(0) == 0)
    def _():
        s_ref[...] = jnp.zeros_like(s_ref)

    ones_blk = _head_block_ones(LANES, RWKV_HEAD)
    streams = [(b, p) for b in range(nb) for p in range(npair)]
    for ci in range(rows // lc):
        rs = slice(ci * lc, (ci + 1) * lc)
        state_b = [s_ref[b * npair + p].astype(BF16) for b, p in streams]
        y = [y0_ref[b, rs, p * LANES:(p + 1) * LANES]
             + _mdot(q_ref[b, rs, p * LANES:(p + 1) * LANES], state_b[i])
             for i, (b, p) in enumerate(streams)]
        for i, (b, p) in enumerate(streams):
            s_ref[b * npair + p] = _mdot(a_ref[b, ci, p], state_b[i]) + c_ref[b, ci, p]
        mu = [_head_sum(t, ones_blk) * (1.0 / RWKV_HEAD) for t in y]
        yc = [y[i] - mu[i] for i in range(len(streams))]
        var = [_head_sum(t * t, ones_blk) * (1.0 / RWKV_HEAD) for t in yc]
        for i, (b, p) in enumerate(streams):
            ls = slice(p * LANES, (p + 1) * LANES)
            yn = yc[i] * lax.rsqrt(var[i] + RWKV_GN_EPS) * gng_ref[:, ls] + gnb_ref[:, ls]
            ya_ref[b, rs, ls] = ((yn + bonus_ref[b, rs, ls]) * g_ref[b, rs, ls]).astype(ya_ref.dtype)


def _rwkv_scan(y0, q, a_mat, c_mat, g, bonus, gn_g, gn_b, batch, seq):
    rows = RWKV_CHUNK
    npair = D_RWKV // LANES
    tok = pl.BlockSpec((batch, rows, D_RWKV), lambda j: (0, j, 0))
    mat = pl.BlockSpec((batch, rows // RWKV_CHUNK, npair, LANES, LANES), lambda j: (0, j, 0, 0, 0))
    vec = pl.BlockSpec((1, D_RWKV), lambda j: (0, 0))
    shape3 = (batch, seq, D_RWKV)
    return pl.pallas_call(
        _rwkv_scan_kernel,
        out_shape=jax.ShapeDtypeStruct(shape3, BF16),
        grid=(seq // rows,),
        in_specs=[tok, tok, mat, mat, tok, tok, vec, vec],
        out_specs=tok,
        scratch_shapes=[pltpu.VMEM((batch * npair, LANES, LANES), F32)],
        compiler_params=_cparams(("arbitrary",)),
        name="rwkv_scan",
    )(y0.reshape(shape3), q.reshape(shape3), a_mat, c_mat, g.reshape(shape3), bonus.reshape(shape3),
      gn_g, gn_b).reshape(batch * seq, D_RWKV)


def _mlstm_kernel(z_ref, zif_ref, wconv_ref, bconv_ref, gbias_ref, yb_ref, tail_ref, ct_ref, n_ref, m_ref):
    lc = z_ref.shape[0]
    dqk, dv, nh = MLSTM_DQK, MLSTM_DV, MLSTM_HEADS

    @pl.when(pl.program_id(1) == 0)
    def _():
        tail_ref[...] = jnp.zeros_like(tail_ref)
        ct_ref[...] = jnp.zeros_like(ct_ref)
        n_ref[...] = jnp.zeros_like(n_ref)
        m_ref[...] = jnp.zeros_like(m_ref)

    zqk = z_ref[:, 0:2 * D_QK]
    ext = jnp.concatenate([tail_ref[...], zqk], axis=0)
    tail_ref[...] = zqk[lc - 8:lc, :]
    conv = bconv_ref[...] + wconv_ref[CONV_K - 1:CONV_K, :] * zqk
    for d in range(1, CONV_K):
        conv = conv + wconv_ref[CONV_K - 1 - d:CONV_K - d, :] * ext[8 - d:8 - d + lc, :]
    qk = conv * _sigmoid(conv)
    q_all = qk[:, 0:D_QK]
    k_all = qk[:, D_QK:] * (dqk ** -0.5)
    v_all = z_ref[:, 2 * D_QK:2 * D_QK + D_MLSTM]
    o_all = z_ref[:, 2 * D_QK + D_MLSTM:]

    gz = zif_ref[...] + gbias_ref[...]
    lf_col = jnp.minimum(gz, 0.0) - jnp.log(1.0 + jnp.exp(-jnp.abs(gz)))
    rr, cc = _iota2((lc, lc), 0), _iota2((lc, lc), 1)
    causal = cc <= rr
    tri = jnp.where(causal, 1.0, 0.0).astype(BF16)
    b_col = _mdot(tri, lf_col, 1, 3)
    sel = jnp.where(_iota2((8, LANES), 0) == _iota2((8, LANES), 1), 1.0, 0.0).astype(BF16)
    g_row = _mdot(sel, gz, 1, 3, NT)
    lf_row = jnp.minimum(g_row, 0.0) - jnp.log(1.0 + jnp.exp(-jnp.abs(g_row)))
    b_row = _mdot(lf_row, tri, 3, 1, NT)

    heads = range(nh)
    qf = [q_all[:, h * dqk:(h + 1) * dqk] for h in heads]
    q = [t.astype(BF16) for t in qf]
    k = [k_all[:, h * dqk:(h + 1) * dqk] for h in heads]
    v = [v_all[:, h * dv:(h + 1) * dv].astype(BF16) for h in heads]
    bc = [b_col[:, nh + h:nh + h + 1] for h in heads]
    ic = [gz[:, h:h + 1] for h in heads]
    br = [b_row[nh + h:nh + h + 1, :] for h in heads]
    ir = [g_row[h:h + 1, :] for h in heads]
    g_tot = [b_col[lc - 1:lc, nh + h:nh + h + 1] for h in heads]
    m_prev = [m_ref[0:1, h:h + 1] for h in heads]
    ct = [ct_ref[h] for h in heads]
    nvec = [n_ref[h:h + 1, :] for h in heads]

    qk = [_mdot(q[h], k[h], dims=NT) for h in heads]
    qc = [_mdot(q[h], ct[h]) for h in heads]
    dm = [jnp.where(causal, bc[h] - br[h] + ir[h], -jnp.inf) for h in heads]
    inter = [bc[h] + m_prev[h] for h in heads]
    mt = [jnp.maximum(inter[h], jnp.max(dm[h], axis=-1, keepdims=True)) for h in heads]
    sc = [qk[h] * jnp.exp(dm[h] - mt[h]) for h in heads]
    winter = [jnp.exp(inter[h] - mt[h]) for h in heads]
    sv = [_mdot(sc[h], v[h]) for h in heads]
    m_new = [jnp.maximum(g_tot[h] + m_prev[h], jnp.max(g_tot[h] - br[h] + ir[h], axis=-1, keepdims=True))
             for h in heads]
    kw = [k[h] * jnp.exp(g_tot[h] - bc[h] + ic[h] - m_new[h]) for h in heads]
    kv = [_mdot(kw[h], v[h], dims=TN) for h in heads]
    for h in heads:
        num = winter[h] * qc[h] + sv[h]
        den = (winter[h] * jnp.sum(qf[h] * nvec[h], axis=-1, keepdims=True)
               + jnp.sum(sc[h], axis=-1, keepdims=True))
        hout = num / jnp.maximum(jnp.abs(den), jnp.exp(-mt[h]))
        wc = jnp.exp(g_tot[h] + m_prev[h] - m_new[h])
        ct_ref[h] = wc * ct[h] + kv[h]
        n_ref[h:h + 1, :] = wc * nvec[h] + jnp.sum(kw[h], axis=0, keepdims=True)
        m_ref[0:1, h:h + 1] = m_new[h]
        o = o_all[:, h * dv:(h + 1) * dv]
        yb_ref[:, h * dv:(h + 1) * dv] = (_sigmoid(o) * hout).astype(yb_ref.dtype)


def _mlstm(z_ml, z_if, w_conv, b_conv, gbias, batch, seq):
    m = z_ml.shape[0]
    lc = MLSTM_CHUNK
    per_b = seq // lc
    const = lambda b, j: (0, 0)
    return pl.pallas_call(
        _mlstm_kernel,
        out_shape=jax.ShapeDtypeStruct((m, D_MLSTM), BF16),
        grid=(batch, per_b),
        in_specs=[pl.BlockSpec((lc, ML_COLS), lambda b, j: (b * per_b + j, 0)),
                  pl.BlockSpec((lc, ML_GATE_PAD), lambda b, j: (b * per_b + j, 0)),
                  pl.BlockSpec((CONV_K, 2 * D_QK), const),
                  pl.BlockSpec((1, 2 * D_QK), const),
                  pl.BlockSpec((1, ML_GATE_PAD), const)],
        out_specs=pl.BlockSpec((lc, D_MLSTM), lambda b, j: (b * per_b + j, 0)),
        scratch_shapes=[pltpu.VMEM((8, 2 * D_QK), F32),
                        pltpu.VMEM((MLSTM_HEADS, MLSTM_DQK, MLSTM_DV), F32),
                        pltpu.VMEM((8, MLSTM_DQK), F32),
                        pltpu.VMEM((8, LANES), F32)],
        compiler_params=_cparams(("arbitrary", "arbitrary")),
        name="mlstm",
    )(z_ml, z_if, w_conv, b_conv, gbias)


def _merge_kernel(ya_ref, yb_ref, zg_ref, wa_ref, wb_ref, o_ref):
    groups = [slice(i, i + EPILOGUE_ROWS) for i in range(0, ya_ref.shape[0], EPILOGUE_ROWS)]
    pab = [(jnp.dot(ya_ref[rows, :], wa_ref[...], preferred_element_type=F32),
            jnp.dot(yb_ref[rows, :], wb_ref[...], preferred_element_type=F32)) for rows in groups]
    for (pa, pb), rows in zip(pab, groups):
        ga = _sigmoid(zg_ref[rows, 0:D_MODEL].astype(F32))
        gb = _sigmoid(zg_ref[rows, D_MODEL:].astype(F32))
        o_ref[rows, :] = (ga * pa + gb * pb).astype(o_ref.dtype)


def _merge(ya, yb, zg, wa, wb):
    m = ya.shape[0]
    tm = 512
    const = lambda i: (0, 0)
    return pl.pallas_call(
        _merge_kernel,
        out_shape=jax.ShapeDtypeStruct((m, D_MODEL), BF16),
        grid=(m // tm,),
        in_specs=[pl.BlockSpec((tm, D_RWKV), lambda i: (i, 0)),
                  pl.BlockSpec((tm, D_MLSTM), lambda i: (i, 0)),
                  pl.BlockSpec((tm, 2 * D_MODEL), lambda i: (i, 0)),
                  pl.BlockSpec((D_RWKV, D_MODEL), const),
                  pl.BlockSpec((D_MLSTM, D_MODEL), const)],
        out_specs=pl.BlockSpec((tm, D_MODEL), lambda i: (i, 0)),
        compiler_params=_cparams(("arbitrary",)),
        name="merge",
    )(ya, yb, zg, wa, wb)


def _outproj_kernel(mg_ref, w_ref, x_ref, ada_ref, g_ref, b_ref, x1_ref, u2_ref):
    groups = [slice(i, i + EPILOGUE_ROWS) for i in range(0, mg_ref.shape[0], EPILOGUE_ROWS)]
    ys = [jnp.dot(mg_ref[rows, :], w_ref[...], preferred_element_type=F32) for rows in groups]
    g1 = ada_ref[0, 2:3, :]
    sh2, sc2 = ada_ref[0, 3:4, :], ada_ref[0, 4:5, :]
    for y, rows in zip(ys, groups):
        x1 = _layernorm(ALPHA * x_ref[rows, :] + g1 * y) * g_ref[...] + b_ref[...]
        x1_ref[rows, :] = x1
        u2_ref[rows, :] = (_layernorm(x1) * (1.0 + sc2) + sh2).astype(u2_ref.dtype)


def _outproj(merged, w_out, x2, ada3, ln_g, ln_b, seq):
    m = merged.shape[0]
    tm = 512
    per_b = seq // tm
    const = lambda i: (0, 0)
    row = pl.BlockSpec((tm, D_MODEL), lambda i: (i, 0))
    return pl.pallas_call(
        _outproj_kernel,
        out_shape=[jax.ShapeDtypeStruct((m, D_MODEL), F32), jax.ShapeDtypeStruct((m, D_MODEL), BF16)],
        grid=(m // tm,),
        in_specs=[row, pl.BlockSpec((D_MODEL, D_MODEL), const, pipeline_mode=pl.Buffered(1)), row,
                  pl.BlockSpec((1, 6, D_MODEL), lambda i: (i // per_b, 0, 0)),
                  pl.BlockSpec((1, D_MODEL), const), pl.BlockSpec((1, D_MODEL), const)],
        out_specs=[row, row],
        compiler_params=_cparams(("arbitrary",)),
        name="outproj",
    )(merged, w_out, x2, ada3, ln_g, ln_b)


def _ff1_kernel(x_ref, w_ref, b_ref, o_ref, wb_ref):
    @pl.when(pl.program_id(1) == 0)
    def _():
        wb_ref[...] = w_ref[...].astype(wb_ref.dtype)

    h = jnp.dot(x_ref[...], wb_ref[...], preferred_element_type=F32) + b_ref[...]
    h = jnp.maximum(h, 0.0)
    o_ref[...] = (h * h).astype(o_ref.dtype)


def _ff1(u2, w1, b1):
    m, k = u2.shape
    n = w1.shape[1]
    tm, tn = 1024, 1024
    return pl.pallas_call(
        _ff1_kernel,
        out_shape=jax.ShapeDtypeStruct((m, n), BF16),
        grid=(n // tn, m // tm),
        in_specs=[pl.BlockSpec((tm, k), lambda j, i: (i, 0)),
                  pl.BlockSpec((k, tn), lambda j, i: (0, j)),
                  pl.BlockSpec((1, tn), lambda j, i: (0, j))],
        out_specs=pl.BlockSpec((tm, tn), lambda j, i: (i, j)),
        scratch_shapes=[pltpu.VMEM((k, tn), BF16)],
        compiler_params=_cparams(("arbitrary", "arbitrary")),
        name="ff1",
    )(u2, w1, b1)


def _ff2_kernel(h_ref, w_ref, b_ref, x1_ref, ada_ref, g_ref, bb_ref, o_ref, acc_ref):
    kk = pl.program_id(1)

    @pl.when(kk == 0)
    def _():
        acc_ref[...] = jnp.zeros_like(acc_ref)

    acc_ref[...] += jnp.dot(h_ref[...], w_ref[...], preferred_element_type=F32)

    @pl.when(kk == pl.num_programs(1) - 1)
    def _():
        y2 = acc_ref[...] + b_ref[...]
        g2 = ada_ref[0, 5:6, :]
        o_ref[...] = (_layernorm(ALPHA * x1_ref[...] + g2 * y2) * g_ref[...] + bb_ref[...]).astype(o_ref.dtype)


def _ff2(h, w2, b2, x1, ada3, ln_g, ln_b, seq, out_dtype):
    m, k = h.shape
    tm, tk = 512, 2048
    per_b = seq // tm
    const = lambda i, kk: (0, 0)
    row = pl.BlockSpec((tm, D_MODEL), lambda i, kk: (i, 0))
    return pl.pallas_call(
        _ff2_kernel,
        out_shape=jax.ShapeDtypeStruct((m, D_MODEL), out_dtype),
        grid=(m // tm, k // tk),
        in_specs=[pl.BlockSpec((tm, tk), lambda i, kk: (i, kk)),
                  pl.BlockSpec((tk, D_MODEL), lambda i, kk: (kk, 0)),
                  pl.BlockSpec((1, D_MODEL), const), row,
                  pl.BlockSpec((1, 6, D_MODEL), lambda i, kk: (i // per_b, 0, 0)),
                  pl.BlockSpec((1, D_MODEL), const), pl.BlockSpec((1, D_MODEL), const)],
        out_specs=row,
        scratch_shapes=[pltpu.VMEM((tm, D_MODEL), F32)],
        compiler_params=_cparams(("arbitrary", "arbitrary")),
        name="ff2",
    )(h, w2, b2, x1, ada3, ln_g, ln_b)


def _pad_rows(w, height):
    return jnp.pad(w, ((0, height - w.shape[0]), (0, 0)))


def _layer(x2, c, batch, seq, w_ada, b_ada, w_in, mu_shift, w0, w_decay_up, a0, w_a_up, w_g_up, k_k, k_a, r_k,
           gn_g, gn_b, w_conv, b_conv, b_igate, b_fgate, w_branch_a, w_branch_b, w_out,
           ln1_g, ln1_b, w_ff1, b_ff1, w_ff2, b_ff2, ln2_g, ln2_b):
    row = lambda v: v.reshape(1, -1)

    ada3 = _ada(c, w_ada, b_ada).reshape(batch, 6, D_MODEL)
    u = _lnmod(x2, ada3, seq)

    assert w_in.shape == (D_MODEL, N_RWKV_COLS + N_MLSTM_COLS + 2 * D_MODEL)
    w_in_t = jnp.swapaxes(w_in, 0, 1)
    z_rw = _win_matmul(u, w_in_t, 0, RW_COLS, 2048, 512, F32, "win_rwkv")
    col_if = N_RWKV_COLS + 2 * D_QK + D_MLSTM
    z_ml = _win_matmul(u, w_in_t, N_RWKV_COLS, ML_COLS, 1024, 1024, F32, "win_mlstm",
                       skip=((2 * D_QK + D_MLSTM) // 1024, 2 * MLSTM_HEADS))
    z_gt = _win_matmul(u, w_in_t, N_RWKV_COLS + N_MLSTM_COLS, 2 * D_MODEL, 1024, 1024, BF16, "win_gate")

    mu_rw = jnp.pad(mu_shift, (0, RW_COLS - N_RWKV_COLS))
    prep = _rwkv_prep(z_rw, row(mu_rw), row(w0), row(a0), row(k_k), row(k_a), row(r_k),
                      _pad_rows(w_decay_up, LORA_PAD), _pad_rows(w_a_up, LORA_PAD), w_g_up, batch, seq)
    r, ld, kt, v, kkn, bb, g, bonus = prep
    y0, q, a_mat, c_mat = _rwkv_chunk(r, ld, kt, v, kkn, bb, batch, seq)
    ya = _rwkv_scan(y0, q, a_mat, c_mat, g, bonus, row(gn_g), row(gn_b), batch, seq)

    gbias = jnp.pad(jnp.concatenate([b_igate, b_fgate]), (0, ML_GATE_PAD - 2 * MLSTM_HEADS))
    z_if = _win_matmul(u, w_in_t, col_if, ML_GATE_PAD, 2048, ML_GATE_PAD, F32, "win_if")
    yb = _mlstm(z_ml, z_if, w_conv, row(b_conv), row(gbias), batch, seq)

    merged = _merge(ya, yb, z_gt, w_branch_a.astype(BF16), w_branch_b.astype(BF16))
    x1, u2 = _outproj(merged, w_out.astype(BF16), x2, ada3, row(ln1_g), row(ln1_b), seq)

    h = _ff1(u2, w_ff1, row(b_ff1))
    return _ff2(h, w_ff2.astype(BF16), row(b_ff2), x1, ada3, row(ln2_g), row(ln2_b), seq, F32)


def kernel(x, c, w_ada, b_ada, w_in, mu_shift, w0, w_decay_up, a0, w_a_up, w_g_up, k_k, k_a, r_k, gn_g, gn_b,
           w_conv, b_conv, b_igate, b_fgate, w_branch_a, w_branch_b, w_out, ln1_g, ln1_b, w_ff1, b_ff1,
           w_ff2, b_ff2, ln2_g, ln2_b):
    out_dtype = x.dtype
    batch, seq, d = x.shape
    assert d == D_MODEL and seq % 1024 == 0 and w_ada.shape[0] == DEPTH
    x2 = x.astype(F32).reshape(batch * seq, d)
    cf = c.astype(F32)
    for l in range(DEPTH):
        x2 = _layer(x2, cf, batch, seq, w_ada[l], b_ada[l], w_in[l], mu_shift[l], w0[l], w_decay_up[l], a0[l],
                    w_a_up[l], w_g_up[l], k_k[l], k_a[l], r_k[l].reshape(-1), gn_g[l], gn_b[l], w_conv[l],
                    b_conv[l], b_igate[l], b_fgate[l], w_branch_a[l], w_branch_b[l], w_out[l], ln1_g[l],
                    ln1_b[l], w_ff1[l], b_ff1[l], w_ff2[l], b_ff2[l], ln2_g[l], ln2_b[l])
    return x2.reshape(batch, seq, d).astype(out_dtype)
```

```python
import functools

import jax
import jax.numpy as jnp
from jax import lax
from jax.experimental import pallas as pl
from jax.experimental.pallas import tpu as pltpu

F32 = jnp.float32
BF16 = jnp.bfloat16

D_MODEL = 2048
DEPTH = 1
D_RWKV = D_MODEL // 2
RWKV_HEAD = 64
DECAY_LORA = 96
A_LORA = 96
GATE_LORA = 256
RWKV_GN_EPS = 64e-5
D_MLSTM = D_MODEL // 2
MLSTM_HEADS = 4
MLSTM_DV = D_MLSTM // MLSTM_HEADS
MLSTM_DQK = MLSTM_DV // 2
D_QK = MLSTM_HEADS * MLSTM_DQK
CONV_K = 4
D_FF = 4 * D_MODEL
LN_EPS = 1e-5
ALPHA = (2.0 * DEPTH) ** 0.25

LANES = 128
LORA_PAD = 128
N_RWKV_COLS = 3 * D_RWKV + DECAY_LORA + A_LORA + GATE_LORA
N_MLSTM_COLS = 2 * D_QK + 2 * D_MLSTM + 2 * MLSTM_HEADS
RW_COLS = 3584
ML_GATE_PAD = 128
ML_COLS = 2 * D_QK + 2 * D_MLSTM
RWKV_CHUNK = 64
RWKV_TB = 256
RWKV_SCAN_ROWS = 128
RWKV_CORE_TB = 1024
MLSTM_CHUNK = 256
EPILOGUE_ROWS = 256
VMEM_LIMIT = 56 * 1024 * 1024

NN = (((1,), (0,)), ((), ()))
NT = (((1,), (1,)), ((), ()))
TN = (((0,), (0,)), ((), ()))


def _cparams(sem):
    return pltpu.CompilerParams(dimension_semantics=sem, vmem_limit_bytes=VMEM_LIMIT)


def _bf16_parts(x, n):
    if x.dtype == BF16:
        return [x]
    parts, rem = [], x
    for i in range(n):
        p = rem.astype(BF16)
        parts.append(p)
        if i + 1 < n:
            rem = rem - p.astype(F32)
    return parts


def _mdot(a, b, na=1, nb=1, dims=NN):
    ap, bp = _bf16_parts(a, na), _bf16_parts(b, nb)
    order = max(len(ap), len(bp))
    acc = None
    for i, x in enumerate(ap):
        for j, y in enumerate(bp):
            if i + j < order:
                t = lax.dot_general(x, y, dims, preferred_element_type=F32)
                acc = t if acc is None else acc + t
    return acc


def _sigmoid(x):
    return 1.0 / (1.0 + jnp.exp(-x))


def _softplus(x):
    return jnp.maximum(x, 0.0) + jnp.log(1.0 + jnp.exp(-jnp.abs(x)))


def _layernorm(x):
    mu = jnp.mean(x, axis=-1, keepdims=True)
    xc = x - mu
    var = jnp.mean(xc * xc, axis=-1, keepdims=True)
    return xc * lax.rsqrt(var + LN_EPS)


def _iota2(shape, dim):
    return lax.broadcasted_iota(jnp.int32, shape, dim)


def _head_block_ones(n, head):
    r, c = _iota2((n, n), 0), _iota2((n, n), 1)
    return jnp.where((r // head) == (c // head), 1.0, 0.0).astype(BF16)


def _head_sum(x, ones_blk):
    return _mdot(x, ones_blk, na=2, nb=1)


def _ada_kernel(ct_ref, w_ref, b_ref, o_ref, sb_ref):
    tn = w_ref.shape[1]
    nb = sb_ref.shape[0]

    @pl.when(pl.program_id(0) == 0)
    def _():
        cv = ct_ref[...]
        sv = cv * _sigmoid(cv)
        for b in range(nb):
            sb_ref[b] = jnp.broadcast_to(sv[:, b:b + 1], sb_ref.shape[1:])

    ntile = tn // LANES

    def body(i, acc):
        r0 = pl.multiple_of(i * 8, 8)
        sb = [sb_ref[b, pl.ds(r0, 8), :] for b in range(nb)]
        wv = [w_ref[pl.ds(r0, 8), t * LANES:(t + 1) * LANES] for t in range(ntile)]
        return tuple(acc[b * ntile + t] + sb[b] * wv[t] for b in range(nb) for t in range(ntile))

    zero = jnp.zeros((8, LANES), F32)
    acc = lax.fori_loop(0, w_ref.shape[0] // 8, body, (zero,) * (nb * ntile), unroll=4)
    for b in range(nb):
        tot = jnp.concatenate([acc[b * ntile + t] for t in range(ntile)], axis=1)
        o_ref[b:b + 1, :] = jnp.sum(tot, axis=0, keepdims=True) + b_ref[...]


def _ada(c, w_ada, b_ada):
    batch = c.shape[0]
    assert batch == 2
    n = w_ada.shape[1]
    tn = 1536
    return pl.pallas_call(
        _ada_kernel,
        out_shape=jax.ShapeDtypeStruct((batch, n), F32),
        grid=(n // tn,),
        in_specs=[pl.BlockSpec((D_MODEL, batch), lambda j: (0, 0)),
                  pl.BlockSpec((D_MODEL, tn), lambda j: (0, j)),
                  pl.BlockSpec((1, tn), lambda j: (0, j))],
        out_specs=pl.BlockSpec((batch, tn), lambda j: (0, j)),
        scratch_shapes=[pltpu.VMEM((batch, D_MODEL, LANES), F32)],
        compiler_params=_cparams(("arbitrary",)),
        name="ada",
    )(c.T, w_ada, b_ada.reshape(1, n))


def _lnmod_kernel(x_ref, ada_ref, o_ref):
    xn = _layernorm(x_ref[...])
    sh, sc = ada_ref[0, 0:1, :], ada_ref[0, 1:2, :]
    o_ref[...] = (xn * (1.0 + sc) + sh).astype(o_ref.dtype)


def _lnmod(x2, ada3, seq):
    m = x2.shape[0]
    tb = 512
    per_b = seq // tb
    return pl.pallas_call(
        _lnmod_kernel,
        out_shape=jax.ShapeDtypeStruct((m, D_MODEL), BF16),
        grid=(m // tb,),
        in_specs=[pl.BlockSpec((tb, D_MODEL), lambda i: (i, 0)),
                  pl.BlockSpec((1, 6, D_MODEL), lambda i: (i // per_b, 0, 0))],
        out_specs=pl.BlockSpec((tb, D_MODEL), lambda i: (i, 0)),
        compiler_params=_cparams(("arbitrary",)),
        name="lnmod",
    )(x2, ada3)


def _win_kernel(n_cast, x_ref, wt_ref, *refs):
    cast_in, o_ref, cast_out, wb_ref = refs[:n_cast], refs[n_cast], refs[n_cast + 1:-1], refs[-1]

    @pl.when(pl.program_id(1) == 0)
    def _():
        wb_ref[...] = wt_ref[...].astype(wb_ref.dtype)

    for src, dst in zip(cast_in, cast_out):
        dst[...] = src[...].astype(dst.dtype)
    o_ref[...] = lax.dot_general(x_ref[...], wb_ref[...], NT, preferred_element_type=F32).astype(o_ref.dtype)


def _win_matmul(x, wt, col0, n_out, tm, tn, out_dtype, name, skip=None, cast_along=()):
    m, k = x.shape
    skip_tile, gap = skip if skip else (n_out // tn, 0)
    assert col0 % 8 == 0 and gap % 8 == 0 and n_out % tn == 0 and col0 + n_out + gap <= wt.shape[0]
    n_m = m // tm
    steps = (n_out // tn) * n_m

    def w_index(j, i):
        return pl.multiple_of(col0 + j * tn + jnp.where(j >= skip_tile, gap, 0), 8), 0

    slabs = []
    for w in cast_along:
        rows = w.shape[0] // steps
        assert rows % 16 == 0 and rows * steps == w.shape[0]
        slabs.append(pl.BlockSpec((rows, w.shape[1]), lambda j, i: (j * n_m + i, 0)))
    outs = pl.pallas_call(
        functools.partial(_win_kernel, len(cast_along)),
        out_shape=[jax.ShapeDtypeStruct((m, n_out), out_dtype)]
        + [jax.ShapeDtypeStruct(w.shape, BF16) for w in cast_along],
        grid=(n_out // tn, n_m),
        in_specs=[pl.BlockSpec((tm, k), lambda j, i: (i, 0)),
                  pl.BlockSpec((pl.Element(tn), pl.Element(k)), w_index)] + slabs,
        out_specs=[pl.BlockSpec((tm, tn), lambda j, i: (i, j))] + slabs,
        scratch_shapes=[pltpu.VMEM((tn, k), BF16)],
        compiler_params=_cparams(("arbitrary", "arbitrary")),
        name=name,
    )(x, wt, *cast_along)
    return outs if cast_along else outs[0]


def _rwkv_prep_kernel(z_ref, mu_ref, w0_ref, a0_ref, kk_ref, ka_ref, rk_ref, wdu_ref, wau_ref, wgu_ref,
                      r_o, ld_o, kt_o, v_o, kkn_o, bb_o, g_o, bonus_o, carry_ref):
    tb = z_ref.shape[0]

    @pl.when(pl.program_id(1) == 0)
    def _():
        carry_ref[...] = jnp.zeros_like(carry_ref)

    z = z_ref[...]
    prev = pltpu.roll(z, 1, 0)
    prev = jnp.where(_iota2(z.shape, 0) == 0, carry_ref[0:1, :], prev)
    carry_ref[0:1, :] = z[tb - 1:tb, :]
    zs = z + (prev - z) * mu_ref[...]

    c = D_RWKV
    r, k, v = zs[:, 0:c], zs[:, c:2 * c], zs[:, 2 * c:3 * c]
    wd = zs[:, 3 * c:3 * c + LORA_PAD]
    ad = zs[:, 3 * c + DECAY_LORA:3 * c + DECAY_LORA + LORA_PAD]
    gd = zs[:, 3 * c + DECAY_LORA + A_LORA:3 * c + DECAY_LORA + A_LORA + GATE_LORA]

    logw = -_softplus(-(w0_ref[...] + _mdot(jnp.tanh(wd), wdu_ref[...], 2, 2))) - 0.5
    ld = -jnp.exp(logw)
    a = _sigmoid(a0_ref[...] + _mdot(ad, wau_ref[...], 2, 2))
    g = _mdot(_sigmoid(gd), wgu_ref[...], 1, 1)

    ones_blk = _head_block_ones(LANES, RWKV_HEAD)
    kq = k * kk_ref[...]
    kt = k * (1.0 + (a - 1.0) * ka_ref[...])
    sq = kq * kq
    bn = r * kt * rk_ref[...]
    ss = jnp.concatenate([_head_sum(sq[:, s:s + LANES], ones_blk) for s in range(0, c, LANES)], axis=1)
    bs = jnp.concatenate([_head_sum(bn[:, s:s + LANES], ones_blk) for s in range(0, c, LANES)], axis=1)
    kkn = kq * lax.rsqrt(ss + 1e-12)

    r_o[...] = r.astype(r_o.dtype)
    ld_o[...] = ld
    kt_o[...] = kt.astype(kt_o.dtype)
    v_o[...] = v.astype(v_o.dtype)
    kkn_o[...] = kkn.astype(kkn_o.dtype)
    bb_o[...] = (a * kkn).astype(bb_o.dtype)
    g_o[...] = g.astype(g_o.dtype)
    bonus_o[...] = (bs * v).astype(bonus_o.dtype)


def _rwkv_prep(z_rw, mu_rw, w0, a0, k_k, k_a, r_k, wdu, wau, wgu, batch, seq):
    m = z_rw.shape[0]
    tb = RWKV_TB
    per_b = seq // tb
    row = lambda b, j: (b * per_b + j, 0)
    const = lambda b, j: (0, 0)
    vec = pl.BlockSpec((1, D_RWKV), const)
    outs = [jax.ShapeDtypeStruct((m, D_RWKV), F32 if i == 1 else BF16) for i in range(8)]
    return pl.pallas_call(
        _rwkv_prep_kernel,
        out_shape=outs,
        grid=(batch, per_b),
        in_specs=[pl.BlockSpec((tb, RW_COLS), row), pl.BlockSpec((1, RW_COLS), const),
                  vec, vec, vec, vec, vec,
                  pl.BlockSpec((LORA_PAD, D_RWKV), const), pl.BlockSpec((LORA_PAD, D_RWKV), const),
                  pl.BlockSpec((GATE_LORA, D_RWKV), const)],
        out_specs=[pl.BlockSpec((tb, D_RWKV), row)] * 8,
        scratch_shapes=[pltpu.VMEM((8, RW_COLS), F32)],
        compiler_params=_cparams(("arbitrary", "arbitrary")),
        name="rwkv_prep",
    )(z_rw, mu_rw, w0, a0, k_k, k_a, r_k, wdu, wau, wgu)


def _rwkv_chunk_kernel(r_ref, ld_ref, kt_ref, v_ref, kk_ref, bb_ref, y0_ref, q_ref, a_ref, c_ref):
    tb = r_ref.shape[0]
    lc = RWKV_CHUNK
    n = 2 * lc
    nchunk = tb // lc

    lane = _iota2((1, LANES), 1)
    m0 = jnp.where(lane < RWKV_HEAD, 1.0, 0.0)
    m1 = 1.0 - m0
    ri, ci = _iota2((n, n), 0), _iota2((n, n), 1)
    same_head = (ri // lc) == (ci // lc)
    strict = jnp.where(same_head & (ri > ci), 1.0, 0.0)
    incl = jnp.where(same_head & (ri >= ci), 1.0, 0.0)
    eye = jnp.where(ri == ci, 1.0, 0.0)

    tw = 4 * lc
    rt, ct = _iota2((tw, tw), 0), _iota2((tw, tw), 1)
    tri = jnp.where(((rt // lc) == (ct // lc)) & (ct <= rt), 1.0, 0.0).astype(BF16)

    ld = ld_ref[...]
    cl = jnp.concatenate([_mdot(tri, ld[i:i + tw], 1, 3) for i in range(0, tb, tw)], axis=0)
    e_pos = jnp.exp(cl)
    e_neg = jnp.exp(-cl)
    a_hat = kk_ref[...] * jnp.exp(cl - ld)
    b_hat = bb_ref[...] * e_neg
    k_hat = kt_ref[...] * e_neg
    r_hat = r_ref[...] * e_pos
    v_all = v_ref[...]

    def stack(x):
        return jnp.concatenate([x * m0, x * m1], axis=0)

    def bf(t):
        return t.astype(BF16)

    chunks = range(nchunk)
    st = [[stack(x[c * lc:(c + 1) * lc]) for x in (a_hat, b_hat, k_hat, r_hat, v_all)] for c in chunks]
    a_st, b_st, k_st, r_st, v_st = ([st[c][i] for c in chunks] for i in range(5))
    b_b, k_b, v_b = ([bf(t) for t in ts] for ts in (b_st, k_st, v_st))
    p = [_mdot(jnp.concatenate([a_st[c], r_st[c]], axis=0),
               jnp.concatenate([b_b[c], k_b[c]], axis=0), dims=NT) for c in chunks]
    mab = [p[c][0:n, 0:n] * strict for c in chunks]
    mab_b = [bf(t) for t in mab]
    mak_b = [bf(p[c][0:n, n:] * strict) for c in chunks]
    mrb_b = [bf(p[c][n:, 0:n] * incl) for c in chunks]
    mrk_b = [bf(p[c][n:, n:] * incl) for c in chunks]

    x = [eye - jnp.where((ri // 2 == ci // 2), mab[c], 0.0) for c in chunks]
    w1 = [_mdot(mak_b[c], v_b[c]) for c in chunks]
    zero_b = jnp.zeros((), BF16)
    s = 2
    while s < lc:
        lvl = (ri // (2 * s) == ci // (2 * s)) & (ri // s != ci // s)
        x_b = [bf(t) for t in x]
        nx = [_mdot(jnp.where(lvl, mab_b[c], zero_b), x_b[c]) for c in chunks]
        x = [x[c] - _mdot(x_b[c], nx[c]) for c in chunks]
        s *= 2

    ta = [_mdot(x[c], jnp.concatenate([w1[c], a_st[c]], axis=1)) for c in chunks]
    zeros_b = jnp.zeros((n, LANES), BF16)
    ugv_b = [jnp.concatenate([bf(jnp.concatenate([-ta[c][:, 0:LANES], ta[c][:, LANES:]], axis=1)),
                              jnp.concatenate([v_b[c], zeros_b], axis=1)], axis=0) for c in chunks]
    ru = [_mdot(jnp.concatenate([mrb_b[c], mrk_b[c]], axis=1), ugv_b[c]) for c in chunks]
    y0 = [ru[c][:, 0:LANES] for c in chunks]
    q = [r_st[c] - ru[c][:, LANES:] for c in chunks]
    pl_row = [e_pos[(c + 1) * lc - 1:(c + 1) * lc, :] for c in chunks]
    bk_p = [bf(jnp.concatenate([b_st[c] * pl_row[c], k_st[c] * pl_row[c]], axis=0)) for c in chunks]
    bu = [_mdot(bk_p[c], ugv_b[c], dims=TN) for c in chunks]
    ct = [bu[c][:, 0:LANES] for c in chunks]
    at = [eye * pl_row[c] - bu[c][:, LANES:] for c in chunks]

    for c in chunks:
        rows = slice(c * lc, (c + 1) * lc)
        y0_ref[rows, :] = y0[c][0:lc] + y0[c][lc:]
        q_ref[rows, :] = (q[c][0:lc] + q[c][lc:]).astype(q_ref.dtype)
        a_ref[0, c, 0] = (at[c][0:RWKV_HEAD] + at[c][RWKV_HEAD:]).astype(a_ref.dtype)
        c_ref[0, c, 0] = ct[c][0:RWKV_HEAD] + ct[c][RWKV_HEAD:]


def _rwkv_chunk(r, ld, kt, v, kkn, bb, batch, seq):
    m = r.shape[0]
    tb = RWKV_CORE_TB
    per_b = seq // tb
    cpb = tb // RWKV_CHUNK
    npair = D_RWKV // LANES
    slab = pl.BlockSpec((tb, LANES), lambda b, p, j: (b * per_b + j, p))
    mat = pl.BlockSpec((1, cpb, 1, RWKV_HEAD, LANES), lambda b, p, j: (b, j, p, 0, 0))
    mshape = (batch, seq // RWKV_CHUNK, npair, RWKV_HEAD, LANES)
    return pl.pallas_call(
        _rwkv_chunk_kernel,
        out_shape=[jax.ShapeDtypeStruct((m, D_RWKV), F32), jax.ShapeDtypeStruct((m, D_RWKV), BF16),
                   jax.ShapeDtypeStruct(mshape, BF16), jax.ShapeDtypeStruct(mshape, F32)],
        grid=(batch, npair, per_b),
        in_specs=[slab] * 6,
        out_specs=[slab, slab, mat, mat],
        compiler_params=_cparams(("arbitrary", "arbitrary", "arbitrary")),
        name="rwkv_chunk",
    )(r, ld, kt, v, kkn, bb)


def _rwkv_scan_kernel(y0_ref, q_ref, a_ref, c_ref, g_ref, bonus_ref, gng_ref, gnb_ref, ya_ref, s_ref):
    nb, rows, _ = y0_ref.shape
    lc = RWKV_CHUNK
    npair = D_RWKV // LANES

    @pl.when(pl.program_id(0) == 0)
    def _():
        s_ref[...] = jnp.zeros_like(s_ref)

    ones_blk = _head_block_ones(LANES, RWKV_HEAD)
    head0 = _iota2((RWKV_HEAD, LANES), 1) < RWKV_HEAD

    def block_diag(t):
        zero = jnp.zeros((), t.dtype)
        return jnp.concatenate([jnp.where(head0, t, zero), jnp.where(head0, zero, t)], axis=0)

    streams = [(b, p) for b in range(nb) for p in range(npair)]
    for ci in range(rows // lc):
        rs = slice(ci * lc, (ci + 1) * lc)
        state_b = [s_ref[b * npair + p].astype(BF16) for b, p in streams]
        y = [y0_ref[b, rs, p * LANES:(p + 1) * LANES]
             + _mdot(q_ref[b, rs, p * LANES:(p + 1) * LANES], state_b[i])
             for i, (b, p) in enumerate(streams)]
        for i, (b, p) in enumerate(streams):
            s_ref[b * npair + p] = (_mdot(block_diag(a_ref[b, ci, p]), state_b[i])
                                    + block_diag(c_ref[b, ci, p]))
        mu = [_head_sum(t, ones_blk) * (1.0 / RWKV_HEAD) for t in y]
        yc = [y[i] - mu[i] for i in range(len(streams))]
        var = [_head_sum(t * t, ones_blk) * (1.0 / RWKV_HEAD) for t in yc]
        for i, (b, p) in enumerate(streams):
            ls = slice(p * LANES, (p + 1) * LANES)
            yn = yc[i] * lax.rsqrt(var[i] + RWKV_GN_EPS) * gng_ref[:, ls] + gnb_ref[:, ls]
            ya_ref[b, rs, ls] = ((yn + bonus_ref[b, rs, ls]) * g_ref[b, rs, ls]).astype(ya_ref.dtype)


def _rwkv_scan(y0, q, a_mat, c_mat, g, bonus, gn_g, gn_b, batch, seq):
    rows = RWKV_SCAN_ROWS
    npair = D_RWKV // LANES
    tok = pl.BlockSpec((batch, rows, D_RWKV), lambda j: (0, j, 0))
    mat = pl.BlockSpec((batch, rows // RWKV_CHUNK, npair, RWKV_HEAD, LANES), lambda j: (0, j, 0, 0, 0))
    vec = pl.BlockSpec((1, D_RWKV), lambda j: (0, 0))
    shape3 = (batch, seq, D_RWKV)
    return pl.pallas_call(
        _rwkv_scan_kernel,
        out_shape=jax.ShapeDtypeStruct(shape3, BF16),
        grid=(seq // rows,),
        in_specs=[tok, tok, mat, mat, tok, tok, vec, vec],
        out_specs=tok,
        scratch_shapes=[pltpu.VMEM((batch * npair, LANES, LANES), F32)],
        compiler_params=_cparams(("arbitrary",)),
        name="rwkv_scan",
    )(y0.reshape(shape3), q.reshape(shape3), a_mat, c_mat, g.reshape(shape3), bonus.reshape(shape3),
      gn_g, gn_b).reshape(batch * seq, D_RWKV)


def _mlstm_kernel(z_ref, zif_ref, wconv_ref, bconv_ref, gbias_ref, yb_ref, tail_ref, ct_ref, n_ref, m_ref):
    lc = z_ref.shape[0]
    dqk, dv, nh = MLSTM_DQK, MLSTM_DV, MLSTM_HEADS

    @pl.when(pl.program_id(1) == 0)
    def _():
        tail_ref[...] = jnp.zeros_like(tail_ref)
        ct_ref[...] = jnp.zeros_like(ct_ref)
        n_ref[...] = jnp.zeros_like(n_ref)
        m_ref[...] = jnp.zeros_like(m_ref)

    zqk = z_ref[:, 0:2 * D_QK]
    ext = jnp.concatenate([tail_ref[...], zqk], axis=0)
    tail_ref[...] = zqk[lc - 8:lc, :]
    conv = bconv_ref[...] + wconv_ref[CONV_K - 1:CONV_K, :] * zqk
    for d in range(1, CONV_K):
        conv = conv + wconv_ref[CONV_K - 1 - d:CONV_K - d, :] * ext[8 - d:8 - d + lc, :]
    qk = conv * _sigmoid(conv)
    q_all = qk[:, 0:D_QK]
    k_all = qk[:, D_QK:] * (dqk ** -0.5)
    v_all = z_ref[:, 2 * D_QK:2 * D_QK + D_MLSTM]
    o_all = z_ref[:, 2 * D_QK + D_MLSTM:]

    gz = zif_ref[...] + gbias_ref[...]
    lf_col = jnp.minimum(gz, 0.0) - jnp.log(1.0 + jnp.exp(-jnp.abs(gz)))
    rr, cc = _iota2((lc, lc), 0), _iota2((lc, lc), 1)
    causal = cc <= rr
    tri = jnp.where(causal, 1.0, 0.0).astype(BF16)
    b_col = _mdot(tri, lf_col, 1, 3)
    sel = jnp.where(_iota2((8, LANES), 0) == _iota2((8, LANES), 1), 1.0, 0.0).astype(BF16)
    g_row = _mdot(sel, gz, 1, 3, NT)
    lf_row = jnp.minimum(g_row, 0.0) - jnp.log(1.0 + jnp.exp(-jnp.abs(g_row)))
    b_row = _mdot(lf_row, tri, 3, 1, NT)

    heads = range(nh)
    qf = [q_all[:, h * dqk:(h + 1) * dqk] for h in heads]
    q = [t.astype(BF16) for t in qf]
    k = [k_all[:, h * dqk:(h + 1) * dqk] for h in heads]
    v = [v_all[:, h * dv:(h + 1) * dv].astype(BF16) for h in heads]
    bc = [b_col[:, nh + h:nh + h + 1] for h in heads]
    ic = [gz[:, h:h + 1] for h in heads]
    br = [b_row[nh + h:nh + h + 1, :] for h in heads]
    ir = [g_row[h:h + 1, :] for h in heads]
    g_tot = [b_col[lc - 1:lc, nh + h:nh + h + 1] for h in heads]
    m_prev = [m_ref[0:1, h:h + 1] for h in heads]
    ct = [ct_ref[h] for h in heads]
    nvec = [n_ref[h:h + 1, :] for h in heads]

    qk = [_mdot(q[h], k[h], dims=NT) for h in heads]
    qc = [_mdot(q[h], ct[h]) for h in heads]
    dm = [jnp.where(causal, bc[h] - br[h] + ir[h], -jnp.inf) for h in heads]
    inter = [bc[h] + m_prev[h] for h in heads]
    mt = [jnp.maximum(inter[h], jnp.max(dm[h], axis=-1, keepdims=True)) for h in heads]
    sc = [qk[h] * jnp.exp(dm[h] - mt[h]) for h in heads]
    winter = [jnp.exp(inter[h] - mt[h]) for h in heads]
    sv = [_mdot(sc[h], v[h]) for h in heads]
    m_new = [jnp.maximum(g_tot[h] + m_prev[h], jnp.max(g_tot[h] - br[h] + ir[h], axis=-1, keepdims=True))
             for h in heads]
    kw = [k[h] * jnp.exp(g_tot[h] - bc[h] + ic[h] - m_new[h]) for h in heads]
    kv = [_mdot(kw[h], v[h], dims=TN) for h in heads]
    for h in heads:
        num = winter[h] * qc[h] + sv[h]
        den = (winter[h] * jnp.sum(qf[h] * nvec[h], axis=-1, keepdims=True)
               + jnp.sum(sc[h], axis=-1, keepdims=True))
        hout = num / jnp.maximum(jnp.abs(den), jnp.exp(-mt[h]))
        wc = jnp.exp(g_tot[h] + m_prev[h] - m_new[h])
        ct_ref[h] = wc * ct[h] + kv[h]
        n_ref[h:h + 1, :] = wc * nvec[h] + jnp.sum(kw[h], axis=0, keepdims=True)
        m_ref[0:1, h:h + 1] = m_new[h]
        o = o_all[:, h * dv:(h + 1) * dv]
        yb_ref[:, h * dv:(h + 1) * dv] = (_sigmoid(o) * hout).astype(yb_ref.dtype)


def _mlstm(z_ml, z_if, w_conv, b_conv, gbias, batch, seq):
    m = z_ml.shape[0]
    lc = MLSTM_CHUNK
    per_b = seq // lc
    const = lambda b, j: (0, 0)
    return pl.pallas_call(
        _mlstm_kernel,
        out_shape=jax.ShapeDtypeStruct((m, D_MLSTM), BF16),
        grid=(batch, per_b),
        in_specs=[pl.BlockSpec((lc, ML_COLS), lambda b, j: (b * per_b + j, 0)),
                  pl.BlockSpec((lc, ML_GATE_PAD), lambda b, j: (b * per_b + j, 0)),
                  pl.BlockSpec((CONV_K, 2 * D_QK), const),
                  pl.BlockSpec((1, 2 * D_QK), const),
                  pl.BlockSpec((1, ML_GATE_PAD), const)],
        out_specs=pl.BlockSpec((lc, D_MLSTM), lambda b, j: (b * per_b + j, 0)),
        scratch_shapes=[pltpu.VMEM((8, 2 * D_QK), F32),
                        pltpu.VMEM((MLSTM_HEADS, MLSTM_DQK, MLSTM_DV), F32),
                        pltpu.VMEM((8, MLSTM_DQK), F32),
                        pltpu.VMEM((8, LANES), F32)],
        compiler_params=_cparams(("arbitrary", "arbitrary")),
        name="mlstm",
    )(z_ml, z_if, w_conv, b_conv, gbias)


def _merge_kernel(ya_ref, yb_ref, zg_ref, wa_ref, wb_ref, o_ref):
    groups = [slice(i, i + EPILOGUE_ROWS) for i in range(0, ya_ref.shape[0], EPILOGUE_ROWS)]
    pab = [(jnp.dot(ya_ref[rows, :], wa_ref[...], preferred_element_type=F32),
            jnp.dot(yb_ref[rows, :], wb_ref[...], preferred_element_type=F32)) for rows in groups]
    for (pa, pb), rows in zip(pab, groups):
        ga = _sigmoid(zg_ref[rows, 0:D_MODEL].astype(F32))
        gb = _sigmoid(zg_ref[rows, D_MODEL:].astype(F32))
        o_ref[rows, :] = (ga * pa + gb * pb).astype(o_ref.dtype)


def _merge(ya, yb, zg, wa, wb):
    m = ya.shape[0]
    tm = 512
    const = lambda i: (0, 0)
    return pl.pallas_call(
        _merge_kernel,
        out_shape=jax.ShapeDtypeStruct((m, D_MODEL), BF16),
        grid=(m // tm,),
        in_specs=[pl.BlockSpec((tm, D_RWKV), lambda i: (i, 0)),
                  pl.BlockSpec((tm, D_MLSTM), lambda i: (i, 0)),
                  pl.BlockSpec((tm, 2 * D_MODEL), lambda i: (i, 0)),
                  pl.BlockSpec((D_RWKV, D_MODEL), const),
                  pl.BlockSpec((D_MLSTM, D_MODEL), const)],
        out_specs=pl.BlockSpec((tm, D_MODEL), lambda i: (i, 0)),
        compiler_params=_cparams(("arbitrary",)),
        name="merge",
    )(ya, yb, zg, wa, wb)


def _outproj_kernel(mg_ref, w_ref, x_ref, ada_ref, g_ref, b_ref, x1_ref, u2_ref):
    groups = [slice(i, i + EPILOGUE_ROWS) for i in range(0, mg_ref.shape[0], EPILOGUE_ROWS)]
    ys = [jnp.dot(mg_ref[rows, :], w_ref[...], preferred_element_type=F32) for rows in groups]
    g1 = ada_ref[0, 2:3, :]
    sh2, sc2 = ada_ref[0, 3:4, :], ada_ref[0, 4:5, :]
    for y, rows in zip(ys, groups):
        x1 = _layernorm(ALPHA * x_ref[rows, :] + g1 * y) * g_ref[...] + b_ref[...]
        x1_ref[rows, :] = x1
        u2_ref[rows, :] = (_layernorm(x1) * (1.0 + sc2) + sh2).astype(u2_ref.dtype)


def _outproj(merged, w_out, x2, ada3, ln_g, ln_b, seq):
    m = merged.shape[0]
    tm = 512
    per_b = seq // tm
    const = lambda i: (0, 0)
    row = pl.BlockSpec((tm, D_MODEL), lambda i: (i, 0))
    return pl.pallas_call(
        _outproj_kernel,
        out_shape=[jax.ShapeDtypeStruct((m, D_MODEL), F32), jax.ShapeDtypeStruct((m, D_MODEL), BF16)],
        grid=(m // tm,),
        in_specs=[row, pl.BlockSpec((D_MODEL, D_MODEL), const, pipeline_mode=pl.Buffered(1)), row,
                  pl.BlockSpec((1, 6, D_MODEL), lambda i: (i // per_b, 0, 0)),
                  pl.BlockSpec((1, D_MODEL), const), pl.BlockSpec((1, D_MODEL), const)],
        out_specs=[row, row],
        compiler_params=_cparams(("arbitrary",)),
        name="outproj",
    )(merged, w_out, x2, ada3, ln_g, ln_b)


def _ff1_kernel(x_ref, w_ref, b_ref, w2_ref, o_ref, w2b_ref, wb_ref):
    @pl.when(pl.program_id(1) == 0)
    def _():
        wb_ref[...] = w_ref[...].astype(wb_ref.dtype)

    w2b_ref[...] = w2_ref[...].astype(w2b_ref.dtype)
    h = jnp.dot(x_ref[...], wb_ref[...], preferred_element_type=F32) + b_ref[...]
    h = jnp.maximum(h, 0.0)
    o_ref[...] = (h * h).astype(o_ref.dtype)


def _ff1(u2, w1, b1, w2):
    m, k = u2.shape
    n = w1.shape[1]
    tm, tn = 1024, 1024
    n_m = m // tm
    rows2 = w2.shape[0] // ((n // tn) * n_m)
    assert rows2 % 16 == 0 and rows2 * (n // tn) * n_m == w2.shape[0]
    slab = pl.BlockSpec((rows2, w2.shape[1]), lambda j, i: (j * n_m + i, 0))
    return pl.pallas_call(
        _ff1_kernel,
        out_shape=[jax.ShapeDtypeStruct((m, n), BF16), jax.ShapeDtypeStruct(w2.shape, BF16)],
        grid=(n // tn, n_m),
        in_specs=[pl.BlockSpec((tm, k), lambda j, i: (i, 0)),
                  pl.BlockSpec((k, tn), lambda j, i: (0, j)),
                  pl.BlockSpec((1, tn), lambda j, i: (0, j)),
                  slab],
        out_specs=[pl.BlockSpec((tm, tn), lambda j, i: (i, j)), slab],
        scratch_shapes=[pltpu.VMEM((k, tn), BF16)],
        compiler_params=_cparams(("arbitrary", "arbitrary")),
        name="ff1",
    )(u2, w1, b1, w2)


def _ff2_kernel(h_ref, w_ref, b_ref, x1_ref, ada_ref, g_ref, bb_ref, o_ref):
    y2 = jnp.dot(h_ref[...], w_ref[...], preferred_element_type=F32) + b_ref[...]
    g2 = ada_ref[0, 5:6, :]
    o_ref[...] = (_layernorm(ALPHA * x1_ref[...] + g2 * y2) * g_ref[...] + bb_ref[...]).astype(o_ref.dtype)


def _ff2(h, w2, b2, x1, ada3, ln_g, ln_b, seq, out_dtype):
    m, k = h.shape
    tm = 256
    per_b = seq // tm
    const = lambda i: (0, 0)
    row = pl.BlockSpec((tm, D_MODEL), lambda i: (i, 0))
    return pl.pallas_call(
        _ff2_kernel,
        out_shape=jax.ShapeDtypeStruct((m, D_MODEL), out_dtype),
        grid=(m // tm,),
        in_specs=[pl.BlockSpec((tm, k), lambda i: (i, 0)),
                  pl.BlockSpec((k, D_MODEL), const, pipeline_mode=pl.Buffered(1)),
                  pl.BlockSpec((1, D_MODEL), const), row,
                  pl.BlockSpec((1, 6, D_MODEL), lambda i: (i // per_b, 0, 0)),
                  pl.BlockSpec((1, D_MODEL), const), pl.BlockSpec((1, D_MODEL), const)],
        out_specs=row,
        compiler_params=_cparams(("arbitrary",)),
        name="ff2",
    )(h, w2, b2, x1, ada3, ln_g, ln_b)


def _pad_rows(w, height):
    return jnp.pad(w, ((0, height - w.shape[0]), (0, 0)))


def _layer(x2, c, batch, seq, w_ada, b_ada, w_in, mu_shift, w0, w_decay_up, a0, w_a_up, w_g_up, k_k, k_a, r_k,
           gn_g, gn_b, w_conv, b_conv, b_igate, b_fgate, w_branch_a, w_branch_b, w_out,
           ln1_g, ln1_b, w_ff1, b_ff1, w_ff2, b_ff2, ln2_g, ln2_b):
    row = lambda v: v.reshape(1, -1)

    ada3 = _ada(c, w_ada, b_ada).reshape(batch, 6, D_MODEL)
    u = _lnmod(x2, ada3, seq)

    assert w_in.shape == (D_MODEL, N_RWKV_COLS + N_MLSTM_COLS + 2 * D_MODEL)
    w_in_t = jnp.swapaxes(w_in, 0, 1)
    z_rw = _win_matmul(u, w_in_t, 0, RW_COLS, 2048, 512, F32, "win_rwkv")
    col_if = N_RWKV_COLS + 2 * D_QK + D_MLSTM
    z_ml = _win_matmul(u, w_in_t, N_RWKV_COLS, ML_COLS, 1024, 1024, F32, "win_mlstm",
                       skip=((2 * D_QK + D_MLSTM) // 1024, 2 * MLSTM_HEADS))
    z_gt, w_a_b, w_b_b, w_out_b = _win_matmul(u, w_in_t, N_RWKV_COLS + N_MLSTM_COLS, 2 * D_MODEL, 1024, 1024, BF16,
                                              "win_gate", cast_along=(w_branch_a, w_branch_b, w_out))

    mu_rw = jnp.pad(mu_shift, (0, RW_COLS - N_RWKV_COLS))
    prep = _rwkv_prep(z_rw, row(mu_rw), row(w0), row(a0), row(k_k), row(k_a), row(r_k),
                      _pad_rows(w_decay_up, LORA_PAD), _pad_rows(w_a_up, LORA_PAD), w_g_up, batch, seq)
    r, ld, kt, v, kkn, bb, g, bonus = prep
    y0, q, a_mat, c_mat = _rwkv_chunk(r, ld, kt, v, kkn, bb, batch, seq)
    ya = _rwkv_scan(y0, q, a_mat, c_mat, g, bonus, row(gn_g), row(gn_b), batch, seq)

    gbias = jnp.pad(jnp.concatenate([b_igate, b_fgate]), (0, ML_GATE_PAD - 2 * MLSTM_HEADS))
    z_if = _win_matmul(u, w_in_t, col_if, ML_GATE_PAD, 2048, ML_GATE_PAD, F32, "win_if")
    yb = _mlstm(z_ml, z_if, w_conv, row(b_conv), row(gbias), batch, seq)

    merged = _merge(ya, yb, z_gt, w_a_b, w_b_b)
    x1, u2 = _outproj(merged, w_out_b, x2, ada3, row(ln1_g), row(ln1_b), seq)

    h, w_ff2_b = _ff1(u2, w_ff1, row(b_ff1), w_ff2)
    return _ff2(h, w_ff2_b, row(b_ff2), x1, ada3, row(ln2_g), row(ln2_b), seq, F32)


def kernel(x, c, w_ada, b_ada, w_in, mu_shift, w0, w_decay_up, a0, w_a_up, w_g_up, k_k, k_a, r_k, gn_g, gn_b,
           w_conv, b_conv, b_igate, b_fgate, w_branch_a, w_branch_b, w_out, ln1_g, ln1_b, w_ff1, b_ff1,
           w_ff2, b_ff2, ln2_g, ln2_b):
    out_dtype = x.dtype
    batch, seq, d = x.shape
    assert d == D_MODEL and seq % 1024 == 0 and w_ada.shape[0] == DEPTH
    x2 = x.astype(F32).reshape(batch * seq, d)
    cf = c.astype(F32)
    for l in range(DEPTH):
        x2 = _layer(x2, cf, batch, seq, w_ada[l], b_ada[l], w_in[l], mu_shift[l], w0[l], w_decay_up[l], a0[l],
                    w_a_up[l], w_g_up[l], k_k[l], k_a[l], r_k[l].reshape(-1), gn_g[l], gn_b[l], w_conv[l],
                    b_conv[l], b_igate[l], b_fgate[l], w_branch_a[l], w_branch_b[l], w_out[l], ln1_g[l],
                    ln1_b[l], w_ff1[l], b_ff1[l], w_ff2[l], b_ff2[l], ln2_g[l], ln2_b[l])
    return x2.reshape(batch, seq, d).astype(out_dtype)
```

```python
import functools
import math

import jax
import jax.numpy as jnp
from jax import lax
from jax.experimental import pallas as pl
from jax.experimental.pallas import tpu as pltpu

F32 = jnp.float32
BF16 = jnp.bfloat16

D_MODEL = 2048
DEPTH = 1
D_RWKV = D_MODEL // 2
RWKV_HEAD = 64
DECAY_LORA = 96
A_LORA = 96
GATE_LORA = 256
RWKV_GN_EPS = 64e-5
D_MLSTM = D_MODEL // 2
MLSTM_HEADS = 4
MLSTM_DV = D_MLSTM // MLSTM_HEADS
MLSTM_DQK = MLSTM_DV // 2
D_QK = MLSTM_HEADS * MLSTM_DQK
CONV_K = 4
D_FF = 4 * D_MODEL
LN_EPS = 1e-5
ALPHA = (2.0 * DEPTH) ** 0.25

LANES = 128
LORA_PAD = 128
N_RWKV_COLS = 3 * D_RWKV + DECAY_LORA + A_LORA + GATE_LORA
N_MLSTM_COLS = 2 * D_QK + 2 * D_MLSTM + 2 * MLSTM_HEADS
RW_COLS = 3584
ML_GATE_PAD = 128
ML_COLS = 2 * D_QK + 2 * D_MLSTM
RWKV_CHUNK = 64
RWKV_TB = 256
RWKV_SCAN_ROWS = 128
RWKV_CORE_TB = 1024
MLSTM_CHUNK = 256
EPILOGUE_ROWS = 256
VMEM_LIMIT = 56 * 1024 * 1024

NN = (((1,), (0,)), ((), ()))
NT = (((1,), (1,)), ((), ()))
TN = (((0,), (0,)), ((), ()))


def _cparams(sem):
    return pltpu.CompilerParams(dimension_semantics=sem, vmem_limit_bytes=VMEM_LIMIT)


def _bf16_parts(x, n):
    if x.dtype == BF16:
        return [x]
    parts, rem = [], x
    for i in range(n):
        p = rem.astype(BF16)
        parts.append(p)
        if i + 1 < n:
            rem = rem - p.astype(F32)
    return parts


def _mdot(a, b, na=1, nb=1, dims=NN):
    ap, bp = _bf16_parts(a, na), _bf16_parts(b, nb)
    order = max(len(ap), len(bp))
    acc = None
    for i, x in enumerate(ap):
        for j, y in enumerate(bp):
            if i + j < order:
                t = lax.dot_general(x, y, dims, preferred_element_type=F32)
                acc = t if acc is None else acc + t
    return acc


def _sigmoid(x):
    return 1.0 / (1.0 + jnp.exp(-x))


def _layernorm(x):
    mu = jnp.mean(x, axis=-1, keepdims=True)
    xc = x - mu
    var = jnp.mean(xc * xc, axis=-1, keepdims=True)
    return xc * lax.rsqrt(var + LN_EPS)


def _iota2(shape, dim):
    return lax.broadcasted_iota(jnp.int32, shape, dim)


def _head_block_ones(n, head):
    r, c = _iota2((n, n), 0), _iota2((n, n), 1)
    return jnp.where((r // head) == (c // head), 1.0, 0.0).astype(BF16)


def _head_sum(x, ones_blk):
    return _mdot(x, ones_blk)


def _ada_kernel(ct_ref, w_ref, b_ref, o_ref, sb_ref):
    tn = w_ref.shape[1]
    nb = sb_ref.shape[0]

    @pl.when(pl.program_id(0) == 0)
    def _():
        cv = ct_ref[...]
        sv = cv * _sigmoid(cv)
        for b in range(nb):
            sb_ref[b] = jnp.broadcast_to(sv[:, b:b + 1], sb_ref.shape[1:])

    ntile = tn // LANES

    def body(i, acc):
        r0 = pl.multiple_of(i * 8, 8)
        sb = [sb_ref[b, pl.ds(r0, 8), :] for b in range(nb)]
        wv = [w_ref[pl.ds(r0, 8), t * LANES:(t + 1) * LANES] for t in range(ntile)]
        return tuple(acc[b * ntile + t] + sb[b] * wv[t] for b in range(nb) for t in range(ntile))

    zero = jnp.zeros((8, LANES), F32)
    acc = lax.fori_loop(0, w_ref.shape[0] // 8, body, (zero,) * (nb * ntile), unroll=4)
    for b in range(nb):
        tot = jnp.concatenate([acc[b * ntile + t] for t in range(ntile)], axis=1)
        o_ref[b:b + 1, :] = jnp.sum(tot, axis=0, keepdims=True) + b_ref[...]


def _ada(c, w_ada, b_ada):
    batch = c.shape[0]
    assert batch == 2
    n = w_ada.shape[1]
    tn = 1536
    return pl.pallas_call(
        _ada_kernel,
        out_shape=jax.ShapeDtypeStruct((batch, n), F32),
        grid=(n // tn,),
        in_specs=[pl.BlockSpec((D_MODEL, batch), lambda j: (0, 0)),
                  pl.BlockSpec((D_MODEL, tn), lambda j: (0, j)),
                  pl.BlockSpec((1, tn), lambda j: (0, j))],
        out_specs=pl.BlockSpec((batch, tn), lambda j: (0, j)),
        scratch_shapes=[pltpu.VMEM((batch, D_MODEL, LANES), F32)],
        compiler_params=_cparams(("arbitrary",)),
        name="ada",
    )(c.T, w_ada, b_ada.reshape(1, n))


def _lnmod_kernel(x_ref, ada_ref, o_ref):
    xn = _layernorm(x_ref[...])
    sh, sc = ada_ref[0, 0:1, :], ada_ref[0, 1:2, :]
    o_ref[...] = (xn * (1.0 + sc) + sh).astype(o_ref.dtype)


def _lnmod(x2, ada3, seq):
    m = x2.shape[0]
    tb = 512
    per_b = seq // tb
    return pl.pallas_call(
        _lnmod_kernel,
        out_shape=jax.ShapeDtypeStruct((m, D_MODEL), BF16),
        grid=(m // tb,),
        in_specs=[pl.BlockSpec((tb, D_MODEL), lambda i: (i, 0)),
                  pl.BlockSpec((1, 6, D_MODEL), lambda i: (i // per_b, 0, 0))],
        out_specs=pl.BlockSpec((tb, D_MODEL), lambda i: (i, 0)),
        compiler_params=_cparams(("arbitrary",)),
        name="lnmod",
    )(x2, ada3)


def _win_kernel(n_cast, x_ref, wt_ref, *refs):
    cast_in, o_ref, cast_out, wb_ref = refs[:n_cast], refs[n_cast], refs[n_cast + 1:-1], refs[-1]

    @pl.when(pl.program_id(1) == 0)
    def _():
        wb_ref[...] = wt_ref[...].astype(wb_ref.dtype)

    for src, dst in zip(cast_in, cast_out):
        dst[...] = src[...].astype(dst.dtype)
    o_ref[...] = lax.dot_general(x_ref[...], wb_ref[...], NT, preferred_element_type=F32).astype(o_ref.dtype)


def _win_matmul(x, wt, col0, n_out, tm, tn, out_dtype, name, skip=None, cast_along=()):
    m, k = x.shape
    skip_tile, gap = skip if skip else (n_out // tn, 0)
    assert col0 % 8 == 0 and gap % 8 == 0 and n_out % tn == 0 and col0 + n_out + gap <= wt.shape[0]
    n_m = m // tm
    steps = (n_out // tn) * n_m

    def w_index(j, i):
        return pl.multiple_of(col0 + j * tn + jnp.where(j >= skip_tile, gap, 0), 8), 0

    slabs = []
    for w in cast_along:
        rows = w.shape[0] // steps
        assert rows % 16 == 0 and rows * steps == w.shape[0]
        slabs.append(pl.BlockSpec((rows, w.shape[1]), lambda j, i: (j * n_m + i, 0)))
    outs = pl.pallas_call(
        functools.partial(_win_kernel, len(cast_along)),
        out_shape=[jax.ShapeDtypeStruct((m, n_out), out_dtype)]
        + [jax.ShapeDtypeStruct(w.shape, BF16) for w in cast_along],
        grid=(n_out // tn, n_m),
        in_specs=[pl.BlockSpec((tm, k), lambda j, i: (i, 0)),
                  pl.BlockSpec((pl.Element(tn), pl.Element(k)), w_index)] + slabs,
        out_specs=[pl.BlockSpec((tm, tn), lambda j, i: (i, j))] + slabs,
        scratch_shapes=[pltpu.VMEM((tn, k), BF16)],
        compiler_params=_cparams(("arbitrary", "arbitrary")),
        name=name,
    )(x, wt, *cast_along)
    return outs if cast_along else outs[0]


def _rwkv_prep_kernel(z_ref, mu_ref, w0_ref, a0_ref, kk_ref, ka_ref, rk_ref, wdu_ref, wau_ref, wgu_ref,
                      r_o, ld_o, kt_o, v_o, kkn_o, bb_o, g_o, bonus_o, carry_ref):
    tb = z_ref.shape[0]

    @pl.when(pl.program_id(1) == 0)
    def _():
        carry_ref[...] = jnp.zeros_like(carry_ref)

    z = z_ref[...]
    prev = pltpu.roll(z, 1, 0)
    prev = jnp.where(_iota2(z.shape, 0) == 0, carry_ref[0:1, :], prev)
    carry_ref[0:1, :] = z[tb - 1:tb, :]
    zs = z + (prev - z) * mu_ref[...]

    c = D_RWKV
    r, k, v = zs[:, 0:c], zs[:, c:2 * c], zs[:, 2 * c:3 * c]
    wd = zs[:, 3 * c:3 * c + LORA_PAD]
    ad = zs[:, 3 * c + DECAY_LORA:3 * c + DECAY_LORA + LORA_PAD]
    gd = zs[:, 3 * c + DECAY_LORA + A_LORA:3 * c + DECAY_LORA + A_LORA + GATE_LORA]

    ld = -math.exp(-0.5) * _sigmoid(w0_ref[...] + _mdot(jnp.tanh(wd), wdu_ref[...], 2, 2))
    a = _sigmoid(a0_ref[...] + _mdot(ad, wau_ref[...]))
    g = _mdot(_sigmoid(gd), wgu_ref[...], 1, 1)

    ones_blk = _head_block_ones(LANES, RWKV_HEAD)
    kq = k * kk_ref[...]
    kt = k * (1.0 + (a - 1.0) * ka_ref[...])
    sq = kq * kq
    bn = r * kt * rk_ref[...]
    ss = jnp.concatenate([_head_sum(sq[:, s:s + LANES], ones_blk) for s in range(0, c, LANES)], axis=1)
    bs = jnp.concatenate([_head_sum(bn[:, s:s + LANES], ones_blk) for s in range(0, c, LANES)], axis=1)
    kkn = kq * lax.rsqrt(ss + 1e-12)

    r_o[...] = r.astype(r_o.dtype)
    ld_o[...] = ld
    kt_o[...] = kt.astype(kt_o.dtype)
    v_o[...] = v.astype(v_o.dtype)
    kkn_o[...] = kkn.astype(kkn_o.dtype)
    bb_o[...] = (a * kkn).astype(bb_o.dtype)
    g_o[...] = g.astype(g_o.dtype)
    bonus_o[...] = (bs * v).astype(bonus_o.dtype)


def _rwkv_prep(z_rw, mu_rw, w0, a0, k_k, k_a, r_k, wdu, wau, wgu, batch, seq):
    m = z_rw.shape[0]
    tb = RWKV_TB
    per_b = seq // tb
    row = lambda b, j: (b * per_b + j, 0)
    const = lambda b, j: (0, 0)
    vec = pl.BlockSpec((1, D_RWKV), const)
    outs = [jax.ShapeDtypeStruct((m, D_RWKV), F32 if i == 1 else BF16) for i in range(8)]
    return pl.pallas_call(
        _rwkv_prep_kernel,
        out_shape=outs,
        grid=(batch, per_b),
        in_specs=[pl.BlockSpec((tb, RW_COLS), row), pl.BlockSpec((1, RW_COLS), const),
                  vec, vec, vec, vec, vec,
                  pl.BlockSpec((LORA_PAD, D_RWKV), const), pl.BlockSpec((LORA_PAD, D_RWKV), const),
                  pl.BlockSpec((GATE_LORA, D_RWKV), const)],
        out_specs=[pl.BlockSpec((tb, D_RWKV), row)] * 8,
        scratch_shapes=[pltpu.VMEM((8, RW_COLS), F32)],
        compiler_params=_cparams(("arbitrary", "arbitrary")),
        name="rwkv_prep",
    )(z_rw, mu_rw, w0, a0, k_k, k_a, r_k, wdu, wau, wgu)


def _rwkv_chunk_kernel(r_ref, ld_ref, kt_ref, v_ref, kk_ref, bb_ref, y0_ref, q_ref, a_ref, c_ref):
    tb = r_ref.shape[0]
    lc = RWKV_CHUNK
    n = 2 * lc
    nchunk = tb // lc

    lane = _iota2((1, LANES), 1)
    m0 = jnp.where(lane < RWKV_HEAD, 1.0, 0.0)
    m1 = 1.0 - m0
    ri, ci = _iota2((n, n), 0), _iota2((n, n), 1)
    same_head = (ri // lc) == (ci // lc)
    strict = jnp.where(same_head & (ri > ci), 1.0, 0.0)
    incl = jnp.where(same_head & (ri >= ci), 1.0, 0.0)
    eye = jnp.where(ri == ci, 1.0, 0.0)

    tw = 4 * lc
    rt, ct = _iota2((tw, tw), 0), _iota2((tw, tw), 1)
    tri = jnp.where(((rt // lc) == (ct // lc)) & (ct <= rt), 1.0, 0.0).astype(BF16)

    ld = ld_ref[...]
    cl = jnp.concatenate([_mdot(tri, ld[i:i + tw], 1, 3) for i in range(0, tb, tw)], axis=0)
    e_pos = jnp.exp(cl)
    e_neg = jnp.exp(-cl)
    a_hat = kk_ref[...] * jnp.exp(cl - ld)
    b_hat = bb_ref[...] * e_neg
    k_hat = kt_ref[...] * e_neg
    r_hat = r_ref[...] * e_pos
    v_all = v_ref[...]

    def stack(x):
        return jnp.concatenate([x * m0, x * m1], axis=0)

    def bf(t):
        return t.astype(BF16)

    chunks = range(nchunk)
    st = [[stack(x[c * lc:(c + 1) * lc]) for x in (a_hat, b_hat, k_hat, r_hat, v_all)] for c in chunks]
    a_st, b_st, k_st, r_st, v_st = ([st[c][i] for c in chunks] for i in range(5))
    b_b, k_b, v_b = ([bf(t) for t in ts] for ts in (b_st, k_st, v_st))
    p = [_mdot(jnp.concatenate([a_st[c], r_st[c]], axis=0),
               jnp.concatenate([b_b[c], k_b[c]], axis=0), dims=NT) for c in chunks]
    mab = [p[c][0:n, 0:n] * strict for c in chunks]
    mab_b = [bf(t) for t in mab]
    mak_b = [bf(p[c][0:n, n:] * strict) for c in chunks]
    mrb_b = [bf(p[c][n:, 0:n] * incl) for c in chunks]
    mrk_b = [bf(p[c][n:, n:] * incl) for c in chunks]

    x = [eye - jnp.where((ri // 2 == ci // 2), mab[c], 0.0) for c in chunks]
    w1 = [_mdot(mak_b[c], v_b[c]) for c in chunks]
    zero_b = jnp.zeros((), BF16)
    s = 2
    while s < lc:
        lvl = (ri // (2 * s) == ci // (2 * s)) & (ri // s != ci // s)
        x_b = [bf(t) for t in x]
        nx = [_mdot(jnp.where(lvl, mab_b[c], zero_b), x_b[c]) for c in chunks]
        x = [x[c] - _mdot(x_b[c], nx[c]) for c in chunks]
        s *= 2

    ta = [_mdot(x[c], jnp.concatenate([w1[c], a_st[c]], axis=1)) for c in chunks]
    zeros_b = jnp.zeros((n, LANES), BF16)
    ugv_b = [jnp.concatenate([bf(jnp.concatenate([-ta[c][:, 0:LANES], ta[c][:, LANES:]], axis=1)),
                              jnp.concatenate([v_b[c], zeros_b], axis=1)], axis=0) for c in chunks]
    ru = [_mdot(jnp.concatenate([mrb_b[c], mrk_b[c]], axis=1), ugv_b[c]) for c in chunks]
    y0 = [ru[c][:, 0:LANES] for c in chunks]
    q = [r_st[c] - ru[c][:, LANES:] for c in chunks]
    pl_row = [e_pos[(c + 1) * lc - 1:(c + 1) * lc, :] for c in chunks]
    bk_p = [bf(jnp.concatenate([b_st[c] * pl_row[c], k_st[c] * pl_row[c]], axis=0)) for c in chunks]
    bu = [_mdot(bk_p[c], ugv_b[c], dims=TN) for c in chunks]
    ct = [bu[c][:, 0:LANES] for c in chunks]
    at = [eye * pl_row[c] - bu[c][:, LANES:] for c in chunks]

    for c in chunks:
        rows = slice(c * lc, (c + 1) * lc)
        y0_ref[rows, :] = y0[c][0:lc] + y0[c][lc:]
        q_ref[rows, :] = (q[c][0:lc] + q[c][lc:]).astype(q_ref.dtype)
        a_ref[0, c, 0] = (at[c][0:RWKV_HEAD] + at[c][RWKV_HEAD:]).astype(a_ref.dtype)
        c_ref[0, c, 0] = ct[c][0:RWKV_HEAD] + ct[c][RWKV_HEAD:]


def _rwkv_chunk(r, ld, kt, v, kkn, bb, batch, seq):
    m = r.shape[0]
    tb = RWKV_CORE_TB
    per_b = seq // tb
    cpb = tb // RWKV_CHUNK
    npair = D_RWKV // LANES
    slab = pl.BlockSpec((tb, LANES), lambda b, p, j: (b * per_b + j, p))
    mat = pl.BlockSpec((1, cpb, 1, RWKV_HEAD, LANES), lambda b, p, j: (b, j, p, 0, 0))
    mshape = (batch, seq // RWKV_CHUNK, npair, RWKV_HEAD, LANES)
    return pl.pallas_call(
        _rwkv_chunk_kernel,
        out_shape=[jax.ShapeDtypeStruct((m, D_RWKV), F32), jax.ShapeDtypeStruct((m, D_RWKV), BF16),
                   jax.ShapeDtypeStruct(mshape, BF16), jax.ShapeDtypeStruct(mshape, F32)],
        grid=(batch, npair, per_b),
        in_specs=[slab] * 6,
        out_specs=[slab, slab, mat, mat],
        compiler_params=_cparams(("arbitrary", "arbitrary", "arbitrary")),
        name="rwkv_chunk",
    )(r, ld, kt, v, kkn, bb)


def _rwkv_scan_kernel(y0_ref, q_ref, a_ref, c_ref, g_ref, bonus_ref, gng_ref, gnb_ref, ya_ref, s_ref):
    nb, rows, _ = y0_ref.shape
    lc = RWKV_CHUNK
    npair = D_RWKV // LANES

    @pl.when(pl.program_id(0) == 0)
    def _():
        s_ref[...] = jnp.zeros_like(s_ref)

    ones_blk = _head_block_ones(LANES, RWKV_HEAD)
    head0 = _iota2((RWKV_HEAD, LANES), 1) < RWKV_HEAD

    def block_diag(t):
        zero = jnp.zeros((), t.dtype)
        return jnp.concatenate([jnp.where(head0, t, zero), jnp.where(head0, zero, t)], axis=0)

    streams = [(b, p) for b in range(nb) for p in range(npair)]
    for ci in range(rows // lc):
        rs = slice(ci * lc, (ci + 1) * lc)
        state_b = [s_ref[b * npair + p].astype(BF16) for b, p in streams]
        y = [y0_ref[b, rs, p * LANES:(p + 1) * LANES]
             + _mdot(q_ref[b, rs, p * LANES:(p + 1) * LANES], state_b[i])
             for i, (b, p) in enumerate(streams)]
        for i, (b, p) in enumerate(streams):
            s_ref[b * npair + p] = (_mdot(block_diag(a_ref[b, ci, p]), state_b[i])
                                    + block_diag(c_ref[b, ci, p]))
        mu = [_head_sum(t, ones_blk) * (1.0 / RWKV_HEAD) for t in y]
        yc = [y[i] - mu[i] for i in range(len(streams))]
        var = [_head_sum(t * t, ones_blk) * (1.0 / RWKV_HEAD) for t in yc]
        for i, (b, p) in enumerate(streams):
            ls = slice(p * LANES, (p + 1) * LANES)
            yn = yc[i] * lax.rsqrt(var[i] + RWKV_GN_EPS) * gng_ref[:, ls] + gnb_ref[:, ls]
            ya_ref[b, rs, ls] = ((yn + bonus_ref[b, rs, ls]) * g_ref[b, rs, ls]).astype(ya_ref.dtype)


def _rwkv_scan(y0, q, a_mat, c_mat, g, bonus, gn_g, gn_b, batch, seq):
    rows = RWKV_SCAN_ROWS
    npair = D_RWKV // LANES
    tok = pl.BlockSpec((batch, rows, D_RWKV), lambda j: (0, j, 0))
    mat = pl.BlockSpec((batch, rows // RWKV_CHUNK, npair, RWKV_HEAD, LANES), lambda j: (0, j, 0, 0, 0))
    vec = pl.BlockSpec((1, D_RWKV), lambda j: (0, 0))
    shape3 = (batch, seq, D_RWKV)
    return pl.pallas_call(
        _rwkv_scan_kernel,
        out_shape=jax.ShapeDtypeStruct(shape3, BF16),
        grid=(seq // rows,),
        in_specs=[tok, tok, mat, mat, tok, tok, vec, vec],
        out_specs=tok,
        scratch_shapes=[pltpu.VMEM((batch * npair, LANES, LANES), F32)],
        compiler_params=_cparams(("arbitrary",)),
        name="rwkv_scan",
    )(y0.reshape(shape3), q.reshape(shape3), a_mat, c_mat, g.reshape(shape3), bonus.reshape(shape3),
      gn_g, gn_b).reshape(batch * seq, D_RWKV)


def _mlstm_kernel(z_ref, zif_ref, wconv_ref, bconv_ref, gbias_ref, yb_ref, tail_ref, ct_ref, n_ref, m_ref):
    lc = z_ref.shape[0]
    dqk, dv, nh = MLSTM_DQK, MLSTM_DV, MLSTM_HEADS

    @pl.when(pl.program_id(1) == 0)
    def _():
        tail_ref[...] = jnp.zeros_like(tail_ref)
        ct_ref[...] = jnp.zeros_like(ct_ref)
        n_ref[...] = jnp.zeros_like(n_ref)
        m_ref[...] = jnp.zeros_like(m_ref)

    zqk = z_ref[:, 0:2 * D_QK]
    ext = jnp.concatenate([tail_ref[...], zqk], axis=0)
    tail_ref[...] = zqk[lc - 8:lc, :]
    conv = bconv_ref[...] + wconv_ref[CONV_K - 1:CONV_K, :] * zqk
    for d in range(1, CONV_K):
        conv = conv + wconv_ref[CONV_K - 1 - d:CONV_K - d, :] * ext[8 - d:8 - d + lc, :]
    qk = conv * _sigmoid(conv)
    q_all = qk[:, 0:D_QK]
    k_all = qk[:, D_QK:] * (dqk ** -0.5)
    v_all = z_ref[:, 2 * D_QK:2 * D_QK + D_MLSTM]
    o_all = z_ref[:, 2 * D_QK + D_MLSTM:]

    gz = zif_ref[...] + gbias_ref[...]
    lf_col = jnp.minimum(gz, 0.0) - jnp.log(1.0 + jnp.exp(-jnp.abs(gz)))
    rr, cc = _iota2((lc, lc), 0), _iota2((lc, lc), 1)
    causal = cc <= rr
    tri = jnp.where(causal, 1.0, 0.0).astype(BF16)
    b_col = _mdot(tri, lf_col, 1, 3)
    sel = jnp.where(_iota2((8, LANES), 0) == _iota2((8, LANES), 1), 1.0, 0.0).astype(BF16)
    g_row = _mdot(sel, gz, 1, 3, NT)
    lf_row = jnp.minimum(g_row, 0.0) - jnp.log(1.0 + jnp.exp(-jnp.abs(g_row)))
    b_row = _mdot(lf_row, tri, 3, 1, NT)

    heads = range(nh)
    qf = [q_all[:, h * dqk:(h + 1) * dqk] for h in heads]
    q = [t.astype(BF16) for t in qf]
    k = [k_all[:, h * dqk:(h + 1) * dqk] for h in heads]
    v = [v_all[:, h * dv:(h + 1) * dv].astype(BF16) for h in heads]
    bc = [b_col[:, nh + h:nh + h + 1] for h in heads]
    ic = [gz[:, h:h + 1] for h in heads]
    br = [b_row[nh + h:nh + h + 1, :] for h in heads]
    ir = [g_row[h:h + 1, :] for h in heads]
    g_tot = [b_col[lc - 1:lc, nh + h:nh + h + 1] for h in heads]
    m_prev = [m_ref[0:1, h:h + 1] for h in heads]
    ct = [ct_ref[h] for h in heads]
    nvec = [n_ref[h:h + 1, :] for h in heads]

    qk = [_mdot(q[h], k[h], dims=NT) for h in heads]
    qc = [_mdot(q[h], ct[h]) for h in heads]
    bri = [br[h] - ir[h] for h in heads]
    dm = [jnp.where(causal, bc[h] - bri[h], -jnp.inf) for h in heads]
    inter = [bc[h] + m_prev[h] for h in heads]
    mt = [jnp.maximum(inter[h], jnp.max(dm[h], axis=-1, keepdims=True)) for h in heads]
    sc = [qk[h] * jnp.exp(dm[h] - mt[h]) for h in heads]
    winter = [jnp.exp(inter[h] - mt[h]) for h in heads]
    sv = [_mdot(sc[h], v[h]) for h in heads]
    m_new = [jnp.maximum(g_tot[h] + m_prev[h], jnp.max(g_tot[h] - bri[h], axis=-1, keepdims=True))
             for h in heads]
    kw = [k[h] * jnp.exp(g_tot[h] - bc[h] + ic[h] - m_new[h]) for h in heads]
    kv = [_mdot(kw[h], v[h], dims=TN) for h in heads]
    for h in heads:
        num = winter[h] * qc[h] + sv[h]
        den = (winter[h] * jnp.sum(qf[h] * nvec[h], axis=-1, keepdims=True)
               + jnp.sum(sc[h], axis=-1, keepdims=True))
        hout = num / jnp.maximum(jnp.abs(den), jnp.exp(-mt[h]))
        wc = jnp.exp(g_tot[h] + m_prev[h] - m_new[h])
        ct_ref[h] = wc * ct[h] + kv[h]
        n_ref[h:h + 1, :] = wc * nvec[h] + jnp.sum(kw[h], axis=0, keepdims=True)
        m_ref[0:1, h:h + 1] = m_new[h]
        o = o_all[:, h * dv:(h + 1) * dv]
        yb_ref[:, h * dv:(h + 1) * dv] = (_sigmoid(o) * hout).astype(yb_ref.dtype)


def _mlstm(z_ml, z_if, w_conv, b_conv, gbias, batch, seq):
    m = z_ml.shape[0]
    lc = MLSTM_CHUNK
    per_b = seq // lc
    const = lambda b, j: (0, 0)
    return pl.pallas_call(
        _mlstm_kernel,
        out_shape=jax.ShapeDtypeStruct((m, D_MLSTM), BF16),
        grid=(batch, per_b),
        in_specs=[pl.BlockSpec((lc, ML_COLS), lambda b, j: (b * per_b + j, 0)),
                  pl.BlockSpec((lc, ML_GATE_PAD), lambda b, j: (b * per_b + j, 0)),
                  pl.BlockSpec((CONV_K, 2 * D_QK), const),
                  pl.BlockSpec((1, 2 * D_QK), const),
                  pl.BlockSpec((1, ML_GATE_PAD), const)],
        out_specs=pl.BlockSpec((lc, D_MLSTM), lambda b, j: (b * per_b + j, 0)),
        scratch_shapes=[pltpu.VMEM((8, 2 * D_QK), F32),
                        pltpu.VMEM((MLSTM_HEADS, MLSTM_DQK, MLSTM_DV), F32),
                        pltpu.VMEM((8, MLSTM_DQK), F32),
                        pltpu.VMEM((8, LANES), F32)],
        compiler_params=_cparams(("arbitrary", "arbitrary")),
        name="mlstm",
    )(z_ml, z_if, w_conv, b_conv, gbias)


def _merge_kernel(ya_ref, yb_ref, zg_ref, wa_ref, wb_ref, o_ref):
    groups = [slice(i, i + EPILOGUE_ROWS) for i in range(0, ya_ref.shape[0], EPILOGUE_ROWS)]
    pab = [(jnp.dot(ya_ref[rows, :], wa_ref[...], preferred_element_type=F32),
            jnp.dot(yb_ref[rows, :], wb_ref[...], preferred_element_type=F32)) for rows in groups]
    for (pa, pb), rows in zip(pab, groups):
        ga = _sigmoid(zg_ref[rows, 0:D_MODEL].astype(F32))
        gb = _sigmoid(zg_ref[rows, D_MODEL:].astype(F32))
        o_ref[rows, :] = (ga * pa + gb * pb).astype(o_ref.dtype)


def _merge(ya, yb, zg, wa, wb):
    m = ya.shape[0]
    tm = 512
    const = lambda i: (0, 0)
    return pl.pallas_call(
        _merge_kernel,
        out_shape=jax.ShapeDtypeStruct((m, D_MODEL), BF16),
        grid=(m // tm,),
        in_specs=[pl.BlockSpec((tm, D_RWKV), lambda i: (i, 0)),
                  pl.BlockSpec((tm, D_MLSTM), lambda i: (i, 0)),
                  pl.BlockSpec((tm, 2 * D_MODEL), lambda i: (i, 0)),
                  pl.BlockSpec((D_RWKV, D_MODEL), const),
                  pl.BlockSpec((D_MLSTM, D_MODEL), const)],
        out_specs=pl.BlockSpec((tm, D_MODEL), lambda i: (i, 0)),
        compiler_params=_cparams(("arbitrary",)),
        name="merge",
    )(ya, yb, zg, wa, wb)


def _outproj_kernel(mg_ref, w_ref, x_ref, ada_ref, g_ref, b_ref, x1_ref, u2_ref):
    groups = [slice(i, i + EPILOGUE_ROWS) for i in range(0, mg_ref.shape[0], EPILOGUE_ROWS)]
    ys = [jnp.dot(mg_ref[rows, :], w_ref[...], preferred_element_type=F32) for rows in groups]
    g1 = ada_ref[0, 2:3, :]
    sh2, sc2 = ada_ref[0, 3:4, :], ada_ref[0, 4:5, :]
    for y, rows in zip(ys, groups):
        x1 = _layernorm(ALPHA * x_ref[rows, :] + g1 * y) * g_ref[...] + b_ref[...]
        x1_ref[rows, :] = x1
        u2_ref[rows, :] = (_layernorm(x1) * (1.0 + sc2) + sh2).astype(u2_ref.dtype)


def _outproj(merged, w_out, x2, ada3, ln_g, ln_b, seq):
    m = merged.shape[0]
    tm = 512
    per_b = seq // tm
    const = lambda i: (0, 0)
    row = pl.BlockSpec((tm, D_MODEL), lambda i: (i, 0))
    return pl.pallas_call(
        _outproj_kernel,
        out_shape=[jax.ShapeDtypeStruct((m, D_MODEL), F32), jax.ShapeDtypeStruct((m, D_MODEL), BF16)],
        grid=(m // tm,),
        in_specs=[row, pl.BlockSpec((D_MODEL, D_MODEL), const, pipeline_mode=pl.Buffered(1)), row,
                  pl.BlockSpec((1, 6, D_MODEL), lambda i: (i // per_b, 0, 0)),
                  pl.BlockSpec((1, D_MODEL), const), pl.BlockSpec((1, D_MODEL), const)],
        out_specs=[row, row],
        compiler_params=_cparams(("arbitrary",)),
        name="outproj",
    )(merged, w_out, x2, ada3, ln_g, ln_b)


def _ff1_kernel(x_ref, w_ref, b_ref, w2_ref, o_ref, w2b_ref, wb_ref):
    @pl.when(pl.program_id(1) == 0)
    def _():
        wb_ref[...] = w_ref[...].astype(wb_ref.dtype)

    w2b_ref[...] = w2_ref[...].astype(w2b_ref.dtype)
    groups = [slice(i, i + 2 * EPILOGUE_ROWS) for i in range(0, x_ref.shape[0], 2 * EPILOGUE_ROWS)]
    hs = [jnp.dot(x_ref[rows, :], wb_ref[...], preferred_element_type=F32) for rows in groups]
    for h, rows in zip(hs, groups):
        h = jnp.maximum(h + b_ref[...], 0.0)
        o_ref[rows, :] = (h * h).astype(o_ref.dtype)


def _ff1(u2, w1, b1, w2):
    m, k = u2.shape
    n = w1.shape[1]
    tm, tn = 1024, 1024
    n_m = m // tm
    rows2 = w2.shape[0] // ((n // tn) * n_m)
    assert rows2 % 16 == 0 and rows2 * (n // tn) * n_m == w2.shape[0]
    slab = pl.BlockSpec((rows2, w2.shape[1]), lambda j, i: (j * n_m + i, 0))
    return pl.pallas_call(
        _ff1_kernel,
        out_shape=[jax.ShapeDtypeStruct((m, n), BF16), jax.ShapeDtypeStruct(w2.shape, BF16)],
        grid=(n // tn, n_m),
        in_specs=[pl.BlockSpec((tm, k), lambda j, i: (i, 0)),
                  pl.BlockSpec((k, tn), lambda j, i: (0, j)),
                  pl.BlockSpec((1, tn), lambda j, i: (0, j)),
                  slab],
        out_specs=[pl.BlockSpec((tm, tn), lambda j, i: (i, j)), slab],
        scratch_shapes=[pltpu.VMEM((k, tn), BF16)],
        compiler_params=_cparams(("arbitrary", "arbitrary")),
        name="ff1",
    )(u2, w1, b1, w2)


def _ff2_kernel(h_ref, w_ref, b_ref, x1_ref, ada_ref, g_ref, bb_ref, o_ref):
    y2 = jnp.dot(h_ref[...], w_ref[...], preferred_element_type=F32) + b_ref[...]
    g2 = ada_ref[0, 5:6, :]
    o_ref[...] = (_layernorm(ALPHA * x1_ref[...] + g2 * y2) * g_ref[...] + bb_ref[...]).astype(o_ref.dtype)


def _ff2(h, w2, b2, x1, ada3, ln_g, ln_b, seq, out_dtype):
    m, k = h.shape
    tm = 256
    per_b = seq // tm
    const = lambda i: (0, 0)
    row = pl.BlockSpec((tm, D_MODEL), lambda i: (i, 0))
    return pl.pallas_call(
        _ff2_kernel,
        out_shape=jax.ShapeDtypeStruct((m, D_MODEL), out_dtype),
        grid=(m // tm,),
        in_specs=[pl.BlockSpec((tm, k), lambda i: (i, 0)),
                  pl.BlockSpec((k, D_MODEL), const, pipeline_mode=pl.Buffered(1)),
                  pl.BlockSpec((1, D_MODEL), const), row,
                  pl.BlockSpec((1, 6, D_MODEL), lambda i: (i // per_b, 0, 0)),
                  pl.BlockSpec((1, D_MODEL), const), pl.BlockSpec((1, D_MODEL), const)],
        out_specs=row,
        compiler_params=_cparams(("arbitrary",)),
        name="ff2",
    )(h, w2, b2, x1, ada3, ln_g, ln_b)


def _pad_rows(w, height):
    return jnp.pad(w, ((0, height - w.shape[0]), (0, 0)))


def _layer(x2, c, batch, seq, w_ada, b_ada, w_in, mu_shift, w0, w_decay_up, a0, w_a_up, w_g_up, k_k, k_a, r_k,
           gn_g, gn_b, w_conv, b_conv, b_igate, b_fgate, w_branch_a, w_branch_b, w_out,
           ln1_g, ln1_b, w_ff1, b_ff1, w_ff2, b_ff2, ln2_g, ln2_b):
    row = lambda v: v.reshape(1, -1)

    ada3 = _ada(c, w_ada, b_ada).reshape(batch, 6, D_MODEL)
    u = _lnmod(x2, ada3, seq)

    assert w_in.shape == (D_MODEL, N_RWKV_COLS + N_MLSTM_COLS + 2 * D_MODEL)
    w_in_t = jnp.swapaxes(w_in, 0, 1)
    z_rw = _win_matmul(u, w_in_t, 0, RW_COLS, 2048, 512, F32, "win_rwkv")
    col_if = N_RWKV_COLS + 2 * D_QK + D_MLSTM
    z_ml = _win_matmul(u, w_in_t, N_RWKV_COLS, ML_COLS, 1024, 1024, F32, "win_mlstm",
                       skip=((2 * D_QK + D_MLSTM) // 1024, 2 * MLSTM_HEADS))
    z_gt, w_a_b, w_b_b, w_out_b = _win_matmul(u, w_in_t, N_RWKV_COLS + N_MLSTM_COLS, 2 * D_MODEL, 1024, 1024, BF16,
                                              "win_gate", cast_along=(w_branch_a, w_branch_b, w_out))

    mu_rw = jnp.pad(mu_shift, (0, RW_COLS - N_RWKV_COLS))
    prep = _rwkv_prep(z_rw, row(mu_rw), row(w0), row(a0), row(k_k), row(k_a), row(r_k),
                      _pad_rows(w_decay_up, LORA_PAD), _pad_rows(w_a_up, LORA_PAD), w_g_up, batch, seq)
    r, ld, kt, v, kkn, bb, g, bonus = prep
    y0, q, a_mat, c_mat = _rwkv_chunk(r, ld, kt, v, kkn, bb, batch, seq)
    ya = _rwkv_scan(y0, q, a_mat, c_mat, g, bonus, row(gn_g), row(gn_b), batch, seq)

    gbias = jnp.pad(jnp.concatenate([b_igate, b_fgate]), (0, ML_GATE_PAD - 2 * MLSTM_HEADS))
    z_if = _win_matmul(u, w_in_t, col_if, ML_GATE_PAD, 2048, ML_GATE_PAD, F32, "win_if")
    yb = _mlstm(z_ml, z_if, w_conv, row(b_conv), row(gbias), batch, seq)

    merged = _merge(ya, yb, z_gt, w_a_b, w_b_b)
    x1, u2 = _outproj(merged, w_out_b, x2, ada3, row(ln1_g), row(ln1_b), seq)

    h, w_ff2_b = _ff1(u2, w_ff1, row(b_ff1), w_ff2)
    return _ff2(h, w_ff2_b, row(b_ff2), x1, ada3, row(ln2_g), row(ln2_b), seq, F32)


def kernel(x, c, w_ada, b_ada, w_in, mu_shift, w0, w_decay_up, a0, w_a_up, w_g_up, k_k, k_a, r_k, gn_g, gn_b,
           w_conv, b_conv, b_igate, b_fgate, w_branch_a, w_branch_b, w_out, ln1_g, ln1_b, w_ff1, b_ff1,
           w_ff2, b_ff2, ln2_g, ln2_b):
    out_dtype = x.dtype
    batch, seq, d = x.shape
    assert d == D_MODEL and seq % 1024 == 0 and w_ada.shape[0] == DEPTH
    x2 = x.astype(F32).reshape(batch * seq, d)
    cf = c.astype(F32)
    for l in range(DEPTH):
        x2 = _layer(x2, cf, batch, seq, w_ada[l], b_ada[l], w_in[l], mu_shift[l], w0[l], w_decay_up[l], a0[l],
                    w_a_up[l], w_g_up[l], k_k[l], k_a[l], r_k[l].reshape(-1), gn_g[l], gn_b[l], w_conv[l],
                    b_conv[l], b_igate[l], b_fgate[l], w_branch_a[l], w_branch_b[l], w_out[l], ln1_g[l],
                    ln1_b[l], w_ff1[l], b_ff1[l], w_ff2[l], b_ff2[l], ln2_g[l], ln2_b[l])
    return x2.reshape(batch, seq, d).astype(out_dtype)
```

```python
import functools
import math

import jax
import jax.numpy as jnp
from jax import lax
from jax.experimental import pallas as pl
from jax.experimental.pallas import tpu as pltpu

F32 = jnp.float32
BF16 = jnp.bfloat16

D_MODEL = 2048
DEPTH = 1
D_RWKV = D_MODEL // 2
RWKV_HEAD = 64
DECAY_LORA = 96
A_LORA = 96
GATE_LORA = 256
RWKV_GN_EPS = 64e-5
D_MLSTM = D_MODEL // 2
MLSTM_HEADS = 4
MLSTM_DV = D_MLSTM // MLSTM_HEADS
MLSTM_DQK = MLSTM_DV // 2
D_QK = MLSTM_HEADS * MLSTM_DQK
CONV_K = 4
D_FF = 4 * D_MODEL
LN_EPS = 1e-5
ALPHA = (2.0 * DEPTH) ** 0.25

LANES = 128
LORA_PAD = 128
N_RWKV_COLS = 3 * D_RWKV + DECAY_LORA + A_LORA + GATE_LORA
N_MLSTM_COLS = 2 * D_QK + 2 * D_MLSTM + 2 * MLSTM_HEADS
RW_COLS = 3584
ML_GATE_PAD = 128
ML_COLS = 2 * D_QK + 2 * D_MLSTM
RWKV_CHUNK = 64
RWKV_TB = 256
RWKV_SCAN_ROWS = 128
RWKV_CORE_TB = 1024
MLSTM_CHUNK = 256
EPILOGUE_ROWS = 256
VMEM_LIMIT = 56 * 1024 * 1024

NN = (((1,), (0,)), ((), ()))
NT = (((1,), (1,)), ((), ()))
TN = (((0,), (0,)), ((), ()))


def _cparams(sem):
    return pltpu.CompilerParams(dimension_semantics=sem, vmem_limit_bytes=VMEM_LIMIT)


def _bf16_parts(x, n):
    if x.dtype == BF16:
        return [x]
    parts, rem = [], x
    for i in range(n):
        p = rem.astype(BF16)
        parts.append(p)
        if i + 1 < n:
            rem = rem - p.astype(F32)
    return parts


def _mdot(a, b, na=1, nb=1, dims=NN):
    ap, bp = _bf16_parts(a, na), _bf16_parts(b, nb)
    order = max(len(ap), len(bp))
    acc = None
    for i, x in enumerate(ap):
        for j, y in enumerate(bp):
            if i + j < order:
                t = lax.dot_general(x, y, dims, preferred_element_type=F32)
                acc = t if acc is None else acc + t
    return acc


def _sigmoid(x):
    return 1.0 / (1.0 + jnp.exp(-x))


def _layernorm(x):
    mu = jnp.mean(x, axis=-1, keepdims=True)
    xc = x - mu
    var = jnp.mean(xc * xc, axis=-1, keepdims=True)
    return xc * lax.rsqrt(var + LN_EPS)


def _iota2(shape, dim):
    return lax.broadcasted_iota(jnp.int32, shape, dim)


def _head_block_ones(n, head):
    r, c = _iota2((n, n), 0), _iota2((n, n), 1)
    return jnp.where((r // head) == (c // head), 1.0, 0.0).astype(BF16)


def _head_sum(x, ones_blk):
    return _mdot(x, ones_blk)


def _ada_kernel(ct_ref, w_ref, b_ref, o_ref, sb_ref):
    tn = w_ref.shape[1]
    nb = sb_ref.shape[0]

    @pl.when(pl.program_id(0) == 0)
    def _():
        cv = ct_ref[...]
        sv = cv * _sigmoid(cv)
        for b in range(nb):
            sb_ref[b] = jnp.broadcast_to(sv[:, b:b + 1], sb_ref.shape[1:])

    ntile = tn // LANES

    def body(i, acc):
        r0 = pl.multiple_of(i * 8, 8)
        sb = [sb_ref[b, pl.ds(r0, 8), :] for b in range(nb)]
        wv = [w_ref[pl.ds(r0, 8), t * LANES:(t + 1) * LANES] for t in range(ntile)]
        return tuple(acc[b * ntile + t] + sb[b] * wv[t] for b in range(nb) for t in range(ntile))

    zero = jnp.zeros((8, LANES), F32)
    acc = lax.fori_loop(0, w_ref.shape[0] // 8, body, (zero,) * (nb * ntile), unroll=4)
    for b in range(nb):
        tot = jnp.concatenate([acc[b * ntile + t] for t in range(ntile)], axis=1)
        o_ref[b:b + 1, :] = jnp.sum(tot, axis=0, keepdims=True) + b_ref[...]


def _ada(c, w_ada, b_ada):
    batch = c.shape[0]
    assert batch == 2
    n = w_ada.shape[1]
    tn = 1536
    return pl.pallas_call(
        _ada_kernel,
        out_shape=jax.ShapeDtypeStruct((batch, n), F32),
        grid=(n // tn,),
        in_specs=[pl.BlockSpec((D_MODEL, batch), lambda j: (0, 0)),
                  pl.BlockSpec((D_MODEL, tn), lambda j: (0, j)),
                  pl.BlockSpec((1, tn), lambda j: (0, j))],
        out_specs=pl.BlockSpec((batch, tn), lambda j: (0, j)),
        scratch_shapes=[pltpu.VMEM((batch, D_MODEL, LANES), F32)],
        compiler_params=_cparams(("arbitrary",)),
        name="ada",
    )(c.T, w_ada, b_ada.reshape(1, n))


def _lnmod_kernel(x_ref, ada_ref, o_ref):
    xn = _layernorm(x_ref[...])
    sh, sc = ada_ref[0, 0:1, :], ada_ref[0, 1:2, :]
    o_ref[...] = (xn * (1.0 + sc) + sh).astype(o_ref.dtype)


def _lnmod(x2, ada3, seq):
    m = x2.shape[0]
    tb = 512
    per_b = seq // tb
    return pl.pallas_call(
        _lnmod_kernel,
        out_shape=jax.ShapeDtypeStruct((m, D_MODEL), BF16),
        grid=(m // tb,),
        in_specs=[pl.BlockSpec((tb, D_MODEL), lambda i: (i, 0)),
                  pl.BlockSpec((1, 6, D_MODEL), lambda i: (i // per_b, 0, 0))],
        out_specs=pl.BlockSpec((tb, D_MODEL), lambda i: (i, 0)),
        compiler_params=_cparams(("arbitrary",)),
        name="lnmod",
    )(x2, ada3)


def _win_kernel(n_cast, x_ref, wt_ref, *refs):
    cast_in, o_ref, cast_out, wb_ref = refs[:n_cast], refs[n_cast], refs[n_cast + 1:-1], refs[-1]

    @pl.when(pl.program_id(1) == 0)
    def _():
        wb_ref[...] = wt_ref[...].astype(wb_ref.dtype)

    for src, dst in zip(cast_in, cast_out):
        dst[...] = src[...].astype(dst.dtype)
    o_ref[...] = lax.dot_general(x_ref[...], wb_ref[...], NT, preferred_element_type=F32).astype(o_ref.dtype)


def _win_matmul(x, wt, col0, n_out, tm, tn, out_dtype, name, skip=None, cast_along=()):
    m, k = x.shape
    skip_tile, gap = skip if skip else (n_out // tn, 0)
    assert col0 % 8 == 0 and gap % 8 == 0 and n_out % tn == 0 and col0 + n_out + gap <= wt.shape[0]
    n_m = m // tm
    steps = (n_out // tn) * n_m

    def w_index(j, i):
        return pl.multiple_of(col0 + j * tn + jnp.where(j >= skip_tile, gap, 0), 8), 0

    slabs = []
    for w, nrows in cast_along:
        rows = nrows // steps
        assert rows % 16 == 0 and rows * steps == nrows <= w.shape[0]
        slabs.append(pl.BlockSpec((rows, w.shape[1]), lambda j, i: (j * n_m + i, 0)))
    cast_along = [w for w, _ in cast_along]
    outs = pl.pallas_call(
        functools.partial(_win_kernel, len(cast_along)),
        out_shape=[jax.ShapeDtypeStruct((m, n_out), out_dtype)]
        + [jax.ShapeDtypeStruct((spec.block_shape[0] * steps, w.shape[1]), BF16)
           for w, spec in zip(cast_along, slabs)],
        grid=(n_out // tn, n_m),
        in_specs=[pl.BlockSpec((tm, k), lambda j, i: (i, 0)),
                  pl.BlockSpec((pl.Element(tn), pl.Element(k)), w_index)] + slabs,
        out_specs=[pl.BlockSpec((tm, tn), lambda j, i: (i, j))] + slabs,
        scratch_shapes=[pltpu.VMEM((tn, k), BF16)],
        compiler_params=_cparams(("arbitrary", "arbitrary")),
        name=name,
    )(x, wt, *cast_along)
    return outs if cast_along else outs[0]


def _rwkv_prep_kernel(u_ref, wt_ref, mu_ref, w0_ref, a0_ref, kk_ref, ka_ref, rk_ref, wdu_ref, wau_ref, wgu_ref,
                      r_o, ld_o, kt_o, v_o, kkn_o, bb_o, g_o, bonus_o, carry_ref):
    tb = u_ref.shape[0]

    @pl.when(pl.program_id(1) == 0)
    def _():
        carry_ref[...] = jnp.zeros_like(carry_ref)

    z = lax.dot_general(u_ref[...], wt_ref[...], NT, preferred_element_type=F32)
    prev = pltpu.roll(z, 1, 0)
    prev = jnp.where(_iota2(z.shape, 0) == 0, carry_ref[0:1, :], prev)
    carry_ref[0:1, :] = z[tb - 1:tb, :]
    zs = z + (prev - z) * mu_ref[...]

    c = D_RWKV
    r, k, v = zs[:, 0:c], zs[:, c:2 * c], zs[:, 2 * c:3 * c]
    wd = zs[:, 3 * c:3 * c + LORA_PAD]
    ad = zs[:, 3 * c + DECAY_LORA:3 * c + DECAY_LORA + LORA_PAD]
    gd = zs[:, 3 * c + DECAY_LORA + A_LORA:3 * c + DECAY_LORA + A_LORA + GATE_LORA]

    ld = -math.exp(-0.5) * _sigmoid(w0_ref[...] + _mdot(jnp.tanh(wd), wdu_ref[...], 2, 2))
    a = _sigmoid(a0_ref[...] + _mdot(ad, wau_ref[...]))
    g = _mdot(_sigmoid(gd), wgu_ref[...], 1, 1)

    ones_blk = _head_block_ones(LANES, RWKV_HEAD)
    kq = k * kk_ref[...]
    kt = k * (1.0 + (a - 1.0) * ka_ref[...])
    sq = kq * kq
    bn = r * kt * rk_ref[...]
    ss = jnp.concatenate([_head_sum(sq[:, s:s + LANES], ones_blk) for s in range(0, c, LANES)], axis=1)
    bs = jnp.concatenate([_head_sum(bn[:, s:s + LANES], ones_blk) for s in range(0, c, LANES)], axis=1)
    kkn = kq * lax.rsqrt(ss + 1e-12)

    r_o[...] = r.astype(r_o.dtype)
    ld_o[...] = ld
    kt_o[...] = kt.astype(kt_o.dtype)
    v_o[...] = v.astype(v_o.dtype)
    kkn_o[...] = kkn.astype(kkn_o.dtype)
    bb_o[...] = (a * kkn).astype(bb_o.dtype)
    g_o[...] = g.astype(g_o.dtype)
    bonus_o[...] = (bs * v).astype(bonus_o.dtype)


def _rwkv_prep(u, wt_rw, mu_rw, w0, a0, k_k, k_a, r_k, wdu, wau, wgu, batch, seq):
    m, k = u.shape
    assert wt_rw.shape == (RW_COLS, k)
    tb = RWKV_TB
    per_b = seq // tb
    row = lambda b, j: (b * per_b + j, 0)
    const = lambda b, j: (0, 0)
    vec = pl.BlockSpec((1, D_RWKV), const)
    outs = [jax.ShapeDtypeStruct((m, D_RWKV), F32 if i == 1 else BF16) for i in range(8)]
    return pl.pallas_call(
        _rwkv_prep_kernel,
        out_shape=outs,
        grid=(batch, per_b),
        in_specs=[pl.BlockSpec((tb, k), row),
                  pl.BlockSpec((RW_COLS, k), const, pipeline_mode=pl.Buffered(1)),
                  pl.BlockSpec((1, RW_COLS), const),
                  vec, vec, vec, vec, vec,
                  pl.BlockSpec((LORA_PAD, D_RWKV), const), pl.BlockSpec((LORA_PAD, D_RWKV), const),
                  pl.BlockSpec((GATE_LORA, D_RWKV), const)],
        out_specs=[pl.BlockSpec((tb, D_RWKV), row)] * 8,
        scratch_shapes=[pltpu.VMEM((8, RW_COLS), F32)],
        compiler_params=_cparams(("arbitrary", "arbitrary")),
        name="rwkv_prep",
    )(u, wt_rw, mu_rw, w0, a0, k_k, k_a, r_k, wdu, wau, wgu)


def _rwkv_chunk_kernel(r_ref, ld_ref, kt_ref, v_ref, kk_ref, bb_ref, y0_ref, q_ref, a_ref, c_ref):
    tb = r_ref.shape[0]
    lc = RWKV_CHUNK
    n = 2 * lc
    nchunk = tb // lc

    lane = _iota2((1, LANES), 1)
    m0 = jnp.where(lane < RWKV_HEAD, 1.0, 0.0)
    m1 = 1.0 - m0
    ri, ci = _iota2((n, n), 0), _iota2((n, n), 1)
    same_head = (ri // lc) == (ci // lc)
    strict = jnp.where(same_head & (ri > ci), 1.0, 0.0)
    incl = jnp.where(same_head & (ri >= ci), 1.0, 0.0)
    eye = jnp.where(ri == ci, 1.0, 0.0)

    tw = 4 * lc
    rt, ct = _iota2((tw, tw), 0), _iota2((tw, tw), 1)
    tri = jnp.where(((rt // lc) == (ct // lc)) & (ct <= rt), 1.0, 0.0).astype(BF16)

    ld = ld_ref[...]
    cl = jnp.concatenate([_mdot(tri, ld[i:i + tw], 1, 3) for i in range(0, tb, tw)], axis=0)
    e_pos = jnp.exp(cl)
    e_neg = jnp.exp(-cl)
    a_hat = kk_ref[...] * jnp.exp(cl - ld)
    b_hat = bb_ref[...] * e_neg
    k_hat = kt_ref[...] * e_neg
    r_hat = r_ref[...] * e_pos
    v_all = v_ref[...]

    def stack(x):
        return jnp.concatenate([x * m0, x * m1], axis=0)

    def bf(t):
        return t.astype(BF16)

    chunks = range(nchunk)
    st = [[stack(x[c * lc:(c + 1) * lc]) for x in (a_hat, b_hat, k_hat, r_hat, v_all)] for c in chunks]
    a_st, b_st, k_st, r_st, v_st = ([st[c][i] for c in chunks] for i in range(5))
    b_b, k_b, v_b = ([bf(t) for t in ts] for ts in (b_st, k_st, v_st))
    p = [_mdot(jnp.concatenate([a_st[c], r_st[c]], axis=0),
               jnp.concatenate([b_b[c], k_b[c]], axis=0), dims=NT) for c in chunks]
    mab = [p[c][0:n, 0:n] * strict for c in chunks]
    mab_b = [bf(t) for t in mab]
    mak_b = [bf(p[c][0:n, n:] * strict) for c in chunks]
    mrb_b = [bf(p[c][n:, 0:n] * incl) for c in chunks]
    mrk_b = [bf(p[c][n:, n:] * incl) for c in chunks]

    x = [eye - jnp.where((ri // 2 == ci // 2), mab[c], 0.0) for c in chunks]
    w1 = [_mdot(mak_b[c], v_b[c]) for c in chunks]
    zero_b = jnp.zeros((), BF16)
    s = 2
    while s < lc:
        lvl = (ri // (2 * s) == ci // (2 * s)) & (ri // s != ci // s)
        x_b = [bf(t) for t in x]
        nx = [_mdot(jnp.where(lvl, mab_b[c], zero_b), x_b[c]) for c in chunks]
        x = [x[c] - _mdot(x_b[c], nx[c]) for c in chunks]
        s *= 2

    ta = [_mdot(x[c], jnp.concatenate([w1[c], a_st[c]], axis=1)) for c in chunks]
    zeros_b = jnp.zeros((n, LANES), BF16)
    ugv_b = [jnp.concatenate([bf(jnp.concatenate([-ta[c][:, 0:LANES], ta[c][:, LANES:]], axis=1)),
                              jnp.concatenate([v_b[c], zeros_b], axis=1)], axis=0) for c in chunks]
    ru = [_mdot(jnp.concatenate([mrb_b[c], mrk_b[c]], axis=1), ugv_b[c]) for c in chunks]
    y0 = [ru[c][:, 0:LANES] for c in chunks]
    q = [r_st[c] - ru[c][:, LANES:] for c in chunks]
    pl_row = [e_pos[(c + 1) * lc - 1:(c + 1) * lc, :] for c in chunks]
    bk_p = [bf(jnp.concatenate([b_st[c] * pl_row[c], k_st[c] * pl_row[c]], axis=0)) for c in chunks]
    bu = [_mdot(bk_p[c], ugv_b[c], dims=TN) for c in chunks]
    ct = [bu[c][:, 0:LANES] for c in chunks]
    at = [eye * pl_row[c] - bu[c][:, LANES:] for c in chunks]

    for c in chunks:
        rows = slice(c * lc, (c + 1) * lc)
        y0_ref[rows, :] = y0[c][0:lc] + y0[c][lc:]
        q_ref[rows, :] = (q[c][0:lc] + q[c][lc:]).astype(q_ref.dtype)
        a_ref[0, c, 0] = (at[c][0:RWKV_HEAD] + at[c][RWKV_HEAD:]).astype(a_ref.dtype)
        c_ref[0, c, 0] = ct[c][0:RWKV_HEAD] + ct[c][RWKV_HEAD:]


def _rwkv_chunk(r, ld, kt, v, kkn, bb, batch, seq):
    m = r.shape[0]
    tb = RWKV_CORE_TB
    per_b = seq // tb
    cpb = tb // RWKV_CHUNK
    npair = D_RWKV // LANES
    slab = pl.BlockSpec((tb, LANES), lambda b, p, j: (b * per_b + j, p))
    mat = pl.BlockSpec((1, cpb, 1, RWKV_HEAD, LANES), lambda b, p, j: (b, j, p, 0, 0))
    mshape = (batch, seq // RWKV_CHUNK, npair, RWKV_HEAD, LANES)
    return pl.pallas_call(
        _rwkv_chunk_kernel,
        out_shape=[jax.ShapeDtypeStruct((m, D_RWKV), F32), jax.ShapeDtypeStruct((m, D_RWKV), BF16),
                   jax.ShapeDtypeStruct(mshape, BF16), jax.ShapeDtypeStruct(mshape, F32)],
        grid=(batch, npair, per_b),
        in_specs=[slab] * 6,
        out_specs=[slab, slab, mat, mat],
        compiler_params=_cparams(("arbitrary", "arbitrary", "arbitrary")),
        name="rwkv_chunk",
    )(r, ld, kt, v, kkn, bb)


def _rwkv_scan_kernel(y0_ref, q_ref, a_ref, c_ref, g_ref, bonus_ref, gng_ref, gnb_ref, ya_ref, s_ref):
    nb, rows, _ = y0_ref.shape
    lc = RWKV_CHUNK
    npair = D_RWKV // LANES

    @pl.when(pl.program_id(0) == 0)
    def _():
        s_ref[...] = jnp.zeros_like(s_ref)

    ones_blk = _head_block_ones(LANES, RWKV_HEAD)
    head0 = _iota2((RWKV_HEAD, LANES), 1) < RWKV_HEAD

    def block_diag(t):
        zero = jnp.zeros((), t.dtype)
        return jnp.concatenate([jnp.where(head0, t, zero), jnp.where(head0, zero, t)], axis=0)

    streams = [(b, p) for b in range(nb) for p in range(npair)]
    for ci in range(rows // lc):
        rs = slice(ci * lc, (ci + 1) * lc)
        state_b = [s_ref[b * npair + p].astype(BF16) for b, p in streams]
        y = [y0_ref[b, rs, p * LANES:(p + 1) * LANES]
             + _mdot(q_ref[b, rs, p * LANES:(p + 1) * LANES], state_b[i])
             for i, (b, p) in enumerate(streams)]
        for i, (b, p) in enumerate(streams):
            s_ref[b * npair + p] = (_mdot(block_diag(a_ref[b, ci, p]), state_b[i])
                                    + block_diag(c_ref[b, ci, p]))
        mu = [_head_sum(t, ones_blk) * (1.0 / RWKV_HEAD) for t in y]
        yc = [y[i] - mu[i] for i in range(len(streams))]
        var = [_head_sum(t * t, ones_blk) * (1.0 / RWKV_HEAD) for t in yc]
        for i, (b, p) in enumerate(streams):
            ls = slice(p * LANES, (p + 1) * LANES)
            yn = yc[i] * lax.rsqrt(var[i] + RWKV_GN_EPS) * gng_ref[:, ls] + gnb_ref[:, ls]
            ya_ref[b, rs, ls] = ((yn + bonus_ref[b, rs, ls]) * g_ref[b, rs, ls]).astype(ya_ref.dtype)


def _rwkv_scan(y0, q, a_mat, c_mat, g, bonus, gn_g, gn_b, batch, seq):
    rows = RWKV_SCAN_ROWS
    npair = D_RWKV // LANES
    tok = pl.BlockSpec((batch, rows, D_RWKV), lambda j: (0, j, 0))
    mat = pl.BlockSpec((batch, rows // RWKV_CHUNK, npair, RWKV_HEAD, LANES), lambda j: (0, j, 0, 0, 0))
    vec = pl.BlockSpec((1, D_RWKV), lambda j: (0, 0))
    shape3 = (batch, seq, D_RWKV)
    return pl.pallas_call(
        _rwkv_scan_kernel,
        out_shape=jax.ShapeDtypeStruct(shape3, BF16),
        grid=(seq // rows,),
        in_specs=[tok, tok, mat, mat, tok, tok, vec, vec],
        out_specs=tok,
        scratch_shapes=[pltpu.VMEM((batch * npair, LANES, LANES), F32)],
        compiler_params=_cparams(("arbitrary",)),
        name="rwkv_scan",
    )(y0.reshape(shape3), q.reshape(shape3), a_mat, c_mat, g.reshape(shape3), bonus.reshape(shape3),
      gn_g, gn_b).reshape(batch * seq, D_RWKV)


def _mlstm_kernel(z_ref, zif_ref, wconv_ref, bconv_ref, gbias_ref, yb_ref, tail_ref, ct_ref, n_ref, m_ref):
    lc = z_ref.shape[0]
    dqk, dv, nh = MLSTM_DQK, MLSTM_DV, MLSTM_HEADS

    @pl.when(pl.program_id(1) == 0)
    def _():
        tail_ref[...] = jnp.zeros_like(tail_ref)
        ct_ref[...] = jnp.zeros_like(ct_ref)
        n_ref[...] = jnp.zeros_like(n_ref)
        m_ref[...] = jnp.zeros_like(m_ref)

    zqk = z_ref[:, 0:2 * D_QK]
    ext = jnp.concatenate([tail_ref[...], zqk], axis=0)
    tail_ref[...] = zqk[lc - 8:lc, :]
    conv = bconv_ref[...] + wconv_ref[CONV_K - 1:CONV_K, :] * zqk
    for d in range(1, CONV_K):
        conv = conv + wconv_ref[CONV_K - 1 - d:CONV_K - d, :] * ext[8 - d:8 - d + lc, :]
    qk = conv * _sigmoid(conv)
    q_all = qk[:, 0:D_QK]
    k_all = qk[:, D_QK:] * (dqk ** -0.5)
    v_all = z_ref[:, 2 * D_QK:2 * D_QK + D_MLSTM]
    o_all = z_ref[:, 2 * D_QK + D_MLSTM:]

    gz = zif_ref[...] + gbias_ref[...]
    lf_col = jnp.minimum(gz, 0.0) - jnp.log(1.0 + jnp.exp(-jnp.abs(gz)))
    rr, cc = _iota2((lc, lc), 0), _iota2((lc, lc), 1)
    causal = cc <= rr
    tri = jnp.where(causal, 1.0, 0.0).astype(BF16)
    b_col = _mdot(tri, lf_col, 1, 3)
    sel = jnp.where(_iota2((8, LANES), 0) == _iota2((8, LANES), 1), 1.0, 0.0).astype(BF16)
    g_row = _mdot(sel, gz, 1, 3, NT)
    lf_row = jnp.minimum(g_row, 0.0) - jnp.log(1.0 + jnp.exp(-jnp.abs(g_row)))
    b_row = _mdot(lf_row, tri, 3, 1, NT)

    heads = range(nh)
    qf = [q_all[:, h * dqk:(h + 1) * dqk] for h in heads]
    q = [t.astype(BF16) for t in qf]
    k = [k_all[:, h * dqk:(h + 1) * dqk] for h in heads]
    v = [v_all[:, h * dv:(h + 1) * dv].astype(BF16) for h in heads]
    bc = [b_col[:, nh + h:nh + h + 1] for h in heads]
    ic = [gz[:, h:h + 1] for h in heads]
    br = [b_row[nh + h:nh + h + 1, :] for h in heads]
    ir = [g_row[h:h + 1, :] for h in heads]
    g_tot = [b_col[lc - 1:lc, nh + h:nh + h + 1] for h in heads]
    m_prev = [m_ref[0:1, h:h + 1] for h in heads]
    ct = [ct_ref[h] for h in heads]
    nvec = [n_ref[h:h + 1, :] for h in heads]

    qk = [_mdot(q[h], k[h], dims=NT) for h in heads]
    qc = [_mdot(q[h], ct[h]) for h in heads]
    bri = [br[h] - ir[h] for h in heads]
    dm = [jnp.where(causal, bc[h] - bri[h], -jnp.inf) for h in heads]
    inter = [bc[h] + m_prev[h] for h in heads]
    mt = [jnp.maximum(inter[h], jnp.max(dm[h], axis=-1, keepdims=True)) for h in heads]
    sc = [qk[h] * jnp.exp(dm[h] - mt[h]) for h in heads]
    winter = [jnp.exp(inter[h] - mt[h]) for h in heads]
    sv = [_mdot(sc[h], v[h]) for h in heads]
    m_new = [jnp.maximum(g_tot[h] + m_prev[h], jnp.max(g_tot[h] - bri[h], axis=-1, keepdims=True))
             for h in heads]
    kw = [k[h] * jnp.exp(g_tot[h] - bc[h] + ic[h] - m_new[h]) for h in heads]
    kv = [_mdot(kw[h], v[h], dims=TN) for h in heads]
    for h in heads:
        num = winter[h] * qc[h] + sv[h]
        den = (winter[h] * jnp.sum(qf[h] * nvec[h], axis=-1, keepdims=True)
               + jnp.sum(sc[h], axis=-1, keepdims=True))
        hout = num / jnp.maximum(jnp.abs(den), jnp.exp(-mt[h]))
        wc = jnp.exp(g_tot[h] + m_prev[h] - m_new[h])
        ct_ref[h] = wc * ct[h] + kv[h]
        n_ref[h:h + 1, :] = wc * nvec[h] + jnp.sum(kw[h], axis=0, keepdims=True)
        m_ref[0:1, h:h + 1] = m_new[h]
        o = o_all[:, h * dv:(h + 1) * dv]
        yb_ref[:, h * dv:(h + 1) * dv] = (_sigmoid(o) * hout).astype(yb_ref.dtype)


def _mlstm(z_ml, z_if, w_conv, b_conv, gbias, batch, seq):
    m = z_ml.shape[0]
    lc = MLSTM_CHUNK
    per_b = seq // lc
    const = lambda b, j: (0, 0)
    return pl.pallas_call(
        _mlstm_kernel,
        out_shape=jax.ShapeDtypeStruct((m, D_MLSTM), BF16),
        grid=(batch, per_b),
        in_specs=[pl.BlockSpec((lc, ML_COLS), lambda b, j: (b * per_b + j, 0)),
                  pl.BlockSpec((lc, ML_GATE_PAD), lambda b, j: (b * per_b + j, 0)),
                  pl.BlockSpec((CONV_K, 2 * D_QK), const),
                  pl.BlockSpec((1, 2 * D_QK), const),
                  pl.BlockSpec((1, ML_GATE_PAD), const)],
        out_specs=pl.BlockSpec((lc, D_MLSTM), lambda b, j: (b * per_b + j, 0)),
        scratch_shapes=[pltpu.VMEM((8, 2 * D_QK), F32),
                        pltpu.VMEM((MLSTM_HEADS, MLSTM_DQK, MLSTM_DV), F32),
                        pltpu.VMEM((8, MLSTM_DQK), F32),
                        pltpu.VMEM((8, LANES), F32)],
        compiler_params=_cparams(("arbitrary", "arbitrary")),
        name="mlstm",
    )(z_ml, z_if, w_conv, b_conv, gbias)


def _merge_kernel(ya_ref, yb_ref, zg_ref, wa_ref, wb_ref, o_ref):
    groups = [slice(i, i + EPILOGUE_ROWS) for i in range(0, ya_ref.shape[0], EPILOGUE_ROWS)]
    pab = [(jnp.dot(ya_ref[rows, :], wa_ref[...], preferred_element_type=F32),
            jnp.dot(yb_ref[rows, :], wb_ref[...], preferred_element_type=F32)) for rows in groups]
    for (pa, pb), rows in zip(pab, groups):
        ga = _sigmoid(zg_ref[rows, 0:D_MODEL].astype(F32))
        gb = _sigmoid(zg_ref[rows, D_MODEL:].astype(F32))
        o_ref[rows, :] = (ga * pa + gb * pb).astype(o_ref.dtype)


def _merge(ya, yb, zg, wa, wb):
    m = ya.shape[0]
    tm = 512
    const = lambda i: (0, 0)
    return pl.pallas_call(
        _merge_kernel,
        out_shape=jax.ShapeDtypeStruct((m, D_MODEL), BF16),
        grid=(m // tm,),
        in_specs=[pl.BlockSpec((tm, D_RWKV), lambda i: (i, 0)),
                  pl.BlockSpec((tm, D_MLSTM), lambda i: (i, 0)),
                  pl.BlockSpec((tm, 2 * D_MODEL), lambda i: (i, 0)),
                  pl.BlockSpec((D_RWKV, D_MODEL), const),
                  pl.BlockSpec((D_MLSTM, D_MODEL), const)],
        out_specs=pl.BlockSpec((tm, D_MODEL), lambda i: (i, 0)),
        compiler_params=_cparams(("arbitrary",)),
        name="merge",
    )(ya, yb, zg, wa, wb)


def _outproj_kernel(mg_ref, w_ref, x_ref, ada_ref, g_ref, b_ref, x1_ref, u2_ref):
    groups = [slice(i, i + EPILOGUE_ROWS) for i in range(0, mg_ref.shape[0], EPILOGUE_ROWS)]
    ys = [jnp.dot(mg_ref[rows, :], w_ref[...], preferred_element_type=F32) for rows in groups]
    g1 = ada_ref[0, 2:3, :]
    sh2, sc2 = ada_ref[0, 3:4, :], ada_ref[0, 4:5, :]
    for y, rows in zip(ys, groups):
        x1 = _layernorm(ALPHA * x_ref[rows, :] + g1 * y) * g_ref[...] + b_ref[...]
        x1_ref[rows, :] = x1
        u2_ref[rows, :] = (_layernorm(x1) * (1.0 + sc2) + sh2).astype(u2_ref.dtype)


def _outproj(merged, w_out, x2, ada3, ln_g, ln_b, seq):
    m = merged.shape[0]
    tm = 512
    per_b = seq // tm
    const = lambda i: (0, 0)
    row = pl.BlockSpec((tm, D_MODEL), lambda i: (i, 0))
    return pl.pallas_call(
        _outproj_kernel,
        out_shape=[jax.ShapeDtypeStruct((m, D_MODEL), F32), jax.ShapeDtypeStruct((m, D_MODEL), BF16)],
        grid=(m // tm,),
        in_specs=[row, pl.BlockSpec((D_MODEL, D_MODEL), const, pipeline_mode=pl.Buffered(1)), row,
                  pl.BlockSpec((1, 6, D_MODEL), lambda i: (i // per_b, 0, 0)),
                  pl.BlockSpec((1, D_MODEL), const), pl.BlockSpec((1, D_MODEL), const)],
        out_specs=[row, row],
        compiler_params=_cparams(("arbitrary",)),
        name="outproj",
    )(merged, w_out, x2, ada3, ln_g, ln_b)


def _ff1_kernel(x_ref, w_ref, b_ref, w2_ref, o_ref, w2b_ref, wb_ref):
    @pl.when(pl.program_id(1) == 0)
    def _():
        wb_ref[...] = w_ref[...].astype(wb_ref.dtype)

    w2b_ref[...] = w2_ref[...].astype(w2b_ref.dtype)
    groups = [slice(i, i + 2 * EPILOGUE_ROWS) for i in range(0, x_ref.shape[0], 2 * EPILOGUE_ROWS)]
    hs = [jnp.dot(x_ref[rows, :], wb_ref[...], preferred_element_type=F32) for rows in groups]
    for h, rows in zip(hs, groups):
        h = jnp.maximum(h + b_ref[...], 0.0)
        o_ref[rows, :] = (h * h).astype(o_ref.dtype)


def _ff1(u2, w1, b1, w2):
    m, k = u2.shape
    n = w1.shape[1]
    tm, tn = 1024, 1024
    n_m = m // tm
    rows2 = w2.shape[0] // ((n // tn) * n_m)
    assert rows2 % 16 == 0 and rows2 * (n // tn) * n_m == w2.shape[0]
    slab = pl.BlockSpec((rows2, w2.shape[1]), lambda j, i: (j * n_m + i, 0))
    return pl.pallas_call(
        _ff1_kernel,
        out_shape=[jax.ShapeDtypeStruct((m, n), BF16), jax.ShapeDtypeStruct(w2.shape, BF16)],
        grid=(n // tn, n_m),
        in_specs=[pl.BlockSpec((tm, k), lambda j, i: (i, 0)),
                  pl.BlockSpec((k, tn), lambda j, i: (0, j)),
                  pl.BlockSpec((1, tn), lambda j, i: (0, j)),
                  slab],
        out_specs=[pl.BlockSpec((tm, tn), lambda j, i: (i, j)), slab],
        scratch_shapes=[pltpu.VMEM((k, tn), BF16)],
        compiler_params=_cparams(("arbitrary", "arbitrary")),
        name="ff1",
    )(u2, w1, b1, w2)


def _ff2_kernel(h_ref, w_ref, b_ref, x1_ref, ada_ref, g_ref, bb_ref, o_ref):
    y2 = jnp.dot(h_ref[...], w_ref[...], preferred_element_type=F32) + b_ref[...]
    g2 = ada_ref[0, 5:6, :]
    o_ref[...] = (_layernorm(ALPHA * x1_ref[...] + g2 * y2) * g_ref[...] + bb_ref[...]).astype(o_ref.dtype)


def _ff2(h, w2, b2, x1, ada3, ln_g, ln_b, seq, out_dtype):
    m, k = h.shape
    tm = 256
    per_b = seq // tm
    const = lambda i: (0, 0)
    row = pl.BlockSpec((tm, D_MODEL), lambda i: (i, 0))
    return pl.pallas_call(
        _ff2_kernel,
        out_shape=jax.ShapeDtypeStruct((m, D_MODEL), out_dtype),
        grid=(m // tm,),
        in_specs=[pl.BlockSpec((tm, k), lambda i: (i, 0)),
                  pl.BlockSpec((k, D_MODEL), const, pipeline_mode=pl.Buffered(1)),
                  pl.BlockSpec((1, D_MODEL), const), row,
                  pl.BlockSpec((1, 6, D_MODEL), lambda i: (i // per_b, 0, 0)),
                  pl.BlockSpec((1, D_MODEL), const), pl.BlockSpec((1, D_MODEL), const)],
        out_specs=row,
        compiler_params=_cparams(("arbitrary",)),
        name="ff2",
    )(h, w2, b2, x1, ada3, ln_g, ln_b)


def _pad_rows(w, height):
    return jnp.pad(w, ((0, height - w.shape[0]), (0, 0)))


def _layer(x2, c, batch, seq, w_ada, b_ada, w_in, mu_shift, w0, w_decay_up, a0, w_a_up, w_g_up, k_k, k_a, r_k,
           gn_g, gn_b, w_conv, b_conv, b_igate, b_fgate, w_branch_a, w_branch_b, w_out,
           ln1_g, ln1_b, w_ff1, b_ff1, w_ff2, b_ff2, ln2_g, ln2_b):
    row = lambda v: v.reshape(1, -1)

    ada3 = _ada(c, w_ada, b_ada).reshape(batch, 6, D_MODEL)
    u = _lnmod(x2, ada3, seq)

    assert w_in.shape == (D_MODEL, N_RWKV_COLS + N_MLSTM_COLS + 2 * D_MODEL)
    w_in_t = jnp.swapaxes(w_in, 0, 1)
    col_if = N_RWKV_COLS + 2 * D_QK + D_MLSTM
    z_ml = _win_matmul(u, w_in_t, N_RWKV_COLS, ML_COLS, 1024, 1024, F32, "win_mlstm",
                       skip=((2 * D_QK + D_MLSTM) // 1024, 2 * MLSTM_HEADS))
    z_gt, w_a_b, w_b_b, w_out_b, wt_rw_b = _win_matmul(
        u, w_in_t, N_RWKV_COLS + N_MLSTM_COLS, 2 * D_MODEL, 1024, 1024, BF16, "win_gate",
        cast_along=((w_branch_a, D_RWKV), (w_branch_b, D_MLSTM), (w_out, D_MODEL), (w_in_t, RW_COLS)))

    mu_rw = jnp.pad(mu_shift, (0, RW_COLS - N_RWKV_COLS))
    prep = _rwkv_prep(u, wt_rw_b, row(mu_rw), row(w0), row(a0), row(k_k), row(k_a), row(r_k),
                      _pad_rows(w_decay_up, LORA_PAD), _pad_rows(w_a_up, LORA_PAD), w_g_up, batch, seq)
    r, ld, kt, v, kkn, bb, g, bonus = prep
    y0, q, a_mat, c_mat = _rwkv_chunk(r, ld, kt, v, kkn, bb, batch, seq)
    ya = _rwkv_scan(y0, q, a_mat, c_mat, g, bonus, row(gn_g), row(gn_b), batch, seq)

    gbias = jnp.pad(jnp.concatenate([b_igate, b_fgate]), (0, ML_GATE_PAD - 2 * MLSTM_HEADS))
    z_if = _win_matmul(u, w_in_t, col_if, ML_GATE_PAD, 2048, ML_GATE_PAD, F32, "win_if")
    yb = _mlstm(z_ml, z_if, w_conv, row(b_conv), row(gbias), batch, seq)

    merged = _merge(ya, yb, z_gt, w_a_b, w_b_b)
    x1, u2 = _outproj(merged, w_out_b, x2, ada3, row(ln1_g), row(ln1_b), seq)

    h, w_ff2_b = _ff1(u2, w_ff1, row(b_ff1), w_ff2)
    return _ff2(h, w_ff2_b, row(b_ff2), x1, ada3, row(ln2_g), row(ln2_b), seq, F32)


def kernel(x, c, w_ada, b_ada, w_in, mu_shift, w0, w_decay_up, a0, w_a_up, w_g_up, k_k, k_a, r_k, gn_g, gn_b,
           w_conv, b_conv, b_igate, b_fgate, w_branch_a, w_branch_b, w_out, ln1_g, ln1_b, w_ff1, b_ff1,
           w_ff2, b_ff2, ln2_g, ln2_b):
    out_dtype = x.dtype
    batch, seq, d = x.shape
    assert d == D_MODEL and seq % 1024 == 0 and w_ada.shape[0] == DEPTH
    x2 = x.astype(F32).reshape(batch * seq, d)
    cf = c.astype(F32)
    for l in range(DEPTH):
        x2 = _layer(x2, cf, batch, seq, w_ada[l], b_ada[l], w_in[l], mu_shift[l], w0[l], w_decay_up[l], a0[l],
                    w_a_up[l], w_g_up[l], k_k[l], k_a[l], r_k[l].reshape(-1), gn_g[l], gn_b[l], w_conv[l],
                    b_conv[l], b_igate[l], b_fgate[l], w_branch_a[l], w_branch_b[l], w_out[l], ln1_g[l],
                    ln1_b[l], w_ff1[l], b_ff1[l], w_ff2[l], b_ff2[l], ln2_g[l], ln2_b[l])
    return x2.reshape(batch, seq, d).astype(out_dtype)
```

```python
import functools
import math

import jax
import jax.numpy as jnp
from jax import lax
from jax.experimental import pallas as pl
from jax.experimental.pallas import tpu as pltpu

F32 = jnp.float32
BF16 = jnp.bfloat16

D_MODEL = 2048
DEPTH = 1
D_RWKV = D_MODEL // 2
RWKV_HEAD = 64
DECAY_LORA = 96
A_LORA = 96
GATE_LORA = 256
RWKV_GN_EPS = 64e-5
D_MLSTM = D_MODEL // 2
MLSTM_HEADS = 4
MLSTM_DV = D_MLSTM // MLSTM_HEADS
MLSTM_DQK = MLSTM_DV // 2
D_QK = MLSTM_HEADS * MLSTM_DQK
CONV_K = 4
D_FF = 4 * D_MODEL
LN_EPS = 1e-5
ALPHA = (2.0 * DEPTH) ** 0.25

LANES = 128
LORA_PAD = 128
N_RWKV_COLS = 3 * D_RWKV + DECAY_LORA + A_LORA + GATE_LORA
N_MLSTM_COLS = 2 * D_QK + 2 * D_MLSTM + 2 * MLSTM_HEADS
RW_COLS = 3584
ML_GATE_PAD = 128
ML_COLS = 2 * D_QK + 2 * D_MLSTM
RWKV_CHUNK = 64
RWKV_TB = 256
RWKV_SCAN_ROWS = 128
RWKV_CORE_TB = 1024
MLSTM_CHUNK = 256
EPILOGUE_ROWS = 256
VMEM_LIMIT = 56 * 1024 * 1024

NN = (((1,), (0,)), ((), ()))
NT = (((1,), (1,)), ((), ()))
TN = (((0,), (0,)), ((), ()))


def _cparams(sem):
    return pltpu.CompilerParams(dimension_semantics=sem, vmem_limit_bytes=VMEM_LIMIT)


def _bf16_parts(x, n):
    if x.dtype == BF16:
        return [x]
    parts, rem = [], x
    for i in range(n):
        p = rem.astype(BF16)
        parts.append(p)
        if i + 1 < n:
            rem = rem - p.astype(F32)
    return parts


def _mdot(a, b, na=1, nb=1, dims=NN):
    ap, bp = _bf16_parts(a, na), _bf16_parts(b, nb)
    order = max(len(ap), len(bp))
    acc = None
    for i, x in enumerate(ap):
        for j, y in enumerate(bp):
            if i + j < order:
                t = lax.dot_general(x, y, dims, preferred_element_type=F32)
                acc = t if acc is None else acc + t
    return acc


def _sigmoid(x):
    return 1.0 / (1.0 + jnp.exp(-x))


def _layernorm(x):
    mu = jnp.mean(x, axis=-1, keepdims=True)
    xc = x - mu
    var = jnp.mean(xc * xc, axis=-1, keepdims=True)
    return xc * lax.rsqrt(var + LN_EPS)


def _iota2(shape, dim):
    return lax.broadcasted_iota(jnp.int32, shape, dim)


def _head_block_ones(n, head):
    r, c = _iota2((n, n), 0), _iota2((n, n), 1)
    return jnp.where((r // head) == (c // head), 1.0, 0.0).astype(BF16)


def _head_sum(x, ones_blk):
    return _mdot(x, ones_blk)


def _ada_kernel(ct_ref, w_ref, b_ref, o_ref, sb_ref):
    tn = w_ref.shape[1]
    nb = sb_ref.shape[0]

    @pl.when(pl.program_id(0) == 0)
    def _():
        cv = ct_ref[...]
        sv = cv * _sigmoid(cv)
        for b in range(nb):
            sb_ref[b] = jnp.broadcast_to(sv[:, b:b + 1], sb_ref.shape[1:])

    ntile = tn // LANES

    def body(i, acc):
        r0 = pl.multiple_of(i * 8, 8)
        sb = [sb_ref[b, pl.ds(r0, 8), :] for b in range(nb)]
        wv = [w_ref[pl.ds(r0, 8), t * LANES:(t + 1) * LANES] for t in range(ntile)]
        return tuple(acc[b * ntile + t] + sb[b] * wv[t] for b in range(nb) for t in range(ntile))

    zero = jnp.zeros((8, LANES), F32)
    acc = lax.fori_loop(0, w_ref.shape[0] // 8, body, (zero,) * (nb * ntile), unroll=4)
    for b in range(nb):
        tot = jnp.concatenate([acc[b * ntile + t] for t in range(ntile)], axis=1)
        o_ref[b:b + 1, :] = jnp.sum(tot, axis=0, keepdims=True) + b_ref[...]


def _ada(c, w_ada, b_ada):
    batch = c.shape[0]
    assert batch == 2
    n = w_ada.shape[1]
    tn = 1536
    return pl.pallas_call(
        _ada_kernel,
        out_shape=jax.ShapeDtypeStruct((batch, n), F32),
        grid=(n // tn,),
        in_specs=[pl.BlockSpec((D_MODEL, batch), lambda j: (0, 0)),
                  pl.BlockSpec((D_MODEL, tn), lambda j: (0, j)),
                  pl.BlockSpec((1, tn), lambda j: (0, j))],
        out_specs=pl.BlockSpec((batch, tn), lambda j: (0, j)),
        scratch_shapes=[pltpu.VMEM((batch, D_MODEL, LANES), F32)],
        compiler_params=_cparams(("arbitrary",)),
        name="ada",
    )(c.T, w_ada, b_ada.reshape(1, n))


def _lnmod_kernel(x_ref, ada_ref, o_ref):
    xn = _layernorm(x_ref[...])
    sh, sc = ada_ref[0, 0:1, :], ada_ref[0, 1:2, :]
    o_ref[...] = (xn * (1.0 + sc) + sh).astype(o_ref.dtype)


def _lnmod(x2, ada3, seq):
    m = x2.shape[0]
    tb = 512
    per_b = seq // tb
    return pl.pallas_call(
        _lnmod_kernel,
        out_shape=jax.ShapeDtypeStruct((m, D_MODEL), BF16),
        grid=(m // tb,),
        in_specs=[pl.BlockSpec((tb, D_MODEL), lambda i: (i, 0)),
                  pl.BlockSpec((1, 6, D_MODEL), lambda i: (i // per_b, 0, 0))],
        out_specs=pl.BlockSpec((tb, D_MODEL), lambda i: (i, 0)),
        compiler_params=_cparams(("arbitrary",)),
        name="lnmod",
    )(x2, ada3)


def _win_kernel(n_cast, x_ref, wt_ref, *refs):
    cast_in, o_ref, cast_out, wb_ref = refs[:n_cast], refs[n_cast], refs[n_cast + 1:-1], refs[-1]

    @pl.when(pl.program_id(1) == 0)
    def _():
        wb_ref[...] = wt_ref[...].astype(wb_ref.dtype)

    for src, dst in zip(cast_in, cast_out):
        dst[...] = src[...].astype(dst.dtype)
    o_ref[...] = lax.dot_general(x_ref[...], wb_ref[...], NT, preferred_element_type=F32).astype(o_ref.dtype)


def _win_matmul(x, wt, col0, n_out, tm, tn, out_dtype, name, skip=None, cast_along=()):
    m, k = x.shape
    skip_tile, gap = skip if skip else (n_out // tn, 0)
    assert col0 % 8 == 0 and gap % 8 == 0 and n_out % tn == 0 and col0 + n_out + gap <= wt.shape[0]
    n_m = m // tm
    steps = (n_out // tn) * n_m

    def w_index(j, i):
        return pl.multiple_of(col0 + j * tn + jnp.where(j >= skip_tile, gap, 0), 8), 0

    slabs = []
    for w, nrows in cast_along:
        rows = nrows // steps
        assert rows % 16 == 0 and rows * steps == nrows <= w.shape[0]
        slabs.append(pl.BlockSpec((rows, w.shape[1]), lambda j, i: (j * n_m + i, 0)))
    cast_along = [w for w, _ in cast_along]
    outs = pl.pallas_call(
        functools.partial(_win_kernel, len(cast_along)),
        out_shape=[jax.ShapeDtypeStruct((m, n_out), out_dtype)]
        + [jax.ShapeDtypeStruct((spec.block_shape[0] * steps, w.shape[1]), BF16)
           for w, spec in zip(cast_along, slabs)],
        grid=(n_out // tn, n_m),
        in_specs=[pl.BlockSpec((tm, k), lambda j, i: (i, 0)),
                  pl.BlockSpec((pl.Element(tn), pl.Element(k)), w_index)] + slabs,
        out_specs=[pl.BlockSpec((tm, tn), lambda j, i: (i, j))] + slabs,
        scratch_shapes=[pltpu.VMEM((tn, k), BF16)],
        compiler_params=_cparams(("arbitrary", "arbitrary")),
        name=name,
    )(x, wt, *cast_along)
    return outs if cast_along else outs[0]


def _rwkv_prep_kernel(u_ref, wt_ref, mu_ref, w0_ref, a0_ref, kk_ref, ka_ref, rk_ref, wdu_ref, wau_ref, wgu_ref,
                      r_o, ld_o, kt_o, v_o, kkn_o, bb_o, g_o, bonus_o, carry_ref):
    tb = u_ref.shape[0]

    @pl.when(pl.program_id(1) == 0)
    def _():
        carry_ref[...] = jnp.zeros_like(carry_ref)

    z = lax.dot_general(u_ref[...], wt_ref[...], NT, preferred_element_type=F32)
    prev = pltpu.roll(z, 1, 0)
    prev = jnp.where(_iota2(z.shape, 0) == 0, carry_ref[0:1, :], prev)
    carry_ref[0:1, :] = z[tb - 1:tb, :]
    zs = z + (prev - z) * mu_ref[...]

    c = D_RWKV
    r, k, v = zs[:, 0:c], zs[:, c:2 * c], zs[:, 2 * c:3 * c]
    wd = zs[:, 3 * c:3 * c + LORA_PAD]
    ad = zs[:, 3 * c + DECAY_LORA:3 * c + DECAY_LORA + LORA_PAD]
    gd = zs[:, 3 * c + DECAY_LORA + A_LORA:3 * c + DECAY_LORA + A_LORA + GATE_LORA]

    ld = -math.exp(-0.5) * _sigmoid(w0_ref[...] + _mdot(jnp.tanh(wd), wdu_ref[...], 2, 2))
    a = _sigmoid(a0_ref[...] + _mdot(ad, wau_ref[...]))
    g = _mdot(_sigmoid(gd), wgu_ref[...], 1, 1)

    ones_blk = _head_block_ones(LANES, RWKV_HEAD)
    kq = k * kk_ref[...]
    kt = k * (1.0 + (a - 1.0) * ka_ref[...])
    sq = kq * kq
    bn = r * kt * rk_ref[...]
    ss = jnp.concatenate([_head_sum(sq[:, s:s + LANES], ones_blk) for s in range(0, c, LANES)], axis=1)
    bs = jnp.concatenate([_head_sum(bn[:, s:s + LANES], ones_blk) for s in range(0, c, LANES)], axis=1)
    kkn = kq * lax.rsqrt(ss + 1e-12)

    r_o[...] = r.astype(r_o.dtype)
    ld_o[...] = ld
    kt_o[...] = kt.astype(kt_o.dtype)
    v_o[...] = v.astype(v_o.dtype)
    kkn_o[...] = kkn.astype(kkn_o.dtype)
    bb_o[...] = (a * kkn).astype(bb_o.dtype)
    g_o[...] = g.astype(g_o.dtype)
    bonus_o[...] = (bs * v).astype(bonus_o.dtype)


def _rwkv_prep(u, wt_rw, mu_rw, w0, a0, k_k, k_a, r_k, wdu, wau, wgu, batch, seq):
    m, k = u.shape
    assert wt_rw.shape == (RW_COLS, k)
    tb = RWKV_TB
    per_b = seq // tb
    row = lambda b, j: (b * per_b + j, 0)
    const = lambda b, j: (0, 0)
    vec = pl.BlockSpec((1, D_RWKV), const)
    outs = [jax.ShapeDtypeStruct((m, D_RWKV), F32 if i == 1 else BF16) for i in range(8)]
    return pl.pallas_call(
        _rwkv_prep_kernel,
        out_shape=outs,
        grid=(batch, per_b),
        in_specs=[pl.BlockSpec((tb, k), row),
                  pl.BlockSpec((RW_COLS, k), const, pipeline_mode=pl.Buffered(1)),
                  pl.BlockSpec((1, RW_COLS), const),
                  vec, vec, vec, vec, vec,
                  pl.BlockSpec((LORA_PAD, D_RWKV), const), pl.BlockSpec((LORA_PAD, D_RWKV), const),
                  pl.BlockSpec((GATE_LORA, D_RWKV), const)],
        out_specs=[pl.BlockSpec((tb, D_RWKV), row)] * 8,
        scratch_shapes=[pltpu.VMEM((8, RW_COLS), F32)],
        compiler_params=_cparams(("arbitrary", "arbitrary")),
        name="rwkv_prep",
    )(u, wt_rw, mu_rw, w0, a0, k_k, k_a, r_k, wdu, wau, wgu)


def _rwkv_chunk_kernel(r_ref, ld_ref, kt_ref, v_ref, kk_ref, bb_ref, y0_ref, q_ref, a_ref, c_ref):
    tb = r_ref.shape[0]
    lc = RWKV_CHUNK
    n = 2 * lc
    nchunk = tb // lc

    lane = _iota2((1, LANES), 1)
    m0 = jnp.where(lane < RWKV_HEAD, 1.0, 0.0)
    m1 = 1.0 - m0
    ri, ci = _iota2((n, n), 0), _iota2((n, n), 1)
    same_head = (ri // lc) == (ci // lc)
    strict = jnp.where(same_head & (ri > ci), 1.0, 0.0)
    incl = jnp.where(same_head & (ri >= ci), 1.0, 0.0)
    eye = jnp.where(ri == ci, 1.0, 0.0)

    tw = 4 * lc
    rt, ct = _iota2((tw, tw), 0), _iota2((tw, tw), 1)
    tri = jnp.where(((rt // lc) == (ct // lc)) & (ct <= rt), 1.0, 0.0).astype(BF16)

    ld = ld_ref[...]
    cl = jnp.concatenate([_mdot(tri, ld[i:i + tw], 1, 3) for i in range(0, tb, tw)], axis=0)
    e_pos = jnp.exp(cl)
    e_neg = jnp.exp(-cl)
    a_hat = kk_ref[...] * jnp.exp(cl - ld)
    b_hat = bb_ref[...] * e_neg
    k_hat = kt_ref[...] * e_neg
    r_hat = r_ref[...] * e_pos
    v_all = v_ref[...]

    def stack(x):
        return jnp.concatenate([x * m0, x * m1], axis=0)

    def bf(t):
        return t.astype(BF16)

    chunks = range(nchunk)
    st = [[stack(x[c * lc:(c + 1) * lc]) for x in (a_hat, b_hat, k_hat, r_hat, v_all)] for c in chunks]
    a_st, b_st, k_st, r_st, v_st = ([st[c][i] for c in chunks] for i in range(5))
    b_b, k_b, v_b = ([bf(t) for t in ts] for ts in (b_st, k_st, v_st))
    p = [_mdot(jnp.concatenate([a_st[c], r_st[c]], axis=0),
               jnp.concatenate([b_b[c], k_b[c]], axis=0), dims=NT) for c in chunks]
    mab = [p[c][0:n, 0:n] * strict for c in chunks]
    mab_b = [bf(t) for t in mab]
    mak_b = [bf(p[c][0:n, n:] * strict) for c in chunks]
    mrb_b = [bf(p[c][n:, 0:n] * incl) for c in chunks]
    mrk_b = [bf(p[c][n:, n:] * incl) for c in chunks]

    x = [eye - jnp.where((ri // 2 == ci // 2), mab[c], 0.0) for c in chunks]
    w1 = [_mdot(mak_b[c], v_b[c]) for c in chunks]
    zero_b = jnp.zeros((), BF16)
    s = 2
    while s < lc:
        lvl = (ri // (2 * s) == ci // (2 * s)) & (ri // s != ci // s)
        x_b = [bf(t) for t in x]
        nx = [_mdot(jnp.where(lvl, mab_b[c], zero_b), x_b[c]) for c in chunks]
        x = [x[c] - _mdot(x_b[c], nx[c]) for c in chunks]
        s *= 2

    ta = [_mdot(x[c], jnp.concatenate([w1[c], a_st[c]], axis=1)) for c in chunks]
    zeros_b = jnp.zeros((n, LANES), BF16)
    ugv_b = [jnp.concatenate([bf(jnp.concatenate([-ta[c][:, 0:LANES], ta[c][:, LANES:]], axis=1)),
                              jnp.concatenate([v_b[c], zeros_b], axis=1)], axis=0) for c in chunks]
    ru = [_mdot(jnp.concatenate([mrb_b[c], mrk_b[c]], axis=1), ugv_b[c]) for c in chunks]
    y0 = [ru[c][:, 0:LANES] for c in chunks]
    q = [r_st[c] - ru[c][:, LANES:] for c in chunks]
    pl_row = [e_pos[(c + 1) * lc - 1:(c + 1) * lc, :] for c in chunks]
    bk_p = [bf(jnp.concatenate([b_st[c] * pl_row[c], k_st[c] * pl_row[c]], axis=0)) for c in chunks]
    bu = [_mdot(bk_p[c], ugv_b[c], dims=TN) for c in chunks]
    ct = [bu[c][:, 0:LANES] for c in chunks]
    at = [eye * pl_row[c] - bu[c][:, LANES:] for c in chunks]

    for c in chunks:
        rows = slice(c * lc, (c + 1) * lc)
        y0_ref[rows, :] = y0[c][0:lc] + y0[c][lc:]
        q_ref[rows, :] = (q[c][0:lc] + q[c][lc:]).astype(q_ref.dtype)
        a_ref[0, c, 0] = (at[c][0:RWKV_HEAD] + at[c][RWKV_HEAD:]).astype(a_ref.dtype)
        c_ref[0, c, 0] = ct[c][0:RWKV_HEAD] + ct[c][RWKV_HEAD:]


def _rwkv_chunk(r, ld, kt, v, kkn, bb, batch, seq):
    m = r.shape[0]
    tb = RWKV_CORE_TB
    per_b = seq // tb
    cpb = tb // RWKV_CHUNK
    npair = D_RWKV // LANES
    slab = pl.BlockSpec((tb, LANES), lambda b, p, j: (b * per_b + j, p))
    mat = pl.BlockSpec((1, cpb, 1, RWKV_HEAD, LANES), lambda b, p, j: (b, j, p, 0, 0))
    mshape = (batch, seq // RWKV_CHUNK, npair, RWKV_HEAD, LANES)
    return pl.pallas_call(
        _rwkv_chunk_kernel,
        out_shape=[jax.ShapeDtypeStruct((m, D_RWKV), F32), jax.ShapeDtypeStruct((m, D_RWKV), BF16),
                   jax.ShapeDtypeStruct(mshape, BF16), jax.ShapeDtypeStruct(mshape, F32)],
        grid=(batch, npair, per_b),
        in_specs=[slab] * 6,
        out_specs=[slab, slab, mat, mat],
        compiler_params=_cparams(("arbitrary", "arbitrary", "arbitrary")),
        name="rwkv_chunk",
    )(r, ld, kt, v, kkn, bb)


def _rwkv_scan_kernel(y0_ref, q_ref, a_ref, c_ref, g_ref, bonus_ref, gng_ref, gnb_ref, ya_ref, s_ref):
    nb, rows, _ = y0_ref.shape
    lc = RWKV_CHUNK
    npair = D_RWKV // LANES

    @pl.when(pl.program_id(0) == 0)
    def _():
        s_ref[...] = jnp.zeros_like(s_ref)

    ones_blk = _head_block_ones(LANES, RWKV_HEAD)
    head0 = _iota2((RWKV_HEAD, LANES), 1) < RWKV_HEAD

    def block_diag(t):
        zero = jnp.zeros((), t.dtype)
        return jnp.concatenate([jnp.where(head0, t, zero), jnp.where(head0, zero, t)], axis=0)

    streams = [(b, p) for b in range(nb) for p in range(npair)]
    for ci in range(rows // lc):
        rs = slice(ci * lc, (ci + 1) * lc)
        state_b = [s_ref[b * npair + p].astype(BF16) for b, p in streams]
        y = [y0_ref[b, rs, p * LANES:(p + 1) * LANES]
             + _mdot(q_ref[b, rs, p * LANES:(p + 1) * LANES], state_b[i])
             for i, (b, p) in enumerate(streams)]
        for i, (b, p) in enumerate(streams):
            s_ref[b * npair + p] = (_mdot(block_diag(a_ref[b, ci, p]), state_b[i])
                                    + block_diag(c_ref[b, ci, p]))
        mu = [_head_sum(t, ones_blk) * (1.0 / RWKV_HEAD) for t in y]
        yc = [y[i] - mu[i] for i in range(len(streams))]
        var = [_head_sum(t * t, ones_blk) * (1.0 / RWKV_HEAD) for t in yc]
        for i, (b, p) in enumerate(streams):
            ls = slice(p * LANES, (p + 1) * LANES)
            yn = yc[i] * lax.rsqrt(var[i] + RWKV_GN_EPS) * gng_ref[:, ls] + gnb_ref[:, ls]
            ya_ref[b, rs, ls] = ((yn + bonus_ref[b, rs, ls]) * g_ref[b, rs, ls]).astype(ya_ref.dtype)


def _rwkv_scan(y0, q, a_mat, c_mat, g, bonus, gn_g, gn_b, batch, seq):
    rows = RWKV_SCAN_ROWS
    npair = D_RWKV // LANES
    tok = pl.BlockSpec((batch, rows, D_RWKV), lambda j: (0, j, 0))
    mat = pl.BlockSpec((batch, rows // RWKV_CHUNK, npair, RWKV_HEAD, LANES), lambda j: (0, j, 0, 0, 0))
    vec = pl.BlockSpec((1, D_RWKV), lambda j: (0, 0))
    shape3 = (batch, seq, D_RWKV)
    return pl.pallas_call(
        _rwkv_scan_kernel,
        out_shape=jax.ShapeDtypeStruct(shape3, BF16),
        grid=(seq // rows,),
        in_specs=[tok, tok, mat, mat, tok, tok, vec, vec],
        out_specs=tok,
        scratch_shapes=[pltpu.VMEM((batch * npair, LANES, LANES), F32)],
        compiler_params=_cparams(("arbitrary",)),
        name="rwkv_scan",
    )(y0.reshape(shape3), q.reshape(shape3), a_mat, c_mat, g.reshape(shape3), bonus.reshape(shape3),
      gn_g, gn_b).reshape(batch * seq, D_RWKV)


def _mlstm_kernel(u_ref, wa_ref, wo_ref, wconv_ref, bconv_ref, gbias_ref, yb_ref,
                  tail_ref, ct_ref, n_ref, m_ref, wab_ref, wob_ref):
    lc = u_ref.shape[0]
    dqk, dv, nh = MLSTM_DQK, MLSTM_DV, MLSTM_HEADS

    @pl.when(pl.program_id(1) == 0)
    def _():
        tail_ref[...] = jnp.zeros_like(tail_ref)
        ct_ref[...] = jnp.zeros_like(ct_ref)
        n_ref[...] = jnp.zeros_like(n_ref)
        m_ref[...] = jnp.zeros_like(m_ref)
        wab_ref[...] = wa_ref[...].astype(wab_ref.dtype)
        wob_ref[...] = wo_ref[...].astype(wob_ref.dtype)

    u = u_ref[...]
    za = lax.dot_general(u, wab_ref[...], NT, preferred_element_type=F32)
    o_all = lax.dot_general(u, wob_ref[...], NT, preferred_element_type=F32)

    zqk = za[:, 0:2 * D_QK]
    ext = jnp.concatenate([tail_ref[...], zqk], axis=0)
    tail_ref[...] = zqk[lc - 8:lc, :]
    conv = bconv_ref[...] + wconv_ref[CONV_K - 1:CONV_K, :] * zqk
    for d in range(1, CONV_K):
        conv = conv + wconv_ref[CONV_K - 1 - d:CONV_K - d, :] * ext[8 - d:8 - d + lc, :]
    qk = conv * _sigmoid(conv)
    q_all = qk[:, 0:D_QK]
    k_all = qk[:, D_QK:] * (dqk ** -0.5)
    v_all = za[:, 2 * D_QK:2 * D_QK + D_MLSTM]

    gz = za[:, 2 * D_QK + D_MLSTM:] + gbias_ref[...]
    lf_col = jnp.minimum(gz, 0.0) - jnp.log(1.0 + jnp.exp(-jnp.abs(gz)))
    rr, cc = _iota2((lc, lc), 0), _iota2((lc, lc), 1)
    causal = cc <= rr
    tri = jnp.where(causal, 1.0, 0.0).astype(BF16)
    b_col = _mdot(tri, lf_col, 1, 3)
    sel = jnp.where(_iota2((8, LANES), 0) == _iota2((8, LANES), 1), 1.0, 0.0).astype(BF16)
    g_row = _mdot(sel, gz, 1, 3, NT)
    lf_row = jnp.minimum(g_row, 0.0) - jnp.log(1.0 + jnp.exp(-jnp.abs(g_row)))
    b_row = _mdot(lf_row, tri, 3, 1, NT)

    heads = range(nh)
    qf = [q_all[:, h * dqk:(h + 1) * dqk] for h in heads]
    q = [t.astype(BF16) for t in qf]
    k = [k_all[:, h * dqk:(h + 1) * dqk] for h in heads]
    v = [v_all[:, h * dv:(h + 1) * dv].astype(BF16) for h in heads]
    bc = [b_col[:, nh + h:nh + h + 1] for h in heads]
    ic = [gz[:, h:h + 1] for h in heads]
    br = [b_row[nh + h:nh + h + 1, :] for h in heads]
    ir = [g_row[h:h + 1, :] for h in heads]
    g_tot = [b_col[lc - 1:lc, nh + h:nh + h + 1] for h in heads]
    m_prev = [m_ref[0:1, h:h + 1] for h in heads]
    ct = [ct_ref[h] for h in heads]
    nvec = [n_ref[h:h + 1, :] for h in heads]

    qk = [_mdot(q[h], k[h], dims=NT) for h in heads]
    qc = [_mdot(q[h], ct[h]) for h in heads]
    bri = [br[h] - ir[h] for h in heads]
    dm = [jnp.where(causal, bc[h] - bri[h], -jnp.inf) for h in heads]
    inter = [bc[h] + m_prev[h] for h in heads]
    mt = [jnp.maximum(inter[h], jnp.max(dm[h], axis=-1, keepdims=True)) for h in heads]
    sc = [qk[h] * jnp.exp(dm[h] - mt[h]) for h in heads]
    winter = [jnp.exp(inter[h] - mt[h]) for h in heads]
    sv = [_mdot(sc[h], v[h]) for h in heads]
    m_new = [jnp.maximum(g_tot[h] + m_prev[h], jnp.max(g_tot[h] - bri[h], axis=-1, keepdims=True))
             for h in heads]
    kw = [k[h] * jnp.exp(g_tot[h] - bc[h] + ic[h] - m_new[h]) for h in heads]
    kv = [_mdot(kw[h], v[h], dims=TN) for h in heads]
    for h in heads:
        num = winter[h] * qc[h] + sv[h]
        den = (winter[h] * jnp.sum(qf[h] * nvec[h], axis=-1, keepdims=True)
               + jnp.sum(sc[h], axis=-1, keepdims=True))
        hout = num / jnp.maximum(jnp.abs(den), jnp.exp(-mt[h]))
        wc = jnp.exp(g_tot[h] + m_prev[h] - m_new[h])
        ct_ref[h] = wc * ct[h] + kv[h]
        n_ref[h:h + 1, :] = wc * nvec[h] + jnp.sum(kw[h], axis=0, keepdims=True)
        m_ref[0:1, h:h + 1] = m_new[h]
        o = o_all[:, h * dv:(h + 1) * dv]
        yb_ref[:, h * dv:(h + 1) * dv] = (_sigmoid(o) * hout).astype(yb_ref.dtype)


def _mlstm(u, w_in_t, col0, w_conv, b_conv, gbias, batch, seq):
    m, k = u.shape
    lc = MLSTM_CHUNK
    per_b = seq // lc
    const = lambda b, j: (0, 0)
    rows_a = 2 * D_QK + D_MLSTM + ML_GATE_PAD
    col_o = col0 + 2 * D_QK + D_MLSTM + 2 * MLSTM_HEADS
    assert col0 % 8 == 0 and col_o % 8 == 0 and col_o + D_MLSTM <= w_in_t.shape[0]
    resident = lambda rows, r0: pl.BlockSpec((pl.Element(rows), pl.Element(k)), lambda b, j: (r0, 0),
                                             pipeline_mode=pl.Buffered(1))
    return pl.pallas_call(
        _mlstm_kernel,
        out_shape=jax.ShapeDtypeStruct((m, D_MLSTM), BF16),
        grid=(batch, per_b),
        in_specs=[pl.BlockSpec((lc, k), lambda b, j: (b * per_b + j, 0)),
                  resident(rows_a, col0), resident(D_MLSTM, col_o),
                  pl.BlockSpec((CONV_K, 2 * D_QK), const),
                  pl.BlockSpec((1, 2 * D_QK), const),
                  pl.BlockSpec((1, ML_GATE_PAD), const)],
        out_specs=pl.BlockSpec((lc, D_MLSTM), lambda b, j: (b * per_b + j, 0)),
        scratch_shapes=[pltpu.VMEM((8, 2 * D_QK), F32),
                        pltpu.VMEM((MLSTM_HEADS, MLSTM_DQK, MLSTM_DV), F32),
                        pltpu.VMEM((8, MLSTM_DQK), F32),
                        pltpu.VMEM((8, LANES), F32),
                        pltpu.VMEM((rows_a, k), BF16),
                        pltpu.VMEM((D_MLSTM, k), BF16)],
        compiler_params=_cparams(("arbitrary", "arbitrary")),
        name="mlstm",
    )(u, w_in_t, w_in_t, w_conv, b_conv, gbias)


def _merge_kernel(ya_ref, yb_ref, zg_ref, wa_ref, wb_ref, o_ref):
    groups = [slice(i, i + EPILOGUE_ROWS) for i in range(0, ya_ref.shape[0], EPILOGUE_ROWS)]
    pab = [(jnp.dot(ya_ref[rows, :], wa_ref[...], preferred_element_type=F32),
            jnp.dot(yb_ref[rows, :], wb_ref[...], preferred_element_type=F32)) for rows in groups]
    for (pa, pb), rows in zip(pab, groups):
        ga = _sigmoid(zg_ref[rows, 0:D_MODEL].astype(F32))
        gb = _sigmoid(zg_ref[rows, D_MODEL:].astype(F32))
        o_ref[rows, :] = (ga * pa + gb * pb).astype(o_ref.dtype)


def _merge(ya, yb, zg, wa, wb):
    m = ya.shape[0]
    tm = 512
    const = lambda i: (0, 0)
    return pl.pallas_call(
        _merge_kernel,
        out_shape=jax.ShapeDtypeStruct((m, D_MODEL), BF16),
        grid=(m // tm,),
        in_specs=[pl.BlockSpec((tm, D_RWKV), lambda i: (i, 0)),
                  pl.BlockSpec((tm, D_MLSTM), lambda i: (i, 0)),
                  pl.BlockSpec((tm, 2 * D_MODEL), lambda i: (i, 0)),
                  pl.BlockSpec((D_RWKV, D_MODEL), const),
                  pl.BlockSpec((D_MLSTM, D_MODEL), const)],
        out_specs=pl.BlockSpec((tm, D_MODEL), lambda i: (i, 0)),
        compiler_params=_cparams(("arbitrary",)),
        name="merge",
    )(ya, yb, zg, wa, wb)


def _outproj_kernel(mg_ref, w_ref, x_ref, ada_ref, g_ref, b_ref, x1_ref, u2_ref):
    groups = [slice(i, i + EPILOGUE_ROWS) for i in range(0, mg_ref.shape[0], EPILOGUE_ROWS)]
    ys = [jnp.dot(mg_ref[rows, :], w_ref[...], preferred_element_type=F32) for rows in groups]
    g1 = ada_ref[0, 2:3, :]
    sh2, sc2 = ada_ref[0, 3:4, :], ada_ref[0, 4:5, :]
    for y, rows in zip(ys, groups):
        x1 = _layernorm(ALPHA * x_ref[rows, :] + g1 * y) * g_ref[...] + b_ref[...]
        x1_ref[rows, :] = x1
        u2_ref[rows, :] = (_layernorm(x1) * (1.0 + sc2) + sh2).astype(u2_ref.dtype)


def _outproj(merged, w_out, x2, ada3, ln_g, ln_b, seq):
    m = merged.shape[0]
    tm = 512
    per_b = seq // tm
    const = lambda i: (0, 0)
    row = pl.BlockSpec((tm, D_MODEL), lambda i: (i, 0))
    return pl.pallas_call(
        _outproj_kernel,
        out_shape=[jax.ShapeDtypeStruct((m, D_MODEL), F32), jax.ShapeDtypeStruct((m, D_MODEL), BF16)],
        grid=(m // tm,),
        in_specs=[row, pl.BlockSpec((D_MODEL, D_MODEL), const, pipeline_mode=pl.Buffered(1)), row,
                  pl.BlockSpec((1, 6, D_MODEL), lambda i: (i // per_b, 0, 0)),
                  pl.BlockSpec((1, D_MODEL), const), pl.BlockSpec((1, D_MODEL), const)],
        out_specs=[row, row],
        compiler_params=_cparams(("arbitrary",)),
        name="outproj",
    )(merged, w_out, x2, ada3, ln_g, ln_b)


def _ff1_kernel(x_ref, w_ref, b_ref, w2_ref, o_ref, w2b_ref, wb_ref):
    @pl.when(pl.program_id(1) == 0)
    def _():
        wb_ref[...] = w_ref[...].astype(wb_ref.dtype)

    w2b_ref[...] = w2_ref[...].astype(w2b_ref.dtype)
    groups = [slice(i, i + 2 * EPILOGUE_ROWS) for i in range(0, x_ref.shape[0], 2 * EPILOGUE_ROWS)]
    hs = [jnp.dot(x_ref[rows, :], wb_ref[...], preferred_element_type=F32) for rows in groups]
    for h, rows in zip(hs, groups):
        h = jnp.maximum(h + b_ref[...], 0.0)
        o_ref[rows, :] = (h * h).astype(o_ref.dtype)


def _ff1(u2, w1, b1, w2):
    m, k = u2.shape
    n = w1.shape[1]
    tm, tn = 1024, 1024
    n_m = m // tm
    rows2 = w2.shape[0] // ((n // tn) * n_m)
    assert rows2 % 16 == 0 and rows2 * (n // tn) * n_m == w2.shape[0]
    slab = pl.BlockSpec((rows2, w2.shape[1]), lambda j, i: (j * n_m + i, 0))
    return pl.pallas_call(
        _ff1_kernel,
        out_shape=[jax.ShapeDtypeStruct((m, n), BF16), jax.ShapeDtypeStruct(w2.shape, BF16)],
        grid=(n // tn, n_m),
        in_specs=[pl.BlockSpec((tm, k), lambda j, i: (i, 0)),
                  pl.BlockSpec((k, tn), lambda j, i: (0, j)),
                  pl.BlockSpec((1, tn), lambda j, i: (0, j)),
                  slab],
        out_specs=[pl.BlockSpec((tm, tn), lambda j, i: (i, j)), slab],
        scratch_shapes=[pltpu.VMEM((k, tn), BF16)],
        compiler_params=_cparams(("arbitrary", "arbitrary")),
        name="ff1",
    )(u2, w1, b1, w2)


def _ff2_kernel(h_ref, w_ref, b_ref, x1_ref, ada_ref, g_ref, bb_ref, o_ref):
    y2 = jnp.dot(h_ref[...], w_ref[...], preferred_element_type=F32) + b_ref[...]
    g2 = ada_ref[0, 5:6, :]
    o_ref[...] = (_layernorm(ALPHA * x1_ref[...] + g2 * y2) * g_ref[...] + bb_ref[...]).astype(o_ref.dtype)


def _ff2(h, w2, b2, x1, ada3, ln_g, ln_b, seq, out_dtype):
    m, k = h.shape
    tm = 256
    per_b = seq // tm
    const = lambda i: (0, 0)
    row = pl.BlockSpec((tm, D_MODEL), lambda i: (i, 0))
    return pl.pallas_call(
        _ff2_kernel,
        out_shape=jax.ShapeDtypeStruct((m, D_MODEL), out_dtype),
        grid=(m // tm,),
        in_specs=[pl.BlockSpec((tm, k), lambda i: (i, 0)),
                  pl.BlockSpec((k, D_MODEL), const, pipeline_mode=pl.Buffered(1)),
                  pl.BlockSpec((1, D_MODEL), const), row,
                  pl.BlockSpec((1, 6, D_MODEL), lambda i: (i // per_b, 0, 0)),
                  pl.BlockSpec((1, D_MODEL), const), pl.BlockSpec((1, D_MODEL), const)],
        out_specs=row,
        compiler_params=_cparams(("arbitrary",)),
        name="ff2",
    )(h, w2, b2, x1, ada3, ln_g, ln_b)


def _pad_rows(w, height):
    return jnp.pad(w, ((0, height - w.shape[0]), (0, 0)))


def _layer(x2, c, batch, seq, w_ada, b_ada, w_in, mu_shift, w0, w_decay_up, a0, w_a_up, w_g_up, k_k, k_a, r_k,
           gn_g, gn_b, w_conv, b_conv, b_igate, b_fgate, w_branch_a, w_branch_b, w_out,
           ln1_g, ln1_b, w_ff1, b_ff1, w_ff2, b_ff2, ln2_g, ln2_b):
    row = lambda v: v.reshape(1, -1)

    ada3 = _ada(c, w_ada, b_ada).reshape(batch, 6, D_MODEL)
    u = _lnmod(x2, ada3, seq)

    assert w_in.shape == (D_MODEL, N_RWKV_COLS + N_MLSTM_COLS + 2 * D_MODEL)
    w_in_t = jnp.swapaxes(w_in, 0, 1)
    z_gt, w_a_b, w_b_b, w_out_b, wt_rw_b = _win_matmul(
        u, w_in_t, N_RWKV_COLS + N_MLSTM_COLS, 2 * D_MODEL, 1024, 1024, BF16, "win_gate",
        cast_along=((w_branch_a, D_RWKV), (w_branch_b, D_MLSTM), (w_out, D_MODEL), (w_in_t, RW_COLS)))

    mu_rw = jnp.pad(mu_shift, (0, RW_COLS - N_RWKV_COLS))
    prep = _rwkv_prep(u, wt_rw_b, row(mu_rw), row(w0), row(a0), row(k_k), row(k_a), row(r_k),
                      _pad_rows(w_decay_up, LORA_PAD), _pad_rows(w_a_up, LORA_PAD), w_g_up, batch, seq)
    r, ld, kt, v, kkn, bb, g, bonus = prep
    y0, q, a_mat, c_mat = _rwkv_chunk(r, ld, kt, v, kkn, bb, batch, seq)
    ya = _rwkv_scan(y0, q, a_mat, c_mat, g, bonus, row(gn_g), row(gn_b), batch, seq)

    gbias = jnp.pad(jnp.concatenate([b_igate, b_fgate]), (0, ML_GATE_PAD - 2 * MLSTM_HEADS))
    yb = _mlstm(u, w_in_t, N_RWKV_COLS, w_conv, row(b_conv), row(gbias), batch, seq)

    merged = _merge(ya, yb, z_gt, w_a_b, w_b_b)
    x1, u2 = _outproj(merged, w_out_b, x2, ada3, row(ln1_g), row(ln1_b), seq)

    h, w_ff2_b = _ff1(u2, w_ff1, row(b_ff1), w_ff2)
    return _ff2(h, w_ff2_b, row(b_ff2), x1, ada3, row(ln2_g), row(ln2_b), seq, F32)


def kernel(x, c, w_ada, b_ada, w_in, mu_shift, w0, w_decay_up, a0, w_a_up, w_g_up, k_k, k_a, r_k, gn_g, gn_b,
           w_conv, b_conv, b_igate, b_fgate, w_branch_a, w_branch_b, w_out, ln1_g, ln1_b, w_ff1, b_ff1,
           w_ff2, b_ff2, ln2_g, ln2_b):
    out_dtype = x.dtype
    batch, seq, d = x.shape
    assert d == D_MODEL and seq % 1024 == 0 and w_ada.shape[0] == DEPTH
    x2 = x.astype(F32).reshape(batch * seq, d)
    cf = c.astype(F32)
    for l in range(DEPTH):
        x2 = _layer(x2, cf, batch, seq, w_ada[l], b_ada[l], w_in[l], mu_shift[l], w0[l], w_decay_up[l], a0[l],
                    w_a_up[l], w_g_up[l], k_k[l], k_a[l], r_k[l].reshape(-1), gn_g[l], gn_b[l], w_conv[l],
                    b_conv[l], b_igate[l], b_fgate[l], w_branch_a[l], w_branch_b[l], w_out[l], ln1_g[l],
                    ln1_b[l], w_ff1[l], b_ff1[l], w_ff2[l], b_ff2[l], ln2_g[l], ln2_b[l])
    return x2.reshape(batch, seq, d).astype(out_dtype)
```

```python
import functools
import math

import jax
import jax.numpy as jnp
from jax import lax
from jax.experimental import pallas as pl
from jax.experimental.pallas import tpu as pltpu

F32 = jnp.float32
BF16 = jnp.bfloat16

D_MODEL = 2048
DEPTH = 1
D_RWKV = D_MODEL // 2
RWKV_HEAD = 64
DECAY_LORA = 96
A_LORA = 96
GATE_LORA = 256
RWKV_GN_EPS = 64e-5
D_MLSTM = D_MODEL // 2
MLSTM_HEADS = 4
MLSTM_DV = D_MLSTM // MLSTM_HEADS
MLSTM_DQK = MLSTM_DV // 2
D_QK = MLSTM_HEADS * MLSTM_DQK
CONV_K = 4
D_FF = 4 * D_MODEL
LN_EPS = 1e-5
ALPHA = (2.0 * DEPTH) ** 0.25

LANES = 128
LORA_PAD = 128
N_RWKV_COLS = 3 * D_RWKV + DECAY_LORA + A_LORA + GATE_LORA
N_MLSTM_COLS = 2 * D_QK + 2 * D_MLSTM + 2 * MLSTM_HEADS
RW_COLS = 3584
ML_GATE_PAD = 128
RWKV_CHUNK = 64
RWKV_TB = 256
RWKV_SCAN_ROWS = 256
RWKV_CORE_TB = 1024
MLSTM_CHUNK = 256
EPILOGUE_ROWS = 256
VMEM_LIMIT = 56 * 1024 * 1024

NN = (((1,), (0,)), ((), ()))
NT = (((1,), (1,)), ((), ()))
TN = (((0,), (0,)), ((), ()))


def _cparams(sem):
    return pltpu.CompilerParams(dimension_semantics=sem, vmem_limit_bytes=VMEM_LIMIT)


def _bf16_parts(x, n):
    if x.dtype == BF16:
        return [x]
    parts, rem = [], x
    for i in range(n):
        p = rem.astype(BF16)
        parts.append(p)
        if i + 1 < n:
            rem = rem - p.astype(F32)
    return parts


def _mdot(a, b, na=1, nb=1, dims=NN):
    ap, bp = _bf16_parts(a, na), _bf16_parts(b, nb)
    order = max(len(ap), len(bp))
    acc = None
    for i, x in enumerate(ap):
        for j, y in enumerate(bp):
            if i + j < order:
                t = lax.dot_general(x, y, dims, preferred_element_type=F32)
                acc = t if acc is None else acc + t
    return acc


def _sigmoid(x):
    return 1.0 / (1.0 + jnp.exp(-x))


def _layernorm(x):
    mu = jnp.mean(x, axis=-1, keepdims=True)
    xc = x - mu
    var = jnp.mean(xc * xc, axis=-1, keepdims=True)
    return xc * lax.rsqrt(var + LN_EPS)


def _iota2(shape, dim):
    return lax.broadcasted_iota(jnp.int32, shape, dim)


def _head_block_ones(n, head):
    r, c = _iota2((n, n), 0), _iota2((n, n), 1)
    return jnp.where((r // head) == (c // head), 1.0, 0.0).astype(BF16)


def _head_sum(x, ones_blk):
    return _mdot(x, ones_blk)


def _ada_kernel(ct_ref, w_ref, b_ref, o_ref, sb_ref):
    tn = w_ref.shape[1]
    nb = sb_ref.shape[0]

    @pl.when(pl.program_id(0) == 0)
    def _():
        cv = ct_ref[...]
        sv = cv * _sigmoid(cv)
        for b in range(nb):
            sb_ref[b] = jnp.broadcast_to(sv[:, b:b + 1], sb_ref.shape[1:])

    ntile = tn // LANES

    def body(i, acc):
        r0 = pl.multiple_of(i * 8, 8)
        sb = [sb_ref[b, pl.ds(r0, 8), :] for b in range(nb)]
        wv = [w_ref[pl.ds(r0, 8), t * LANES:(t + 1) * LANES] for t in range(ntile)]
        return tuple(acc[b * ntile + t] + sb[b] * wv[t] for b in range(nb) for t in range(ntile))

    zero = jnp.zeros((8, LANES), F32)
    acc = lax.fori_loop(0, w_ref.shape[0] // 8, body, (zero,) * (nb * ntile), unroll=4)
    for b in range(nb):
        tot = jnp.concatenate([acc[b * ntile + t] for t in range(ntile)], axis=1)
        o_ref[b:b + 1, :] = jnp.sum(tot, axis=0, keepdims=True) + b_ref[...]


def _ada(c, w_ada, b_ada):
    batch = c.shape[0]
    assert batch == 2
    n = w_ada.shape[1]
    tn = 1536
    return pl.pallas_call(
        _ada_kernel,
        out_shape=jax.ShapeDtypeStruct((batch, n), F32),
        grid=(n // tn,),
        in_specs=[pl.BlockSpec((D_MODEL, batch), lambda j: (0, 0)),
                  pl.BlockSpec((D_MODEL, tn), lambda j: (0, j)),
                  pl.BlockSpec((1, tn), lambda j: (0, j))],
        out_specs=pl.BlockSpec((batch, tn), lambda j: (0, j)),
        scratch_shapes=[pltpu.VMEM((batch, D_MODEL, LANES), F32)],
        compiler_params=_cparams(("arbitrary",)),
        name="ada",
    )(c.T, w_ada, b_ada.reshape(1, n))


def _lnmod_kernel(x_ref, ada_ref, o_ref):
    xn = _layernorm(x_ref[...])
    sh, sc = ada_ref[0, 0:1, :], ada_ref[0, 1:2, :]
    o_ref[...] = (xn * (1.0 + sc) + sh).astype(o_ref.dtype)


def _lnmod(x2, ada3, seq):
    m = x2.shape[0]
    tb = 1024
    per_b = seq // tb
    return pl.pallas_call(
        _lnmod_kernel,
        out_shape=jax.ShapeDtypeStruct((m, D_MODEL), BF16),
        grid=(m // tb,),
        in_specs=[pl.BlockSpec((tb, D_MODEL), lambda i: (i, 0)),
                  pl.BlockSpec((1, 6, D_MODEL), lambda i: (i // per_b, 0, 0))],
        out_specs=pl.BlockSpec((tb, D_MODEL), lambda i: (i, 0)),
        compiler_params=_cparams(("arbitrary",)),
        name="lnmod",
    )(x2, ada3)


def _win_kernel(n_cast, x_ref, wt_ref, *refs):
    cast_in, o_ref, cast_out, wb_ref = refs[:n_cast], refs[n_cast], refs[n_cast + 1:-1], refs[-1]

    @pl.when(pl.program_id(1) == 0)
    def _():
        wb_ref[...] = wt_ref[...].astype(wb_ref.dtype)

    for src, dst in zip(cast_in, cast_out):
        dst[...] = src[...].astype(dst.dtype)
    o_ref[...] = lax.dot_general(x_ref[...], wb_ref[...], NT, preferred_element_type=F32).astype(o_ref.dtype)


def _win_matmul(x, wt, col0, n_out, tm, tn, out_dtype, name, cast_along=()):
    m, k = x.shape
    assert col0 % 8 == 0 and n_out % tn == 0 and col0 + n_out <= wt.shape[0]
    n_m = m // tm
    steps = (n_out // tn) * n_m

    def w_index(j, i):
        return pl.multiple_of(col0 + j * tn, 8), 0

    slabs = []
    for w, nrows in cast_along:
        rows = nrows // steps
        assert rows % 16 == 0 and rows * steps == nrows <= w.shape[0]
        slabs.append(pl.BlockSpec((rows, w.shape[1]), lambda j, i: (j * n_m + i, 0)))
    cast_along = [w for w, _ in cast_along]
    outs = pl.pallas_call(
        functools.partial(_win_kernel, len(cast_along)),
        out_shape=[jax.ShapeDtypeStruct((m, n_out), out_dtype)]
        + [jax.ShapeDtypeStruct((spec.block_shape[0] * steps, w.shape[1]), BF16)
           for w, spec in zip(cast_along, slabs)],
        grid=(n_out // tn, n_m),
        in_specs=[pl.BlockSpec((tm, k), lambda j, i: (i, 0)),
                  pl.BlockSpec((pl.Element(tn), pl.Element(k)), w_index)] + slabs,
        out_specs=[pl.BlockSpec((tm, tn), lambda j, i: (i, j))] + slabs,
        scratch_shapes=[pltpu.VMEM((tn, k), BF16)],
        compiler_params=_cparams(("arbitrary", "arbitrary")),
        name=name,
    )(x, wt, *cast_along)
    return outs if cast_along else outs[0]


def _rwkv_prep_kernel(u_ref, wt_ref, mu_ref, w0_ref, a0_ref, kk_ref, ka_ref, rk_ref, wdu_ref, wau_ref, wgu_ref,
                      r_o, ld_o, kt_o, v_o, kkn_o, bb_o, g_o, bonus_o, carry_ref):
    tb = u_ref.shape[0]

    @pl.when(pl.program_id(1) == 0)
    def _():
        carry_ref[...] = jnp.zeros_like(carry_ref)

    z = lax.dot_general(u_ref[...], wt_ref[...], NT, preferred_element_type=F32)
    prev = pltpu.roll(z, 1, 0)
    prev = jnp.where(_iota2(z.shape, 0) == 0, carry_ref[0:1, :], prev)
    carry_ref[0:1, :] = z[tb - 1:tb, :]
    zs = z + (prev - z) * mu_ref[...]

    c = D_RWKV
    r, k, v = zs[:, 0:c], zs[:, c:2 * c], zs[:, 2 * c:3 * c]
    wd = zs[:, 3 * c:3 * c + LORA_PAD]
    ad = zs[:, 3 * c + DECAY_LORA:3 * c + DECAY_LORA + LORA_PAD]
    gd = zs[:, 3 * c + DECAY_LORA + A_LORA:3 * c + DECAY_LORA + A_LORA + GATE_LORA]

    ld = -math.exp(-0.5) * _sigmoid(w0_ref[...] + _mdot(jnp.tanh(wd), wdu_ref[...], 2, 2))
    a = _sigmoid(a0_ref[...] + _mdot(ad, wau_ref[...]))
    g = _mdot(_sigmoid(gd), wgu_ref[...], 1, 1)

    ones_blk = _head_block_ones(LANES, RWKV_HEAD)
    kq = k * kk_ref[...]
    kt = k * (1.0 + (a - 1.0) * ka_ref[...])
    sq = kq * kq
    bn = r * kt * rk_ref[...]
    ss = jnp.concatenate([_head_sum(sq[:, s:s + LANES], ones_blk) for s in range(0, c, LANES)], axis=1)
    bs = jnp.concatenate([_head_sum(bn[:, s:s + LANES], ones_blk) for s in range(0, c, LANES)], axis=1)
    kkn = kq * lax.rsqrt(ss + 1e-12)

    r_o[...] = r.astype(r_o.dtype)
    ld_o[...] = ld
    kt_o[...] = kt.astype(kt_o.dtype)
    v_o[...] = v.astype(v_o.dtype)
    kkn_o[...] = kkn.astype(kkn_o.dtype)
    bb_o[...] = (a * kkn).astype(bb_o.dtype)
    g_o[...] = g.astype(g_o.dtype)
    bonus_o[...] = (bs * v).astype(bonus_o.dtype)


def _rwkv_prep(u, wt_rw, mu_rw, w0, a0, k_k, k_a, r_k, wdu, wau, wgu, batch, seq):
    m, k = u.shape
    assert wt_rw.shape == (RW_COLS, k)
    tb = RWKV_TB
    per_b = seq // tb
    row = lambda b, j: (b * per_b + j, 0)
    const = lambda b, j: (0, 0)
    vec = pl.BlockSpec((1, D_RWKV), const)
    outs = [jax.ShapeDtypeStruct((m, D_RWKV), F32 if i == 1 else BF16) for i in range(8)]
    return pl.pallas_call(
        _rwkv_prep_kernel,
        out_shape=outs,
        grid=(batch, per_b),
        in_specs=[pl.BlockSpec((tb, k), row),
                  pl.BlockSpec((RW_COLS, k), const, pipeline_mode=pl.Buffered(1)),
                  pl.BlockSpec((1, RW_COLS), const),
                  vec, vec, vec, vec, vec,
                  pl.BlockSpec((LORA_PAD, D_RWKV), const), pl.BlockSpec((LORA_PAD, D_RWKV), const),
                  pl.BlockSpec((GATE_LORA, D_RWKV), const)],
        out_specs=[pl.BlockSpec((tb, D_RWKV), row)] * 8,
        scratch_shapes=[pltpu.VMEM((8, RW_COLS), F32)],
        compiler_params=_cparams(("arbitrary", "arbitrary")),
        name="rwkv_prep",
    )(u, wt_rw, mu_rw, w0, a0, k_k, k_a, r_k, wdu, wau, wgu)


def _rwkv_chunk_kernel(r_ref, ld_ref, kt_ref, v_ref, kk_ref, bb_ref, y0_ref, q_ref, a_ref, c_ref):
    tb = r_ref.shape[0]
    lc = RWKV_CHUNK
    n = 2 * lc
    nchunk = tb // lc

    lane = _iota2((1, LANES), 1)
    m0 = jnp.where(lane < RWKV_HEAD, 1.0, 0.0)
    m1 = 1.0 - m0
    ri, ci = _iota2((n, n), 0), _iota2((n, n), 1)
    same_head = (ri // lc) == (ci // lc)
    strict = jnp.where(same_head & (ri > ci), 1.0, 0.0)
    incl = jnp.where(same_head & (ri >= ci), 1.0, 0.0)
    eye = jnp.where(ri == ci, 1.0, 0.0)

    tw = 4 * lc
    rt, ct = _iota2((tw, tw), 0), _iota2((tw, tw), 1)
    tri = jnp.where(((rt // lc) == (ct // lc)) & (ct <= rt), 1.0, 0.0).astype(BF16)

    ld = ld_ref[...]
    cl = jnp.concatenate([_mdot(tri, ld[i:i + tw], 1, 3) for i in range(0, tb, tw)], axis=0)
    e_pos = jnp.exp(cl)
    e_neg = jnp.exp(-cl)
    a_hat = kk_ref[...] * jnp.exp(cl - ld)
    b_hat = bb_ref[...] * e_neg
    k_hat = kt_ref[...] * e_neg
    r_hat = r_ref[...] * e_pos
    v_all = v_ref[...]

    def stack(x):
        return jnp.concatenate([x * m0, x * m1], axis=0)

    def bf(t):
        return t.astype(BF16)

    chunks = range(nchunk)
    st = [[stack(x[c * lc:(c + 1) * lc]) for x in (a_hat, b_hat, k_hat, r_hat, v_all)] for c in chunks]
    a_st, b_st, k_st, r_st, v_st = ([st[c][i] for c in chunks] for i in range(5))
    b_b, k_b, v_b = ([bf(t) for t in ts] for ts in (b_st, k_st, v_st))
    p = [_mdot(jnp.concatenate([a_st[c], r_st[c]], axis=0),
               jnp.concatenate([b_b[c], k_b[c]], axis=0), dims=NT) for c in chunks]
    mab = [p[c][0:n, 0:n] * strict for c in chunks]
    mab_b = [bf(t) for t in mab]
    mak_b = [bf(p[c][0:n, n:] * strict) for c in chunks]
    mrb_b = [bf(p[c][n:, 0:n] * incl) for c in chunks]
    mrk_b = [bf(p[c][n:, n:] * incl) for c in chunks]

    x = [eye - jnp.where((ri // 2 == ci // 2), mab[c], 0.0) for c in chunks]
    w1 = [_mdot(mak_b[c], v_b[c]) for c in chunks]
    zero_b = jnp.zeros((), BF16)
    s = 2
    while s < lc:
        lvl = (ri // (2 * s) == ci // (2 * s)) & (ri // s != ci // s)
        x_b = [bf(t) for t in x]
        nx = [_mdot(jnp.where(lvl, mab_b[c], zero_b), x_b[c]) for c in chunks]
        x = [x[c] - _mdot(x_b[c], nx[c]) for c in chunks]
        s *= 2

    ta = [_mdot(x[c], jnp.concatenate([w1[c], a_st[c]], axis=1)) for c in chunks]
    zeros_b = jnp.zeros((n, LANES), BF16)
    ugv_b = [jnp.concatenate([bf(jnp.concatenate([-ta[c][:, 0:LANES], ta[c][:, LANES:]], axis=1)),
                              jnp.concatenate([v_b[c], zeros_b], axis=1)], axis=0) for c in chunks]
    ru = [_mdot(jnp.concatenate([mrb_b[c], mrk_b[c]], axis=1), ugv_b[c]) for c in chunks]
    y0 = [ru[c][:, 0:LANES] for c in chunks]
    q = [r_st[c] - ru[c][:, LANES:] for c in chunks]
    pl_row = [e_pos[(c + 1) * lc - 1:(c + 1) * lc, :] for c in chunks]
    bk_p = [bf(jnp.concatenate([b_st[c] * pl_row[c], k_st[c] * pl_row[c]], axis=0)) for c in chunks]
    bu = [_mdot(bk_p[c], ugv_b[c], dims=TN) for c in chunks]
    ct = [bu[c][:, 0:LANES] for c in chunks]
    at = [eye * pl_row[c] - bu[c][:, LANES:] for c in chunks]

    for c in chunks:
        rows = slice(c * lc, (c + 1) * lc)
        y0_ref[rows, :] = y0[c][0:lc] + y0[c][lc:]
        q_ref[rows, :] = (q[c][0:lc] + q[c][lc:]).astype(q_ref.dtype)
        a_ref[0, c, 0] = (at[c][0:RWKV_HEAD] + at[c][RWKV_HEAD:]).astype(a_ref.dtype)
        c_ref[0, c, 0] = ct[c][0:RWKV_HEAD] + ct[c][RWKV_HEAD:]


def _rwkv_chunk(r, ld, kt, v, kkn, bb, batch, seq):
    m = r.shape[0]
    tb = RWKV_CORE_TB
    per_b = seq // tb
    cpb = tb // RWKV_CHUNK
    npair = D_RWKV // LANES
    slab = pl.BlockSpec((tb, LANES), lambda b, p, j: (b * per_b + j, p))
    mat = pl.BlockSpec((1, cpb, 1, RWKV_HEAD, LANES), lambda b, p, j: (b, j, p, 0, 0))
    mshape = (batch, seq // RWKV_CHUNK, npair, RWKV_HEAD, LANES)
    return pl.pallas_call(
        _rwkv_chunk_kernel,
        out_shape=[jax.ShapeDtypeStruct((m, D_RWKV), F32), jax.ShapeDtypeStruct((m, D_RWKV), BF16),
                   jax.ShapeDtypeStruct(mshape, BF16), jax.ShapeDtypeStruct(mshape, F32)],
        grid=(batch, npair, per_b),
        in_specs=[slab] * 6,
        out_specs=[slab, slab, mat, mat],
        compiler_params=_cparams(("arbitrary", "arbitrary", "arbitrary")),
        name="rwkv_chunk",
    )(r, ld, kt, v, kkn, bb)


def _rwkv_scan_kernel(y0_ref, q_ref, a_ref, c_ref, g_ref, bonus_ref, gng_ref, gnb_ref, ya_ref, s_ref):
    nb, rows, _ = y0_ref.shape
    lc = RWKV_CHUNK
    npair = D_RWKV // LANES

    @pl.when(pl.program_id(0) == 0)
    def _():
        s_ref[...] = jnp.zeros_like(s_ref)

    ones_blk = _head_block_ones(LANES, RWKV_HEAD)
    head0 = _iota2((RWKV_HEAD, LANES), 1) < RWKV_HEAD

    def block_diag(t):
        zero = jnp.zeros((), t.dtype)
        return jnp.concatenate([jnp.where(head0, t, zero), jnp.where(head0, zero, t)], axis=0)

    streams = [(b, p) for b in range(nb) for p in range(npair)]
    for ci in range(rows // lc):
        rs = slice(ci * lc, (ci + 1) * lc)
        state_b = [s_ref[b * npair + p].astype(BF16) for b, p in streams]
        y = [y0_ref[b, rs, p * LANES:(p + 1) * LANES]
             + _mdot(q_ref[b, rs, p * LANES:(p + 1) * LANES], state_b[i])
             for i, (b, p) in enumerate(streams)]
        for i, (b, p) in enumerate(streams):
            s_ref[b * npair + p] = (_mdot(block_diag(a_ref[b, ci, p]), state_b[i])
                                    + block_diag(c_ref[b, ci, p]))
        mu = [_head_sum(t, ones_blk) * (1.0 / RWKV_HEAD) for t in y]
        yc = [y[i] - mu[i] for i in range(len(streams))]
        var = [_head_sum(t * t, ones_blk) * (1.0 / RWKV_HEAD) for t in yc]
        for i, (b, p) in enumerate(streams):
            ls = slice(p * LANES, (p + 1) * LANES)
            yn = yc[i] * lax.rsqrt(var[i] + RWKV_GN_EPS) * gng_ref[:, ls] + gnb_ref[:, ls]
            ya_ref[b, rs, ls] = ((yn + bonus_ref[b, rs, ls]) * g_ref[b, rs, ls]).astype(ya_ref.dtype)


def _rwkv_scan(y0, q, a_mat, c_mat, g, bonus, gn_g, gn_b, batch, seq):
    rows = RWKV_SCAN_ROWS
    npair = D_RWKV // LANES
    tok = pl.BlockSpec((batch, rows, D_RWKV), lambda j: (0, j, 0))
    mat = pl.BlockSpec((batch, rows // RWKV_CHUNK, npair, RWKV_HEAD, LANES), lambda j: (0, j, 0, 0, 0))
    vec = pl.BlockSpec((1, D_RWKV), lambda j: (0, 0))
    shape3 = (batch, seq, D_RWKV)
    return pl.pallas_call(
        _rwkv_scan_kernel,
        out_shape=jax.ShapeDtypeStruct(shape3, BF16),
        grid=(seq // rows,),
        in_specs=[tok, tok, mat, mat, tok, tok, vec, vec],
        out_specs=tok,
        scratch_shapes=[pltpu.VMEM((batch * npair, LANES, LANES), F32)],
        compiler_params=_cparams(("arbitrary",)),
        name="rwkv_scan",
    )(y0.reshape(shape3), q.reshape(shape3), a_mat, c_mat, g.reshape(shape3), bonus.reshape(shape3),
      gn_g, gn_b).reshape(batch * seq, D_RWKV)


def _mlstm_kernel(u_ref, wa_ref, wo_ref, wconv_ref, bconv_ref, gbias_ref, yb_ref,
                  tail_ref, ct_ref, n_ref, m_ref, wab_ref, wob_ref):
    lc = u_ref.shape[0]
    dqk, dv, nh = MLSTM_DQK, MLSTM_DV, MLSTM_HEADS

    @pl.when(pl.program_id(1) == 0)
    def _():
        tail_ref[...] = jnp.zeros_like(tail_ref)
        ct_ref[...] = jnp.zeros_like(ct_ref)
        n_ref[...] = jnp.zeros_like(n_ref)
        m_ref[...] = jnp.zeros_like(m_ref)

    @pl.when((pl.program_id(0) == 0) & (pl.program_id(1) == 0))
    def _():
        wab_ref[...] = wa_ref[...].astype(wab_ref.dtype)
        wob_ref[...] = wo_ref[...].astype(wob_ref.dtype)

    u = u_ref[...]
    za = lax.dot_general(u, wab_ref[...], NT, preferred_element_type=F32)
    o_all = lax.dot_general(u, wob_ref[...], NT, preferred_element_type=F32)

    zqk = za[:, 0:2 * D_QK]
    ext = jnp.concatenate([tail_ref[...], zqk], axis=0)
    tail_ref[...] = zqk[lc - 8:lc, :]
    conv = bconv_ref[...] + wconv_ref[CONV_K - 1:CONV_K, :] * zqk
    for d in range(1, CONV_K):
        conv = conv + wconv_ref[CONV_K - 1 - d:CONV_K - d, :] * ext[8 - d:8 - d + lc, :]
    qk = conv * _sigmoid(conv)
    q_all = qk[:, 0:D_QK]
    k_all = qk[:, D_QK:] * (dqk ** -0.5)
    v_all = za[:, 2 * D_QK:2 * D_QK + D_MLSTM]

    gz = za[:, 2 * D_QK + D_MLSTM:] + gbias_ref[...]
    lf_col = jnp.minimum(gz, 0.0) - jnp.log(1.0 + jnp.exp(-jnp.abs(gz)))
    rr, cc = _iota2((lc, lc), 0), _iota2((lc, lc), 1)
    causal = cc <= rr
    tri = jnp.where(causal, 1.0, 0.0).astype(BF16)
    b_col = _mdot(tri, lf_col, 1, 3)
    sel = jnp.where(_iota2((8, LANES), 0) == _iota2((8, LANES), 1), 1.0, 0.0).astype(BF16)
    g_row = _mdot(sel, gz, 1, 3, NT)
    lf_row = jnp.minimum(g_row, 0.0) - jnp.log(1.0 + jnp.exp(-jnp.abs(g_row)))
    b_row = _mdot(lf_row, tri, 3, 1, NT)

    heads = range(nh)
    qf = [q_all[:, h * dqk:(h + 1) * dqk] for h in heads]
    q = [t.astype(BF16) for t in qf]
    k = [k_all[:, h * dqk:(h + 1) * dqk] for h in heads]
    v = [v_all[:, h * dv:(h + 1) * dv].astype(BF16) for h in heads]
    bc = [b_col[:, nh + h:nh + h + 1] for h in heads]
    ic = [gz[:, h:h + 1] for h in heads]
    br = [b_row[nh + h:nh + h + 1, :] for h in heads]
    ir = [g_row[h:h + 1, :] for h in heads]
    g_tot = [b_col[lc - 1:lc, nh + h:nh + h + 1] for h in heads]
    m_prev = [m_ref[0:1, h:h + 1] for h in heads]
    ct = [ct_ref[h] for h in heads]
    nvec = [n_ref[h:h + 1, :] for h in heads]

    qk = [_mdot(q[h], k[h], dims=NT) for h in heads]
    qc = [_mdot(q[h], ct[h]) for h in heads]
    bri = [br[h] - ir[h] for h in heads]
    dm = [jnp.where(causal, bc[h] - bri[h], -jnp.inf) for h in heads]
    inter = [bc[h] + m_prev[h] for h in heads]
    mt = [jnp.maximum(inter[h], jnp.max(dm[h], axis=-1, keepdims=True)) for h in heads]
    sc = [qk[h] * jnp.exp(dm[h] - mt[h]) for h in heads]
    winter = [jnp.exp(inter[h] - mt[h]) for h in heads]
    sv = [_mdot(sc[h], v[h]) for h in heads]
    m_new = [jnp.maximum(g_tot[h] + m_prev[h], jnp.max(g_tot[h] - bri[h], axis=-1, keepdims=True))
             for h in heads]
    kw = [k[h] * jnp.exp(g_tot[h] - bc[h] + ic[h] - m_new[h]) for h in heads]
    kv = [_mdot(kw[h], v[h], dims=TN) for h in heads]
    for h in heads:
        num = winter[h] * qc[h] + sv[h]
        den = (winter[h] * jnp.sum(qf[h] * nvec[h], axis=-1, keepdims=True)
               + jnp.sum(sc[h], axis=-1, keepdims=True))
        hout = num / jnp.maximum(jnp.abs(den), jnp.exp(-mt[h]))
        wc = jnp.exp(g_tot[h] + m_prev[h] - m_new[h])
        ct_ref[h] = wc * ct[h] + kv[h]
        n_ref[h:h + 1, :] = wc * nvec[h] + jnp.sum(kw[h], axis=0, keepdims=True)
        m_ref[0:1, h:h + 1] = m_new[h]
        o = o_all[:, h * dv:(h + 1) * dv]
        yb_ref[:, h * dv:(h + 1) * dv] = (_sigmoid(o) * hout).astype(yb_ref.dtype)


def _mlstm(u, w_in_t, col0, w_conv, b_conv, gbias, batch, seq):
    m, k = u.shape
    lc = MLSTM_CHUNK
    per_b = seq // lc
    const = lambda b, j: (0, 0)
    rows_a = 2 * D_QK + D_MLSTM + ML_GATE_PAD
    col_o = col0 + 2 * D_QK + D_MLSTM + 2 * MLSTM_HEADS
    assert col0 % 8 == 0 and col_o % 8 == 0 and col_o + D_MLSTM <= w_in_t.shape[0]
    resident = lambda rows, r0: pl.BlockSpec((pl.Element(rows), pl.Element(k)), lambda b, j: (r0, 0),
                                             pipeline_mode=pl.Buffered(1))
    return pl.pallas_call(
        _mlstm_kernel,
        out_shape=jax.ShapeDtypeStruct((m, D_MLSTM), BF16),
        grid=(batch, per_b),
        in_specs=[pl.BlockSpec((lc, k), lambda b, j: (b * per_b + j, 0)),
                  resident(rows_a, col0), resident(D_MLSTM, col_o),
                  pl.BlockSpec((CONV_K, 2 * D_QK), const),
                  pl.BlockSpec((1, 2 * D_QK), const),
                  pl.BlockSpec((1, ML_GATE_PAD), const)],
        out_specs=pl.BlockSpec((lc, D_MLSTM), lambda b, j: (b * per_b + j, 0)),
        scratch_shapes=[pltpu.VMEM((8, 2 * D_QK), F32),
                        pltpu.VMEM((MLSTM_HEADS, MLSTM_DQK, MLSTM_DV), F32),
                        pltpu.VMEM((8, MLSTM_DQK), F32),
                        pltpu.VMEM((8, LANES), F32),
                        pltpu.VMEM((rows_a, k), BF16),
                        pltpu.VMEM((D_MLSTM, k), BF16)],
        compiler_params=_cparams(("arbitrary", "arbitrary")),
        name="mlstm",
    )(u, w_in_t, w_in_t, w_conv, b_conv, gbias)


def _merge_kernel(ya_ref, yb_ref, zg_ref, wa_ref, wb_ref, o_ref):
    groups = [slice(i, i + EPILOGUE_ROWS) for i in range(0, ya_ref.shape[0], EPILOGUE_ROWS)]
    pab = [(jnp.dot(ya_ref[rows, :], wa_ref[...], preferred_element_type=F32),
            jnp.dot(yb_ref[rows, :], wb_ref[...], preferred_element_type=F32)) for rows in groups]
    for (pa, pb), rows in zip(pab, groups):
        ga = _sigmoid(zg_ref[rows, 0:D_MODEL].astype(F32))
        gb = _sigmoid(zg_ref[rows, D_MODEL:].astype(F32))
        o_ref[rows, :] = (ga * pa + gb * pb).astype(o_ref.dtype)


def _merge(ya, yb, zg, wa, wb):
    m = ya.shape[0]
    tm = 512
    const = lambda i: (0, 0)
    return pl.pallas_call(
        _merge_kernel,
        out_shape=jax.ShapeDtypeStruct((m, D_MODEL), BF16),
        grid=(m // tm,),
        in_specs=[pl.BlockSpec((tm, D_RWKV), lambda i: (i, 0)),
                  pl.BlockSpec((tm, D_MLSTM), lambda i: (i, 0)),
                  pl.BlockSpec((tm, 2 * D_MODEL), lambda i: (i, 0)),
                  pl.BlockSpec((D_RWKV, D_MODEL), const),
                  pl.BlockSpec((D_MLSTM, D_MODEL), const)],
        out_specs=pl.BlockSpec((tm, D_MODEL), lambda i: (i, 0)),
        compiler_params=_cparams(("arbitrary",)),
        name="merge",
    )(ya, yb, zg, wa, wb)


def _outproj_kernel(mg_ref, w_ref, x_ref, ada_ref, g_ref, b_ref, x1_ref, u2_ref):
    groups = [slice(i, i + EPILOGUE_ROWS) for i in range(0, mg_ref.shape[0], EPILOGUE_ROWS)]
    ys = [jnp.dot(mg_ref[rows, :], w_ref[...], preferred_element_type=F32) for rows in groups]
    g1 = ada_ref[0, 2:3, :]
    sh2, sc2 = ada_ref[0, 3:4, :], ada_ref[0, 4:5, :]
    for y, rows in zip(ys, groups):
        x1 = _layernorm(ALPHA * x_ref[rows, :] + g1 * y) * g_ref[...] + b_ref[...]
        x1_ref[rows, :] = x1
        u2_ref[rows, :] = (_layernorm(x1) * (1.0 + sc2) + sh2).astype(u2_ref.dtype)


def _outproj(merged, w_out, x2, ada3, ln_g, ln_b, seq):
    m = merged.shape[0]
    tm = 512
    per_b = seq // tm
    const = lambda i: (0, 0)
    row = pl.BlockSpec((tm, D_MODEL), lambda i: (i, 0))
    return pl.pallas_call(
        _outproj_kernel,
        out_shape=[jax.ShapeDtypeStruct((m, D_MODEL), F32), jax.ShapeDtypeStruct((m, D_MODEL), BF16)],
        grid=(m // tm,),
        in_specs=[row, pl.BlockSpec((D_MODEL, D_MODEL), const, pipeline_mode=pl.Buffered(1)), row,
                  pl.BlockSpec((1, 6, D_MODEL), lambda i: (i // per_b, 0, 0)),
                  pl.BlockSpec((1, D_MODEL), const), pl.BlockSpec((1, D_MODEL), const)],
        out_specs=[row, row],
        compiler_params=_cparams(("arbitrary",)),
        name="outproj",
    )(merged, w_out, x2, ada3, ln_g, ln_b)


def _ff1_kernel(x_ref, w_ref, b_ref, w2_ref, o_ref, w2b_ref, wb_ref):
    @pl.when(pl.program_id(1) == 0)
    def _():
        wb_ref[...] = w_ref[...].astype(wb_ref.dtype)

    w2b_ref[...] = w2_ref[...].astype(w2b_ref.dtype)
    groups = [slice(i, i + 2 * EPILOGUE_ROWS) for i in range(0, x_ref.shape[0], 2 * EPILOGUE_ROWS)]
    hs = [jnp.dot(x_ref[rows, :], wb_ref[...], preferred_element_type=F32) for rows in groups]
    for h, rows in zip(hs, groups):
        h = jnp.maximum(h + b_ref[...], 0.0)
        o_ref[rows, :] = (h * h).astype(o_ref.dtype)


def _ff1(u2, w1, b1, w2):
    m, k = u2.shape
    n = w1.shape[1]
    tm, tn = 1024, 1024
    n_m = m // tm
    rows2 = w2.shape[0] // ((n // tn) * n_m)
    assert rows2 % 16 == 0 and rows2 * (n // tn) * n_m == w2.shape[0]
    slab = pl.BlockSpec((rows2, w2.shape[1]), lambda j, i: (j * n_m + i, 0))
    return pl.pallas_call(
        _ff1_kernel,
        out_shape=[jax.ShapeDtypeStruct((m, n), BF16), jax.ShapeDtypeStruct(w2.shape, BF16)],
        grid=(n // tn, n_m),
        in_specs=[pl.BlockSpec((tm, k), lambda j, i: (i, 0)),
                  pl.BlockSpec((k, tn), lambda j, i: (0, j)),
                  pl.BlockSpec((1, tn), lambda j, i: (0, j)),
                  slab],
        out_specs=[pl.BlockSpec((tm, tn), lambda j, i: (i, j)), slab],
        scratch_shapes=[pltpu.VMEM((k, tn), BF16)],
        compiler_params=_cparams(("arbitrary", "arbitrary")),
        name="ff1",
    )(u2, w1, b1, w2)


def _ff2_kernel(h_ref, w_ref, b_ref, x1_ref, ada_ref, g_ref, bb_ref, o_ref):
    y2 = jnp.dot(h_ref[...], w_ref[...], preferred_element_type=F32) + b_ref[...]
    g2 = ada_ref[0, 5:6, :]
    o_ref[...] = (_layernorm(ALPHA * x1_ref[...] + g2 * y2) * g_ref[...] + bb_ref[...]).astype(o_ref.dtype)


def _ff2(h, w2, b2, x1, ada3, ln_g, ln_b, seq, out_dtype):
    m, k = h.shape
    tm = 256
    per_b = seq // tm
    const = lambda i: (0, 0)
    row = pl.BlockSpec((tm, D_MODEL), lambda i: (i, 0))
    return pl.pallas_call(
        _ff2_kernel,
        out_shape=jax.ShapeDtypeStruct((m, D_MODEL), out_dtype),
        grid=(m // tm,),
        in_specs=[pl.BlockSpec((tm, k), lambda i: (i, 0)),
                  pl.BlockSpec((k, D_MODEL), const, pipeline_mode=pl.Buffered(1)),
                  pl.BlockSpec((1, D_MODEL), const), row,
                  pl.BlockSpec((1, 6, D_MODEL), lambda i: (i // per_b, 0, 0)),
                  pl.BlockSpec((1, D_MODEL), const), pl.BlockSpec((1, D_MODEL), const)],
        out_specs=row,
        compiler_params=_cparams(("arbitrary",)),
        name="ff2",
    )(h, w2, b2, x1, ada3, ln_g, ln_b)


def _pad_rows(w, height):
    return jnp.pad(w, ((0, height - w.shape[0]), (0, 0)))


def _layer(x2, c, batch, seq, w_ada, b_ada, w_in, mu_shift, w0, w_decay_up, a0, w_a_up, w_g_up, k_k, k_a, r_k,
           gn_g, gn_b, w_conv, b_conv, b_igate, b_fgate, w_branch_a, w_branch_b, w_out,
           ln1_g, ln1_b, w_ff1, b_ff1, w_ff2, b_ff2, ln2_g, ln2_b):
    row = lambda v: v.reshape(1, -1)

    ada3 = _ada(c, w_ada, b_ada).reshape(batch, 6, D_MODEL)
    u = _lnmod(x2, ada3, seq)

    assert w_in.shape == (D_MODEL, N_RWKV_COLS + N_MLSTM_COLS + 2 * D_MODEL)
    w_in_t = jnp.swapaxes(w_in, 0, 1)
    z_gt, w_a_b, w_b_b, w_out_b, wt_rw_b = _win_matmul(
        u, w_in_t, N_RWKV_COLS + N_MLSTM_COLS, 2 * D_MODEL, 1024, 1024, BF16, "win_gate",
        cast_along=((w_branch_a, D_RWKV), (w_branch_b, D_MLSTM), (w_out, D_MODEL), (w_in_t, RW_COLS)))

    mu_rw = jnp.pad(mu_shift, (0, RW_COLS - N_RWKV_COLS))
    prep = _rwkv_prep(u, wt_rw_b, row(mu_rw), row(w0), row(a0), row(k_k), row(k_a), row(r_k),
                      _pad_rows(w_decay_up, LORA_PAD), _pad_rows(w_a_up, LORA_PAD), w_g_up, batch, seq)
    r, ld, kt, v, kkn, bb, g, bonus = prep
    y0, q, a_mat, c_mat = _rwkv_chunk(r, ld, kt, v, kkn, bb, batch, seq)
    ya = _rwkv_scan(y0, q, a_mat, c_mat, g, bonus, row(gn_g), row(gn_b), batch, seq)

    gbias = jnp.pad(jnp.concatenate([b_igate, b_fgate]), (0, ML_GATE_PAD - 2 * MLSTM_HEADS))
    yb = _mlstm(u, w_in_t, N_RWKV_COLS, w_conv, row(b_conv), row(gbias), batch, seq)

    merged = _merge(ya, yb, z_gt, w_a_b, w_b_b)
    x1, u2 = _outproj(merged, w_out_b, x2, ada3, row(ln1_g), row(ln1_b), seq)

    h, w_ff2_b = _ff1(u2, w_ff1, row(b_ff1), w_ff2)
    return _ff2(h, w_ff2_b, row(b_ff2), x1, ada3, row(ln2_g), row(ln2_b), seq, F32)


def kernel(x, c, w_ada, b_ada, w_in, mu_shift, w0, w_decay_up, a0, w_a_up, w_g_up, k_k, k_a, r_k, gn_g, gn_b,
           w_conv, b_conv, b_igate, b_fgate, w_branch_a, w_branch_b, w_out, ln1_g, ln1_b, w_ff1, b_ff1,
           w_ff2, b_ff2, ln2_g, ln2_b):
    out_dtype = x.dtype
    batch, seq, d = x.shape
    assert d == D_MODEL and seq % 1024 == 0 and w_ada.shape[0] == DEPTH
    x2 = x.astype(F32).reshape(batch * seq, d)
    cf = c.astype(F32)
    for l in range(DEPTH):
        x2 = _layer(x2, cf, batch, seq, w_ada[l], b_ada[l], w_in[l], mu_shift[l], w0[l], w_decay_up[l], a0[l],
                    w_a_up[l], w_g_up[l], k_k[l], k_a[l], r_k[l].reshape(-1), gn_g[l], gn_b[l], w_conv[l],
                    b_conv[l], b_igate[l], b_fgate[l], w_branch_a[l], w_branch_b[l], w_out[l], ln1_g[l],
                    ln1_b[l], w_ff1[l], b_ff1[l], w_ff2[l], b_ff2[l], ln2_g[l], ln2_b[l])
    return x2.reshape(batch, seq, d).astype(out_dtype)
```

```python
import functools
import math

import jax
import jax.numpy as jnp
from jax import lax
from jax.experimental import pallas as pl
from jax.experimental.pallas import tpu as pltpu

F32 = jnp.float32
BF16 = jnp.bfloat16

D_MODEL = 2048
DEPTH = 1
D_RWKV = D_MODEL // 2
RWKV_HEAD = 64
DECAY_LORA = 96
A_LORA = 96
GATE_LORA = 256
RWKV_GN_EPS = 64e-5
D_MLSTM = D_MODEL // 2
MLSTM_HEADS = 4
MLSTM_DV = D_MLSTM // MLSTM_HEADS
MLSTM_DQK = MLSTM_DV // 2
D_QK = MLSTM_HEADS * MLSTM_DQK
CONV_K = 4
D_FF = 4 * D_MODEL
LN_EPS = 1e-5
ALPHA = (2.0 * DEPTH) ** 0.25

LANES = 128
LORA_PAD = 128
N_RWKV_COLS = 3 * D_RWKV + DECAY_LORA + A_LORA + GATE_LORA
N_MLSTM_COLS = 2 * D_QK + 2 * D_MLSTM + 2 * MLSTM_HEADS
RW_COLS = 3584
ML_GATE_PAD = 128
RWKV_CHUNK = 64
RWKV_TB = 512
RWKV_SCAN_ROWS = 256
RWKV_CORE_TB = 1024
MLSTM_CHUNK = 256
EPILOGUE_ROWS = 256
VMEM_LIMIT = 56 * 1024 * 1024

NN = (((1,), (0,)), ((), ()))
NT = (((1,), (1,)), ((), ()))
TN = (((0,), (0,)), ((), ()))


def _cparams(sem):
    return pltpu.CompilerParams(dimension_semantics=sem, vmem_limit_bytes=VMEM_LIMIT)


def _bf16_parts(x, n):
    if x.dtype == BF16:
        return [x]
    parts, rem = [], x
    for i in range(n):
        p = rem.astype(BF16)
        parts.append(p)
        if i + 1 < n:
            rem = rem - p.astype(F32)
    return parts


def _mdot(a, b, na=1, nb=1, dims=NN):
    ap, bp = _bf16_parts(a, na), _bf16_parts(b, nb)
    order = max(len(ap), len(bp))
    acc = None
    for i, x in enumerate(ap):
        for j, y in enumerate(bp):
            if i + j < order:
                t = lax.dot_general(x, y, dims, preferred_element_type=F32)
                acc = t if acc is None else acc + t
    return acc


def _sigmoid(x):
    return 1.0 / (1.0 + jnp.exp(-x))


def _layernorm(x):
    mu = jnp.mean(x, axis=-1, keepdims=True)
    xc = x - mu
    var = jnp.mean(xc * xc, axis=-1, keepdims=True)
    return xc * lax.rsqrt(var + LN_EPS)


def _iota2(shape, dim):
    return lax.broadcasted_iota(jnp.int32, shape, dim)


def _head_block_ones(n, head):
    r, c = _iota2((n, n), 0), _iota2((n, n), 1)
    return jnp.where((r // head) == (c // head), 1.0, 0.0).astype(BF16)


def _head_sum(x, ones_blk):
    return _mdot(x, ones_blk)


def _ada_kernel(ct_ref, w_ref, b_ref, o_ref, sb_ref):
    tn = w_ref.shape[1]
    nb = sb_ref.shape[0]

    @pl.when(pl.program_id(0) == 0)
    def _():
        cv = ct_ref[...]
        sv = cv * _sigmoid(cv)
        for b in range(nb):
            sb_ref[b] = jnp.broadcast_to(sv[:, b:b + 1], sb_ref.shape[1:])

    ntile = tn // LANES

    def body(i, acc):
        r0 = pl.multiple_of(i * 8, 8)
        sb = [sb_ref[b, pl.ds(r0, 8), :] for b in range(nb)]
        wv = [w_ref[pl.ds(r0, 8), t * LANES:(t + 1) * LANES] for t in range(ntile)]
        return tuple(acc[b * ntile + t] + sb[b] * wv[t] for b in range(nb) for t in range(ntile))

    zero = jnp.zeros((8, LANES), F32)
    acc = lax.fori_loop(0, w_ref.shape[0] // 8, body, (zero,) * (nb * ntile), unroll=4)
    for b in range(nb):
        tot = jnp.concatenate([acc[b * ntile + t] for t in range(ntile)], axis=1)
        o_ref[b:b + 1, :] = jnp.sum(tot, axis=0, keepdims=True) + b_ref[...]


def _ada(c, w_ada, b_ada):
    batch = c.shape[0]
    assert batch == 2
    n = w_ada.shape[1]
    tn = 1536
    return pl.pallas_call(
        _ada_kernel,
        out_shape=jax.ShapeDtypeStruct((batch, n), F32),
        grid=(n // tn,),
        in_specs=[pl.BlockSpec((D_MODEL, batch), lambda j: (0, 0)),
                  pl.BlockSpec((D_MODEL, tn), lambda j: (0, j)),
                  pl.BlockSpec((1, tn), lambda j: (0, j))],
        out_specs=pl.BlockSpec((batch, tn), lambda j: (0, j)),
        scratch_shapes=[pltpu.VMEM((batch, D_MODEL, LANES), F32)],
        compiler_params=_cparams(("arbitrary",)),
        name="ada",
    )(c.T, w_ada, b_ada.reshape(1, n))


def _lnmod_kernel(x_ref, ada_ref, o_ref):
    xn = _layernorm(x_ref[...])
    sh, sc = ada_ref[0, 0:1, :], ada_ref[0, 1:2, :]
    o_ref[...] = (xn * (1.0 + sc) + sh).astype(o_ref.dtype)


def _lnmod(x2, ada3, seq):
    m = x2.shape[0]
    tb = 1024
    per_b = seq // tb
    return pl.pallas_call(
        _lnmod_kernel,
        out_shape=jax.ShapeDtypeStruct((m, D_MODEL), BF16),
        grid=(m // tb,),
        in_specs=[pl.BlockSpec((tb, D_MODEL), lambda i: (i, 0)),
                  pl.BlockSpec((1, 6, D_MODEL), lambda i: (i // per_b, 0, 0))],
        out_specs=pl.BlockSpec((tb, D_MODEL), lambda i: (i, 0)),
        compiler_params=_cparams(("arbitrary",)),
        name="lnmod",
    )(x2, ada3)


def _win_kernel(n_cast, x_ref, wt_ref, *refs):
    cast_in, o_ref, cast_out, wb_ref = refs[:n_cast], refs[n_cast], refs[n_cast + 1:-1], refs[-1]

    @pl.when(pl.program_id(1) == 0)
    def _():
        wb_ref[...] = wt_ref[...].astype(wb_ref.dtype)

    for src, dst in zip(cast_in, cast_out):
        dst[...] = src[...].astype(dst.dtype)
    o_ref[...] = lax.dot_general(x_ref[...], wb_ref[...], NT, preferred_element_type=F32).astype(o_ref.dtype)


def _win_matmul(x, wt, col0, n_out, tm, tn, out_dtype, name, cast_along=()):
    m, k = x.shape
    assert col0 % 8 == 0 and n_out % tn == 0 and col0 + n_out <= wt.shape[0]
    n_m = m // tm
    steps = (n_out // tn) * n_m

    def w_index(j, i):
        return pl.multiple_of(col0 + j * tn, 8), 0

    slabs = []
    for w, nrows in cast_along:
        rows = nrows // steps
        assert rows % 16 == 0 and rows * steps == nrows <= w.shape[0]
        slabs.append(pl.BlockSpec((rows, w.shape[1]), lambda j, i: (j * n_m + i, 0)))
    cast_along = [w for w, _ in cast_along]
    outs = pl.pallas_call(
        functools.partial(_win_kernel, len(cast_along)),
        out_shape=[jax.ShapeDtypeStruct((m, n_out), out_dtype)]
        + [jax.ShapeDtypeStruct((spec.block_shape[0] * steps, w.shape[1]), BF16)
           for w, spec in zip(cast_along, slabs)],
        grid=(n_out // tn, n_m),
        in_specs=[pl.BlockSpec((tm, k), lambda j, i: (i, 0)),
                  pl.BlockSpec((pl.Element(tn), pl.Element(k)), w_index)] + slabs,
        out_specs=[pl.BlockSpec((tm, tn), lambda j, i: (i, j))] + slabs,
        scratch_shapes=[pltpu.VMEM((tn, k), BF16)],
        compiler_params=_cparams(("arbitrary", "arbitrary")),
        name=name,
    )(x, wt, *cast_along)
    return outs if cast_along else outs[0]


def _rwkv_prep_kernel(u_ref, wt_ref, mu_ref, w0_ref, a0_ref, kk_ref, ka_ref, rk_ref, wdu_ref, wau_ref, wgu_ref,
                      r_o, ld_o, kt_o, v_o, kkn_o, bb_o, g_o, bonus_o, carry_ref):
    tb = u_ref.shape[0]

    @pl.when(pl.program_id(1) == 0)
    def _():
        carry_ref[...] = jnp.zeros_like(carry_ref)

    groups = [slice(i, i + EPILOGUE_ROWS) for i in range(0, tb, EPILOGUE_ROWS)]
    zs_all = [lax.dot_general(u_ref[rows, :], wt_ref[...], NT, preferred_element_type=F32) for rows in groups]
    ones_blk = _head_block_ones(LANES, RWKV_HEAD)
    last_row = carry_ref[0:1, :]
    for z, rows in zip(zs_all, groups):
        prev = pltpu.roll(z, 1, 0)
        prev = jnp.where(_iota2(z.shape, 0) == 0, last_row, prev)
        last_row = z[EPILOGUE_ROWS - 1:EPILOGUE_ROWS, :]
        zs = z + (prev - z) * mu_ref[...]

        c = D_RWKV
        r, k, v = zs[:, 0:c], zs[:, c:2 * c], zs[:, 2 * c:3 * c]
        wd = zs[:, 3 * c:3 * c + LORA_PAD]
        ad = zs[:, 3 * c + DECAY_LORA:3 * c + DECAY_LORA + LORA_PAD]
        gd = zs[:, 3 * c + DECAY_LORA + A_LORA:3 * c + DECAY_LORA + A_LORA + GATE_LORA]

        ld = -math.exp(-0.5) * _sigmoid(w0_ref[...] + _mdot(jnp.tanh(wd), wdu_ref[...], 2, 2))
        a = _sigmoid(a0_ref[...] + _mdot(ad, wau_ref[...]))
        g = _mdot(_sigmoid(gd), wgu_ref[...], 1, 1)

        kq = k * kk_ref[...]
        kt = k * (1.0 + (a - 1.0) * ka_ref[...])
        sq = kq * kq
        bn = r * kt * rk_ref[...]
        ss = jnp.concatenate([_head_sum(sq[:, s:s + LANES], ones_blk) for s in range(0, c, LANES)], axis=1)
        bs = jnp.concatenate([_head_sum(bn[:, s:s + LANES], ones_blk) for s in range(0, c, LANES)], axis=1)
        kkn = kq * lax.rsqrt(ss + 1e-12)

        r_o[rows, :] = r.astype(r_o.dtype)
        ld_o[rows, :] = ld
        kt_o[rows, :] = kt.astype(kt_o.dtype)
        v_o[rows, :] = v.astype(v_o.dtype)
        kkn_o[rows, :] = kkn.astype(kkn_o.dtype)
        bb_o[rows, :] = (a * kkn).astype(bb_o.dtype)
        g_o[rows, :] = g.astype(g_o.dtype)
        bonus_o[rows, :] = (bs * v).astype(bonus_o.dtype)
    carry_ref[0:1, :] = last_row


def _rwkv_prep(u, wt_rw, mu_rw, w0, a0, k_k, k_a, r_k, wdu, wau, wgu, batch, seq):
    m, k = u.shape
    assert wt_rw.shape == (RW_COLS, k)
    tb = RWKV_TB
    per_b = seq // tb
    row = lambda b, j: (b * per_b + j, 0)
    const = lambda b, j: (0, 0)
    vec = pl.BlockSpec((1, D_RWKV), const)
    outs = [jax.ShapeDtypeStruct((m, D_RWKV), F32 if i == 1 else BF16) for i in range(8)]
    return pl.pallas_call(
        _rwkv_prep_kernel,
        out_shape=outs,
        grid=(batch, per_b),
        in_specs=[pl.BlockSpec((tb, k), row),
                  pl.BlockSpec((RW_COLS, k), const, pipeline_mode=pl.Buffered(1)),
                  pl.BlockSpec((1, RW_COLS), const),
                  vec, vec, vec, vec, vec,
                  pl.BlockSpec((LORA_PAD, D_RWKV), const), pl.BlockSpec((LORA_PAD, D_RWKV), const),
                  pl.BlockSpec((GATE_LORA, D_RWKV), const)],
        out_specs=[pl.BlockSpec((tb, D_RWKV), row)] * 8,
        scratch_shapes=[pltpu.VMEM((8, RW_COLS), F32)],
        compiler_params=_cparams(("arbitrary", "arbitrary")),
        name="rwkv_prep",
    )(u, wt_rw, mu_rw, w0, a0, k_k, k_a, r_k, wdu, wau, wgu)


def _rwkv_chunk_kernel(r_ref, ld_ref, kt_ref, v_ref, kk_ref, bb_ref, y0_ref, q_ref, a_ref, c_ref):
    tb = r_ref.shape[0]
    lc = RWKV_CHUNK
    n = 2 * lc
    nchunk = tb // lc

    lane = _iota2((1, LANES), 1)
    m0 = jnp.where(lane < RWKV_HEAD, 1.0, 0.0)
    m1 = 1.0 - m0
    ri, ci = _iota2((n, n), 0), _iota2((n, n), 1)
    same_head = (ri // lc) == (ci // lc)
    strict = jnp.where(same_head & (ri > ci), 1.0, 0.0)
    incl = jnp.where(same_head & (ri >= ci), 1.0, 0.0)
    eye = jnp.where(ri == ci, 1.0, 0.0)

    tw = 4 * lc
    rt, ct = _iota2((tw, tw), 0), _iota2((tw, tw), 1)
    tri = jnp.where(((rt // lc) == (ct // lc)) & (ct <= rt), 1.0, 0.0).astype(BF16)

    ld = ld_ref[...]
    cl = jnp.concatenate([_mdot(tri, ld[i:i + tw], 1, 3) for i in range(0, tb, tw)], axis=0)
    e_pos = jnp.exp(cl)
    e_neg = jnp.exp(-cl)
    a_hat = kk_ref[...] * jnp.exp(cl - ld)
    b_hat = bb_ref[...] * e_neg
    k_hat = kt_ref[...] * e_neg
    r_hat = r_ref[...] * e_pos
    v_all = v_ref[...]

    def stack(x):
        return jnp.concatenate([x * m0, x * m1], axis=0)

    def bf(t):
        return t.astype(BF16)

    chunks = range(nchunk)
    st = [[stack(x[c * lc:(c + 1) * lc]) for x in (a_hat, b_hat, k_hat, r_hat, v_all)] for c in chunks]
    a_st, b_st, k_st, r_st, v_st = ([st[c][i] for c in chunks] for i in range(5))
    b_b, k_b, v_b = ([bf(t) for t in ts] for ts in (b_st, k_st, v_st))
    p = [_mdot(jnp.concatenate([a_st[c], r_st[c]], axis=0),
               jnp.concatenate([b_b[c], k_b[c]], axis=0), dims=NT) for c in chunks]
    mab = [p[c][0:n, 0:n] * strict for c in chunks]
    mab_b = [bf(t) for t in mab]
    mak_b = [bf(p[c][0:n, n:] * strict) for c in chunks]
    mrb_b = [bf(p[c][n:, 0:n] * incl) for c in chunks]
    mrk_b = [bf(p[c][n:, n:] * incl) for c in chunks]

    x = [eye - jnp.where((ri // 2 == ci // 2), mab[c], 0.0) for c in chunks]
    w1 = [_mdot(mak_b[c], v_b[c]) for c in chunks]
    zero_b = jnp.zeros((), BF16)
    s = 2
    while s < lc:
        lvl = (ri // (2 * s) == ci // (2 * s)) & (ri // s != ci // s)
        x_b = [bf(t) for t in x]
        nx = [_mdot(jnp.where(lvl, mab_b[c], zero_b), x_b[c]) for c in chunks]
        x = [x[c] - _mdot(x_b[c], nx[c]) for c in chunks]
        s *= 2

    ta = [_mdot(x[c], jnp.concatenate([w1[c], a_st[c]], axis=1)) for c in chunks]
    zeros_b = jnp.zeros((n, LANES), BF16)
    ugv_b = [jnp.concatenate([bf(jnp.concatenate([-ta[c][:, 0:LANES], ta[c][:, LANES:]], axis=1)),
                              jnp.concatenate([v_b[c], zeros_b], axis=1)], axis=0) for c in chunks]
    ru = [_mdot(jnp.concatenate([mrb_b[c], mrk_b[c]], axis=1), ugv_b[c]) for c in chunks]
    y0 = [ru[c][:, 0:LANES] for c in chunks]
    q = [r_st[c] - ru[c][:, LANES:] for c in chunks]
    pl_row = [e_pos[(c + 1) * lc - 1:(c + 1) * lc, :] for c in chunks]
    bk_p = [bf(jnp.concatenate([b_st[c] * pl_row[c], k_st[c] * pl_row[c]], axis=0)) for c in chunks]
    bu = [_mdot(bk_p[c], ugv_b[c], dims=TN) for c in chunks]
    ct = [bu[c][:, 0:LANES] for c in chunks]
    at = [eye * pl_row[c] - bu[c][:, LANES:] for c in chunks]

    for c in chunks:
        rows = slice(c * lc, (c + 1) * lc)
        y0_ref[rows, :] = y0[c][0:lc] + y0[c][lc:]
        q_ref[rows, :] = (q[c][0:lc] + q[c][lc:]).astype(q_ref.dtype)
        a_ref[0, c, 0] = (at[c][0:RWKV_HEAD] + at[c][RWKV_HEAD:]).astype(a_ref.dtype)
        c_ref[0, c, 0] = ct[c][0:RWKV_HEAD] + ct[c][RWKV_HEAD:]


def _rwkv_chunk(r, ld, kt, v, kkn, bb, batch, seq):
    m = r.shape[0]
    tb = RWKV_CORE_TB
    per_b = seq // tb
    cpb = tb // RWKV_CHUNK
    npair = D_RWKV // LANES
    slab = pl.BlockSpec((tb, LANES), lambda b, p, j: (b * per_b + j, p))
    mat = pl.BlockSpec((1, cpb, 1, RWKV_HEAD, LANES), lambda b, p, j: (b, j, p, 0, 0))
    mshape = (batch, seq // RWKV_CHUNK, npair, RWKV_HEAD, LANES)
    return pl.pallas_call(
        _rwkv_chunk_kernel,
        out_shape=[jax.ShapeDtypeStruct((m, D_RWKV), F32), jax.ShapeDtypeStruct((m, D_RWKV), BF16),
                   jax.ShapeDtypeStruct(mshape, BF16), jax.ShapeDtypeStruct(mshape, F32)],
        grid=(batch, npair, per_b),
        in_specs=[slab] * 6,
        out_specs=[slab, slab, mat, mat],
        compiler_params=_cparams(("arbitrary", "arbitrary", "arbitrary")),
        name="rwkv_chunk",
    )(r, ld, kt, v, kkn, bb)


def _rwkv_scan_kernel(y0_ref, q_ref, a_ref, c_ref, g_ref, bonus_ref, gng_ref, gnb_ref, ya_ref, s_ref):
    nb, rows, _ = y0_ref.shape
    lc = RWKV_CHUNK
    npair = D_RWKV // LANES

    @pl.when(pl.program_id(0) == 0)
    def _():
        s_ref[...] = jnp.zeros_like(s_ref)

    ones_blk = _head_block_ones(LANES, RWKV_HEAD)
    head0 = _iota2((RWKV_HEAD, LANES), 1) < RWKV_HEAD

    def block_diag(t):
        zero = jnp.zeros((), t.dtype)
        return jnp.concatenate([jnp.where(head0, t, zero), jnp.where(head0, zero, t)], axis=0)

    streams = [(b, p) for b in range(nb) for p in range(npair)]
    for ci in range(rows // lc):
        rs = slice(ci * lc, (ci + 1) * lc)
        state_b = [s_ref[b * npair + p].astype(BF16) for b, p in streams]
        y = [y0_ref[b, rs, p * LANES:(p + 1) * LANES]
             + _mdot(q_ref[b, rs, p * LANES:(p + 1) * LANES], state_b[i])
             for i, (b, p) in enumerate(streams)]
        for i, (b, p) in enumerate(streams):
            s_ref[b * npair + p] = (_mdot(block_diag(a_ref[b, ci, p]), state_b[i])
                                    + block_diag(c_ref[b, ci, p]))
        mu = [_head_sum(t, ones_blk) * (1.0 / RWKV_HEAD) for t in y]
        yc = [y[i] - mu[i] for i in range(len(streams))]
        var = [_head_sum(t * t, ones_blk) * (1.0 / RWKV_HEAD) for t in yc]
        for i, (b, p) in enumerate(streams):
            ls = slice(p * LANES, (p + 1) * LANES)
            yn = yc[i] * lax.rsqrt(var[i] + RWKV_GN_EPS) * gng_ref[:, ls] + gnb_ref[:, ls]
            ya_ref[b, rs, ls] = ((yn + bonus_ref[b, rs, ls]) * g_ref[b, rs, ls]).astype(ya_ref.dtype)


def _rwkv_scan(y0, q, a_mat, c_mat, g, bonus, gn_g, gn_b, batch, seq):
    rows = RWKV_SCAN_ROWS
    npair = D_RWKV // LANES
    tok = pl.BlockSpec((batch, rows, D_RWKV), lambda j: (0, j, 0))
    mat = pl.BlockSpec((batch, rows // RWKV_CHUNK, npair, RWKV_HEAD, LANES), lambda j: (0, j, 0, 0, 0))
    vec = pl.BlockSpec((1, D_RWKV), lambda j: (0, 0))
    shape3 = (batch, seq, D_RWKV)
    return pl.pallas_call(
        _rwkv_scan_kernel,
        out_shape=jax.ShapeDtypeStruct(shape3, BF16),
        grid=(seq // rows,),
        in_specs=[tok, tok, mat, mat, tok, tok, vec, vec],
        out_specs=tok,
        scratch_shapes=[pltpu.VMEM((batch * npair, LANES, LANES), F32)],
        compiler_params=_cparams(("arbitrary",)),
        name="rwkv_scan",
    )(y0.reshape(shape3), q.reshape(shape3), a_mat, c_mat, g.reshape(shape3), bonus.reshape(shape3),
      gn_g, gn_b).reshape(batch * seq, D_RWKV)


def _mlstm_kernel(u_ref, wa_ref, wo_ref, wconv_ref, bconv_ref, gbias_ref, yb_ref,
                  tail_ref, ct_ref, n_ref, m_ref, wab_ref, wob_ref):
    lc = u_ref.shape[0]
    dqk, dv, nh = MLSTM_DQK, MLSTM_DV, MLSTM_HEADS

    @pl.when(pl.program_id(1) == 0)
    def _():
        tail_ref[...] = jnp.zeros_like(tail_ref)
        ct_ref[...] = jnp.zeros_like(ct_ref)
        n_ref[...] = jnp.zeros_like(n_ref)
        m_ref[...] = jnp.zeros_like(m_ref)

    @pl.when((pl.program_id(0) == 0) & (pl.program_id(1) == 0))
    def _():
        wab_ref[...] = wa_ref[...].astype(wab_ref.dtype)
        wob_ref[...] = wo_ref[...].astype(wob_ref.dtype)

    u = u_ref[...]
    za = lax.dot_general(u, wab_ref[...], NT, preferred_element_type=F32)
    o_all = lax.dot_general(u, wob_ref[...], NT, preferred_element_type=F32)

    zqk = za[:, 0:2 * D_QK]
    ext = jnp.concatenate([tail_ref[...], zqk], axis=0)
    tail_ref[...] = zqk[lc - 8:lc, :]
    conv = bconv_ref[...] + wconv_ref[CONV_K - 1:CONV_K, :] * zqk
    for d in range(1, CONV_K):
        conv = conv + wconv_ref[CONV_K - 1 - d:CONV_K - d, :] * ext[8 - d:8 - d + lc, :]
    qk = conv * _sigmoid(conv)
    q_all = qk[:, 0:D_QK]
    k_all = qk[:, D_QK:] * (dqk ** -0.5)
    v_all = za[:, 2 * D_QK:2 * D_QK + D_MLSTM]

    gz = za[:, 2 * D_QK + D_MLSTM:] + gbias_ref[...]
    lf_col = jnp.minimum(gz, 0.0) - jnp.log(1.0 + jnp.exp(-jnp.abs(gz)))
    rr, cc = _iota2((lc, lc), 0), _iota2((lc, lc), 1)
    causal = cc <= rr
    tri = jnp.where(causal, 1.0, 0.0).astype(BF16)
    b_col = _mdot(tri, lf_col, 1, 3)
    sel = jnp.where(_iota2((8, LANES), 0) == _iota2((8, LANES), 1), 1.0, 0.0).astype(BF16)
    g_row = _mdot(sel, gz, 1, 3, NT)
    lf_row = jnp.minimum(g_row, 0.0) - jnp.log(1.0 + jnp.exp(-jnp.abs(g_row)))
    b_row = _mdot(lf_row, tri, 3, 1, NT)

    heads = range(nh)
    qf = [q_all[:, h * dqk:(h + 1) * dqk] for h in heads]
    q = [t.astype(BF16) for t in qf]
    k = [k_all[:, h * dqk:(h + 1) * dqk] for h in heads]
    v = [v_all[:, h * dv:(h + 1) * dv].astype(BF16) for h in heads]
    bc = [b_col[:, nh + h:nh + h + 1] for h in heads]
    ic = [gz[:, h:h + 1] for h in heads]
    br = [b_row[nh + h:nh + h + 1, :] for h in heads]
    ir = [g_row[h:h + 1, :] for h in heads]
    g_tot = [b_col[lc - 1:lc, nh + h:nh + h + 1] for h in heads]
    m_prev = [m_ref[0:1, h:h + 1] for h in heads]
    ct = [ct_ref[h] for h in heads]
    nvec = [n_ref[h:h + 1, :] for h in heads]

    qk = [_mdot(q[h], k[h], dims=NT) for h in heads]
    qc = [_mdot(q[h], ct[h]) for h in heads]
    bri = [br[h] - ir[h] for h in heads]
    dm = [jnp.where(causal, bc[h] - bri[h], -jnp.inf) for h in heads]
    inter = [bc[h] + m_prev[h] for h in heads]
    mt = [jnp.maximum(inter[h], jnp.max(dm[h], axis=-1, keepdims=True)) for h in heads]
    sc = [qk[h] * jnp.exp(dm[h] - mt[h]) for h in heads]
    winter = [jnp.exp(inter[h] - mt[h]) for h in heads]
    sv = [_mdot(sc[h], v[h]) for h in heads]
    m_new = [jnp.maximum(g_tot[h] + m_prev[h], jnp.max(g_tot[h] - bri[h], axis=-1, keepdims=True))
             for h in heads]
    kw = [k[h] * jnp.exp(g_tot[h] - bc[h] + ic[h] - m_new[h]) for h in heads]
    kv = [_mdot(kw[h], v[h], dims=TN) for h in heads]
    for h in heads:
        num = winter[h] * qc[h] + sv[h]
        den = (winter[h] * jnp.sum(qf[h] * nvec[h], axis=-1, keepdims=True)
               + jnp.sum(sc[h], axis=-1, keepdims=True))
        hout = num / jnp.maximum(jnp.abs(den), jnp.exp(-mt[h]))
        wc = jnp.exp(g_tot[h] + m_prev[h] - m_new[h])
        ct_ref[h] = wc * ct[h] + kv[h]
        n_ref[h:h + 1, :] = wc * nvec[h] + jnp.sum(kw[h], axis=0, keepdims=True)
        m_ref[0:1, h:h + 1] = m_new[h]
        o = o_all[:, h * dv:(h + 1) * dv]
        yb_ref[:, h * dv:(h + 1) * dv] = (_sigmoid(o) * hout).astype(yb_ref.dtype)


def _mlstm(u, w_in_t, col0, w_conv, b_conv, gbias, batch, seq):
    m, k = u.shape
    lc = MLSTM_CHUNK
    per_b = seq // lc
    const = lambda b, j: (0, 0)
    rows_a = 2 * D_QK + D_MLSTM + ML_GATE_PAD
    col_o = col0 + 2 * D_QK + D_MLSTM + 2 * MLSTM_HEADS
    assert col0 % 8 == 0 and col_o % 8 == 0 and col_o + D_MLSTM <= w_in_t.shape[0]
    resident = lambda rows, r0: pl.BlockSpec((pl.Element(rows), pl.Element(k)), lambda b, j: (r0, 0),
                                             pipeline_mode=pl.Buffered(1))
    return pl.pallas_call(
        _mlstm_kernel,
        out_shape=jax.ShapeDtypeStruct((m, D_MLSTM), BF16),
        grid=(batch, per_b),
        in_specs=[pl.BlockSpec((lc, k), lambda b, j: (b * per_b + j, 0)),
                  resident(rows_a, col0), resident(D_MLSTM, col_o),
                  pl.BlockSpec((CONV_K, 2 * D_QK), const),
                  pl.BlockSpec((1, 2 * D_QK), const),
                  pl.BlockSpec((1, ML_GATE_PAD), const)],
        out_specs=pl.BlockSpec((lc, D_MLSTM), lambda b, j: (b * per_b + j, 0)),
        scratch_shapes=[pltpu.VMEM((8, 2 * D_QK), F32),
                        pltpu.VMEM((MLSTM_HEADS, MLSTM_DQK, MLSTM_DV), F32),
                        pltpu.VMEM((8, MLSTM_DQK), F32),
                        pltpu.VMEM((8, LANES), F32),
                        pltpu.VMEM((rows_a, k), BF16),
                        pltpu.VMEM((D_MLSTM, k), BF16)],
        compiler_params=_cparams(("arbitrary", "arbitrary")),
        name="mlstm",
    )(u, w_in_t, w_in_t, w_conv, b_conv, gbias)


def _merge_kernel(ya_ref, yb_ref, zg_ref, wa_ref, wb_ref, o_ref):
    groups = [slice(i, i + EPILOGUE_ROWS) for i in range(0, ya_ref.shape[0], EPILOGUE_ROWS)]
    pab = [(jnp.dot(ya_ref[rows, :], wa_ref[...], preferred_element_type=F32),
            jnp.dot(yb_ref[rows, :], wb_ref[...], preferred_element_type=F32)) for rows in groups]
    for (pa, pb), rows in zip(pab, groups):
        ga = _sigmoid(zg_ref[rows, 0:D_MODEL].astype(F32))
        gb = _sigmoid(zg_ref[rows, D_MODEL:].astype(F32))
        o_ref[rows, :] = (ga * pa + gb * pb).astype(o_ref.dtype)


def _merge(ya, yb, zg, wa, wb):
    m = ya.shape[0]
    tm = 512
    const = lambda i: (0, 0)
    return pl.pallas_call(
        _merge_kernel,
        out_shape=jax.ShapeDtypeStruct((m, D_MODEL), BF16),
        grid=(m // tm,),
        in_specs=[pl.BlockSpec((tm, D_RWKV), lambda i: (i, 0)),
                  pl.BlockSpec((tm, D_MLSTM), lambda i: (i, 0)),
                  pl.BlockSpec((tm, 2 * D_MODEL), lambda i: (i, 0)),
                  pl.BlockSpec((D_RWKV, D_MODEL), const),
                  pl.BlockSpec((D_MLSTM, D_MODEL), const)],
        out_specs=pl.BlockSpec((tm, D_MODEL), lambda i: (i, 0)),
        compiler_params=_cparams(("arbitrary",)),
        name="merge",
    )(ya, yb, zg, wa, wb)


def _zero_like_dep(x, dtype):
    bits = lax.shift_right_logical(lax.shift_right_logical(pltpu.bitcast(x, jnp.int32), 16), 16)
    return bits.astype(F32).astype(dtype)


def _outproj_kernel(mg_ref, w_ref, x_ref, ada_ref, g_ref, b_ref, x1_ref, u2_ref, ya_ref, yb_ref):
    step = pl.program_id(0)

    @pl.when(step == 0)
    def _():
        yb_ref[...] = jnp.zeros_like(yb_ref)

    def run(y_new_ref, y_old_ref):
        tm, n = y_new_ref.shape
        parts = 4
        cw, rh = n // parts, tm // parts
        g1 = ada_ref[0, 2:3, :]
        sh2, sc2 = ada_ref[0, 3:4, :], ada_ref[0, 4:5, :]
        deps = []
        for j in range(parts):
            cols, rows = slice(j * cw, (j + 1) * cw), slice(j * rh, (j + 1) * rh)
            x1 = _layernorm(ALPHA * x_ref[rows, :] + g1 * y_old_ref[rows, :]) * g_ref[...] + b_ref[...]
            x1_ref[rows, :] = x1
            u2 = (_layernorm(x1) * (1.0 + sc2) + sh2).astype(u2_ref.dtype)
            u2_ref[rows, :] = u2
            colsum = jnp.sum(x1 + u2.astype(F32), axis=0, keepdims=True)
            deps.append(_zero_like_dep(colsum, mg_ref.dtype))
            lhs = mg_ref[...]
            if j >= 1:
                lhs = jnp.concatenate([lhs[0:16, :] + deps[j - 1], lhs[16:, :]], axis=0)
            y_new_ref[:, cols] = jnp.dot(lhs, w_ref[:, cols], preferred_element_type=F32)

    @pl.when(step % 2 == 0)
    def _():
        run(ya_ref, yb_ref)

    @pl.when(step % 2 == 1)
    def _():
        run(yb_ref, ya_ref)


def _outproj(merged, w_out, x2, ada3, ln_g, ln_b, seq):
    m = merged.shape[0]
    tm = 512
    n_tiles = m // tm
    per_b = seq // tm
    const = lambda i: (0, 0)
    prev = lambda i: (jnp.maximum(i - 1, 0), 0)
    row_prev = pl.BlockSpec((tm, D_MODEL), prev)
    return pl.pallas_call(
        _outproj_kernel,
        out_shape=[jax.ShapeDtypeStruct((m, D_MODEL), F32), jax.ShapeDtypeStruct((m, D_MODEL), BF16)],
        grid=(n_tiles + 1,),
        in_specs=[pl.BlockSpec((tm, D_MODEL), lambda i: (jnp.minimum(i, n_tiles - 1), 0)),
                  pl.BlockSpec((D_MODEL, D_MODEL), const, pipeline_mode=pl.Buffered(1)), row_prev,
                  pl.BlockSpec((1, 6, D_MODEL), lambda i: (jnp.maximum(i - 1, 0) // per_b, 0, 0)),
                  pl.BlockSpec((1, D_MODEL), const), pl.BlockSpec((1, D_MODEL), const)],
        out_specs=[row_prev, row_prev],
        scratch_shapes=[pltpu.VMEM((tm, D_MODEL), F32), pltpu.VMEM((tm, D_MODEL), F32)],
        compiler_params=_cparams(("arbitrary",)),
        name="outproj",
    )(merged, w_out, x2, ada3, ln_g, ln_b)


def _ff1_kernel(x_ref, w_ref, b_ref, w2_ref, o_ref, w2b_ref, wb_ref):
    @pl.when(pl.program_id(1) == 0)
    def _():
        wb_ref[...] = w_ref[...].astype(wb_ref.dtype)

    w2b_ref[...] = w2_ref[...].astype(w2b_ref.dtype)
    groups = [slice(i, i + 2 * EPILOGUE_ROWS) for i in range(0, x_ref.shape[0], 2 * EPILOGUE_ROWS)]
    hs = [jnp.dot(x_ref[rows, :], wb_ref[...], preferred_element_type=F32) for rows in groups]
    for h, rows in zip(hs, groups):
        h = jnp.maximum(h + b_ref[...], 0.0)
        o_ref[rows, :] = (h * h).astype(o_ref.dtype)


def _ff1(u2, w1, b1, w2):
    m, k = u2.shape
    n = w1.shape[1]
    tm, tn = 1024, 1024
    n_m = m // tm
    rows2 = w2.shape[0] // ((n // tn) * n_m)
    assert rows2 % 16 == 0 and rows2 * (n // tn) * n_m == w2.shape[0]
    slab = pl.BlockSpec((rows2, w2.shape[1]), lambda j, i: (j * n_m + i, 0))
    return pl.pallas_call(
        _ff1_kernel,
        out_shape=[jax.ShapeDtypeStruct((m, n), BF16), jax.ShapeDtypeStruct(w2.shape, BF16)],
        grid=(n // tn, n_m),
        in_specs=[pl.BlockSpec((tm, k), lambda j, i: (i, 0)),
                  pl.BlockSpec((k, tn), lambda j, i: (0, j)),
                  pl.BlockSpec((1, tn), lambda j, i: (0, j)),
                  slab],
        out_specs=[pl.BlockSpec((tm, tn), lambda j, i: (i, j)), slab],
        scratch_shapes=[pltpu.VMEM((k, tn), BF16)],
        compiler_params=_cparams(("arbitrary", "arbitrary")),
        name="ff1",
    )(u2, w1, b1, w2)


def _ff2_kernel(h_ref, w_ref, b_ref, x1_ref, ada_ref, g_ref, bb_ref, o_ref):
    y2 = jnp.dot(h_ref[...], w_ref[...], preferred_element_type=F32) + b_ref[...]
    g2 = ada_ref[0, 5:6, :]
    o_ref[...] = (_layernorm(ALPHA * x1_ref[...] + g2 * y2) * g_ref[...] + bb_ref[...]).astype(o_ref.dtype)


def _ff2(h, w2, b2, x1, ada3, ln_g, ln_b, seq, out_dtype):
    m, k = h.shape
    tm = 256
    per_b = seq // tm
    const = lambda i: (0, 0)
    row = pl.BlockSpec((tm, D_MODEL), lambda i: (i, 0))
    return pl.pallas_call(
        _ff2_kernel,
        out_shape=jax.ShapeDtypeStruct((m, D_MODEL), out_dtype),
        grid=(m // tm,),
        in_specs=[pl.BlockSpec((tm, k), lambda i: (i, 0)),
                  pl.BlockSpec((k, D_MODEL), const, pipeline_mode=pl.Buffered(1)),
                  pl.BlockSpec((1, D_MODEL), const), row,
                  pl.BlockSpec((1, 6, D_MODEL), lambda i: (i // per_b, 0, 0)),
                  pl.BlockSpec((1, D_MODEL), const), pl.BlockSpec((1, D_MODEL), const)],
        out_specs=row,
        compiler_params=_cparams(("arbitrary",)),
        name="ff2",
    )(h, w2, b2, x1, ada3, ln_g, ln_b)


def _pad_rows(w, height):
    return jnp.pad(w, ((0, height - w.shape[0]), (0, 0)))


def _layer(x2, c, batch, seq, w_ada, b_ada, w_in, mu_shift, w0, w_decay_up, a0, w_a_up, w_g_up, k_k, k_a, r_k,
           gn_g, gn_b, w_conv, b_conv, b_igate, b_fgate, w_branch_a, w_branch_b, w_out,
           ln1_g, ln1_b, w_ff1, b_ff1, w_ff2, b_ff2, ln2_g, ln2_b):
    row = lambda v: v.reshape(1, -1)

    ada3 = _ada(c, w_ada, b_ada).reshape(batch, 6, D_MODEL)
    u = _lnmod(x2, ada3, seq)

    assert w_in.shape == (D_MODEL, N_RWKV_COLS + N_MLSTM_COLS + 2 * D_MODEL)
    w_in_t = jnp.swapaxes(w_in, 0, 1)
    z_gt, w_a_b, w_b_b, w_out_b, wt_rw_b = _win_matmul(
        u, w_in_t, N_RWKV_COLS + N_MLSTM_COLS, 2 * D_MODEL, 1024, 1024, BF16, "win_gate",
        cast_along=((w_branch_a, D_RWKV), (w_branch_b, D_MLSTM), (w_out, D_MODEL), (w_in_t, RW_COLS)))

    mu_rw = jnp.pad(mu_shift, (0, RW_COLS - N_RWKV_COLS))
    prep = _rwkv_prep(u, wt_rw_b, row(mu_rw), row(w0), row(a0), row(k_k), row(k_a), row(r_k),
                      _pad_rows(w_decay_up, LORA_PAD), _pad_rows(w_a_up, LORA_PAD), w_g_up, batch, seq)
    r, ld, kt, v, kkn, bb, g, bonus = prep
    y0, q, a_mat, c_mat = _rwkv_chunk(r, ld, kt, v, kkn, bb, batch, seq)
    ya = _rwkv_scan(y0, q, a_mat, c_mat, g, bonus, row(gn_g), row(gn_b), batch, seq)

    gbias = jnp.pad(jnp.concatenate([b_igate, b_fgate]), (0, ML_GATE_PAD - 2 * MLSTM_HEADS))
    yb = _mlstm(u, w_in_t, N_RWKV_COLS, w_conv, row(b_conv), row(gbias), batch, seq)

    merged = _merge(ya, yb, z_gt, w_a_b, w_b_b)
    x1, u2 = _outproj(merged, w_out_b, x2, ada3, row(ln1_g), row(ln1_b), seq)

    h, w_ff2_b = _ff1(u2, w_ff1, row(b_ff1), w_ff2)
    return _ff2(h, w_ff2_b, row(b_ff2), x1, ada3, row(ln2_g), row(ln2_b), seq, F32)


def kernel(x, c, w_ada, b_ada, w_in, mu_shift, w0, w_decay_up, a0, w_a_up, w_g_up, k_k, k_a, r_k, gn_g, gn_b,
           w_conv, b_conv, b_igate, b_fgate, w_branch_a, w_branch_b, w_out, ln1_g, ln1_b, w_ff1, b_ff1,
           w_ff2, b_ff2, ln2_g, ln2_b):
    out_dtype = x.dtype
    batch, seq, d = x.shape
    assert d == D_MODEL and seq % 1024 == 0 and w_ada.shape[0] == DEPTH
    x2 = x.astype(F32).reshape(batch * seq, d)
    cf = c.astype(F32)
    for l in range(DEPTH):
        x2 = _layer(x2, cf, batch, seq, w_ada[l], b_ada[l], w_in[l], mu_shift[l], w0[l], w_decay_up[l], a0[l],
                    w_a_up[l], w_g_up[l], k_k[l], k_a[l], r_k[l].reshape(-1), gn_g[l], gn_b[l], w_conv[l],
                    b_conv[l], b_igate[l], b_fgate[l], w_branch_a[l], w_branch_b[l], w_out[l], ln1_g[l],
                    ln1_b[l], w_ff1[l], b_ff1[l], w_ff2[l], b_ff2[l], ln2_g[l], ln2_b[l])
    return x2.reshape(batch, seq, d).astype(out_dtype)
```

```python
import functools
import math

import jax
import jax.numpy as jnp
from jax import lax
from jax.experimental import pallas as pl
from jax.experimental.pallas import tpu as pltpu

F32 = jnp.float32
BF16 = jnp.bfloat16

D_MODEL = 2048
DEPTH = 1
D_RWKV = D_MODEL // 2
RWKV_HEAD = 64
DECAY_LORA = 96
A_LORA = 96
GATE_LORA = 256
RWKV_GN_EPS = 64e-5
D_MLSTM = D_MODEL // 2
MLSTM_HEADS = 4
MLSTM_DV = D_MLSTM // MLSTM_HEADS
MLSTM_DQK = MLSTM_DV // 2
D_QK = MLSTM_HEADS * MLSTM_DQK
CONV_K = 4
D_FF = 4 * D_MODEL
LN_EPS = 1e-5
ALPHA = (2.0 * DEPTH) ** 0.25

LANES = 128
LORA_PAD = 128
N_RWKV_COLS = 3 * D_RWKV + DECAY_LORA + A_LORA + GATE_LORA
N_MLSTM_COLS = 2 * D_QK + 2 * D_MLSTM + 2 * MLSTM_HEADS
RW_COLS = 3584
ML_GATE_PAD = 128
ML_ROWS_A = 2560
RWKV_CHUNK = 64
RWKV_TB = 512
RWKV_SCAN_ROWS = 256
RWKV_CORE_TB = 1024
MLSTM_CHUNK = 256
MLSTM_TB = 512
EPILOGUE_ROWS = 256
VMEM_LIMIT = 56 * 1024 * 1024

NN = (((1,), (0,)), ((), ()))
NT = (((1,), (1,)), ((), ()))
TN = (((0,), (0,)), ((), ()))


def _cparams(sem):
    return pltpu.CompilerParams(dimension_semantics=sem, vmem_limit_bytes=VMEM_LIMIT)


def _bf16_parts(x, n):
    if x.dtype == BF16:
        return [x]
    parts, rem = [], x
    for i in range(n):
        p = rem.astype(BF16)
        parts.append(p)
        if i + 1 < n:
            rem = rem - p.astype(F32)
    return parts


def _mdot(a, b, na=1, nb=1, dims=NN):
    ap, bp = _bf16_parts(a, na), _bf16_parts(b, nb)
    order = max(len(ap), len(bp))
    acc = None
    for i, x in enumerate(ap):
        for j, y in enumerate(bp):
            if i + j < order:
                t = lax.dot_general(x, y, dims, preferred_element_type=F32)
                acc = t if acc is None else acc + t
    return acc


def _sigmoid(x):
    return 1.0 / (1.0 + jnp.exp(-x))


def _layernorm(x):
    mu = jnp.mean(x, axis=-1, keepdims=True)
    xc = x - mu
    var = jnp.mean(xc * xc, axis=-1, keepdims=True)
    return xc * lax.rsqrt(var + LN_EPS)


def _iota2(shape, dim):
    return lax.broadcasted_iota(jnp.int32, shape, dim)


def _head_block_ones(n, head):
    r, c = _iota2((n, n), 0), _iota2((n, n), 1)
    return jnp.where((r // head) == (c // head), 1.0, 0.0).astype(BF16)


def _head_sum(x, ones_blk):
    return _mdot(x, ones_blk)


def _ada_kernel(ct_ref, w_ref, b_ref, o_ref, sb_ref):
    tn = w_ref.shape[1]
    nb = sb_ref.shape[0]

    @pl.when(pl.program_id(0) == 0)
    def _():
        cv = ct_ref[...]
        sv = cv * _sigmoid(cv)
        for b in range(nb):
            sb_ref[b] = jnp.broadcast_to(sv[:, b:b + 1], sb_ref.shape[1:])

    ntile = tn // LANES

    def body(i, acc):
        r0 = pl.multiple_of(i * 8, 8)
        sb = [sb_ref[b, pl.ds(r0, 8), :] for b in range(nb)]
        wv = [w_ref[pl.ds(r0, 8), t * LANES:(t + 1) * LANES] for t in range(ntile)]
        return tuple(acc[b * ntile + t] + sb[b] * wv[t] for b in range(nb) for t in range(ntile))

    zero = jnp.zeros((8, LANES), F32)
    acc = lax.fori_loop(0, w_ref.shape[0] // 8, body, (zero,) * (nb * ntile), unroll=4)
    for b in range(nb):
        tot = jnp.concatenate([acc[b * ntile + t] for t in range(ntile)], axis=1)
        o_ref[b:b + 1, :] = jnp.sum(tot, axis=0, keepdims=True) + b_ref[...]


def _ada(c, w_ada, b_ada):
    batch = c.shape[0]
    assert batch == 2
    n = w_ada.shape[1]
    tn = 1536
    return pl.pallas_call(
        _ada_kernel,
        out_shape=jax.ShapeDtypeStruct((batch, n), F32),
        grid=(n // tn,),
        in_specs=[pl.BlockSpec((D_MODEL, batch), lambda j: (0, 0)),
                  pl.BlockSpec((D_MODEL, tn), lambda j: (0, j)),
                  pl.BlockSpec((1, tn), lambda j: (0, j))],
        out_specs=pl.BlockSpec((batch, tn), lambda j: (0, j)),
        scratch_shapes=[pltpu.VMEM((batch, D_MODEL, LANES), F32)],
        compiler_params=_cparams(("arbitrary",)),
        name="ada",
    )(c.T, w_ada, b_ada.reshape(1, n))


def _lnmod_kernel(x_ref, ada_ref, o_ref):
    xn = _layernorm(x_ref[...])
    sh, sc = ada_ref[0, 0:1, :], ada_ref[0, 1:2, :]
    o_ref[...] = (xn * (1.0 + sc) + sh).astype(o_ref.dtype)


def _lnmod(x2, ada3, seq):
    m = x2.shape[0]
    tb = 1024
    per_b = seq // tb
    return pl.pallas_call(
        _lnmod_kernel,
        out_shape=jax.ShapeDtypeStruct((m, D_MODEL), BF16),
        grid=(m // tb,),
        in_specs=[pl.BlockSpec((tb, D_MODEL), lambda i: (i, 0)),
                  pl.BlockSpec((1, 6, D_MODEL), lambda i: (i // per_b, 0, 0))],
        out_specs=pl.BlockSpec((tb, D_MODEL), lambda i: (i, 0)),
        compiler_params=_cparams(("arbitrary",)),
        name="lnmod",
    )(x2, ada3)


def _win_kernel(n_cast, x_ref, wt_ref, *refs):
    cast_in, o_ref, cast_out, wb_ref = refs[:n_cast], refs[n_cast], refs[n_cast + 1:-1], refs[-1]

    @pl.when(pl.program_id(1) == 0)
    def _():
        wb_ref[...] = wt_ref[...].astype(wb_ref.dtype)

    for src, dst in zip(cast_in, cast_out):
        dst[...] = src[...].astype(dst.dtype)
    o_ref[...] = lax.dot_general(x_ref[...], wb_ref[...], NT, preferred_element_type=F32).astype(o_ref.dtype)


def _win_matmul(x, wt, col0, n_out, tm, tn, out_dtype, name, cast_along=()):
    m, k = x.shape
    assert col0 % 8 == 0 and n_out % tn == 0 and col0 + n_out <= wt.shape[0]
    n_m = m // tm
    steps = (n_out // tn) * n_m

    def w_index(j, i):
        return pl.multiple_of(col0 + j * tn, 8), 0

    slabs_in, slabs_out, out_shapes = [], [], []
    for w, row0, nrows in cast_along:
        rows = nrows // steps
        assert rows % 16 == 0 and row0 % 8 == 0 and rows * steps == nrows and row0 + nrows <= w.shape[0]
        slabs_in.append(pl.BlockSpec(
            (pl.Element(rows), pl.Element(w.shape[1])),
            functools.partial(lambda j, i, row0, rows: (pl.multiple_of(row0 + (j * n_m + i) * rows, 8), 0),
                              row0=row0, rows=rows)))
        slabs_out.append(pl.BlockSpec((rows, w.shape[1]), lambda j, i: (j * n_m + i, 0)))
        out_shapes.append(jax.ShapeDtypeStruct((nrows, w.shape[1]), BF16))
    cast_along = [w for w, _, _ in cast_along]
    outs = pl.pallas_call(
        functools.partial(_win_kernel, len(cast_along)),
        out_shape=[jax.ShapeDtypeStruct((m, n_out), out_dtype)] + out_shapes,
        grid=(n_out // tn, n_m),
        in_specs=[pl.BlockSpec((tm, k), lambda j, i: (i, 0)),
                  pl.BlockSpec((pl.Element(tn), pl.Element(k)), w_index)] + slabs_in,
        out_specs=[pl.BlockSpec((tm, tn), lambda j, i: (i, j))] + slabs_out,
        scratch_shapes=[pltpu.VMEM((tn, k), BF16)],
        compiler_params=_cparams(("arbitrary", "arbitrary")),
        name=name,
    )(x, wt, *cast_along)
    return outs if cast_along else outs[0]


def _rwkv_prep_kernel(u_ref, wt_ref, mu_ref, w0_ref, a0_ref, kk_ref, ka_ref, rk_ref, wdu_ref, wau_ref, wgu_ref,
                      r_o, ld_o, kt_o, v_o, kkn_o, bb_o, g_o, bonus_o, carry_ref):
    tb = u_ref.shape[0]

    @pl.when(pl.program_id(1) == 0)
    def _():
        carry_ref[...] = jnp.zeros_like(carry_ref)

    groups = [slice(i, i + EPILOGUE_ROWS) for i in range(0, tb, EPILOGUE_ROWS)]
    zs_all = [lax.dot_general(u_ref[rows, :], wt_ref[...], NT, preferred_element_type=F32) for rows in groups]
    ones_blk = _head_block_ones(LANES, RWKV_HEAD)
    last_row = carry_ref[0:1, :]
    for z, rows in zip(zs_all, groups):
        prev = pltpu.roll(z, 1, 0)
        prev = jnp.where(_iota2(z.shape, 0) == 0, last_row, prev)
        last_row = z[EPILOGUE_ROWS - 1:EPILOGUE_ROWS, :]
        zs = z + (prev - z) * mu_ref[...]

        c = D_RWKV
        r, k, v = zs[:, 0:c], zs[:, c:2 * c], zs[:, 2 * c:3 * c]
        wd = zs[:, 3 * c:3 * c + LORA_PAD]
        ad = zs[:, 3 * c + DECAY_LORA:3 * c + DECAY_LORA + LORA_PAD]
        gd = zs[:, 3 * c + DECAY_LORA + A_LORA:3 * c + DECAY_LORA + A_LORA + GATE_LORA]

        ld = -math.exp(-0.5) * _sigmoid(w0_ref[...] + _mdot(jnp.tanh(wd), wdu_ref[...], 2, 2))
        a = _sigmoid(a0_ref[...] + _mdot(ad, wau_ref[...]))
        g = _mdot(_sigmoid(gd), wgu_ref[...], 1, 1)

        kq = k * kk_ref[...]
        kt = k * (1.0 + (a - 1.0) * ka_ref[...])
        sq = kq * kq
        bn = r * kt * rk_ref[...]
        ss = jnp.concatenate([_head_sum(sq[:, s:s + LANES], ones_blk) for s in range(0, c, LANES)], axis=1)
        bs = jnp.concatenate([_head_sum(bn[:, s:s + LANES], ones_blk) for s in range(0, c, LANES)], axis=1)
        kkn = kq * lax.rsqrt(ss + 1e-12)

        r_o[rows, :] = r.astype(r_o.dtype)
        ld_o[rows, :] = ld
        kt_o[rows, :] = kt.astype(kt_o.dtype)
        v_o[rows, :] = v.astype(v_o.dtype)
        kkn_o[rows, :] = kkn.astype(kkn_o.dtype)
        bb_o[rows, :] = (a * kkn).astype(bb_o.dtype)
        g_o[rows, :] = g.astype(g_o.dtype)
        bonus_o[rows, :] = (bs * v).astype(bonus_o.dtype)
    carry_ref[0:1, :] = last_row


def _rwkv_prep(u, wt_rw, mu_rw, w0, a0, k_k, k_a, r_k, wdu, wau, wgu, batch, seq):
    m, k = u.shape
    assert wt_rw.shape == (RW_COLS, k)
    tb = RWKV_TB
    per_b = seq // tb
    row = lambda b, j: (b * per_b + j, 0)
    const = lambda b, j: (0, 0)
    vec = pl.BlockSpec((1, D_RWKV), const)
    outs = [jax.ShapeDtypeStruct((m, D_RWKV), F32 if i == 1 else BF16) for i in range(8)]
    return pl.pallas_call(
        _rwkv_prep_kernel,
        out_shape=outs,
        grid=(batch, per_b),
        in_specs=[pl.BlockSpec((tb, k), row),
                  pl.BlockSpec((RW_COLS, k), const, pipeline_mode=pl.Buffered(1)),
                  pl.BlockSpec((1, RW_COLS), const),
                  vec, vec, vec, vec, vec,
                  pl.BlockSpec((LORA_PAD, D_RWKV), const), pl.BlockSpec((LORA_PAD, D_RWKV), const),
                  pl.BlockSpec((GATE_LORA, D_RWKV), const)],
        out_specs=[pl.BlockSpec((tb, D_RWKV), row)] * 8,
        scratch_shapes=[pltpu.VMEM((8, RW_COLS), F32)],
        compiler_params=_cparams(("arbitrary", "arbitrary")),
        name="rwkv_prep",
    )(u, wt_rw, mu_rw, w0, a0, k_k, k_a, r_k, wdu, wau, wgu)


def _rwkv_chunk_kernel(r_ref, ld_ref, kt_ref, v_ref, kk_ref, bb_ref, y0_ref, q_ref, a_ref, c_ref):
    tb = r_ref.shape[0]
    lc = RWKV_CHUNK
    n = 2 * lc
    nchunk = tb // lc

    lane = _iota2((1, LANES), 1)
    m0 = jnp.where(lane < RWKV_HEAD, 1.0, 0.0)
    m1 = 1.0 - m0
    ri, ci = _iota2((n, n), 0), _iota2((n, n), 1)
    same_head = (ri // lc) == (ci // lc)
    strict = jnp.where(same_head & (ri > ci), 1.0, 0.0)
    incl = jnp.where(same_head & (ri >= ci), 1.0, 0.0)
    eye = jnp.where(ri == ci, 1.0, 0.0)

    tw = 4 * lc
    rt, ct = _iota2((tw, tw), 0), _iota2((tw, tw), 1)
    tri = jnp.where(((rt // lc) == (ct // lc)) & (ct <= rt), 1.0, 0.0).astype(BF16)

    ld = ld_ref[...]
    cl = jnp.concatenate([_mdot(tri, ld[i:i + tw], 1, 3) for i in range(0, tb, tw)], axis=0)
    e_pos = jnp.exp(cl)
    e_neg = jnp.exp(-cl)
    a_hat = kk_ref[...] * jnp.exp(cl - ld)
    b_hat = bb_ref[...] * e_neg
    k_hat = kt_ref[...] * e_neg
    r_hat = r_ref[...] * e_pos
    v_all = v_ref[...]

    def stack(x):
        return jnp.concatenate([x * m0, x * m1], axis=0)

    def bf(t):
        return t.astype(BF16)

    chunks = range(nchunk)
    st = [[stack(x[c * lc:(c + 1) * lc]) for x in (a_hat, b_hat, k_hat, r_hat, v_all)] for c in chunks]
    a_st, b_st, k_st, r_st, v_st = ([st[c][i] for c in chunks] for i in range(5))
    b_b, k_b, v_b = ([bf(t) for t in ts] for ts in (b_st, k_st, v_st))
    p = [_mdot(jnp.concatenate([a_st[c], r_st[c]], axis=0),
               jnp.concatenate([b_b[c], k_b[c]], axis=0), dims=NT) for c in chunks]
    mab = [p[c][0:n, 0:n] * strict for c in chunks]
    mab_b = [bf(t) for t in mab]
    mak_b = [bf(p[c][0:n, n:] * strict) for c in chunks]
    mrb_b = [bf(p[c][n:, 0:n] * incl) for c in chunks]
    mrk_b = [bf(p[c][n:, n:] * incl) for c in chunks]

    x = [eye - jnp.where((ri // 2 == ci // 2), mab[c], 0.0) for c in chunks]
    w1 = [_mdot(mak_b[c], v_b[c]) for c in chunks]
    zero_b = jnp.zeros((), BF16)
    s = 2
    while s < lc:
        lvl = (ri // (2 * s) == ci // (2 * s)) & (ri // s != ci // s)
        x_b = [bf(t) for t in x]
        nx = [_mdot(jnp.where(lvl, mab_b[c], zero_b), x_b[c]) for c in chunks]
        x = [x[c] - _mdot(x_b[c], nx[c]) for c in chunks]
        s *= 2

    ta = [_mdot(x[c], jnp.concatenate([w1[c], a_st[c]], axis=1)) for c in chunks]
    zeros_b = jnp.zeros((n, LANES), BF16)
    ugv_b = [jnp.concatenate([bf(jnp.concatenate([-ta[c][:, 0:LANES], ta[c][:, LANES:]], axis=1)),
                              jnp.concatenate([v_b[c], zeros_b], axis=1)], axis=0) for c in chunks]
    ru = [_mdot(jnp.concatenate([mrb_b[c], mrk_b[c]], axis=1), ugv_b[c]) for c in chunks]
    y0 = [ru[c][:, 0:LANES] for c in chunks]
    q = [r_st[c] - ru[c][:, LANES:] for c in chunks]
    pl_row = [e_pos[(c + 1) * lc - 1:(c + 1) * lc, :] for c in chunks]
    bk_p = [bf(jnp.concatenate([b_st[c] * pl_row[c], k_st[c] * pl_row[c]], axis=0)) for c in chunks]
    bu = [_mdot(bk_p[c], ugv_b[c], dims=TN) for c in chunks]
    ct = [bu[c][:, 0:LANES] for c in chunks]
    at = [eye * pl_row[c] - bu[c][:, LANES:] for c in chunks]

    for c in chunks:
        rows = slice(c * lc, (c + 1) * lc)
        y0_ref[rows, :] = y0[c][0:lc] + y0[c][lc:]
        q_ref[rows, :] = (q[c][0:lc] + q[c][lc:]).astype(q_ref.dtype)
        a_ref[0, c, 0] = (at[c][0:RWKV_HEAD] + at[c][RWKV_HEAD:]).astype(a_ref.dtype)
        c_ref[0, c, 0] = ct[c][0:RWKV_HEAD] + ct[c][RWKV_HEAD:]


def _rwkv_chunk(r, ld, kt, v, kkn, bb, batch, seq):
    m = r.shape[0]
    tb = RWKV_CORE_TB
    per_b = seq // tb
    cpb = tb // RWKV_CHUNK
    npair = D_RWKV // LANES
    slab = pl.BlockSpec((tb, LANES), lambda b, p, j: (b * per_b + j, p))
    mat = pl.BlockSpec((1, cpb, 1, RWKV_HEAD, LANES), lambda b, p, j: (b, j, p, 0, 0))
    mshape = (batch, seq // RWKV_CHUNK, npair, RWKV_HEAD, LANES)
    return pl.pallas_call(
        _rwkv_chunk_kernel,
        out_shape=[jax.ShapeDtypeStruct((m, D_RWKV), F32), jax.ShapeDtypeStruct((m, D_RWKV), BF16),
                   jax.ShapeDtypeStruct(mshape, BF16), jax.ShapeDtypeStruct(mshape, F32)],
        grid=(batch, npair, per_b),
        in_specs=[slab] * 6,
        out_specs=[slab, slab, mat, mat],
        compiler_params=_cparams(("arbitrary", "arbitrary", "arbitrary")),
        name="rwkv_chunk",
    )(r, ld, kt, v, kkn, bb)


def _rwkv_scan_kernel(y0_ref, q_ref, a_ref, c_ref, g_ref, bonus_ref, gng_ref, gnb_ref, ya_ref, s_ref):
    nb, rows, _ = y0_ref.shape
    lc = RWKV_CHUNK
    npair = D_RWKV // LANES

    @pl.when(pl.program_id(0) == 0)
    def _():
        s_ref[...] = jnp.zeros_like(s_ref)

    ones_blk = _head_block_ones(LANES, RWKV_HEAD)
    head0 = _iota2((RWKV_HEAD, LANES), 1) < RWKV_HEAD

    def block_diag(t):
        zero = jnp.zeros((), t.dtype)
        return jnp.concatenate([jnp.where(head0, t, zero), jnp.where(head0, zero, t)], axis=0)

    streams = [(b, p) for b in range(nb) for p in range(npair)]
    for ci in range(rows // lc):
        rs = slice(ci * lc, (ci + 1) * lc)
        state_b = [s_ref[b * npair + p].astype(BF16) for b, p in streams]
        y = [y0_ref[b, rs, p * LANES:(p + 1) * LANES]
             + _mdot(q_ref[b, rs, p * LANES:(p + 1) * LANES], state_b[i])
             for i, (b, p) in enumerate(streams)]
        for i, (b, p) in enumerate(streams):
            s_ref[b * npair + p] = (_mdot(block_diag(a_ref[b, ci, p]), state_b[i])
                                    + block_diag(c_ref[b, ci, p]))
        mu = [_head_sum(t, ones_blk) * (1.0 / RWKV_HEAD) for t in y]
        yc = [y[i] - mu[i] for i in range(len(streams))]
        var = [_head_sum(t * t, ones_blk) * (1.0 / RWKV_HEAD) for t in yc]
        for i, (b, p) in enumerate(streams):
            ls = slice(p * LANES, (p + 1) * LANES)
            yn = yc[i] * lax.rsqrt(var[i] + RWKV_GN_EPS) * gng_ref[:, ls] + gnb_ref[:, ls]
            ya_ref[b, rs, ls] = ((yn + bonus_ref[b, rs, ls]) * g_ref[b, rs, ls]).astype(ya_ref.dtype)


def _rwkv_scan(y0, q, a_mat, c_mat, g, bonus, gn_g, gn_b, batch, seq):
    rows = RWKV_SCAN_ROWS
    npair = D_RWKV // LANES
    tok = pl.BlockSpec((batch, rows, D_RWKV), lambda j: (0, j, 0))
    mat = pl.BlockSpec((batch, rows // RWKV_CHUNK, npair, RWKV_HEAD, LANES), lambda j: (0, j, 0, 0, 0))
    vec = pl.BlockSpec((1, D_RWKV), lambda j: (0, 0))
    shape3 = (batch, seq, D_RWKV)
    return pl.pallas_call(
        _rwkv_scan_kernel,
        out_shape=jax.ShapeDtypeStruct(shape3, BF16),
        grid=(seq // rows,),
        in_specs=[tok, tok, mat, mat, tok, tok, vec, vec],
        out_specs=tok,
        scratch_shapes=[pltpu.VMEM((batch * npair, LANES, LANES), F32)],
        compiler_params=_cparams(("arbitrary",)),
        name="rwkv_scan",
    )(y0.reshape(shape3), q.reshape(shape3), a_mat, c_mat, g.reshape(shape3), bonus.reshape(shape3),
      gn_g, gn_b).reshape(batch * seq, D_RWKV)


def _mlstm_kernel(u_ref, wa_ref, wo_ref, wconv_ref, bconv_ref, gbias_ref, yb_ref, tail_ref, ct_ref, n_ref, m_ref):
    lc = MLSTM_CHUNK
    rows_a = 2 * D_QK + D_MLSTM + ML_GATE_PAD

    @pl.when(pl.program_id(1) == 0)
    def _():
        tail_ref[...] = jnp.zeros_like(tail_ref)
        ct_ref[...] = jnp.zeros_like(ct_ref)
        n_ref[...] = jnp.zeros_like(n_ref)
        m_ref[...] = jnp.zeros_like(m_ref)

    groups = [slice(i, i + lc) for i in range(0, u_ref.shape[0], lc)]
    za_all = [lax.dot_general(u_ref[rows, :], wa_ref[0:rows_a, :], NT, preferred_element_type=F32)
              for rows in groups]
    o_list = [lax.dot_general(u_ref[rows, :], wo_ref[...], NT, preferred_element_type=F32) for rows in groups]
    for za, o_all, rows in zip(za_all, o_list, groups):
        _mlstm_chunk(za, o_all, rows, wconv_ref, bconv_ref, gbias_ref, yb_ref, tail_ref, ct_ref, n_ref, m_ref)


def _mlstm_chunk(za, o_all, rows, wconv_ref, bconv_ref, gbias_ref, yb_ref, tail_ref, ct_ref, n_ref, m_ref):
    lc = za.shape[0]
    dqk, dv, nh = MLSTM_DQK, MLSTM_DV, MLSTM_HEADS

    zqk = za[:, 0:2 * D_QK]
    ext = jnp.concatenate([tail_ref[...], zqk], axis=0)
    tail_ref[...] = zqk[lc - 8:lc, :]
    conv = bconv_ref[...] + wconv_ref[CONV_K - 1:CONV_K, :] * zqk
    for d in range(1, CONV_K):
        conv = conv + wconv_ref[CONV_K - 1 - d:CONV_K - d, :] * ext[8 - d:8 - d + lc, :]
    qk = conv * _sigmoid(conv)
    q_all = qk[:, 0:D_QK]
    k_all = qk[:, D_QK:] * (dqk ** -0.5)
    v_all = za[:, 2 * D_QK:2 * D_QK + D_MLSTM]

    gz = za[:, 2 * D_QK + D_MLSTM:] + gbias_ref[...]
    lf_col = jnp.minimum(gz, 0.0) - jnp.log(1.0 + jnp.exp(-jnp.abs(gz)))
    rr, cc = _iota2((lc, lc), 0), _iota2((lc, lc), 1)
    causal = cc <= rr
    tri = jnp.where(causal, 1.0, 0.0).astype(BF16)
    b_col = _mdot(tri, lf_col, 1, 3)
    sel = jnp.where(_iota2((8, LANES), 0) == _iota2((8, LANES), 1), 1.0, 0.0).astype(BF16)
    g_row = _mdot(sel, gz, 1, 3, NT)
    lf_row = jnp.minimum(g_row, 0.0) - jnp.log(1.0 + jnp.exp(-jnp.abs(g_row)))
    b_row = _mdot(lf_row, tri, 3, 1, NT)

    heads = range(nh)
    qf = [q_all[:, h * dqk:(h + 1) * dqk] for h in heads]
    q = [t.astype(BF16) for t in qf]
    k = [k_all[:, h * dqk:(h + 1) * dqk] for h in heads]
    v = [v_all[:, h * dv:(h + 1) * dv].astype(BF16) for h in heads]
    bc = [b_col[:, nh + h:nh + h + 1] for h in heads]
    ic = [gz[:, h:h + 1] for h in heads]
    br = [b_row[nh + h:nh + h + 1, :] for h in heads]
    ir = [g_row[h:h + 1, :] for h in heads]
    g_tot = [b_col[lc - 1:lc, nh + h:nh + h + 1] for h in heads]
    m_prev = [m_ref[0:1, h:h + 1] for h in heads]
    ct = [ct_ref[h] for h in heads]
    nvec = [n_ref[h:h + 1, :] for h in heads]

    qk = [_mdot(q[h], k[h], dims=NT) for h in heads]
    qc = [_mdot(q[h], ct[h]) for h in heads]
    bri = [br[h] - ir[h] for h in heads]
    dm = [jnp.where(causal, bc[h] - bri[h], -jnp.inf) for h in heads]
    inter = [bc[h] + m_prev[h] for h in heads]
    mt = [jnp.maximum(inter[h], jnp.max(dm[h], axis=-1, keepdims=True)) for h in heads]
    sc = [qk[h] * jnp.exp(dm[h] - mt[h]) for h in heads]
    winter = [jnp.exp(inter[h] - mt[h]) for h in heads]
    sv = [_mdot(sc[h], v[h]) for h in heads]
    m_new = [jnp.maximum(g_tot[h] + m_prev[h], jnp.max(g_tot[h] - bri[h], axis=-1, keepdims=True))
             for h in heads]
    kw = [k[h] * jnp.exp(g_tot[h] - bc[h] + ic[h] - m_new[h]) for h in heads]
    kv = [_mdot(kw[h], v[h], dims=TN) for h in heads]
    for h in heads:
        num = winter[h] * qc[h] + sv[h]
        den = (winter[h] * jnp.sum(qf[h] * nvec[h], axis=-1, keepdims=True)
               + jnp.sum(sc[h], axis=-1, keepdims=True))
        hout = num / jnp.maximum(jnp.abs(den), jnp.exp(-mt[h]))
        wc = jnp.exp(g_tot[h] + m_prev[h] - m_new[h])
        ct_ref[h] = wc * ct[h] + kv[h]
        n_ref[h:h + 1, :] = wc * nvec[h] + jnp.sum(kw[h], axis=0, keepdims=True)
        m_ref[0:1, h:h + 1] = m_new[h]
        o = o_all[:, h * dv:(h + 1) * dv]
        yb_ref[rows, h * dv:(h + 1) * dv] = (_sigmoid(o) * hout).astype(yb_ref.dtype)


def _mlstm(u, wt_a, wt_o, w_conv, b_conv, gbias, batch, seq):
    m, k = u.shape
    tb = MLSTM_TB
    per_b = seq // tb
    const = lambda b, j: (0, 0)
    assert wt_a.shape[0] >= 2 * D_QK + D_MLSTM + ML_GATE_PAD and wt_o.shape == (D_MLSTM, k)
    resident = lambda w: pl.BlockSpec(w.shape, const, pipeline_mode=pl.Buffered(1))
    return pl.pallas_call(
        _mlstm_kernel,
        out_shape=jax.ShapeDtypeStruct((m, D_MLSTM), BF16),
        grid=(batch, per_b),
        in_specs=[pl.BlockSpec((tb, k), lambda b, j: (b * per_b + j, 0)),
                  resident(wt_a), resident(wt_o),
                  pl.BlockSpec((CONV_K, 2 * D_QK), const),
                  pl.BlockSpec((1, 2 * D_QK), const),
                  pl.BlockSpec((1, ML_GATE_PAD), const)],
        out_specs=pl.BlockSpec((tb, D_MLSTM), lambda b, j: (b * per_b + j, 0)),
        scratch_shapes=[pltpu.VMEM((8, 2 * D_QK), F32),
                        pltpu.VMEM((MLSTM_HEADS, MLSTM_DQK, MLSTM_DV), F32),
                        pltpu.VMEM((8, MLSTM_DQK), F32),
                        pltpu.VMEM((8, LANES), F32)],
        compiler_params=_cparams(("arbitrary", "arbitrary")),
        name="mlstm",
    )(u, wt_a, wt_o, w_conv, b_conv, gbias)


def _merge_kernel(ya_ref, yb_ref, zg_ref, wa_ref, wb_ref, o_ref):
    groups = [slice(i, i + EPILOGUE_ROWS) for i in range(0, ya_ref.shape[0], EPILOGUE_ROWS)]
    pab = [(jnp.dot(ya_ref[rows, :], wa_ref[...], preferred_element_type=F32),
            jnp.dot(yb_ref[rows, :], wb_ref[...], preferred_element_type=F32)) for rows in groups]
    for (pa, pb), rows in zip(pab, groups):
        ga = _sigmoid(zg_ref[rows, 0:D_MODEL].astype(F32))
        gb = _sigmoid(zg_ref[rows, D_MODEL:].astype(F32))
        o_ref[rows, :] = (ga * pa + gb * pb).astype(o_ref.dtype)


def _merge(ya, yb, zg, wa, wb):
    m = ya.shape[0]
    tm = 512
    const = lambda i: (0, 0)
    return pl.pallas_call(
        _merge_kernel,
        out_shape=jax.ShapeDtypeStruct((m, D_MODEL), BF16),
        grid=(m // tm,),
        in_specs=[pl.BlockSpec((tm, D_RWKV), lambda i: (i, 0)),
                  pl.BlockSpec((tm, D_MLSTM), lambda i: (i, 0)),
                  pl.BlockSpec((tm, 2 * D_MODEL), lambda i: (i, 0)),
                  pl.BlockSpec((D_RWKV, D_MODEL), const),
                  pl.BlockSpec((D_MLSTM, D_MODEL), const)],
        out_specs=pl.BlockSpec((tm, D_MODEL), lambda i: (i, 0)),
        compiler_params=_cparams(("arbitrary",)),
        name="merge",
    )(ya, yb, zg, wa, wb)


def _outproj_kernel(mg_ref, w_ref, x_ref, ada_ref, g_ref, b_ref, x1_ref, u2_ref):
    groups = [slice(i, i + EPILOGUE_ROWS) for i in range(0, mg_ref.shape[0], EPILOGUE_ROWS)]
    ys = [jnp.dot(mg_ref[rows, :], w_ref[...], preferred_element_type=F32) for rows in groups]
    g1 = ada_ref[0, 2:3, :]
    sh2, sc2 = ada_ref[0, 3:4, :], ada_ref[0, 4:5, :]
    for y, rows in zip(ys, groups):
        x1 = _layernorm(ALPHA * x_ref[rows, :] + g1 * y) * g_ref[...] + b_ref[...]
        x1_ref[rows, :] = x1
        u2_ref[rows, :] = (_layernorm(x1) * (1.0 + sc2) + sh2).astype(u2_ref.dtype)


def _outproj(merged, w_out, x2, ada3, ln_g, ln_b, seq):
    m = merged.shape[0]
    tm = 512
    per_b = seq // tm
    const = lambda i: (0, 0)
    row = pl.BlockSpec((tm, D_MODEL), lambda i: (i, 0))
    return pl.pallas_call(
        _outproj_kernel,
        out_shape=[jax.ShapeDtypeStruct((m, D_MODEL), F32), jax.ShapeDtypeStruct((m, D_MODEL), BF16)],
        grid=(m // tm,),
        in_specs=[row, pl.BlockSpec((D_MODEL, D_MODEL), const, pipeline_mode=pl.Buffered(1)), row,
                  pl.BlockSpec((1, 6, D_MODEL), lambda i: (i // per_b, 0, 0)),
                  pl.BlockSpec((1, D_MODEL), const), pl.BlockSpec((1, D_MODEL), const)],
        out_specs=[row, row],
        compiler_params=_cparams(("arbitrary",)),
        name="outproj",
    )(merged, w_out, x2, ada3, ln_g, ln_b)


def _ff1_kernel(x_ref, w_ref, b_ref, w2_ref, o_ref, w2b_ref, wb_ref):
    @pl.when(pl.program_id(1) == 0)
    def _():
        wb_ref[...] = w_ref[...].astype(wb_ref.dtype)

    w2b_ref[...] = w2_ref[...].astype(w2b_ref.dtype)
    groups = [slice(i, i + 2 * EPILOGUE_ROWS) for i in range(0, x_ref.shape[0], 2 * EPILOGUE_ROWS)]
    hs = [jnp.dot(x_ref[rows, :], wb_ref[...], preferred_element_type=F32) for rows in groups]
    for h, rows in zip(hs, groups):
        h = jnp.maximum(h + b_ref[...], 0.0)
        o_ref[rows, :] = (h * h).astype(o_ref.dtype)


def _ff1(u2, w1, b1, w2):
    m, k = u2.shape
    n = w1.shape[1]
    tm, tn = 1024, 1024
    n_m = m // tm
    rows2 = w2.shape[0] // ((n // tn) * n_m)
    assert rows2 % 16 == 0 and rows2 * (n // tn) * n_m == w2.shape[0]
    slab = pl.BlockSpec((rows2, w2.shape[1]), lambda j, i: (j * n_m + i, 0))
    return pl.pallas_call(
        _ff1_kernel,
        out_shape=[jax.ShapeDtypeStruct((m, n), BF16), jax.ShapeDtypeStruct(w2.shape, BF16)],
        grid=(n // tn, n_m),
        in_specs=[pl.BlockSpec((tm, k), lambda j, i: (i, 0)),
                  pl.BlockSpec((k, tn), lambda j, i: (0, j)),
                  pl.BlockSpec((1, tn), lambda j, i: (0, j)),
                  slab],
        out_specs=[pl.BlockSpec((tm, tn), lambda j, i: (i, j)), slab],
        scratch_shapes=[pltpu.VMEM((k, tn), BF16)],
        compiler_params=_cparams(("arbitrary", "arbitrary")),
        name="ff1",
    )(u2, w1, b1, w2)


def _ff2_kernel(h_ref, w_ref, b_ref, x1_ref, ada_ref, g_ref, bb_ref, o_ref):
    y2 = jnp.dot(h_ref[...], w_ref[...], preferred_element_type=F32) + b_ref[...]
    g2 = ada_ref[0, 5:6, :]
    o_ref[...] = (_layernorm(ALPHA * x1_ref[...] + g2 * y2) * g_ref[...] + bb_ref[...]).astype(o_ref.dtype)


def _ff2(h, w2, b2, x1, ada3, ln_g, ln_b, seq, out_dtype):
    m, k = h.shape
    tm = 256
    per_b = seq // tm
    const = lambda i: (0, 0)
    row = pl.BlockSpec((tm, D_MODEL), lambda i: (i, 0))
    return pl.pallas_call(
        _ff2_kernel,
        out_shape=jax.ShapeDtypeStruct((m, D_MODEL), out_dtype),
        grid=(m // tm,),
        in_specs=[pl.BlockSpec((tm, k), lambda i: (i, 0)),
                  pl.BlockSpec((k, D_MODEL), const, pipeline_mode=pl.Buffered(1)),
                  pl.BlockSpec((1, D_MODEL), const), row,
                  pl.BlockSpec((1, 6, D_MODEL), lambda i: (i // per_b, 0, 0)),
                  pl.BlockSpec((1, D_MODEL), const), pl.BlockSpec((1, D_MODEL), const)],
        out_specs=row,
        compiler_params=_cparams(("arbitrary",)),
        name="ff2",
    )(h, w2, b2, x1, ada3, ln_g, ln_b)


def _pad_rows(w, height):
    return jnp.pad(w, ((0, height - w.shape[0]), (0, 0)))


def _layer(x2, c, batch, seq, w_ada, b_ada, w_in, mu_shift, w0, w_decay_up, a0, w_a_up, w_g_up, k_k, k_a, r_k,
           gn_g, gn_b, w_conv, b_conv, b_igate, b_fgate, w_branch_a, w_branch_b, w_out,
           ln1_g, ln1_b, w_ff1, b_ff1, w_ff2, b_ff2, ln2_g, ln2_b):
    row = lambda v: v.reshape(1, -1)

    ada3 = _ada(c, w_ada, b_ada).reshape(batch, 6, D_MODEL)
    u = _lnmod(x2, ada3, seq)

    assert w_in.shape == (D_MODEL, N_RWKV_COLS + N_MLSTM_COLS + 2 * D_MODEL)
    w_in_t = jnp.swapaxes(w_in, 0, 1)
    col_o = N_RWKV_COLS + 2 * D_QK + D_MLSTM + 2 * MLSTM_HEADS
    z_gt, w_a_b, w_b_b, w_out_b, wt_rw_b, wt_ma_b, wt_mo_b = _win_matmul(
        u, w_in_t, N_RWKV_COLS + N_MLSTM_COLS, 2 * D_MODEL, 1024, 1024, BF16, "win_gate",
        cast_along=((w_branch_a, 0, D_RWKV), (w_branch_b, 0, D_MLSTM), (w_out, 0, D_MODEL),
                    (w_in_t, 0, RW_COLS), (w_in_t, N_RWKV_COLS, ML_ROWS_A), (w_in_t, col_o, D_MLSTM)))

    mu_rw = jnp.pad(mu_shift, (0, RW_COLS - N_RWKV_COLS))
    prep = _rwkv_prep(u, wt_rw_b, row(mu_rw), row(w0), row(a0), row(k_k), row(k_a), row(r_k),
                      _pad_rows(w_decay_up, LORA_PAD), _pad_rows(w_a_up, LORA_PAD), w_g_up, batch, seq)
    r, ld, kt, v, kkn, bb, g, bonus = prep
    y0, q, a_mat, c_mat = _rwkv_chunk(r, ld, kt, v, kkn, bb, batch, seq)
    ya = _rwkv_scan(y0, q, a_mat, c_mat, g, bonus, row(gn_g), row(gn_b), batch, seq)

    gbias = jnp.pad(jnp.concatenate([b_igate, b_fgate]), (0, ML_GATE_PAD - 2 * MLSTM_HEADS))
    yb = _mlstm(u, wt_ma_b, wt_mo_b, w_conv, row(b_conv), row(gbias), batch, seq)

    merged = _merge(ya, yb, z_gt, w_a_b, w_b_b)
    x1, u2 = _outproj(merged, w_out_b, x2, ada3, row(ln1_g), row(ln1_b), seq)

    h, w_ff2_b = _ff1(u2, w_ff1, row(b_ff1), w_ff2)
    return _ff2(h, w_ff2_b, row(b_ff2), x1, ada3, row(ln2_g), row(ln2_b), seq, F32)


def kernel(x, c, w_ada, b_ada, w_in, mu_shift, w0, w_decay_up, a0, w_a_up, w_g_up, k_k, k_a, r_k, gn_g, gn_b,
           w_conv, b_conv, b_igate, b_fgate, w_branch_a, w_branch_b, w_out, ln1_g, ln1_b, w_ff1, b_ff1,
           w_ff2, b_ff2, ln2_g, ln2_b):
    out_dtype = x.dtype
    batch, seq, d = x.shape
    assert d == D_MODEL and seq % 1024 == 0 and w_ada.shape[0] == DEPTH
    x2 = x.astype(F32).reshape(batch * seq, d)
    cf = c.astype(F32)
    for l in range(DEPTH):
        x2 = _layer(x2, cf, batch, seq, w_ada[l], b_ada[l], w_in[l], mu_shift[l], w0[l], w_decay_up[l], a0[l],
                    w_a_up[l], w_g_up[l], k_k[l], k_a[l], r_k[l].reshape(-1), gn_g[l], gn_b[l], w_conv[l],
                    b_conv[l], b_igate[l], b_fgate[l], w_branch_a[l], w_branch_b[l], w_out[l], ln1_g[l],
                    ln1_b[l], w_ff1[l], b_ff1[l], w_ff2[l], b_ff2[l], ln2_g[l], ln2_b[l])
    return x2.reshape(batch, seq, d).astype(out_dtype)
```

```python
import functools
import math

import jax
import jax.numpy as jnp
from jax import lax
from jax.experimental import pallas as pl
from jax.experimental.pallas import tpu as pltpu

F32 = jnp.float32
BF16 = jnp.bfloat16

D_MODEL = 2048
DEPTH = 1
D_RWKV = D_MODEL // 2
RWKV_HEAD = 64
DECAY_LORA = 96
A_LORA = 96
GATE_LORA = 256
RWKV_GN_EPS = 64e-5
D_MLSTM = D_MODEL // 2
MLSTM_HEADS = 4
MLSTM_DV = D_MLSTM // MLSTM_HEADS
MLSTM_DQK = MLSTM_DV // 2
D_QK = MLSTM_HEADS * MLSTM_DQK
CONV_K = 4
D_FF = 4 * D_MODEL
LN_EPS = 1e-5
ALPHA = (2.0 * DEPTH) ** 0.25

LANES = 128
LORA_PAD = 128
N_RWKV_COLS = 3 * D_RWKV + DECAY_LORA + A_LORA + GATE_LORA
N_MLSTM_COLS = 2 * D_QK + 2 * D_MLSTM + 2 * MLSTM_HEADS
RW_COLS = 3584
ML_GATE_PAD = 128
ML_ROWS_A = 2560
RWKV_CHUNK = 64
RWKV_TB = 512
RWKV_SCAN_ROWS = 256
RWKV_CORE_TB = 1024
MLSTM_CHUNK = 256
MLSTM_TB = 1024
EPILOGUE_ROWS = 256
VMEM_LIMIT = 56 * 1024 * 1024

NN = (((1,), (0,)), ((), ()))
NT = (((1,), (1,)), ((), ()))
TN = (((0,), (0,)), ((), ()))


def _cparams(sem):
    return pltpu.CompilerParams(dimension_semantics=sem, vmem_limit_bytes=VMEM_LIMIT)


def _bf16_parts(x, n):
    if x.dtype == BF16:
        return [x]
    parts, rem = [], x
    for i in range(n):
        p = rem.astype(BF16)
        parts.append(p)
        if i + 1 < n:
            rem = rem - p.astype(F32)
    return parts


def _mdot(a, b, na=1, nb=1, dims=NN):
    ap, bp = _bf16_parts(a, na), _bf16_parts(b, nb)
    order = max(len(ap), len(bp))
    acc = None
    for i, x in enumerate(ap):
        for j, y in enumerate(bp):
            if i + j < order:
                t = lax.dot_general(x, y, dims, preferred_element_type=F32)
                acc = t if acc is None else acc + t
    return acc


def _sigmoid(x):
    return 1.0 / (1.0 + jnp.exp(-x))


def _layernorm(x):
    mu = jnp.mean(x, axis=-1, keepdims=True)
    xc = x - mu
    var = jnp.mean(xc * xc, axis=-1, keepdims=True)
    return xc * lax.rsqrt(var + LN_EPS)


def _iota2(shape, dim):
    return lax.broadcasted_iota(jnp.int32, shape, dim)


def _head_block_ones(n, head):
    r, c = _iota2((n, n), 0), _iota2((n, n), 1)
    return jnp.where((r // head) == (c // head), 1.0, 0.0).astype(BF16)


def _head_sum(x, ones_blk):
    return _mdot(x, ones_blk)


def _ada_kernel(ct_ref, w_ref, b_ref, o_ref, sb_ref):
    tn = w_ref.shape[1]
    nb = sb_ref.shape[0]

    @pl.when(pl.program_id(0) == 0)
    def _():
        cv = ct_ref[...]
        sv = cv * _sigmoid(cv)
        for b in range(nb):
            sb_ref[b] = jnp.broadcast_to(sv[:, b:b + 1], sb_ref.shape[1:])

    ntile = tn // LANES

    def body(i, acc):
        r0 = pl.multiple_of(i * 8, 8)
        sb = [sb_ref[b, pl.ds(r0, 8), :] for b in range(nb)]
        wv = [w_ref[pl.ds(r0, 8), t * LANES:(t + 1) * LANES] for t in range(ntile)]
        return tuple(acc[b * ntile + t] + sb[b] * wv[t] for b in range(nb) for t in range(ntile))

    zero = jnp.zeros((8, LANES), F32)
    acc = lax.fori_loop(0, w_ref.shape[0] // 8, body, (zero,) * (nb * ntile), unroll=4)
    for b in range(nb):
        tot = jnp.concatenate([acc[b * ntile + t] for t in range(ntile)], axis=1)
        o_ref[b:b + 1, :] = jnp.sum(tot, axis=0, keepdims=True) + b_ref[...]


def _ada(c, w_ada, b_ada):
    batch = c.shape[0]
    assert batch == 2
    n = w_ada.shape[1]
    tn = 1536
    return pl.pallas_call(
        _ada_kernel,
        out_shape=jax.ShapeDtypeStruct((batch, n), F32),
        grid=(n // tn,),
        in_specs=[pl.BlockSpec((D_MODEL, batch), lambda j: (0, 0)),
                  pl.BlockSpec((D_MODEL, tn), lambda j: (0, j)),
                  pl.BlockSpec((1, tn), lambda j: (0, j))],
        out_specs=pl.BlockSpec((batch, tn), lambda j: (0, j)),
        scratch_shapes=[pltpu.VMEM((batch, D_MODEL, LANES), F32)],
        compiler_params=_cparams(("arbitrary",)),
        name="ada",
    )(c.T, w_ada, b_ada.reshape(1, n))


def _lnmod_kernel(x_ref, ada_ref, o_ref):
    xn = _layernorm(x_ref[...])
    sh, sc = ada_ref[0, 0:1, :], ada_ref[0, 1:2, :]
    o_ref[...] = (xn * (1.0 + sc) + sh).astype(o_ref.dtype)


def _lnmod(x2, ada3, seq):
    m = x2.shape[0]
    tb = 1024
    per_b = seq // tb
    return pl.pallas_call(
        _lnmod_kernel,
        out_shape=jax.ShapeDtypeStruct((m, D_MODEL), BF16),
        grid=(m // tb,),
        in_specs=[pl.BlockSpec((tb, D_MODEL), lambda i: (i, 0)),
                  pl.BlockSpec((1, 6, D_MODEL), lambda i: (i // per_b, 0, 0))],
        out_specs=pl.BlockSpec((tb, D_MODEL), lambda i: (i, 0)),
        compiler_params=_cparams(("arbitrary",)),
        name="lnmod",
    )(x2, ada3)


def _win_kernel(n_cast, x_ref, wt_ref, *refs):
    cast_in, o_ref, cast_out, wb_ref = refs[:n_cast], refs[n_cast], refs[n_cast + 1:-1], refs[-1]

    @pl.when(pl.program_id(1) == 0)
    def _():
        wb_ref[...] = wt_ref[...].astype(wb_ref.dtype)

    for src, dst in zip(cast_in, cast_out):
        dst[...] = src[...].astype(dst.dtype)
    o_ref[...] = lax.dot_general(x_ref[...], wb_ref[...], NT, preferred_element_type=F32).astype(o_ref.dtype)


def _win_matmul(x, wt, col0, n_out, tm, tn, out_dtype, name, cast_along=()):
    m, k = x.shape
    assert col0 % 8 == 0 and n_out % tn == 0 and col0 + n_out <= wt.shape[0]
    n_m = m // tm
    steps = (n_out // tn) * n_m

    def w_index(j, i):
        return pl.multiple_of(col0 + j * tn, 8), 0

    slabs_in, slabs_out, out_shapes = [], [], []
    for w, row0, nrows in cast_along:
        rows = nrows // steps
        assert rows % 16 == 0 and row0 % 8 == 0 and rows * steps == nrows and row0 + nrows <= w.shape[0]
        slabs_in.append(pl.BlockSpec(
            (pl.Element(rows), pl.Element(w.shape[1])),
            functools.partial(lambda j, i, row0, rows: (pl.multiple_of(row0 + (j * n_m + i) * rows, 8), 0),
                              row0=row0, rows=rows)))
        slabs_out.append(pl.BlockSpec((rows, w.shape[1]), lambda j, i: (j * n_m + i, 0)))
        out_shapes.append(jax.ShapeDtypeStruct((nrows, w.shape[1]), BF16))
    cast_along = [w for w, _, _ in cast_along]
    outs = pl.pallas_call(
        functools.partial(_win_kernel, len(cast_along)),
        out_shape=[jax.ShapeDtypeStruct((m, n_out), out_dtype)] + out_shapes,
        grid=(n_out // tn, n_m),
        in_specs=[pl.BlockSpec((tm, k), lambda j, i: (i, 0)),
                  pl.BlockSpec((pl.Element(tn), pl.Element(k)), w_index)] + slabs_in,
        out_specs=[pl.BlockSpec((tm, tn), lambda j, i: (i, j))] + slabs_out,
        scratch_shapes=[pltpu.VMEM((tn, k), BF16)],
        compiler_params=_cparams(("arbitrary", "arbitrary")),
        name=name,
    )(x, wt, *cast_along)
    return outs if cast_along else outs[0]


def _rwkv_prep_kernel(u_ref, wt_ref, mu_ref, w0_ref, a0_ref, kk_ref, ka_ref, rk_ref, wdu_ref, wau_ref, wgu_ref,
                      r_o, ld_o, kt_o, v_o, kkn_o, bb_o, g_o, bonus_o, carry_ref):
    tb = u_ref.shape[0]

    @pl.when(pl.program_id(1) == 0)
    def _():
        carry_ref[...] = jnp.zeros_like(carry_ref)

    groups = [slice(i, i + EPILOGUE_ROWS) for i in range(0, tb, EPILOGUE_ROWS)]
    zs_all = [lax.dot_general(u_ref[rows, :], wt_ref[...], NT, preferred_element_type=F32) for rows in groups]
    ones_blk = _head_block_ones(LANES, RWKV_HEAD)
    last_row = carry_ref[0:1, :]
    for z, rows in zip(zs_all, groups):
        prev = pltpu.roll(z, 1, 0)
        prev = jnp.where(_iota2(z.shape, 0) == 0, last_row, prev)
        last_row = z[EPILOGUE_ROWS - 1:EPILOGUE_ROWS, :]
        zs = z + (prev - z) * mu_ref[...]

        c = D_RWKV
        r, k, v = zs[:, 0:c], zs[:, c:2 * c], zs[:, 2 * c:3 * c]
        wd = zs[:, 3 * c:3 * c + LORA_PAD]
        ad = zs[:, 3 * c + DECAY_LORA:3 * c + DECAY_LORA + LORA_PAD]
        gd = zs[:, 3 * c + DECAY_LORA + A_LORA:3 * c + DECAY_LORA + A_LORA + GATE_LORA]

        ld = -math.exp(-0.5) * _sigmoid(w0_ref[...] + _mdot(jnp.tanh(wd), wdu_ref[...], 2, 2))
        a = _sigmoid(a0_ref[...] + _mdot(ad, wau_ref[...]))
        g = _mdot(_sigmoid(gd), wgu_ref[...], 1, 1)

        kq = k * kk_ref[...]
        kt = k * (1.0 + (a - 1.0) * ka_ref[...])
        sq = kq * kq
        bn = r * kt * rk_ref[...]
        ss = jnp.concatenate([_head_sum(sq[:, s:s + LANES], ones_blk) for s in range(0, c, LANES)], axis=1)
        bs = jnp.concatenate([_head_sum(bn[:, s:s + LANES], ones_blk) for s in range(0, c, LANES)], axis=1)
        kkn = kq * lax.rsqrt(ss + 1e-12)

        r_o[rows, :] = r.astype(r_o.dtype)
        ld_o[rows, :] = ld
        kt_o[rows, :] = kt.astype(kt_o.dtype)
        v_o[rows, :] = v.astype(v_o.dtype)
        kkn_o[rows, :] = kkn.astype(kkn_o.dtype)
        bb_o[rows, :] = (a * kkn).astype(bb_o.dtype)
        g_o[rows, :] = g.astype(g_o.dtype)
        bonus_o[rows, :] = (bs * v).astype(bonus_o.dtype)
    carry_ref[0:1, :] = last_row


def _rwkv_prep(u, wt_rw, mu_rw, w0, a0, k_k, k_a, r_k, wdu, wau, wgu, batch, seq):
    m, k = u.shape
    assert wt_rw.shape == (RW_COLS, k)
    tb = RWKV_TB
    per_b = seq // tb
    row = lambda b, j: (b * per_b + j, 0)
    const = lambda b, j: (0, 0)
    vec = pl.BlockSpec((1, D_RWKV), const)
    outs = [jax.ShapeDtypeStruct((m, D_RWKV), F32 if i == 1 else BF16) for i in range(8)]
    return pl.pallas_call(
        _rwkv_prep_kernel,
        out_shape=outs,
        grid=(batch, per_b),
        in_specs=[pl.BlockSpec((tb, k), row),
                  pl.BlockSpec((RW_COLS, k), const, pipeline_mode=pl.Buffered(1)),
                  pl.BlockSpec((1, RW_COLS), const),
                  vec, vec, vec, vec, vec,
                  pl.BlockSpec((LORA_PAD, D_RWKV), const), pl.BlockSpec((LORA_PAD, D_RWKV), const),
                  pl.BlockSpec((GATE_LORA, D_RWKV), const)],
        out_specs=[pl.BlockSpec((tb, D_RWKV), row)] * 8,
        scratch_shapes=[pltpu.VMEM((8, RW_COLS), F32)],
        compiler_params=_cparams(("arbitrary", "arbitrary")),
        name="rwkv_prep",
    )(u, wt_rw, mu_rw, w0, a0, k_k, k_a, r_k, wdu, wau, wgu)


def _rwkv_chunk_kernel(r_ref, ld_ref, kt_ref, v_ref, kk_ref, bb_ref, y0_ref, q_ref, a_ref, c_ref):
    tb = r_ref.shape[0]
    lc = RWKV_CHUNK
    n = 2 * lc
    nchunk = tb // lc

    lane = _iota2((1, LANES), 1)
    m0 = jnp.where(lane < RWKV_HEAD, 1.0, 0.0)
    m1 = 1.0 - m0
    ri, ci = _iota2((n, n), 0), _iota2((n, n), 1)
    same_head = (ri // lc) == (ci // lc)
    strict = jnp.where(same_head & (ri > ci), 1.0, 0.0)
    incl = jnp.where(same_head & (ri >= ci), 1.0, 0.0)
    eye = jnp.where(ri == ci, 1.0, 0.0)

    tw = 4 * lc
    rt, ct = _iota2((tw, tw), 0), _iota2((tw, tw), 1)
    tri = jnp.where(((rt // lc) == (ct // lc)) & (ct <= rt), 1.0, 0.0).astype(BF16)

    ld = ld_ref[...]
    cl = jnp.concatenate([_mdot(tri, ld[i:i + tw], 1, 3) for i in range(0, tb, tw)], axis=0)
    e_pos = jnp.exp(cl)
    e_neg = jnp.exp(-cl)
    a_hat = kk_ref[...] * jnp.exp(cl - ld)
    b_hat = bb_ref[...] * e_neg
    k_hat = kt_ref[...] * e_neg
    r_hat = r_ref[...] * e_pos
    v_all = v_ref[...]

    def stack(x):
        return jnp.concatenate([x * m0, x * m1], axis=0)

    def bf(t):
        return t.astype(BF16)

    head0 = _iota2((lc, LANES), 1) < RWKV_HEAD

    def stack_b(x):
        xb, zero = x.astype(BF16), jnp.zeros((), BF16)
        return jnp.concatenate([jnp.where(head0, xb, zero), jnp.where(head0, zero, xb)], axis=0)

    chunks = range(nchunk)
    rows = [slice(c * lc, (c + 1) * lc) for c in chunks]
    a_b, b_b, k_b, v_b = ([stack_b(x[rows[c]]) for c in chunks] for x in (a_hat, b_hat, k_hat, v_all))
    r_st = [stack(r_hat[rows[c]]) for c in chunks]
    p = [_mdot(jnp.concatenate([a_b[c], bf(r_st[c])], axis=0),
               jnp.concatenate([b_b[c], k_b[c]], axis=0), dims=NT) for c in chunks]
    mab = [p[c][0:n, 0:n] * strict for c in chunks]
    mab_b = [bf(t) for t in mab]
    mak_b = [bf(p[c][0:n, n:] * strict) for c in chunks]
    mrb_b = [bf(p[c][n:, 0:n] * incl) for c in chunks]
    mrk_b = [bf(p[c][n:, n:] * incl) for c in chunks]

    x = [eye - jnp.where((ri // 2 == ci // 2), mab[c], 0.0) for c in chunks]
    w1 = [_mdot(mak_b[c], v_b[c]) for c in chunks]
    zero_b = jnp.zeros((), BF16)
    s = 2
    while s < lc:
        lvl = (ri // (2 * s) == ci // (2 * s)) & (ri // s != ci // s)
        x_b = [bf(t) for t in x]
        nx = [_mdot(jnp.where(lvl, mab_b[c], zero_b), x_b[c]) for c in chunks]
        x = [x[c] - _mdot(x_b[c], nx[c]) for c in chunks]
        s *= 2

    ta = [_mdot(x[c], jnp.concatenate([bf(w1[c]), a_b[c]], axis=1)) for c in chunks]
    zeros_b = jnp.zeros((n, LANES), BF16)
    ugv_b = [jnp.concatenate([bf(jnp.concatenate([-ta[c][:, 0:LANES], ta[c][:, LANES:]], axis=1)),
                              jnp.concatenate([v_b[c], zeros_b], axis=1)], axis=0) for c in chunks]
    ru = [_mdot(jnp.concatenate([mrb_b[c], mrk_b[c]], axis=1), ugv_b[c]) for c in chunks]
    y0 = [ru[c][:, 0:LANES] for c in chunks]
    q = [r_st[c] - ru[c][:, LANES:] for c in chunks]
    pl_row = [e_pos[(c + 1) * lc - 1:(c + 1) * lc, :] for c in chunks]
    bk_p = [jnp.concatenate([stack_b(b_hat[rows[c]] * pl_row[c]), stack_b(k_hat[rows[c]] * pl_row[c])], axis=0)
            for c in chunks]
    bu = [_mdot(bk_p[c], ugv_b[c], dims=TN) for c in chunks]
    ct = [bu[c][:, 0:LANES] for c in chunks]
    at = [eye * pl_row[c] - bu[c][:, LANES:] for c in chunks]

    for c in chunks:
        rows = slice(c * lc, (c + 1) * lc)
        y0_ref[rows, :] = y0[c][0:lc] + y0[c][lc:]
        q_ref[rows, :] = (q[c][0:lc] + q[c][lc:]).astype(q_ref.dtype)
        a_ref[0, c, 0] = (at[c][0:RWKV_HEAD] + at[c][RWKV_HEAD:]).astype(a_ref.dtype)
        c_ref[0, c, 0] = ct[c][0:RWKV_HEAD] + ct[c][RWKV_HEAD:]


def _rwkv_chunk(r, ld, kt, v, kkn, bb, batch, seq):
    m = r.shape[0]
    tb = RWKV_CORE_TB
    per_b = seq // tb
    cpb = tb // RWKV_CHUNK
    npair = D_RWKV // LANES
    slab = pl.BlockSpec((tb, LANES), lambda b, p, j: (b * per_b + j, p))
    mat = pl.BlockSpec((1, cpb, 1, RWKV_HEAD, LANES), lambda b, p, j: (b, j, p, 0, 0))
    mshape = (batch, seq // RWKV_CHUNK, npair, RWKV_HEAD, LANES)
    return pl.pallas_call(
        _rwkv_chunk_kernel,
        out_shape=[jax.ShapeDtypeStruct((m, D_RWKV), F32), jax.ShapeDtypeStruct((m, D_RWKV), BF16),
                   jax.ShapeDtypeStruct(mshape, BF16), jax.ShapeDtypeStruct(mshape, F32)],
        grid=(batch, npair, per_b),
        in_specs=[slab] * 6,
        out_specs=[slab, slab, mat, mat],
        compiler_params=_cparams(("arbitrary", "arbitrary", "arbitrary")),
        name="rwkv_chunk",
    )(r, ld, kt, v, kkn, bb)


def _rwkv_scan_kernel(y0_ref, q_ref, a_ref, c_ref, g_ref, bonus_ref, gng_ref, gnb_ref, ya_ref, s_ref):
    nb, rows, _ = y0_ref.shape
    lc = RWKV_CHUNK
    npair = D_RWKV // LANES

    @pl.when(pl.program_id(0) == 0)
    def _():
        s_ref[...] = jnp.zeros_like(s_ref)

    ones_blk = _head_block_ones(LANES, RWKV_HEAD)
    head0 = _iota2((RWKV_HEAD, LANES), 1) < RWKV_HEAD

    def block_diag(t):
        zero = jnp.zeros((), t.dtype)
        return jnp.concatenate([jnp.where(head0, t, zero), jnp.where(head0, zero, t)], axis=0)

    streams = [(b, p) for b in range(nb) for p in range(npair)]
    for ci in range(rows // lc):
        rs = slice(ci * lc, (ci + 1) * lc)
        state_b = [s_ref[b * npair + p].astype(BF16) for b, p in streams]
        y = [y0_ref[b, rs, p * LANES:(p + 1) * LANES]
             + _mdot(q_ref[b, rs, p * LANES:(p + 1) * LANES], state_b[i])
             for i, (b, p) in enumerate(streams)]
        for i, (b, p) in enumerate(streams):
            s_ref[b * npair + p] = (_mdot(block_diag(a_ref[b, ci, p]), state_b[i])
                                    + block_diag(c_ref[b, ci, p]))
        mu = [_head_sum(t, ones_blk) * (1.0 / RWKV_HEAD) for t in y]
        yc = [y[i] - mu[i] for i in range(len(streams))]
        var = [_head_sum(t * t, ones_blk) * (1.0 / RWKV_HEAD) for t in yc]
        for i, (b, p) in enumerate(streams):
            ls = slice(p * LANES, (p + 1) * LANES)
            yn = yc[i] * lax.rsqrt(var[i] + RWKV_GN_EPS) * gng_ref[:, ls] + gnb_ref[:, ls]
            ya_ref[b, rs, ls] = ((yn + bonus_ref[b, rs, ls]) * g_ref[b, rs, ls]).astype(ya_ref.dtype)


def _rwkv_scan(y0, q, a_mat, c_mat, g, bonus, gn_g, gn_b, batch, seq):
    rows = RWKV_SCAN_ROWS
    npair = D_RWKV // LANES
    tok = pl.BlockSpec((batch, rows, D_RWKV), lambda j: (0, j, 0))
    mat = pl.BlockSpec((batch, rows // RWKV_CHUNK, npair, RWKV_HEAD, LANES), lambda j: (0, j, 0, 0, 0))
    vec = pl.BlockSpec((1, D_RWKV), lambda j: (0, 0))
    shape3 = (batch, seq, D_RWKV)
    return pl.pallas_call(
        _rwkv_scan_kernel,
        out_shape=jax.ShapeDtypeStruct(shape3, BF16),
        grid=(seq // rows,),
        in_specs=[tok, tok, mat, mat, tok, tok, vec, vec],
        out_specs=tok,
        scratch_shapes=[pltpu.VMEM((batch * npair, LANES, LANES), F32)],
        compiler_params=_cparams(("arbitrary",)),
        name="rwkv_scan",
    )(y0.reshape(shape3), q.reshape(shape3), a_mat, c_mat, g.reshape(shape3), bonus.reshape(shape3),
      gn_g, gn_b).reshape(batch * seq, D_RWKV)


def _mlstm_kernel(u_ref, wa_ref, wo_ref, wconv_ref, bconv_ref, gbias_ref, yb_ref, tail_ref, ct_ref, n_ref, m_ref):
    lc = MLSTM_CHUNK
    rows_a = 2 * D_QK + D_MLSTM + ML_GATE_PAD

    @pl.when(pl.program_id(1) == 0)
    def _():
        tail_ref[...] = jnp.zeros_like(tail_ref)
        ct_ref[...] = jnp.zeros_like(ct_ref)
        n_ref[...] = jnp.zeros_like(n_ref)
        m_ref[...] = jnp.zeros_like(m_ref)

    def project(rows, piece):
        w = (wa_ref[0:2 * D_QK, :], wa_ref[2 * D_QK:rows_a, :], wo_ref[...])[piece]
        return lax.dot_general(u_ref[rows, :], w, NT, preferred_element_type=F32)

    groups = [slice(i, i + lc) for i in range(0, u_ref.shape[0], lc)]
    cur = [project(groups[0], piece) for piece in range(3)]
    for gi, rows in enumerate(groups):
        nxt = []

        def issue_next(piece, gi=gi, nxt=nxt):
            if gi + 1 < len(groups):
                nxt.append(project(groups[gi + 1], piece))

        _mlstm_chunk(cur[0], cur[1], cur[2], rows, issue_next,
                     wconv_ref, bconv_ref, gbias_ref, yb_ref, tail_ref, ct_ref, n_ref, m_ref)
        cur = nxt


def _mlstm_chunk(zqk, zvg, o_all, rows, issue_next,
                 wconv_ref, bconv_ref, gbias_ref, yb_ref, tail_ref, ct_ref, n_ref, m_ref):
    lc = zqk.shape[0]
    dqk, dv, nh = MLSTM_DQK, MLSTM_DV, MLSTM_HEADS

    ext = jnp.concatenate([tail_ref[...], zqk], axis=0)
    tail_ref[...] = zqk[lc - 8:lc, :]
    conv = bconv_ref[...] + wconv_ref[CONV_K - 1:CONV_K, :] * zqk
    for d in range(1, CONV_K):
        conv = conv + wconv_ref[CONV_K - 1 - d:CONV_K - d, :] * ext[8 - d:8 - d + lc, :]
    qk = conv * _sigmoid(conv)
    q_all = qk[:, 0:D_QK]
    k_all = qk[:, D_QK:] * (dqk ** -0.5)
    v_all = zvg[:, 0:D_MLSTM]

    gz = zvg[:, D_MLSTM:] + gbias_ref[...]
    lf_col = jnp.minimum(gz, 0.0) - jnp.log(1.0 + jnp.exp(-jnp.abs(gz)))
    rr, cc = _iota2((lc, lc), 0), _iota2((lc, lc), 1)
    causal = cc <= rr
    tri = jnp.where(causal, 1.0, 0.0).astype(BF16)
    b_col = _mdot(tri, lf_col, 1, 3)
    sel = jnp.where(_iota2((8, LANES), 0) == _iota2((8, LANES), 1), 1.0, 0.0).astype(BF16)
    g_row = _mdot(sel, gz, 1, 3, NT)
    lf_row = jnp.minimum(g_row, 0.0) - jnp.log(1.0 + jnp.exp(-jnp.abs(g_row)))
    b_row = _mdot(lf_row, tri, 3, 1, NT)

    heads = range(nh)
    qf = [q_all[:, h * dqk:(h + 1) * dqk] for h in heads]
    q = [t.astype(BF16) for t in qf]
    k = [k_all[:, h * dqk:(h + 1) * dqk] for h in heads]
    v = [v_all[:, h * dv:(h + 1) * dv].astype(BF16) for h in heads]
    bc = [b_col[:, nh + h:nh + h + 1] for h in heads]
    ic = [gz[:, h:h + 1] for h in heads]
    br = [b_row[nh + h:nh + h + 1, :] for h in heads]
    ir = [g_row[h:h + 1, :] for h in heads]
    g_tot = [b_col[lc - 1:lc, nh + h:nh + h + 1] for h in heads]
    m_prev = [m_ref[0:1, h:h + 1] for h in heads]
    ct = [ct_ref[h] for h in heads]
    nvec = [n_ref[h:h + 1, :] for h in heads]

    issue_next(0)
    qk = [_mdot(q[h], k[h], dims=NT) for h in heads]
    qc = [_mdot(q[h], ct[h]) for h in heads]
    bri = [br[h] - ir[h] for h in heads]
    dm = [jnp.where(causal, bc[h] - bri[h], -jnp.inf) for h in heads]
    inter = [bc[h] + m_prev[h] for h in heads]
    mt = [jnp.maximum(inter[h], jnp.max(dm[h], axis=-1, keepdims=True)) for h in heads]
    sc = [qk[h] * jnp.exp(dm[h] - mt[h]) for h in heads]
    winter = [jnp.exp(inter[h] - mt[h]) for h in heads]
    issue_next(1)
    sv = [_mdot(sc[h], v[h]) for h in heads]
    m_new = [jnp.maximum(g_tot[h] + m_prev[h], jnp.max(g_tot[h] - bri[h], axis=-1, keepdims=True))
             for h in heads]
    kw = [k[h] * jnp.exp(g_tot[h] - bc[h] + ic[h] - m_new[h]) for h in heads]
    kv = [_mdot(kw[h], v[h], dims=TN) for h in heads]
    issue_next(2)
    for h in heads:
        num = winter[h] * qc[h] + sv[h]
        den = (winter[h] * jnp.sum(qf[h] * nvec[h], axis=-1, keepdims=True)
               + jnp.sum(sc[h], axis=-1, keepdims=True))
        hout = num / jnp.maximum(jnp.abs(den), jnp.exp(-mt[h]))
        wc = jnp.exp(g_tot[h] + m_prev[h] - m_new[h])
        ct_ref[h] = wc * ct[h] + kv[h]
        n_ref[h:h + 1, :] = wc * nvec[h] + jnp.sum(kw[h], axis=0, keepdims=True)
        m_ref[0:1, h:h + 1] = m_new[h]
        o = o_all[:, h * dv:(h + 1) * dv]
        yb_ref[rows, h * dv:(h + 1) * dv] = (_sigmoid(o) * hout).astype(yb_ref.dtype)


def _mlstm(u, wt_a, wt_o, w_conv, b_conv, gbias, batch, seq):
    m, k = u.shape
    tb = MLSTM_TB
    per_b = seq // tb
    const = lambda b, j: (0, 0)
    assert wt_a.shape[0] >= 2 * D_QK + D_MLSTM + ML_GATE_PAD and wt_o.shape == (D_MLSTM, k)
    resident = lambda w: pl.BlockSpec(w.shape, const, pipeline_mode=pl.Buffered(1))
    return pl.pallas_call(
        _mlstm_kernel,
        out_shape=jax.ShapeDtypeStruct((m, D_MLSTM), BF16),
        grid=(batch, per_b),
        in_specs=[pl.BlockSpec((tb, k), lambda b, j: (b * per_b + j, 0)),
                  resident(wt_a), resident(wt_o),
                  pl.BlockSpec((CONV_K, 2 * D_QK), const),
                  pl.BlockSpec((1, 2 * D_QK), const),
                  pl.BlockSpec((1, ML_GATE_PAD), const)],
        out_specs=pl.BlockSpec((tb, D_MLSTM), lambda b, j: (b * per_b + j, 0)),
        scratch_shapes=[pltpu.VMEM((8, 2 * D_QK), F32),
                        pltpu.VMEM((MLSTM_HEADS, MLSTM_DQK, MLSTM_DV), F32),
                        pltpu.VMEM((8, MLSTM_DQK), F32),
                        pltpu.VMEM((8, LANES), F32)],
        compiler_params=_cparams(("arbitrary", "arbitrary")),
        name="mlstm",
    )(u, wt_a, wt_o, w_conv, b_conv, gbias)


def _merge_kernel(ya_ref, yb_ref, zg_ref, wa_ref, wb_ref, o_ref):
    groups = [slice(i, i + EPILOGUE_ROWS) for i in range(0, ya_ref.shape[0], EPILOGUE_ROWS)]
    pab = [(jnp.dot(ya_ref[rows, :], wa_ref[...], preferred_element_type=F32),
            jnp.dot(yb_ref[rows, :], wb_ref[...], preferred_element_type=F32)) for rows in groups]
    for (pa, pb), rows in zip(pab, groups):
        ga = _sigmoid(zg_ref[rows, 0:D_MODEL].astype(F32))
        gb = _sigmoid(zg_ref[rows, D_MODEL:].astype(F32))
        o_ref[rows, :] = (ga * pa + gb * pb).astype(o_ref.dtype)


def _merge(ya, yb, zg, wa, wb):
    m = ya.shape[0]
    tm = 512
    const = lambda i: (0, 0)
    return pl.pallas_call(
        _merge_kernel,
        out_shape=jax.ShapeDtypeStruct((m, D_MODEL), BF16),
        grid=(m // tm,),
        in_specs=[pl.BlockSpec((tm, D_RWKV), lambda i: (i, 0)),
                  pl.BlockSpec((tm, D_MLSTM), lambda i: (i, 0)),
                  pl.BlockSpec((tm, 2 * D_MODEL), lambda i: (i, 0)),
                  pl.BlockSpec((D_RWKV, D_MODEL), const),
                  pl.BlockSpec((D_MLSTM, D_MODEL), const)],
        out_specs=pl.BlockSpec((tm, D_MODEL), lambda i: (i, 0)),
        compiler_params=_cparams(("arbitrary",)),
        name="merge",
    )(ya, yb, zg, wa, wb)


def _outproj_kernel(mg_ref, w_ref, x_ref, ada_ref, g_ref, b_ref, x1_ref, u2_ref):
    groups = [slice(i, i + EPILOGUE_ROWS) for i in range(0, mg_ref.shape[0], EPILOGUE_ROWS)]
    ys = [jnp.dot(mg_ref[rows, :], w_ref[...], preferred_element_type=F32) for rows in groups]
    g1 = ada_ref[0, 2:3, :]
    sh2, sc2 = ada_ref[0, 3:4, :], ada_ref[0, 4:5, :]
    for y, rows in zip(ys, groups):
        x1 = _layernorm(ALPHA * x_ref[rows, :] + g1 * y) * g_ref[...] + b_ref[...]
        x1_ref[rows, :] = x1
        u2_ref[rows, :] = (_layernorm(x1) * (1.0 + sc2) + sh2).astype(u2_ref.dtype)


def _outproj(merged, w_out, x2, ada3, ln_g, ln_b, seq):
    m = merged.shape[0]
    tm = 512
    per_b = seq // tm
    const = lambda i: (0, 0)
    row = pl.BlockSpec((tm, D_MODEL), lambda i: (i, 0))
    return pl.pallas_call(
        _outproj_kernel,
        out_shape=[jax.ShapeDtypeStruct((m, D_MODEL), F32), jax.ShapeDtypeStruct((m, D_MODEL), BF16)],
        grid=(m // tm,),
        in_specs=[row, pl.BlockSpec((D_MODEL, D_MODEL), const, pipeline_mode=pl.Buffered(1)), row,
                  pl.BlockSpec((1, 6, D_MODEL), lambda i: (i // per_b, 0, 0)),
                  pl.BlockSpec((1, D_MODEL), const), pl.BlockSpec((1, D_MODEL), const)],
        out_specs=[row, row],
        compiler_params=_cparams(("arbitrary",)),
        name="outproj",
    )(merged, w_out, x2, ada3, ln_g, ln_b)


def _ff1_kernel(x_ref, w_ref, b_ref, w2_ref, o_ref, w2b_ref, wb_ref):
    @pl.when(pl.program_id(1) == 0)
    def _():
        wb_ref[...] = w_ref[...].astype(wb_ref.dtype)

    w2b_ref[...] = w2_ref[...].astype(w2b_ref.dtype)
    groups = [slice(i, i + 2 * EPILOGUE_ROWS) for i in range(0, x_ref.shape[0], 2 * EPILOGUE_ROWS)]
    hs = [jnp.dot(x_ref[rows, :], wb_ref[...], preferred_element_type=F32) for rows in groups]
    for h, rows in zip(hs, groups):
        h = jnp.maximum(h + b_ref[...], 0.0)
        o_ref[rows, :] = (h * h).astype(o_ref.dtype)


def _ff1(u2, w1, b1, w2):
    m, k = u2.shape
    n = w1.shape[1]
    tm, tn = 1024, 1024
    n_m = m // tm
    rows2 = w2.shape[0] // ((n // tn) * n_m)
    assert rows2 % 16 == 0 and rows2 * (n // tn) * n_m == w2.shape[0]
    slab = pl.BlockSpec((rows2, w2.shape[1]), lambda j, i: (j * n_m + i, 0))
    return pl.pallas_call(
        _ff1_kernel,
        out_shape=[jax.ShapeDtypeStruct((m, n), BF16), jax.ShapeDtypeStruct(w2.shape, BF16)],
        grid=(n // tn, n_m),
        in_specs=[pl.BlockSpec((tm, k), lambda j, i: (i, 0)),
                  pl.BlockSpec((k, tn), lambda j, i: (0, j)),
                  pl.BlockSpec((1, tn), lambda j, i: (0, j)),
                  slab],
        out_specs=[pl.BlockSpec((tm, tn), lambda j, i: (i, j)), slab],
        scratch_shapes=[pltpu.VMEM((k, tn), BF16)],
        compiler_params=_cparams(("arbitrary", "arbitrary")),
        name="ff1",
    )(u2, w1, b1, w2)


def _ff2_kernel(h_ref, w_ref, b_ref, x1_ref, ada_ref, g_ref, bb_ref, o_ref):
    y2 = jnp.dot(h_ref[...], w_ref[...], preferred_element_type=F32) + b_ref[...]
    g2 = ada_ref[0, 5:6, :]
    o_ref[...] = (_layernorm(ALPHA * x1_ref[...] + g2 * y2) * g_ref[...] + bb_ref[...]).astype(o_ref.dtype)


def _ff2(h, w2, b2, x1, ada3, ln_g, ln_b, seq, out_dtype):
    m, k = h.shape
    tm = 256
    per_b = seq // tm
    const = lambda i: (0, 0)
    row = pl.BlockSpec((tm, D_MODEL), lambda i: (i, 0))
    return pl.pallas_call(
        _ff2_kernel,
        out_shape=jax.ShapeDtypeStruct((m, D_MODEL), out_dtype),
        grid=(m // tm,),
        in_specs=[pl.BlockSpec((tm, k), lambda i: (i, 0)),
                  pl.BlockSpec((k, D_MODEL), const, pipeline_mode=pl.Buffered(1)),
                  pl.BlockSpec((1, D_MODEL), const), row,
                  pl.BlockSpec((1, 6, D_MODEL), lambda i: (i // per_b, 0, 0)),
                  pl.BlockSpec((1, D_MODEL), const), pl.BlockSpec((1, D_MODEL), const)],
        out_specs=row,
        compiler_params=_cparams(("arbitrary",)),
        name="ff2",
    )(h, w2, b2, x1, ada3, ln_g, ln_b)


def _pad_rows(w, height):
    return jnp.pad(w, ((0, height - w.shape[0]), (0, 0)))


def _layer(x2, c, batch, seq, w_ada, b_ada, w_in, mu_shift, w0, w_decay_up, a0, w_a_up, w_g_up, k_k, k_a, r_k,
           gn_g, gn_b, w_conv, b_conv, b_igate, b_fgate, w_branch_a, w_branch_b, w_out,
           ln1_g, ln1_b, w_ff1, b_ff1, w_ff2, b_ff2, ln2_g, ln2_b):
    row = lambda v: v.reshape(1, -1)

    ada3 = _ada(c, w_ada, b_ada).reshape(batch, 6, D_MODEL)
    u = _lnmod(x2, ada3, seq)

    assert w_in.shape == (D_MODEL, N_RWKV_COLS + N_MLSTM_COLS + 2 * D_MODEL)
    w_in_t = jnp.swapaxes(w_in, 0, 1)
    col_o = N_RWKV_COLS + 2 * D_QK + D_MLSTM + 2 * MLSTM_HEADS
    z_gt, w_a_b, w_b_b, w_out_b, wt_rw_b, wt_ma_b, wt_mo_b = _win_matmul(
        u, w_in_t, N_RWKV_COLS + N_MLSTM_COLS, 2 * D_MODEL, 1024, 1024, BF16, "win_gate",
        cast_along=((w_branch_a, 0, D_RWKV), (w_branch_b, 0, D_MLSTM), (w_out, 0, D_MODEL),
                    (w_in_t, 0, RW_COLS), (w_in_t, N_RWKV_COLS, ML_ROWS_A), (w_in_t, col_o, D_MLSTM)))

    mu_rw = jnp.pad(mu_shift, (0, RW_COLS - N_RWKV_COLS))
    prep = _rwkv_prep(u, wt_rw_b, row(mu_rw), row(w0), row(a0), row(k_k), row(k_a), row(r_k),
                      _pad_rows(w_decay_up, LORA_PAD), _pad_rows(w_a_up, LORA_PAD), w_g_up, batch, seq)
    r, ld, kt, v, kkn, bb, g, bonus = prep
    y0, q, a_mat, c_mat = _rwkv_chunk(r, ld, kt, v, kkn, bb, batch, seq)
    ya = _rwkv_scan(y0, q, a_mat, c_mat, g, bonus, row(gn_g), row(gn_b), batch, seq)

    gbias = jnp.pad(jnp.concatenate([b_igate, b_fgate]), (0, ML_GATE_PAD - 2 * MLSTM_HEADS))
    yb = _mlstm(u, wt_ma_b, wt_mo_b, w_conv, row(b_conv), row(gbias), batch, seq)

    merged = _merge(ya, yb, z_gt, w_a_b, w_b_b)
    x1, u2 = _outproj(merged, w_out_b, x2, ada3, row(ln1_g), row(ln1_b), seq)

    h, w_ff2_b = _ff1(u2, w_ff1, row(b_ff1), w_ff2)
    return _ff2(h, w_ff2_b, row(b_ff2), x1, ada3, row(ln2_g), row(ln2_b), seq, F32)


def kernel(x, c, w_ada, b_ada, w_in, mu_shift, w0, w_decay_up, a0, w_a_up, w_g_up, k_k, k_a, r_k, gn_g, gn_b,
           w_conv, b_conv, b_igate, b_fgate, w_branch_a, w_branch_b, w_out, ln1_g, ln1_b, w_ff1, b_ff1,
           w_ff2, b_ff2, ln2_g, ln2_b):
    out_dtype = x.dtype
    batch, seq, d = x.shape
    assert d == D_MODEL and seq % 1024 == 0 and w_ada.shape[0] == DEPTH
    x2 = x.astype(F32).reshape(batch * seq, d)
    cf = c.astype(F32)
    for l in range(DEPTH):
        x2 = _layer(x2, cf, batch, seq, w_ada[l], b_ada[l], w_in[l], mu_shift[l], w0[l], w_decay_up[l], a0[l],
                    w_a_up[l], w_g_up[l], k_k[l], k_a[l], r_k[l].reshape(-1), gn_g[l], gn_b[l], w_conv[l],
                    b_conv[l], b_igate[l], b_fgate[l], w_branch_a[l], w_branch_b[l], w_out[l], ln1_g[l],
                    ln1_b[l], w_ff1[l], b_ff1[l], w_ff2[l], b_ff2[l], ln2_g[l], ln2_b[l])
    return x2.reshape(batch, seq, d).astype(out_dtype)
```

```python
import functools
import math

import jax
import jax.numpy as jnp
from jax import lax
from jax.experimental import pallas as pl
from jax.experimental.pallas import tpu as pltpu

F32 = jnp.float32
BF16 = jnp.bfloat16

D_MODEL = 2048
DEPTH = 1
D_RWKV = D_MODEL // 2
RWKV_HEAD = 64
DECAY_LORA = 96
A_LORA = 96
GATE_LORA = 256
RWKV_GN_EPS = 64e-5
D_MLSTM = D_MODEL // 2
MLSTM_HEADS = 4
MLSTM_DV = D_MLSTM // MLSTM_HEADS
MLSTM_DQK = MLSTM_DV // 2
D_QK = MLSTM_HEADS * MLSTM_DQK
CONV_K = 4
D_FF = 4 * D_MODEL
LN_EPS = 1e-5
ALPHA = (2.0 * DEPTH) ** 0.25

LANES = 128
LORA_PAD = 128
N_RWKV_COLS = 3 * D_RWKV + DECAY_LORA + A_LORA + GATE_LORA
N_MLSTM_COLS = 2 * D_QK + 2 * D_MLSTM + 2 * MLSTM_HEADS
RW_COLS = 3584
ML_GATE_PAD = 128
ML_ROWS_A = 2560
RWKV_CHUNK = 64
RWKV_TB = 512
RWKV_SCAN_ROWS = 256
RWKV_CORE_TB = 1024
MLSTM_CHUNK = 256
MLSTM_TB = 1024
EPILOGUE_ROWS = 256
VMEM_LIMIT = 56 * 1024 * 1024

NN = (((1,), (0,)), ((), ()))
NT = (((1,), (1,)), ((), ()))
TN = (((0,), (0,)), ((), ()))


def _cparams(sem):
    return pltpu.CompilerParams(dimension_semantics=sem, vmem_limit_bytes=VMEM_LIMIT)


def _bf16_parts(x, n):
    if x.dtype == BF16:
        return [x]
    parts, rem = [], x
    for i in range(n):
        p = rem.astype(BF16)
        parts.append(p)
        if i + 1 < n:
            rem = rem - p.astype(F32)
    return parts


def _mdot(a, b, na=1, nb=1, dims=NN):
    ap, bp = _bf16_parts(a, na), _bf16_parts(b, nb)
    order = max(len(ap), len(bp))
    acc = None
    for i, x in enumerate(ap):
        for j, y in enumerate(bp):
            if i + j < order:
                t = lax.dot_general(x, y, dims, preferred_element_type=F32)
                acc = t if acc is None else acc + t
    return acc


def _sigmoid(x):
    return 1.0 / (1.0 + jnp.exp(-x))


def _layernorm(x):
    mu = jnp.mean(x, axis=-1, keepdims=True)
    xc = x - mu
    var = jnp.mean(xc * xc, axis=-1, keepdims=True)
    return xc * lax.rsqrt(var + LN_EPS)


def _iota2(shape, dim):
    return lax.broadcasted_iota(jnp.int32, shape, dim)


def _head_block_ones(n, head):
    r, c = _iota2((n, n), 0), _iota2((n, n), 1)
    return jnp.where((r // head) == (c // head), 1.0, 0.0).astype(BF16)


def _head_sum(x, ones_blk):
    return _mdot(x, ones_blk)


def _ada_kernel(ct_ref, w_ref, b_ref, o_ref, sb_ref):
    tn = w_ref.shape[1]
    nb = sb_ref.shape[0]

    @pl.when(pl.program_id(0) == 0)
    def _():
        cv = ct_ref[...]
        sv = cv * _sigmoid(cv)
        for b in range(nb):
            sb_ref[b] = jnp.broadcast_to(sv[:, b:b + 1], sb_ref.shape[1:])

    ntile = tn // LANES

    def body(i, acc):
        r0 = pl.multiple_of(i * 8, 8)
        sb = [sb_ref[b, pl.ds(r0, 8), :] for b in range(nb)]
        wv = [w_ref[pl.ds(r0, 8), t * LANES:(t + 1) * LANES] for t in range(ntile)]
        return tuple(acc[b * ntile + t] + sb[b] * wv[t] for b in range(nb) for t in range(ntile))

    zero = jnp.zeros((8, LANES), F32)
    acc = lax.fori_loop(0, w_ref.shape[0] // 8, body, (zero,) * (nb * ntile), unroll=4)
    for b in range(nb):
        tot = jnp.concatenate([acc[b * ntile + t] for t in range(ntile)], axis=1)
        o_ref[b:b + 1, :] = jnp.sum(tot, axis=0, keepdims=True) + b_ref[...]


def _ada(c, w_ada, b_ada):
    batch = c.shape[0]
    assert batch == 2
    n = w_ada.shape[1]
    tn = 1536
    return pl.pallas_call(
        _ada_kernel,
        out_shape=jax.ShapeDtypeStruct((batch, n), F32),
        grid=(n // tn,),
        in_specs=[pl.BlockSpec((D_MODEL, batch), lambda j: (0, 0)),
                  pl.BlockSpec((D_MODEL, tn), lambda j: (0, j)),
                  pl.BlockSpec((1, tn), lambda j: (0, j))],
        out_specs=pl.BlockSpec((batch, tn), lambda j: (0, j)),
        scratch_shapes=[pltpu.VMEM((batch, D_MODEL, LANES), F32)],
        compiler_params=_cparams(("arbitrary",)),
        name="ada",
    )(c.T, w_ada, b_ada.reshape(1, n))


def _lnmod_kernel(x_ref, ada_ref, o_ref):
    xn = _layernorm(x_ref[...])
    sh, sc = ada_ref[0, 0:1, :], ada_ref[0, 1:2, :]
    o_ref[...] = (xn * (1.0 + sc) + sh).astype(o_ref.dtype)


def _lnmod(x2, ada3, seq):
    m = x2.shape[0]
    tb = 1024
    per_b = seq // tb
    return pl.pallas_call(
        _lnmod_kernel,
        out_shape=jax.ShapeDtypeStruct((m, D_MODEL), BF16),
        grid=(m // tb,),
        in_specs=[pl.BlockSpec((tb, D_MODEL), lambda i: (i, 0)),
                  pl.BlockSpec((1, 6, D_MODEL), lambda i: (i // per_b, 0, 0))],
        out_specs=pl.BlockSpec((tb, D_MODEL), lambda i: (i, 0)),
        compiler_params=_cparams(("arbitrary",)),
        name="lnmod",
    )(x2, ada3)


def _win_kernel(n_cast, x_ref, wt_ref, *refs):
    cast_in, o_ref, cast_out, wb_ref = refs[:n_cast], refs[n_cast], refs[n_cast + 1:-1], refs[-1]

    @pl.when(pl.program_id(1) == 0)
    def _():
        wb_ref[...] = wt_ref[...].astype(wb_ref.dtype)

    for src, dst in zip(cast_in, cast_out):
        dst[...] = src[...].astype(dst.dtype)
    o_ref[...] = lax.dot_general(x_ref[...], wb_ref[...], NT, preferred_element_type=F32).astype(o_ref.dtype)


def _win_matmul(x, wt, col0, n_out, tm, tn, out_dtype, name, cast_along=()):
    m, k = x.shape
    assert col0 % 8 == 0 and n_out % tn == 0 and col0 + n_out <= wt.shape[0]
    n_m = m // tm
    steps = (n_out // tn) * n_m

    def w_index(j, i):
        return pl.multiple_of(col0 + j * tn, 8), 0

    slabs_in, slabs_out, out_shapes = [], [], []
    for w, row0, nrows in cast_along:
        rows = nrows // steps
        assert rows % 16 == 0 and row0 % 8 == 0 and rows * steps == nrows and row0 + nrows <= w.shape[0]
        slabs_in.append(pl.BlockSpec(
            (pl.Element(rows), pl.Element(w.shape[1])),
            functools.partial(lambda j, i, row0, rows: (pl.multiple_of(row0 + (j * n_m + i) * rows, 8), 0),
                              row0=row0, rows=rows)))
        slabs_out.append(pl.BlockSpec((rows, w.shape[1]), lambda j, i: (j * n_m + i, 0)))
        out_shapes.append(jax.ShapeDtypeStruct((nrows, w.shape[1]), BF16))
    cast_along = [w for w, _, _ in cast_along]
    outs = pl.pallas_call(
        functools.partial(_win_kernel, len(cast_along)),
        out_shape=[jax.ShapeDtypeStruct((m, n_out), out_dtype)] + out_shapes,
        grid=(n_out // tn, n_m),
        in_specs=[pl.BlockSpec((tm, k), lambda j, i: (i, 0)),
                  pl.BlockSpec((pl.Element(tn), pl.Element(k)), w_index)] + slabs_in,
        out_specs=[pl.BlockSpec((tm, tn), lambda j, i: (i, j))] + slabs_out,
        scratch_shapes=[pltpu.VMEM((tn, k), BF16)],
        compiler_params=_cparams(("arbitrary", "arbitrary")),
        name=name,
    )(x, wt, *cast_along)
    return outs if cast_along else outs[0]


def _rwkv_prep_kernel(u_ref, wt_ref, mu_ref, w0_ref, a0_ref, kk_ref, ka_ref, rk_ref, wdu_ref, wau_ref, wgu_ref,
                      r_o, ld_o, kt_o, v_o, kkn_o, bb_o, g_o, bonus_o, carry_ref):
    tb = u_ref.shape[0]

    @pl.when(pl.program_id(1) == 0)
    def _():
        carry_ref[...] = jnp.zeros_like(carry_ref)

    groups = [slice(i, i + EPILOGUE_ROWS) for i in range(0, tb, EPILOGUE_ROWS)]
    zs_all = [lax.dot_general(u_ref[rows, :], wt_ref[...], NT, preferred_element_type=F32) for rows in groups]
    ones_blk = _head_block_ones(LANES, RWKV_HEAD)
    last_row = carry_ref[0:1, :]
    for z, rows in zip(zs_all, groups):
        prev = pltpu.roll(z, 1, 0)
        prev = jnp.where(_iota2(z.shape, 0) == 0, last_row, prev)
        last_row = z[EPILOGUE_ROWS - 1:EPILOGUE_ROWS, :]
        zs = z + (prev - z) * mu_ref[...]

        c = D_RWKV
        r, k, v = zs[:, 0:c], zs[:, c:2 * c], zs[:, 2 * c:3 * c]
        wd = zs[:, 3 * c:3 * c + LORA_PAD]
        ad = zs[:, 3 * c + DECAY_LORA:3 * c + DECAY_LORA + LORA_PAD]
        gd = zs[:, 3 * c + DECAY_LORA + A_LORA:3 * c + DECAY_LORA + A_LORA + GATE_LORA]

        ld = -math.exp(-0.5) * _sigmoid(w0_ref[...] + _mdot(jnp.tanh(wd), wdu_ref[...], 2, 2))
        a = _sigmoid(a0_ref[...] + _mdot(ad, wau_ref[...]))
        g = _mdot(_sigmoid(gd), wgu_ref[...], 1, 1)

        kq = k * kk_ref[...]
        kt = k * (1.0 + (a - 1.0) * ka_ref[...])
        sq = kq * kq
        bn = r * kt * rk_ref[...]
        ss = jnp.concatenate([_head_sum(sq[:, s:s + LANES], ones_blk) for s in range(0, c, LANES)], axis=1)
        bs = jnp.concatenate([_head_sum(bn[:, s:s + LANES], ones_blk) for s in range(0, c, LANES)], axis=1)
        kkn = kq * lax.rsqrt(ss + 1e-12)

        r_o[rows, :] = r.astype(r_o.dtype)
        ld_o[rows, :] = ld
        kt_o[rows, :] = kt.astype(kt_o.dtype)
        v_o[rows, :] = v.astype(v_o.dtype)
        kkn_o[rows, :] = kkn.astype(kkn_o.dtype)
        bb_o[rows, :] = (a * kkn).astype(bb_o.dtype)
        g_o[rows, :] = g.astype(g_o.dtype)
        bonus_o[rows, :] = (bs * v).astype(bonus_o.dtype)
    carry_ref[0:1, :] = last_row


def _rwkv_prep(u, wt_rw, mu_rw, w0, a0, k_k, k_a, r_k, wdu, wau, wgu, batch, seq):
    m, k = u.shape
    assert wt_rw.shape == (RW_COLS, k)
    tb = RWKV_TB
    per_b = seq // tb
    row = lambda b, j: (b * per_b + j, 0)
    const = lambda b, j: (0, 0)
    vec = pl.BlockSpec((1, D_RWKV), const)
    outs = [jax.ShapeDtypeStruct((m, D_RWKV), F32 if i == 1 else BF16) for i in range(8)]
    return pl.pallas_call(
        _rwkv_prep_kernel,
        out_shape=outs,
        grid=(batch, per_b),
        in_specs=[pl.BlockSpec((tb, k), row),
                  pl.BlockSpec((RW_COLS, k), const, pipeline_mode=pl.Buffered(1)),
                  pl.BlockSpec((1, RW_COLS), const),
                  vec, vec, vec, vec, vec,
                  pl.BlockSpec((LORA_PAD, D_RWKV), const), pl.BlockSpec((LORA_PAD, D_RWKV), const),
                  pl.BlockSpec((GATE_LORA, D_RWKV), const)],
        out_specs=[pl.BlockSpec((tb, D_RWKV), row)] * 8,
        scratch_shapes=[pltpu.VMEM((8, RW_COLS), F32)],
        compiler_params=_cparams(("arbitrary", "arbitrary")),
        name="rwkv_prep",
    )(u, wt_rw, mu_rw, w0, a0, k_k, k_a, r_k, wdu, wau, wgu)


def _rwkv_chunk_kernel(r_ref, ld_ref, kt_ref, v_ref, kk_ref, bb_ref, y0_ref, q_ref, a_ref, c_ref):
    tb = r_ref.shape[0]
    lc = RWKV_CHUNK
    n = 2 * lc
    nchunk = tb // lc

    lane = _iota2((1, LANES), 1)
    m0 = jnp.where(lane < RWKV_HEAD, 1.0, 0.0)
    m1 = 1.0 - m0
    ri, ci = _iota2((n, n), 0), _iota2((n, n), 1)
    same_head = (ri // lc) == (ci // lc)
    strict = jnp.where(same_head & (ri > ci), 1.0, 0.0)
    incl = jnp.where(same_head & (ri >= ci), 1.0, 0.0)
    eye = jnp.where(ri == ci, 1.0, 0.0)

    tw = 4 * lc
    rt, ct = _iota2((tw, tw), 0), _iota2((tw, tw), 1)
    tri = jnp.where(((rt // lc) == (ct // lc)) & (ct <= rt), 1.0, 0.0).astype(BF16)

    ld = ld_ref[...]
    cl = jnp.concatenate([_mdot(tri, ld[i:i + tw], 1, 3) for i in range(0, tb, tw)], axis=0)
    e_pos = jnp.exp(cl)
    e_neg = jnp.exp(-cl)
    a_hat = kk_ref[...] * jnp.exp(cl - ld)
    b_hat = bb_ref[...] * e_neg
    k_hat = kt_ref[...] * e_neg
    r_hat = r_ref[...] * e_pos
    v_all = v_ref[...]

    def stack(x):
        return jnp.concatenate([x * m0, x * m1], axis=0)

    def bf(t):
        return t.astype(BF16)

    head0 = _iota2((lc, LANES), 1) < RWKV_HEAD

    def stack_b(x):
        xb, zero = x.astype(BF16), jnp.zeros((), BF16)
        return jnp.concatenate([jnp.where(head0, xb, zero), jnp.where(head0, zero, xb)], axis=0)

    chunks = range(nchunk)
    rows = [slice(c * lc, (c + 1) * lc) for c in chunks]
    a_b, b_b, k_b, v_b = ([stack_b(x[rows[c]]) for c in chunks] for x in (a_hat, b_hat, k_hat, v_all))
    r_st = [stack(r_hat[rows[c]]) for c in chunks]
    p = [_mdot(jnp.concatenate([a_b[c], bf(r_st[c])], axis=0),
               jnp.concatenate([b_b[c], k_b[c]], axis=0), dims=NT) for c in chunks]
    mab = [p[c][0:n, 0:n] * strict for c in chunks]
    mab_b = [bf(t) for t in mab]
    mak_b = [bf(p[c][0:n, n:] * strict) for c in chunks]
    mrb_b = [bf(p[c][n:, 0:n] * incl) for c in chunks]
    mrk_b = [bf(p[c][n:, n:] * incl) for c in chunks]

    x = [eye - jnp.where((ri // 2 == ci // 2), mab[c], 0.0) for c in chunks]
    w1 = [_mdot(mak_b[c], v_b[c]) for c in chunks]
    zero_b = jnp.zeros((), BF16)
    s = 2
    while s < lc:
        lvl = (ri // (2 * s) == ci // (2 * s)) & (ri // s != ci // s)
        x_b = [bf(t) for t in x]
        nx = [_mdot(jnp.where(lvl, mab_b[c], zero_b), x_b[c]) for c in chunks]
        x = [x[c] - _mdot(x_b[c], nx[c]) for c in chunks]
        s *= 2

    ta = [_mdot(x[c], jnp.concatenate([bf(w1[c]), a_b[c]], axis=1)) for c in chunks]
    zeros_b = jnp.zeros((n, LANES), BF16)
    ugv_b = [jnp.concatenate([bf(jnp.concatenate([-ta[c][:, 0:LANES], ta[c][:, LANES:]], axis=1)),
                              jnp.concatenate([v_b[c], zeros_b], axis=1)], axis=0) for c in chunks]
    ru = [_mdot(jnp.concatenate([mrb_b[c], mrk_b[c]], axis=1), ugv_b[c]) for c in chunks]
    y0 = [ru[c][:, 0:LANES] for c in chunks]
    q = [r_st[c] - ru[c][:, LANES:] for c in chunks]
    pl_row = [e_pos[(c + 1) * lc - 1:(c + 1) * lc, :] for c in chunks]
    bk_p = [jnp.concatenate([stack_b(b_hat[rows[c]] * pl_row[c]), stack_b(k_hat[rows[c]] * pl_row[c])], axis=0)
            for c in chunks]
    bu = [_mdot(bk_p[c], ugv_b[c], dims=TN) for c in chunks]
    ct = [bu[c][:, 0:LANES] for c in chunks]
    at = [eye * pl_row[c] - bu[c][:, LANES:] for c in chunks]

    for c in chunks:
        rows = slice(c * lc, (c + 1) * lc)
        y0_ref[rows, :] = y0[c][0:lc] + y0[c][lc:]
        q_ref[rows, :] = (q[c][0:lc] + q[c][lc:]).astype(q_ref.dtype)
        a_ref[0, c, 0] = (at[c][0:RWKV_HEAD] + at[c][RWKV_HEAD:]).astype(a_ref.dtype)
        c_ref[0, c, 0] = ct[c][0:RWKV_HEAD] + ct[c][RWKV_HEAD:]


def _rwkv_chunk(r, ld, kt, v, kkn, bb, batch, seq):
    m = r.shape[0]
    tb = RWKV_CORE_TB
    per_b = seq // tb
    cpb = tb // RWKV_CHUNK
    npair = D_RWKV // LANES
    slab = pl.BlockSpec((tb, LANES), lambda b, p, j: (b * per_b + j, p))
    mat = pl.BlockSpec((1, cpb, 1, RWKV_HEAD, LANES), lambda b, p, j: (b, j, p, 0, 0))
    mshape = (batch, seq // RWKV_CHUNK, npair, RWKV_HEAD, LANES)
    return pl.pallas_call(
        _rwkv_chunk_kernel,
        out_shape=[jax.ShapeDtypeStruct((m, D_RWKV), F32), jax.ShapeDtypeStruct((m, D_RWKV), BF16),
                   jax.ShapeDtypeStruct(mshape, BF16), jax.ShapeDtypeStruct(mshape, F32)],
        grid=(batch, npair, per_b),
        in_specs=[slab] * 6,
        out_specs=[slab, slab, mat, mat],
        compiler_params=_cparams(("arbitrary", "arbitrary", "arbitrary")),
        name="rwkv_chunk",
    )(r, ld, kt, v, kkn, bb)


def _rwkv_scan_kernel(y0_ref, q_ref, a_ref, c_ref, g_ref, bonus_ref, gng_ref, gnb_ref, ya_ref, s_ref):
    nb, rows, _ = y0_ref.shape
    lc = RWKV_CHUNK
    npair = D_RWKV // LANES

    @pl.when(pl.program_id(0) == 0)
    def _():
        s_ref[...] = jnp.zeros_like(s_ref)

    ones_blk = _head_block_ones(LANES, RWKV_HEAD)
    head0 = _iota2((RWKV_HEAD, LANES), 1) < RWKV_HEAD

    def block_diag(t):
        zero = jnp.zeros((), t.dtype)
        return jnp.concatenate([jnp.where(head0, t, zero), jnp.where(head0, zero, t)], axis=0)

    streams = [(b, p) for b in range(nb) for p in range(npair)]
    for ci in range(rows // lc):
        rs = slice(ci * lc, (ci + 1) * lc)
        state_b = [s_ref[b * npair + p].astype(BF16) for b, p in streams]
        y = [y0_ref[b, rs, p * LANES:(p + 1) * LANES]
             + _mdot(q_ref[b, rs, p * LANES:(p + 1) * LANES], state_b[i])
             for i, (b, p) in enumerate(streams)]
        for i, (b, p) in enumerate(streams):
            s_ref[b * npair + p] = (_mdot(block_diag(a_ref[b, ci, p]), state_b[i])
                                    + block_diag(c_ref[b, ci, p]))
        mu = [_head_sum(t, ones_blk) * (1.0 / RWKV_HEAD) for t in y]
        yc = [y[i] - mu[i] for i in range(len(streams))]
        var = [_head_sum(t * t, ones_blk) * (1.0 / RWKV_HEAD) for t in yc]
        for i, (b, p) in enumerate(streams):
            ls = slice(p * LANES, (p + 1) * LANES)
            yn = yc[i] * lax.rsqrt(var[i] + RWKV_GN_EPS) * gng_ref[:, ls] + gnb_ref[:, ls]
            ya_ref[b, rs, ls] = ((yn + bonus_ref[b, rs, ls]) * g_ref[b, rs, ls]).astype(ya_ref.dtype)


def _rwkv_scan(y0, q, a_mat, c_mat, g, bonus, gn_g, gn_b, batch, seq):
    rows = RWKV_SCAN_ROWS
    npair = D_RWKV // LANES
    tok = pl.BlockSpec((batch, rows, D_RWKV), lambda j: (0, j, 0))
    mat = pl.BlockSpec((batch, rows // RWKV_CHUNK, npair, RWKV_HEAD, LANES), lambda j: (0, j, 0, 0, 0))
    vec = pl.BlockSpec((1, D_RWKV), lambda j: (0, 0))
    shape3 = (batch, seq, D_RWKV)
    return pl.pallas_call(
        _rwkv_scan_kernel,
        out_shape=jax.ShapeDtypeStruct(shape3, BF16),
        grid=(seq // rows,),
        in_specs=[tok, tok, mat, mat, tok, tok, vec, vec],
        out_specs=tok,
        scratch_shapes=[pltpu.VMEM((batch * npair, LANES, LANES), F32)],
        compiler_params=_cparams(("arbitrary",)),
        name="rwkv_scan",
    )(y0.reshape(shape3), q.reshape(shape3), a_mat, c_mat, g.reshape(shape3), bonus.reshape(shape3),
      gn_g, gn_b).reshape(batch * seq, D_RWKV)


def _mlstm_kernel(u_ref, un_ref, wa_ref, wo_ref, wconv_ref, bconv_ref, gbias_ref, yb_ref,
                  tail_ref, ct_ref, n_ref, m_ref, p0_ref, p1_ref, p2_ref):
    lc = MLSTM_CHUNK
    rows_a = 2 * D_QK + D_MLSTM + ML_GATE_PAD
    pieces = (p0_ref, p1_ref, p2_ref)

    @pl.when(pl.program_id(1) == 0)
    def _():
        tail_ref[...] = jnp.zeros_like(tail_ref)
        ct_ref[...] = jnp.zeros_like(ct_ref)
        n_ref[...] = jnp.zeros_like(n_ref)
        m_ref[...] = jnp.zeros_like(m_ref)

    def project(x, piece):
        w = (wa_ref[0:2 * D_QK, :], wa_ref[2 * D_QK:rows_a, :], wo_ref[...])[piece]
        return lax.dot_general(x, w, NT, preferred_element_type=F32)

    groups = [slice(i, i + lc) for i in range(0, u_ref.shape[0], lc)]

    @pl.when((pl.program_id(0) == 0) & (pl.program_id(1) == 0))
    def _():
        for piece in range(3):
            pieces[piece][...] = project(u_ref[groups[0], :], piece)

    cur = [p[...] for p in pieces]
    for gi, rows in enumerate(groups):
        nxt = []

        def issue_next(piece, gi=gi, nxt=nxt):
            if gi + 1 < len(groups):
                nxt.append(project(u_ref[groups[gi + 1], :], piece))
            else:
                pieces[piece][...] = project(un_ref[...], piece)

        _mlstm_chunk(cur[0], cur[1], cur[2], rows, issue_next,
                     wconv_ref, bconv_ref, gbias_ref, yb_ref, tail_ref, ct_ref, n_ref, m_ref)
        cur = nxt


def _mlstm_chunk(zqk, zvg, o_all, rows, issue_next,
                 wconv_ref, bconv_ref, gbias_ref, yb_ref, tail_ref, ct_ref, n_ref, m_ref):
    lc = zqk.shape[0]
    dqk, dv, nh = MLSTM_DQK, MLSTM_DV, MLSTM_HEADS

    ext = jnp.concatenate([tail_ref[...], zqk], axis=0)
    tail_ref[...] = zqk[lc - 8:lc, :]
    conv = bconv_ref[...] + wconv_ref[CONV_K - 1:CONV_K, :] * zqk
    for d in range(1, CONV_K):
        conv = conv + wconv_ref[CONV_K - 1 - d:CONV_K - d, :] * ext[8 - d:8 - d + lc, :]
    qk = conv * _sigmoid(conv)
    q_all = qk[:, 0:D_QK]
    k_all = qk[:, D_QK:] * (dqk ** -0.5)
    v_all = zvg[:, 0:D_MLSTM]

    gz = zvg[:, D_MLSTM:] + gbias_ref[...]
    lf_col = jnp.minimum(gz, 0.0) - jnp.log(1.0 + jnp.exp(-jnp.abs(gz)))
    rr, cc = _iota2((lc, lc), 0), _iota2((lc, lc), 1)
    causal = cc <= rr
    tri = jnp.where(causal, 1.0, 0.0).astype(BF16)
    b_col = _mdot(tri, lf_col, 1, 3)
    sel = jnp.where(_iota2((8, LANES), 0) == _iota2((8, LANES), 1), 1.0, 0.0).astype(BF16)
    g_row = _mdot(sel, gz, 1, 3, NT)
    lf_row = jnp.minimum(g_row, 0.0) - jnp.log(1.0 + jnp.exp(-jnp.abs(g_row)))
    b_row = _mdot(lf_row, tri, 3, 1, NT)

    heads = range(nh)
    qf = [q_all[:, h * dqk:(h + 1) * dqk] for h in heads]
    q = [t.astype(BF16) for t in qf]
    k = [k_all[:, h * dqk:(h + 1) * dqk] for h in heads]
    v = [v_all[:, h * dv:(h + 1) * dv].astype(BF16) for h in heads]
    bc = [b_col[:, nh + h:nh + h + 1] for h in heads]
    ic = [gz[:, h:h + 1] for h in heads]
    br = [b_row[nh + h:nh + h + 1, :] for h in heads]
    ir = [g_row[h:h + 1, :] for h in heads]
    g_tot = [b_col[lc - 1:lc, nh + h:nh + h + 1] for h in heads]
    m_prev = [m_ref[0:1, h:h + 1] for h in heads]
    ct = [ct_ref[h] for h in heads]
    nvec = [n_ref[h:h + 1, :] for h in heads]

    issue_next(0)
    qk = [_mdot(q[h], k[h], dims=NT) for h in heads]
    qc = [_mdot(q[h], ct[h]) for h in heads]
    bri = [br[h] - ir[h] for h in heads]
    dm = [jnp.where(causal, bc[h] - bri[h], -jnp.inf) for h in heads]
    inter = [bc[h] + m_prev[h] for h in heads]
    mt = [jnp.maximum(inter[h], jnp.max(dm[h], axis=-1, keepdims=True)) for h in heads]
    sc = [qk[h] * jnp.exp(dm[h] - mt[h]) for h in heads]
    winter = [jnp.exp(inter[h] - mt[h]) for h in heads]
    issue_next(1)
    sv = [_mdot(sc[h], v[h]) for h in heads]
    m_new = [jnp.maximum(g_tot[h] + m_prev[h], jnp.max(g_tot[h] - bri[h], axis=-1, keepdims=True))
             for h in heads]
    kw = [k[h] * jnp.exp(g_tot[h] - bc[h] + ic[h] - m_new[h]) for h in heads]
    kv = [_mdot(kw[h], v[h], dims=TN) for h in heads]
    issue_next(2)
    for h in heads:
        num = winter[h] * qc[h] + sv[h]
        den = (winter[h] * jnp.sum(qf[h] * nvec[h], axis=-1, keepdims=True)
               + jnp.sum(sc[h], axis=-1, keepdims=True))
        hout = num / jnp.maximum(jnp.abs(den), jnp.exp(-mt[h]))
        wc = jnp.exp(g_tot[h] + m_prev[h] - m_new[h])
        ct_ref[h] = wc * ct[h] + kv[h]
        n_ref[h:h + 1, :] = wc * nvec[h] + jnp.sum(kw[h], axis=0, keepdims=True)
        m_ref[0:1, h:h + 1] = m_new[h]
        o = o_all[:, h * dv:(h + 1) * dv]
        yb_ref[rows, h * dv:(h + 1) * dv] = (_sigmoid(o) * hout).astype(yb_ref.dtype)


def _mlstm(u, wt_a, wt_o, w_conv, b_conv, gbias, batch, seq):
    m, k = u.shape
    tb = MLSTM_TB
    per_b = seq // tb
    const = lambda b, j: (0, 0)
    assert wt_a.shape[0] >= 2 * D_QK + D_MLSTM + ML_GATE_PAD and wt_o.shape == (D_MLSTM, k)
    resident = lambda w: pl.BlockSpec(w.shape, const, pipeline_mode=pl.Buffered(1))
    lc = MLSTM_CHUNK
    cps = tb // lc
    next_first = lambda b, j: (jnp.minimum((b * per_b + j + 1) * cps, m // lc - 1), 0)
    rows_a = 2 * D_QK + D_MLSTM + ML_GATE_PAD
    return pl.pallas_call(
        _mlstm_kernel,
        out_shape=jax.ShapeDtypeStruct((m, D_MLSTM), BF16),
        grid=(batch, per_b),
        in_specs=[pl.BlockSpec((tb, k), lambda b, j: (b * per_b + j, 0)),
                  pl.BlockSpec((lc, k), next_first),
                  resident(wt_a), resident(wt_o),
                  pl.BlockSpec((CONV_K, 2 * D_QK), const),
                  pl.BlockSpec((1, 2 * D_QK), const),
                  pl.BlockSpec((1, ML_GATE_PAD), const)],
        out_specs=pl.BlockSpec((tb, D_MLSTM), lambda b, j: (b * per_b + j, 0)),
        scratch_shapes=[pltpu.VMEM((8, 2 * D_QK), F32),
                        pltpu.VMEM((MLSTM_HEADS, MLSTM_DQK, MLSTM_DV), F32),
                        pltpu.VMEM((8, MLSTM_DQK), F32),
                        pltpu.VMEM((8, LANES), F32),
                        pltpu.VMEM((lc, 2 * D_QK), F32),
                        pltpu.VMEM((lc, rows_a - 2 * D_QK), F32),
                        pltpu.VMEM((lc, D_MLSTM), F32)],
        compiler_params=_cparams(("arbitrary", "arbitrary")),
        name="mlstm",
    )(u, u, wt_a, wt_o, w_conv, b_conv, gbias)


def _merge_kernel(ya_ref, yb_ref, zg_ref, wa_ref, wb_ref, o_ref):
    groups = [slice(i, i + EPILOGUE_ROWS) for i in range(0, ya_ref.shape[0], EPILOGUE_ROWS)]
    pab = [(jnp.dot(ya_ref[rows, :], wa_ref[...], preferred_element_type=F32),
            jnp.dot(yb_ref[rows, :], wb_ref[...], preferred_element_type=F32)) for rows in groups]
    for (pa, pb), rows in zip(pab, groups):
        ga = _sigmoid(zg_ref[rows, 0:D_MODEL].astype(F32))
        gb = _sigmoid(zg_ref[rows, D_MODEL:].astype(F32))
        o_ref[rows, :] = (ga * pa + gb * pb).astype(o_ref.dtype)


def _merge(ya, yb, zg, wa, wb):
    m = ya.shape[0]
    tm = 512
    const = lambda i: (0, 0)
    return pl.pallas_call(
        _merge_kernel,
        out_shape=jax.ShapeDtypeStruct((m, D_MODEL), BF16),
        grid=(m // tm,),
        in_specs=[pl.BlockSpec((tm, D_RWKV), lambda i: (i, 0)),
                  pl.BlockSpec((tm, D_MLSTM), lambda i: (i, 0)),
                  pl.BlockSpec((tm, 2 * D_MODEL), lambda i: (i, 0)),
                  pl.BlockSpec((D_RWKV, D_MODEL), const),
                  pl.BlockSpec((D_MLSTM, D_MODEL), const)],
        out_specs=pl.BlockSpec((tm, D_MODEL), lambda i: (i, 0)),
        compiler_params=_cparams(("arbitrary",)),
        name="merge",
    )(ya, yb, zg, wa, wb)


def _outproj_kernel(mg_ref, w_ref, x_ref, ada_ref, g_ref, b_ref, x1_ref, u2_ref):
    groups = [slice(i, i + EPILOGUE_ROWS) for i in range(0, mg_ref.shape[0], EPILOGUE_ROWS)]
    ys = [jnp.dot(mg_ref[rows, :], w_ref[...], preferred_element_type=F32) for rows in groups]
    g1 = ada_ref[0, 2:3, :]
    sh2, sc2 = ada_ref[0, 3:4, :], ada_ref[0, 4:5, :]
    for y, rows in zip(ys, groups):
        x1 = _layernorm(ALPHA * x_ref[rows, :] + g1 * y) * g_ref[...] + b_ref[...]
        x1_ref[rows, :] = x1
        u2_ref[rows, :] = (_layernorm(x1) * (1.0 + sc2) + sh2).astype(u2_ref.dtype)


def _outproj(merged, w_out, x2, ada3, ln_g, ln_b, seq):
    m = merged.shape[0]
    tm = 512
    per_b = seq // tm
    const = lambda i: (0, 0)
    row = pl.BlockSpec((tm, D_MODEL), lambda i: (i, 0))
    return pl.pallas_call(
        _outproj_kernel,
        out_shape=[jax.ShapeDtypeStruct((m, D_MODEL), F32), jax.ShapeDtypeStruct((m, D_MODEL), BF16)],
        grid=(m // tm,),
        in_specs=[row, pl.BlockSpec((D_MODEL, D_MODEL), const, pipeline_mode=pl.Buffered(1)), row,
                  pl.BlockSpec((1, 6, D_MODEL), lambda i: (i // per_b, 0, 0)),
                  pl.BlockSpec((1, D_MODEL), const), pl.BlockSpec((1, D_MODEL), const)],
        out_specs=[row, row],
        compiler_params=_cparams(("arbitrary",)),
        name="outproj",
    )(merged, w_out, x2, ada3, ln_g, ln_b)


def _ff1_kernel(x_ref, w_ref, b_ref, w2_ref, o_ref, w2b_ref, wb_ref):
    @pl.when(pl.program_id(1) == 0)
    def _():
        wb_ref[...] = w_ref[...].astype(wb_ref.dtype)

    w2b_ref[...] = w2_ref[...].astype(w2b_ref.dtype)
    groups = [slice(i, i + 2 * EPILOGUE_ROWS) for i in range(0, x_ref.shape[0], 2 * EPILOGUE_ROWS)]
    hs = [jnp.dot(x_ref[rows, :], wb_ref[...], preferred_element_type=F32) for rows in groups]
    for h, rows in zip(hs, groups):
        h = jnp.maximum(h + b_ref[...], 0.0)
        o_ref[rows, :] = (h * h).astype(o_ref.dtype)


def _ff1(u2, w1, b1, w2):
    m, k = u2.shape
    n = w1.shape[1]
    tm, tn = 1024, 1024
    n_m = m // tm
    rows2 = w2.shape[0] // ((n // tn) * n_m)
    assert rows2 % 16 == 0 and rows2 * (n // tn) * n_m == w2.shape[0]
    slab = pl.BlockSpec((rows2, w2.shape[1]), lambda j, i: (j * n_m + i, 0))
    return pl.pallas_call(
        _ff1_kernel,
        out_shape=[jax.ShapeDtypeStruct((m, n), BF16), jax.ShapeDtypeStruct(w2.shape, BF16)],
        grid=(n // tn, n_m),
        in_specs=[pl.BlockSpec((tm, k), lambda j, i: (i, 0)),
                  pl.BlockSpec((k, tn), lambda j, i: (0, j)),
                  pl.BlockSpec((1, tn), lambda j, i: (0, j)),
                  slab],
        out_specs=[pl.BlockSpec((tm, tn), lambda j, i: (i, j)), slab],
        scratch_shapes=[pltpu.VMEM((k, tn), BF16)],
        compiler_params=_cparams(("arbitrary", "arbitrary")),
        name="ff1",
    )(u2, w1, b1, w2)


def _ff2_kernel(h_ref, w_ref, b_ref, x1_ref, ada_ref, g_ref, bb_ref, o_ref):
    y2 = jnp.dot(h_ref[...], w_ref[...], preferred_element_type=F32) + b_ref[...]
    g2 = ada_ref[0, 5:6, :]
    o_ref[...] = (_layernorm(ALPHA * x1_ref[...] + g2 * y2) * g_ref[...] + bb_ref[...]).astype(o_ref.dtype)


def _ff2(h, w2, b2, x1, ada3, ln_g, ln_b, seq, out_dtype):
    m, k = h.shape
    tm = 256
    per_b = seq // tm
    const = lambda i: (0, 0)
    row = pl.BlockSpec((tm, D_MODEL), lambda i: (i, 0))
    return pl.pallas_call(
        _ff2_kernel,
        out_shape=jax.ShapeDtypeStruct((m, D_MODEL), out_dtype),
        grid=(m // tm,),
        in_specs=[pl.BlockSpec((tm, k), lambda i: (i, 0)),
                  pl.BlockSpec((k, D_MODEL), const, pipeline_mode=pl.Buffered(1)),
                  pl.BlockSpec((1, D_MODEL), const), row,
                  pl.BlockSpec((1, 6, D_MODEL), lambda i: (i // per_b, 0, 0)),
                  pl.BlockSpec((1, D_MODEL), const), pl.BlockSpec((1, D_MODEL), const)],
        out_specs=row,
        compiler_params=_cparams(("arbitrary",)),
        name="ff2",
    )(h, w2, b2, x1, ada3, ln_g, ln_b)


def _pad_rows(w, height):
    return jnp.pad(w, ((0, height - w.shape[0]), (0, 0)))


def _layer(x2, c, batch, seq, w_ada, b_ada, w_in, mu_shift, w0, w_decay_up, a0, w_a_up, w_g_up, k_k, k_a, r_k,
           gn_g, gn_b, w_conv, b_conv, b_igate, b_fgate, w_branch_a, w_branch_b, w_out,
           ln1_g, ln1_b, w_ff1, b_ff1, w_ff2, b_ff2, ln2_g, ln2_b):
    row = lambda v: v.reshape(1, -1)

    ada3 = _ada(c, w_ada, b_ada).reshape(batch, 6, D_MODEL)
    u = _lnmod(x2, ada3, seq)

    assert w_in.shape == (D_MODEL, N_RWKV_COLS + N_MLSTM_COLS + 2 * D_MODEL)
    w_in_t = jnp.swapaxes(w_in, 0, 1)
    col_o = N_RWKV_COLS + 2 * D_QK + D_MLSTM + 2 * MLSTM_HEADS
    z_gt, w_a_b, w_b_b, w_out_b, wt_rw_b, wt_ma_b, wt_mo_b = _win_matmul(
        u, w_in_t, N_RWKV_COLS + N_MLSTM_COLS, 2 * D_MODEL, 1024, 1024, BF16, "win_gate",
        cast_along=((w_branch_a, 0, D_RWKV), (w_branch_b, 0, D_MLSTM), (w_out, 0, D_MODEL),
                    (w_in_t, 0, RW_COLS), (w_in_t, N_RWKV_COLS, ML_ROWS_A), (w_in_t, col_o, D_MLSTM)))

    mu_rw = jnp.pad(mu_shift, (0, RW_COLS - N_RWKV_COLS))
    prep = _rwkv_prep(u, wt_rw_b, row(mu_rw), row(w0), row(a0), row(k_k), row(k_a), row(r_k),
                      _pad_rows(w_decay_up, LORA_PAD), _pad_rows(w_a_up, LORA_PAD), w_g_up, batch, seq)
    r, ld, kt, v, kkn, bb, g, bonus = prep
    y0, q, a_mat, c_mat = _rwkv_chunk(r, ld, kt, v, kkn, bb, batch, seq)
    ya = _rwkv_scan(y0, q, a_mat, c_mat, g, bonus, row(gn_g), row(gn_b), batch, seq)

    gbias = jnp.pad(jnp.concatenate([b_igate, b_fgate]), (0, ML_GATE_PAD - 2 * MLSTM_HEADS))
    yb = _mlstm(u, wt_ma_b, wt_mo_b, w_conv, row(b_conv), row(gbias), batch, seq)

    merged = _merge(ya, yb, z_gt, w_a_b, w_b_b)
    x1, u2 = _outproj(merged, w_out_b, x2, ada3, row(ln1_g), row(ln1_b), seq)

    h, w_ff2_b = _ff1(u2, w_ff1, row(b_ff1), w_ff2)
    return _ff2(h, w_ff2_b, row(b_ff2), x1, ada3, row(ln2_g), row(ln2_b), seq, F32)


def kernel(x, c, w_ada, b_ada, w_in, mu_shift, w0, w_decay_up, a0, w_a_up, w_g_up, k_k, k_a, r_k, gn_g, gn_b,
           w_conv, b_conv, b_igate, b_fgate, w_branch_a, w_branch_b, w_out, ln1_g, ln1_b, w_ff1, b_ff1,
           w_ff2, b_ff2, ln2_g, ln2_b):
    out_dtype = x.dtype
    batch, seq, d = x.shape
    assert d == D_MODEL and seq % 1024 == 0 and w_ada.shape[0] == DEPTH
    x2 = x.astype(F32).reshape(batch * seq, d)
    cf = c.astype(F32)
    for l in range(DEPTH):
        x2 = _layer(x2, cf, batch, seq, w_ada[l], b_ada[l], w_in[l], mu_shift[l], w0[l], w_decay_up[l], a0[l],
                    w_a_up[l], w_g_up[l], k_k[l], k_a[l], r_k[l].reshape(-1), gn_g[l], gn_b[l], w_conv[l],
                    b_conv[l], b_igate[l], b_fgate[l], w_branch_a[l], w_branch_b[l], w_out[l], ln1_g[l],
                    ln1_b[l], w_ff1[l], b_ff1[l], w_ff2[l], b_ff2[l], ln2_g[l], ln2_b[l])
    return x2.reshape(batch, seq, d).astype(out_dtype)
```

```python
import functools
import math

import jax
import jax.numpy as jnp
from jax import lax
from jax.experimental import pallas as pl
from jax.experimental.pallas import tpu as pltpu

F32 = jnp.float32
BF16 = jnp.bfloat16

D_MODEL = 2048
DEPTH = 1
D_RWKV = D_MODEL // 2
RWKV_HEAD = 64
DECAY_LORA = 96
A_LORA = 96
GATE_LORA = 256
RWKV_GN_EPS = 64e-5
D_MLSTM = D_MODEL // 2
MLSTM_HEADS = 4
MLSTM_DV = D_MLSTM // MLSTM_HEADS
MLSTM_DQK = MLSTM_DV // 2
D_QK = MLSTM_HEADS * MLSTM_DQK
CONV_K = 4
D_FF = 4 * D_MODEL
LN_EPS = 1e-5
ALPHA = (2.0 * DEPTH) ** 0.25

LANES = 128
LORA_PAD = 128
N_RWKV_COLS = 3 * D_RWKV + DECAY_LORA + A_LORA + GATE_LORA
N_MLSTM_COLS = 2 * D_QK + 2 * D_MLSTM + 2 * MLSTM_HEADS
RW_COLS = 3584
ML_GATE_PAD = 128
ML_ROWS_A = 2560
RWKV_CHUNK = 64
RWKV_TB = 512
RWKV_SCAN_ROWS = 256
RWKV_CORE_TB = 1024
MLSTM_CHUNK = 256
MLSTM_TB = 1024
EPILOGUE_ROWS = 256
VMEM_LIMIT = 56 * 1024 * 1024

NN = (((1,), (0,)), ((), ()))
NT = (((1,), (1,)), ((), ()))
TN = (((0,), (0,)), ((), ()))


def _cparams(sem):
    return pltpu.CompilerParams(dimension_semantics=sem, vmem_limit_bytes=VMEM_LIMIT)


def _bf16_parts(x, n):
    if x.dtype == BF16:
        return [x]
    parts, rem = [], x
    for i in range(n):
        p = rem.astype(BF16)
        parts.append(p)
        if i + 1 < n:
            rem = rem - p.astype(F32)
    return parts


def _mdot(a, b, na=1, nb=1, dims=NN):
    ap, bp = _bf16_parts(a, na), _bf16_parts(b, nb)
    order = max(len(ap), len(bp))
    acc = None
    for i, x in enumerate(ap):
        for j, y in enumerate(bp):
            if i + j < order:
                t = lax.dot_general(x, y, dims, preferred_element_type=F32)
                acc = t if acc is None else acc + t
    return acc


def _sigmoid(x):
    return 1.0 / (1.0 + jnp.exp(-x))


def _layernorm(x):
    mu = jnp.mean(x, axis=-1, keepdims=True)
    xc = x - mu
    var = jnp.mean(xc * xc, axis=-1, keepdims=True)
    return xc * lax.rsqrt(var + LN_EPS)


def _iota2(shape, dim):
    return lax.broadcasted_iota(jnp.int32, shape, dim)


def _head_block_ones(n, head):
    r, c = _iota2((n, n), 0), _iota2((n, n), 1)
    return jnp.where((r // head) == (c // head), 1.0, 0.0).astype(BF16)


def _head_sum(x, ones_blk):
    return _mdot(x, ones_blk)


def _ada_kernel(ct_ref, w_ref, b_ref, o_ref, sb_ref):
    tn = w_ref.shape[1]
    nb = sb_ref.shape[0]

    @pl.when(pl.program_id(0) == 0)
    def _():
        cv = ct_ref[...]
        sv = cv * _sigmoid(cv)
        for b in range(nb):
            sb_ref[b] = jnp.broadcast_to(sv[:, b:b + 1], sb_ref.shape[1:])

    ntile = tn // LANES

    def body(i, acc):
        r0 = pl.multiple_of(i * 8, 8)
        sb = [sb_ref[b, pl.ds(r0, 8), :] for b in range(nb)]
        wv = [w_ref[pl.ds(r0, 8), t * LANES:(t + 1) * LANES] for t in range(ntile)]
        return tuple(acc[b * ntile + t] + sb[b] * wv[t] for b in range(nb) for t in range(ntile))

    zero = jnp.zeros((8, LANES), F32)
    acc = lax.fori_loop(0, w_ref.shape[0] // 8, body, (zero,) * (nb * ntile), unroll=4)
    for b in range(nb):
        tot = jnp.concatenate([acc[b * ntile + t] for t in range(ntile)], axis=1)
        o_ref[b:b + 1, :] = jnp.sum(tot, axis=0, keepdims=True) + b_ref[...]


def _ada(c, w_ada, b_ada):
    batch = c.shape[0]
    assert batch == 2
    n = w_ada.shape[1]
    tn = 1536
    return pl.pallas_call(
        _ada_kernel,
        out_shape=jax.ShapeDtypeStruct((batch, n), F32),
        grid=(n // tn,),
        in_specs=[pl.BlockSpec((D_MODEL, batch), lambda j: (0, 0)),
                  pl.BlockSpec((D_MODEL, tn), lambda j: (0, j)),
                  pl.BlockSpec((1, tn), lambda j: (0, j))],
        out_specs=pl.BlockSpec((batch, tn), lambda j: (0, j)),
        scratch_shapes=[pltpu.VMEM((batch, D_MODEL, LANES), F32)],
        compiler_params=_cparams(("arbitrary",)),
        name="ada",
    )(c.T, w_ada, b_ada.reshape(1, n))


def _lnmod_kernel(x_ref, ada_ref, o_ref):
    xn = _layernorm(x_ref[...])
    sh, sc = ada_ref[0, 0:1, :], ada_ref[0, 1:2, :]
    o_ref[...] = (xn * (1.0 + sc) + sh).astype(o_ref.dtype)


def _lnmod(x2, ada3, seq):
    m = x2.shape[0]
    tb = 1024
    per_b = seq // tb
    def stream(x_hbm, ada_hbm, o_hbm):
        pltpu.emit_pipeline(
            _lnmod_kernel,
            grid=(m // tb,),
            in_specs=[pl.BlockSpec((tb, D_MODEL), lambda i: (i, 0), pipeline_mode=pl.Buffered(3)),
                      pl.BlockSpec((1, 6, D_MODEL), lambda i: (i // per_b, 0, 0))],
            out_specs=[pl.BlockSpec((tb, D_MODEL), lambda i: (i, 0))],
        )(x_hbm, ada_hbm, o_hbm)

    return pl.pallas_call(
        stream,
        out_shape=jax.ShapeDtypeStruct((m, D_MODEL), BF16),
        in_specs=[pl.BlockSpec(memory_space=pl.ANY), pl.BlockSpec(memory_space=pl.ANY)],
        out_specs=pl.BlockSpec(memory_space=pl.ANY),
        compiler_params=pltpu.CompilerParams(vmem_limit_bytes=VMEM_LIMIT),
        name="lnmod",
    )(x2, ada3)


def _win_kernel(n_cast, x_ref, wt_ref, *refs):
    cast_in, o_ref, cast_out, wb_ref = refs[:n_cast], refs[n_cast], refs[n_cast + 1:-1], refs[-1]

    @pl.when(pl.program_id(1) == 0)
    def _():
        wb_ref[...] = wt_ref[...].astype(wb_ref.dtype)

    for src, dst in zip(cast_in, cast_out):
        dst[...] = src[...].astype(dst.dtype)
    o_ref[...] = lax.dot_general(x_ref[...], wb_ref[...], NT, preferred_element_type=F32).astype(o_ref.dtype)


def _win_matmul(x, wt, col0, n_out, tm, tn, out_dtype, name, cast_along=()):
    m, k = x.shape
    assert col0 % 8 == 0 and n_out % tn == 0 and col0 + n_out <= wt.shape[0]
    n_m = m // tm
    steps = (n_out // tn) * n_m

    def w_index(j, i):
        return pl.multiple_of(col0 + j * tn, 8), 0

    slabs_in, slabs_out, out_shapes = [], [], []
    for w, row0, nrows in cast_along:
        rows = nrows // steps
        assert rows % 16 == 0 and row0 % 8 == 0 and rows * steps == nrows and row0 + nrows <= w.shape[0]
        slabs_in.append(pl.BlockSpec(
            (pl.Element(rows), pl.Element(w.shape[1])),
            functools.partial(lambda j, i, row0, rows: (pl.multiple_of(row0 + (j * n_m + i) * rows, 8), 0),
                              row0=row0, rows=rows)))
        slabs_out.append(pl.BlockSpec((rows, w.shape[1]), lambda j, i: (j * n_m + i, 0)))
        out_shapes.append(jax.ShapeDtypeStruct((nrows, w.shape[1]), BF16))
    cast_along = [w for w, _, _ in cast_along]
    outs = pl.pallas_call(
        functools.partial(_win_kernel, len(cast_along)),
        out_shape=[jax.ShapeDtypeStruct((m, n_out), out_dtype)] + out_shapes,
        grid=(n_out // tn, n_m),
        in_specs=[pl.BlockSpec((tm, k), lambda j, i: (i, 0)),
                  pl.BlockSpec((pl.Element(tn), pl.Element(k)), w_index)] + slabs_in,
        out_specs=[pl.BlockSpec((tm, tn), lambda j, i: (i, j))] + slabs_out,
        scratch_shapes=[pltpu.VMEM((tn, k), BF16)],
        compiler_params=_cparams(("arbitrary", "arbitrary")),
        name=name,
    )(x, wt, *cast_along)
    return outs if cast_along else outs[0]


def _rwkv_prep_kernel(u_ref, wt_ref, mu_ref, w0_ref, a0_ref, kk_ref, ka_ref, rk_ref, wdu_ref, wau_ref, wgu_ref,
                      r_o, ld_o, kt_o, v_o, kkn_o, bb_o, g_o, bonus_o, carry_ref):
    tb = u_ref.shape[0]

    @pl.when(pl.program_id(1) == 0)
    def _():
        carry_ref[...] = jnp.zeros_like(carry_ref)

    groups = [slice(i, i + EPILOGUE_ROWS) for i in range(0, tb, EPILOGUE_ROWS)]
    zs_all = [lax.dot_general(u_ref[rows, :], wt_ref[...], NT, preferred_element_type=F32) for rows in groups]
    ones_blk = _head_block_ones(LANES, RWKV_HEAD)
    last_row = carry_ref[0:1, :]
    for z, rows in zip(zs_all, groups):
        prev = pltpu.roll(z, 1, 0)
        prev = jnp.where(_iota2(z.shape, 0) == 0, last_row, prev)
        last_row = z[EPILOGUE_ROWS - 1:EPILOGUE_ROWS, :]
        zs = z + (prev - z) * mu_ref[...]

        c = D_RWKV
        r, k, v = zs[:, 0:c], zs[:, c:2 * c], zs[:, 2 * c:3 * c]
        wd = zs[:, 3 * c:3 * c + LORA_PAD]
        ad = zs[:, 3 * c + DECAY_LORA:3 * c + DECAY_LORA + LORA_PAD]
        gd = zs[:, 3 * c + DECAY_LORA + A_LORA:3 * c + DECAY_LORA + A_LORA + GATE_LORA]

        ld = -math.exp(-0.5) * _sigmoid(w0_ref[...] + _mdot(jnp.tanh(wd), wdu_ref[...], 2, 2))
        a = _sigmoid(a0_ref[...] + _mdot(ad, wau_ref[...]))
        g = _mdot(_sigmoid(gd), wgu_ref[...], 1, 1)

        kq = k * kk_ref[...]
        kt = k * (1.0 + (a - 1.0) * ka_ref[...])
        sq = kq * kq
        bn = r * kt * rk_ref[...]
        ss = jnp.concatenate([_head_sum(sq[:, s:s + LANES], ones_blk) for s in range(0, c, LANES)], axis=1)
        bs = jnp.concatenate([_head_sum(bn[:, s:s + LANES], ones_blk) for s in range(0, c, LANES)], axis=1)
        kkn = kq * lax.rsqrt(ss + 1e-12)

        r_o[rows, :] = r.astype(r_o.dtype)
        ld_o[rows, :] = ld
        kt_o[rows, :] = kt.astype(kt_o.dtype)
        v_o[rows, :] = v.astype(v_o.dtype)
        kkn_o[rows, :] = kkn.astype(kkn_o.dtype)
        bb_o[rows, :] = (a * kkn).astype(bb_o.dtype)
        g_o[rows, :] = g.astype(g_o.dtype)
        bonus_o[rows, :] = (bs * v).astype(bonus_o.dtype)
    carry_ref[0:1, :] = last_row


def _rwkv_prep(u, wt_rw, mu_rw, w0, a0, k_k, k_a, r_k, wdu, wau, wgu, batch, seq):
    m, k = u.shape
    assert wt_rw.shape == (RW_COLS, k)
    tb = RWKV_TB
    per_b = seq // tb
    row = lambda b, j: (b * per_b + j, 0)
    const = lambda b, j: (0, 0)
    vec = pl.BlockSpec((1, D_RWKV), const)
    outs = [jax.ShapeDtypeStruct((m, D_RWKV), F32 if i == 1 else BF16) for i in range(8)]
    return pl.pallas_call(
        _rwkv_prep_kernel,
        out_shape=outs,
        grid=(batch, per_b),
        in_specs=[pl.BlockSpec((tb, k), row),
                  pl.BlockSpec((RW_COLS, k), const, pipeline_mode=pl.Buffered(1)),
                  pl.BlockSpec((1, RW_COLS), const),
                  vec, vec, vec, vec, vec,
                  pl.BlockSpec((LORA_PAD, D_RWKV), const), pl.BlockSpec((LORA_PAD, D_RWKV), const),
                  pl.BlockSpec((GATE_LORA, D_RWKV), const)],
        out_specs=[pl.BlockSpec((tb, D_RWKV), row)] * 8,
        scratch_shapes=[pltpu.VMEM((8, RW_COLS), F32)],
        compiler_params=_cparams(("arbitrary", "arbitrary")),
        name="rwkv_prep",
    )(u, wt_rw, mu_rw, w0, a0, k_k, k_a, r_k, wdu, wau, wgu)


def _rwkv_chunk_kernel(r_ref, ld_ref, kt_ref, v_ref, kk_ref, bb_ref, y0_ref, q_ref, a_ref, c_ref):
    tb = r_ref.shape[0]
    lc = RWKV_CHUNK
    n = 2 * lc
    nchunk = tb // lc

    lane = _iota2((1, LANES), 1)
    m0 = jnp.where(lane < RWKV_HEAD, 1.0, 0.0)
    m1 = 1.0 - m0
    ri, ci = _iota2((n, n), 0), _iota2((n, n), 1)
    same_head = (ri // lc) == (ci // lc)
    strict = jnp.where(same_head & (ri > ci), 1.0, 0.0)
    incl = jnp.where(same_head & (ri >= ci), 1.0, 0.0)
    eye = jnp.where(ri == ci, 1.0, 0.0)

    tw = 4 * lc
    rt, ct = _iota2((tw, tw), 0), _iota2((tw, tw), 1)
    tri = jnp.where(((rt // lc) == (ct // lc)) & (ct <= rt), 1.0, 0.0).astype(BF16)

    ld = ld_ref[...]
    cl = jnp.concatenate([_mdot(tri, ld[i:i + tw], 1, 3) for i in range(0, tb, tw)], axis=0)
    e_pos = jnp.exp(cl)
    e_neg = jnp.exp(-cl)
    a_hat = kk_ref[...] * jnp.exp(cl - ld)
    b_hat = bb_ref[...] * e_neg
    k_hat = kt_ref[...] * e_neg
    r_hat = r_ref[...] * e_pos
    v_all = v_ref[...]

    def stack(x):
        return jnp.concatenate([x * m0, x * m1], axis=0)

    def bf(t):
        return t.astype(BF16)

    head0 = _iota2((lc, LANES), 1) < RWKV_HEAD

    def stack_b(x):
        xb, zero = x.astype(BF16), jnp.zeros((), BF16)
        return jnp.concatenate([jnp.where(head0, xb, zero), jnp.where(head0, zero, xb)], axis=0)

    chunks = range(nchunk)
    rows = [slice(c * lc, (c + 1) * lc) for c in chunks]
    a_b, b_b, k_b, v_b = ([stack_b(x[rows[c]]) for c in chunks] for x in (a_hat, b_hat, k_hat, v_all))
    r_st = [stack(r_hat[rows[c]]) for c in chunks]
    p = [_mdot(jnp.concatenate([a_b[c], bf(r_st[c])], axis=0),
               jnp.concatenate([b_b[c], k_b[c]], axis=0), dims=NT) for c in chunks]
    mab = [p[c][0:n, 0:n] * strict for c in chunks]
    mab_b = [bf(t) for t in mab]
    mak_b = [bf(p[c][0:n, n:] * strict) for c in chunks]
    mrb_b = [bf(p[c][n:, 0:n] * incl) for c in chunks]
    mrk_b = [bf(p[c][n:, n:] * incl) for c in chunks]

    x = [eye - jnp.where((ri // 2 == ci // 2), mab[c], 0.0) for c in chunks]
    w1 = [_mdot(mak_b[c], v_b[c]) for c in chunks]
    zero_b = jnp.zeros((), BF16)
    s = 2
    while s < lc:
        lvl = (ri // (2 * s) == ci // (2 * s)) & (ri // s != ci // s)
        x_b = [bf(t) for t in x]
        nx = [_mdot(jnp.where(lvl, mab_b[c], zero_b), x_b[c]) for c in chunks]
        x = [x[c] - _mdot(x_b[c], nx[c]) for c in chunks]
        s *= 2

    ta = [_mdot(x[c], jnp.concatenate([bf(w1[c]), a_b[c]], axis=1)) for c in chunks]
    zeros_b = jnp.zeros((n, LANES), BF16)
    ugv_b = [jnp.concatenate([bf(jnp.concatenate([-ta[c][:, 0:LANES], ta[c][:, LANES:]], axis=1)),
                              jnp.concatenate([v_b[c], zeros_b], axis=1)], axis=0) for c in chunks]
    ru = [_mdot(jnp.concatenate([mrb_b[c], mrk_b[c]], axis=1), ugv_b[c]) for c in chunks]
    y0 = [ru[c][:, 0:LANES] for c in chunks]
    q = [r_st[c] - ru[c][:, LANES:] for c in chunks]
    pl_row = [e_pos[(c + 1) * lc - 1:(c + 1) * lc, :] for c in chunks]
    bk_p = [jnp.concatenate([stack_b(b_hat[rows[c]] * pl_row[c]), stack_b(k_hat[rows[c]] * pl_row[c])], axis=0)
            for c in chunks]
    bu = [_mdot(bk_p[c], ugv_b[c], dims=TN) for c in chunks]
    ct = [bu[c][:, 0:LANES] for c in chunks]
    at = [eye * pl_row[c] - bu[c][:, LANES:] for c in chunks]

    for c in chunks:
        rows = slice(c * lc, (c + 1) * lc)
        y0_ref[rows, :] = y0[c][0:lc] + y0[c][lc:]
        q_ref[rows, :] = (q[c][0:lc] + q[c][lc:]).astype(q_ref.dtype)
        a_ref[0, c, 0] = (at[c][0:RWKV_HEAD] + at[c][RWKV_HEAD:]).astype(a_ref.dtype)
        c_ref[0, c, 0] = ct[c][0:RWKV_HEAD] + ct[c][RWKV_HEAD:]


def _rwkv_chunk(r, ld, kt, v, kkn, bb, batch, seq):
    m = r.shape[0]
    tb = RWKV_CORE_TB
    per_b = seq // tb
    cpb = tb // RWKV_CHUNK
    npair = D_RWKV // LANES
    slab = pl.BlockSpec((tb, LANES), lambda b, p, j: (b * per_b + j, p))
    mat = pl.BlockSpec((1, cpb, 1, RWKV_HEAD, LANES), lambda b, p, j: (b, j, p, 0, 0))
    mshape = (batch, seq // RWKV_CHUNK, npair, RWKV_HEAD, LANES)
    return pl.pallas_call(
        _rwkv_chunk_kernel,
        out_shape=[jax.ShapeDtypeStruct((m, D_RWKV), F32), jax.ShapeDtypeStruct((m, D_RWKV), BF16),
                   jax.ShapeDtypeStruct(mshape, BF16), jax.ShapeDtypeStruct(mshape, F32)],
        grid=(batch, npair, per_b),
        in_specs=[slab] * 6,
        out_specs=[slab, slab, mat, mat],
        compiler_params=_cparams(("arbitrary", "arbitrary", "arbitrary")),
        name="rwkv_chunk",
    )(r, ld, kt, v, kkn, bb)


def _rwkv_scan_kernel(y0_ref, q_ref, a_ref, c_ref, g_ref, bonus_ref, gng_ref, gnb_ref, ya_ref, s_ref):
    nb, rows, _ = y0_ref.shape
    lc = RWKV_CHUNK
    npair = D_RWKV // LANES

    @pl.when(pl.program_id(0) == 0)
    def _():
        s_ref[...] = jnp.zeros_like(s_ref)

    ones_blk = _head_block_ones(LANES, RWKV_HEAD)
    head0 = _iota2((RWKV_HEAD, LANES), 1) < RWKV_HEAD

    def block_diag(t):
        zero = jnp.zeros((), t.dtype)
        return jnp.concatenate([jnp.where(head0, t, zero), jnp.where(head0, zero, t)], axis=0)

    streams = [(b, p) for b in range(nb) for p in range(npair)]
    for ci in range(rows // lc):
        rs = slice(ci * lc, (ci + 1) * lc)
        state_b = [s_ref[b * npair + p].astype(BF16) for b, p in streams]
        y = [y0_ref[b, rs, p * LANES:(p + 1) * LANES]
             + _mdot(q_ref[b, rs, p * LANES:(p + 1) * LANES], state_b[i])
             for i, (b, p) in enumerate(streams)]
        for i, (b, p) in enumerate(streams):
            s_ref[b * npair + p] = (_mdot(block_diag(a_ref[b, ci, p]), state_b[i])
                                    + block_diag(c_ref[b, ci, p]))
        mu = [_head_sum(t, ones_blk) * (1.0 / RWKV_HEAD) for t in y]
        yc = [y[i] - mu[i] for i in range(len(streams))]
        var = [_head_sum(t * t, ones_blk) * (1.0 / RWKV_HEAD) for t in yc]
        for i, (b, p) in enumerate(streams):
            ls = slice(p * LANES, (p + 1) * LANES)
            yn = yc[i] * lax.rsqrt(var[i] + RWKV_GN_EPS) * gng_ref[:, ls] + gnb_ref[:, ls]
            ya_ref[b, rs, ls] = ((yn + bonus_ref[b, rs, ls]) * g_ref[b, rs, ls]).astype(ya_ref.dtype)


def _rwkv_scan(y0, q, a_mat, c_mat, g, bonus, gn_g, gn_b, batch, seq):
    rows = RWKV_SCAN_ROWS
    npair = D_RWKV // LANES
    tok = pl.BlockSpec((batch, rows, D_RWKV), lambda j: (0, j, 0))
    mat = pl.BlockSpec((batch, rows // RWKV_CHUNK, npair, RWKV_HEAD, LANES), lambda j: (0, j, 0, 0, 0))
    vec = pl.BlockSpec((1, D_RWKV), lambda j: (0, 0))
    shape3 = (batch, seq, D_RWKV)
    return pl.pallas_call(
        _rwkv_scan_kernel,
        out_shape=jax.ShapeDtypeStruct(shape3, BF16),
        grid=(seq // rows,),
        in_specs=[tok, tok, mat, mat, tok, tok, vec, vec],
        out_specs=tok,
        scratch_shapes=[pltpu.VMEM((batch * npair, LANES, LANES), F32)],
        compiler_params=_cparams(("arbitrary",)),
        name="rwkv_scan",
    )(y0.reshape(shape3), q.reshape(shape3), a_mat, c_mat, g.reshape(shape3), bonus.reshape(shape3),
      gn_g, gn_b).reshape(batch * seq, D_RWKV)


def _mlstm_kernel(u_ref, un_ref, wa_ref, wo_ref, wconv_ref, bconv_ref, gbias_ref, yb_ref,
                  tail_ref, ct_ref, n_ref, m_ref, p0_ref, p1_ref, p2_ref):
    lc = MLSTM_CHUNK
    rows_a = 2 * D_QK + D_MLSTM + ML_GATE_PAD
    pieces = (p0_ref, p1_ref, p2_ref)

    @pl.when(pl.program_id(1) == 0)
    def _():
        tail_ref[...] = jnp.zeros_like(tail_ref)
        ct_ref[...] = jnp.zeros_like(ct_ref)
        n_ref[...] = jnp.zeros_like(n_ref)
        m_ref[...] = jnp.zeros_like(m_ref)

    def project(x, piece):
        w = (wa_ref[0:2 * D_QK, :], wa_ref[2 * D_QK:rows_a, :], wo_ref[...])[piece]
        return lax.dot_general(x, w, NT, preferred_element_type=F32)

    groups = [slice(i, i + lc) for i in range(0, u_ref.shape[0], lc)]

    @pl.when((pl.program_id(0) == 0) & (pl.program_id(1) == 0))
    def _():
        for piece in range(3):
            pieces[piece][...] = project(u_ref[groups[0], :], piece)

    cur = [p[...] for p in pieces]
    for gi, rows in enumerate(groups):
        nxt = []

        def issue_next(piece, gi=gi, nxt=nxt):
            if gi + 1 < len(groups):
                nxt.append(project(u_ref[groups[gi + 1], :], piece))
            else:
                pieces[piece][...] = project(un_ref[...], piece)

        _mlstm_chunk(cur[0], cur[1], cur[2], rows, issue_next,
                     wconv_ref, bconv_ref, gbias_ref, yb_ref, tail_ref, ct_ref, n_ref, m_ref)
        cur = nxt


def _mlstm_chunk(zqk, zvg, o_all, rows, issue_next,
                 wconv_ref, bconv_ref, gbias_ref, yb_ref, tail_ref, ct_ref, n_ref, m_ref):
    lc = zqk.shape[0]
    dqk, dv, nh = MLSTM_DQK, MLSTM_DV, MLSTM_HEADS

    ext = jnp.concatenate([tail_ref[...], zqk], axis=0)
    tail_ref[...] = zqk[lc - 8:lc, :]
    conv = bconv_ref[...] + wconv_ref[CONV_K - 1:CONV_K, :] * zqk
    for d in range(1, CONV_K):
        conv = conv + wconv_ref[CONV_K - 1 - d:CONV_K - d, :] * ext[8 - d:8 - d + lc, :]
    qk = conv * _sigmoid(conv)
    q_all = qk[:, 0:D_QK]
    k_all = qk[:, D_QK:] * (dqk ** -0.5)
    v_all = zvg[:, 0:D_MLSTM]

    gz = zvg[:, D_MLSTM:] + gbias_ref[...]
    lf_col = jnp.minimum(gz, 0.0) - jnp.log(1.0 + jnp.exp(-jnp.abs(gz)))
    rr, cc = _iota2((lc, lc), 0), _iota2((lc, lc), 1)
    causal = cc <= rr
    tri = jnp.where(causal, 1.0, 0.0).astype(BF16)
    b_col = _mdot(tri, lf_col, 1, 3)
    sel = jnp.where(_iota2((8, LANES), 0) == _iota2((8, LANES), 1), 1.0, 0.0).astype(BF16)
    g_row = _mdot(sel, gz, 1, 3, NT)
    lf_row = jnp.minimum(g_row, 0.0) - jnp.log(1.0 + jnp.exp(-jnp.abs(g_row)))
    b_row = _mdot(lf_row, tri, 3, 1, NT)

    heads = range(nh)
    qf = [q_all[:, h * dqk:(h + 1) * dqk] for h in heads]
    q = [t.astype(BF16) for t in qf]
    k = [k_all[:, h * dqk:(h + 1) * dqk] for h in heads]
    v = [v_all[:, h * dv:(h + 1) * dv].astype(BF16) for h in heads]
    bc = [b_col[:, nh + h:nh + h + 1] for h in heads]
    ic = [gz[:, h:h + 1] for h in heads]
    br = [b_row[nh + h:nh + h + 1, :] for h in heads]
    ir = [g_row[h:h + 1, :] for h in heads]
    g_tot = [b_col[lc - 1:lc, nh + h:nh + h + 1] for h in heads]
    m_prev = [m_ref[0:1, h:h + 1] for h in heads]
    ct = [ct_ref[h] for h in heads]
    nvec = [n_ref[h:h + 1, :] for h in heads]

    issue_next(0)
    qk = [_mdot(q[h], k[h], dims=NT) for h in heads]
    qc = [_mdot(q[h], ct[h]) for h in heads]
    bri = [br[h] - ir[h] for h in heads]
    dm = [jnp.where(causal, bc[h] - bri[h], -jnp.inf) for h in heads]
    inter = [bc[h] + m_prev[h] for h in heads]
    mt = [jnp.maximum(inter[h], jnp.max(dm[h], axis=-1, keepdims=True)) for h in heads]
    sc = [qk[h] * jnp.exp(dm[h] - mt[h]) for h in heads]
    winter = [jnp.exp(inter[h] - mt[h]) for h in heads]
    issue_next(1)
    sv = [_mdot(sc[h], v[h]) for h in heads]
    m_new = [jnp.maximum(g_tot[h] + m_prev[h], jnp.max(g_tot[h] - bri[h], axis=-1, keepdims=True))
             for h in heads]
    kw = [k[h] * jnp.exp(g_tot[h] - bc[h] + ic[h] - m_new[h]) for h in heads]
    kv = [_mdot(kw[h], v[h], dims=TN) for h in heads]
    issue_next(2)
    for h in heads:
        num = winter[h] * qc[h] + sv[h]
        den = (winter[h] * jnp.sum(qf[h] * nvec[h], axis=-1, keepdims=True)
               + jnp.sum(sc[h], axis=-1, keepdims=True))
        hout = num / jnp.maximum(jnp.abs(den), jnp.exp(-mt[h]))
        wc = jnp.exp(g_tot[h] + m_prev[h] - m_new[h])
        ct_ref[h] = wc * ct[h] + kv[h]
        n_ref[h:h + 1, :] = wc * nvec[h] + jnp.sum(kw[h], axis=0, keepdims=True)
        m_ref[0:1, h:h + 1] = m_new[h]
        o = o_all[:, h * dv:(h + 1) * dv]
        yb_ref[rows, h * dv:(h + 1) * dv] = (_sigmoid(o) * hout).astype(yb_ref.dtype)


def _mlstm(u, wt_a, wt_o, w_conv, b_conv, gbias, batch, seq):
    m, k = u.shape
    tb = MLSTM_TB
    per_b = seq // tb
    const = lambda b, j: (0, 0)
    assert wt_a.shape[0] >= 2 * D_QK + D_MLSTM + ML_GATE_PAD and wt_o.shape == (D_MLSTM, k)
    resident = lambda w: pl.BlockSpec(w.shape, const, pipeline_mode=pl.Buffered(1))
    lc = MLSTM_CHUNK
    cps = tb // lc
    next_first = lambda b, j: (jnp.minimum((b * per_b + j + 1) * cps, m // lc - 1), 0)
    rows_a = 2 * D_QK + D_MLSTM + ML_GATE_PAD
    return pl.pallas_call(
        _mlstm_kernel,
        out_shape=jax.ShapeDtypeStruct((m, D_MLSTM), BF16),
        grid=(batch, per_b),
        in_specs=[pl.BlockSpec((tb, k), lambda b, j: (b * per_b + j, 0)),
                  pl.BlockSpec((lc, k), next_first),
                  resident(wt_a), resident(wt_o),
                  pl.BlockSpec((CONV_K, 2 * D_QK), const),
                  pl.BlockSpec((1, 2 * D_QK), const),
                  pl.BlockSpec((1, ML_GATE_PAD), const)],
        out_specs=pl.BlockSpec((tb, D_MLSTM), lambda b, j: (b * per_b + j, 0)),
        scratch_shapes=[pltpu.VMEM((8, 2 * D_QK), F32),
                        pltpu.VMEM((MLSTM_HEADS, MLSTM_DQK, MLSTM_DV), F32),
                        pltpu.VMEM((8, MLSTM_DQK), F32),
                        pltpu.VMEM((8, LANES), F32),
                        pltpu.VMEM((lc, 2 * D_QK), F32),
                        pltpu.VMEM((lc, rows_a - 2 * D_QK), F32),
                        pltpu.VMEM((lc, D_MLSTM), F32)],
        compiler_params=_cparams(("arbitrary", "arbitrary")),
        name="mlstm",
    )(u, u, wt_a, wt_o, w_conv, b_conv, gbias)


def _merge_kernel(ya_ref, yb_ref, zg_ref, wa_ref, wb_ref, o_ref):
    groups = [slice(i, i + EPILOGUE_ROWS) for i in range(0, ya_ref.shape[0], EPILOGUE_ROWS)]
    pab = [(jnp.dot(ya_ref[rows, :], wa_ref[...], preferred_element_type=F32),
            jnp.dot(yb_ref[rows, :], wb_ref[...], preferred_element_type=F32)) for rows in groups]
    for (pa, pb), rows in zip(pab, groups):
        ga = _sigmoid(zg_ref[rows, 0:D_MODEL].astype(F32))
        gb = _sigmoid(zg_ref[rows, D_MODEL:].astype(F32))
        o_ref[rows, :] = (ga * pa + gb * pb).astype(o_ref.dtype)


def _merge(ya, yb, zg, wa, wb):
    m = ya.shape[0]
    tm = 512
    const = lambda i: (0, 0)
    return pl.pallas_call(
        _merge_kernel,
        out_shape=jax.ShapeDtypeStruct((m, D_MODEL), BF16),
        grid=(m // tm,),
        in_specs=[pl.BlockSpec((tm, D_RWKV), lambda i: (i, 0)),
                  pl.BlockSpec((tm, D_MLSTM), lambda i: (i, 0)),
                  pl.BlockSpec((tm, 2 * D_MODEL), lambda i: (i, 0)),
                  pl.BlockSpec((D_RWKV, D_MODEL), const),
                  pl.BlockSpec((D_MLSTM, D_MODEL), const)],
        out_specs=pl.BlockSpec((tm, D_MODEL), lambda i: (i, 0)),
        compiler_params=_cparams(("arbitrary",)),
        name="merge",
    )(ya, yb, zg, wa, wb)


def _outproj_kernel(mg_ref, w_ref, x_ref, ada_ref, g_ref, b_ref, x1_ref, u2_ref):
    groups = [slice(i, i + EPILOGUE_ROWS) for i in range(0, mg_ref.shape[0], EPILOGUE_ROWS)]
    ys = [jnp.dot(mg_ref[rows, :], w_ref[...], preferred_element_type=F32) for rows in groups]
    g1 = ada_ref[0, 2:3, :]
    sh2, sc2 = ada_ref[0, 3:4, :], ada_ref[0, 4:5, :]
    for y, rows in zip(ys, groups):
        x1 = _layernorm(ALPHA * x_ref[rows, :] + g1 * y) * g_ref[...] + b_ref[...]
        x1_ref[rows, :] = x1
        u2_ref[rows, :] = (_layernorm(x1) * (1.0 + sc2) + sh2).astype(u2_ref.dtype)


def _outproj(merged, w_out, x2, ada3, ln_g, ln_b, seq):
    m = merged.shape[0]
    tm = 512
    per_b = seq // tm
    const = lambda i: (0, 0)
    row = pl.BlockSpec((tm, D_MODEL), lambda i: (i, 0))
    return pl.pallas_call(
        _outproj_kernel,
        out_shape=[jax.ShapeDtypeStruct((m, D_MODEL), F32), jax.ShapeDtypeStruct((m, D_MODEL), BF16)],
        grid=(m // tm,),
        in_specs=[row, pl.BlockSpec((D_MODEL, D_MODEL), const, pipeline_mode=pl.Buffered(1)), row,
                  pl.BlockSpec((1, 6, D_MODEL), lambda i: (i // per_b, 0, 0)),
                  pl.BlockSpec((1, D_MODEL), const), pl.BlockSpec((1, D_MODEL), const)],
        out_specs=[row, row],
        compiler_params=_cparams(("arbitrary",)),
        name="outproj",
    )(merged, w_out, x2, ada3, ln_g, ln_b)


def _ff1_kernel(x_ref, w_ref, b_ref, w2_ref, o_ref, w2b_ref, wb_ref):
    @pl.when(pl.program_id(1) == 0)
    def _():
        wb_ref[...] = w_ref[...].astype(wb_ref.dtype)

    w2b_ref[...] = w2_ref[...].astype(w2b_ref.dtype)
    groups = [slice(i, i + 2 * EPILOGUE_ROWS) for i in range(0, x_ref.shape[0], 2 * EPILOGUE_ROWS)]
    hs = [jnp.dot(x_ref[rows, :], wb_ref[...], preferred_element_type=F32) for rows in groups]
    for h, rows in zip(hs, groups):
        h = jnp.maximum(h + b_ref[...], 0.0)
        o_ref[rows, :] = (h * h).astype(o_ref.dtype)


def _ff1(u2, w1, b1, w2):
    m, k = u2.shape
    n = w1.shape[1]
    tm, tn = 1024, 1024
    n_m = m // tm
    rows2 = w2.shape[0] // ((n // tn) * n_m)
    assert rows2 % 16 == 0 and rows2 * (n // tn) * n_m == w2.shape[0]
    slab = pl.BlockSpec((rows2, w2.shape[1]), lambda j, i: (j * n_m + i, 0))
    return pl.pallas_call(
        _ff1_kernel,
        out_shape=[jax.ShapeDtypeStruct((m, n), BF16), jax.ShapeDtypeStruct(w2.shape, BF16)],
        grid=(n // tn, n_m),
        in_specs=[pl.BlockSpec((tm, k), lambda j, i: (i, 0)),
                  pl.BlockSpec((k, tn), lambda j, i: (0, j)),
                  pl.BlockSpec((1, tn), lambda j, i: (0, j)),
                  slab],
        out_specs=[pl.BlockSpec((tm, tn), lambda j, i: (i, j)), slab],
        scratch_shapes=[pltpu.VMEM((k, tn), BF16)],
        compiler_params=_cparams(("arbitrary", "arbitrary")),
        name="ff1",
    )(u2, w1, b1, w2)


def _ff2_kernel(h_ref, w_ref, b_ref, x1_ref, ada_ref, g_ref, bb_ref, o_ref):
    y2 = jnp.dot(h_ref[...], w_ref[...], preferred_element_type=F32) + b_ref[...]
    g2 = ada_ref[0, 5:6, :]
    o_ref[...] = (_layernorm(ALPHA * x1_ref[...] + g2 * y2) * g_ref[...] + bb_ref[...]).astype(o_ref.dtype)


def _ff2(h, w2, b2, x1, ada3, ln_g, ln_b, seq, out_dtype):
    m, k = h.shape
    tm = 256
    per_b = seq // tm
    const = lambda i: (0, 0)
    row = pl.BlockSpec((tm, D_MODEL), lambda i: (i, 0))
    return pl.pallas_call(
        _ff2_kernel,
        out_shape=jax.ShapeDtypeStruct((m, D_MODEL), out_dtype),
        grid=(m // tm,),
        in_specs=[pl.BlockSpec((tm, k), lambda i: (i, 0)),
                  pl.BlockSpec((k, D_MODEL), const, pipeline_mode=pl.Buffered(1)),
                  pl.BlockSpec((1, D_MODEL), const), row,
                  pl.BlockSpec((1, 6, D_MODEL), lambda i: (i // per_b, 0, 0)),
                  pl.BlockSpec((1, D_MODEL), const), pl.BlockSpec((1, D_MODEL), const)],
        out_specs=row,
        compiler_params=_cparams(("arbitrary",)),
        name="ff2",
    )(h, w2, b2, x1, ada3, ln_g, ln_b)


def _pad_rows(w, height):
    return jnp.pad(w, ((0, height - w.shape[0]), (0, 0)))


def _layer(x2, c, batch, seq, w_ada, b_ada, w_in, mu_shift, w0, w_decay_up, a0, w_a_up, w_g_up, k_k, k_a, r_k,
           gn_g, gn_b, w_conv, b_conv, b_igate, b_fgate, w_branch_a, w_branch_b, w_out,
           ln1_g, ln1_b, w_ff1, b_ff1, w_ff2, b_ff2, ln2_g, ln2_b):
    row = lambda v: v.reshape(1, -1)

    ada3 = _ada(c, w_ada, b_ada).reshape(batch, 6, D_MODEL)
    u = _lnmod(x2, ada3, seq)

    assert w_in.shape == (D_MODEL, N_RWKV_COLS + N_MLSTM_COLS + 2 * D_MODEL)
    w_in_t = jnp.swapaxes(w_in, 0, 1)
    col_o = N_RWKV_COLS + 2 * D_QK + D_MLSTM + 2 * MLSTM_HEADS
    z_gt, w_a_b, w_b_b, w_out_b, wt_rw_b, wt_ma_b, wt_mo_b = _win_matmul(
        u, w_in_t, N_RWKV_COLS + N_MLSTM_COLS, 2 * D_MODEL, 1024, 1024, BF16, "win_gate",
        cast_along=((w_branch_a, 0, D_RWKV), (w_branch_b, 0, D_MLSTM), (w_out, 0, D_MODEL),
                    (w_in_t, 0, RW_COLS), (w_in_t, N_RWKV_COLS, ML_ROWS_A), (w_in_t, col_o, D_MLSTM)))

    mu_rw = jnp.pad(mu_shift, (0, RW_COLS - N_RWKV_COLS))
    prep = _rwkv_prep(u, wt_rw_b, row(mu_rw), row(w0), row(a0), row(k_k), row(k_a), row(r_k),
                      _pad_rows(w_decay_up, LORA_PAD), _pad_rows(w_a_up, LORA_PAD), w_g_up, batch, seq)
    r, ld, kt, v, kkn, bb, g, bonus = prep
    y0, q, a_mat, c_mat = _rwkv_chunk(r, ld, kt, v, kkn, bb, batch, seq)
    ya = _rwkv_scan(y0, q, a_mat, c_mat, g, bonus, row(gn_g), row(gn_b), batch, seq)

    gbias = jnp.pad(jnp.concatenate([b_igate, b_fgate]), (0, ML_GATE_PAD - 2 * MLSTM_HEADS))
    yb = _mlstm(u, wt_ma_b, wt_mo_b, w_conv, row(b_conv), row(gbias), batch, seq)

    merged = _merge(ya, yb, z_gt, w_a_b, w_b_b)
    x1, u2 = _outproj(merged, w_out_b, x2, ada3, row(ln1_g), row(ln1_b), seq)

    h, w_ff2_b = _ff1(u2, w_ff1, row(b_ff1), w_ff2)
    return _ff2(h, w_ff2_b, row(b_ff2), x1, ada3, row(ln2_g), row(ln2_b), seq, F32)


def kernel(x, c, w_ada, b_ada, w_in, mu_shift, w0, w_decay_up, a0, w_a_up, w_g_up, k_k, k_a, r_k, gn_g, gn_b,
           w_conv, b_conv, b_igate, b_fgate, w_branch_a, w_branch_b, w_out, ln1_g, ln1_b, w_ff1, b_ff1,
           w_ff2, b_ff2, ln2_g, ln2_b):
    out_dtype = x.dtype
    batch, seq, d = x.shape
    assert d == D_MODEL and seq % 1024 == 0 and w_ada.shape[0] == DEPTH
    x2 = x.astype(F32).reshape(batch * seq, d)
    cf = c.astype(F32)
    for l in range(DEPTH):
        x2 = _layer(x2, cf, batch, seq, w_ada[l], b_ada[l], w_in[l], mu_shift[l], w0[l], w_decay_up[l], a0[l],
                    w_a_up[l], w_g_up[l], k_k[l], k_a[l], r_k[l].reshape(-1), gn_g[l], gn_b[l], w_conv[l],
                    b_conv[l], b_igate[l], b_fgate[l], w_branch_a[l], w_branch_b[l], w_out[l], ln1_g[l],
                    ln1_b[l], w_ff1[l], b_ff1[l], w_ff2[l], b_ff2[l], ln2_g[l], ln2_b[l])
    return x2.reshape(batch, seq, d).astype(out_dtype)
```
